```python
import jax, jax.numpy as jnp
from jax import lax
import numpy as np

D_MODEL = 2048
BATCH = 4
SEQ = 4096
DEPTH = 4

CHUNK = 64
LEFT_CHUNKS = 8
BAND_CHUNKS = LEFT_CHUNKS + 1
Q_BLOCK = 128

A_HEAD_DIM = 128
A_HEADS = D_MODEL // (2 * A_HEAD_DIM)
A_WIDTH = A_HEADS * A_HEAD_DIM
REL_CLIP = 128

B_NOPE = 128
B_ROPE = 64
B_V = 128
B_HEADS = D_MODEL // (2 * B_V)
B_WIDTH = B_HEADS * B_V
Q_LORA = 768
KV_LORA = 512
ROPE_THETA = 10000.0

MIX_WIDTH = A_WIDTH + B_WIDTH
IN_SIZES = (A_WIDTH, A_WIDTH, A_WIDTH, Q_LORA, KV_LORA, B_ROPE)
IN_WIDTH = sum(IN_SIZES)
IN_SPLITS = tuple(int(v) for v in np.cumsum(IN_SIZES)[:-1])

N_MEM = 256
X_HEADS = 4
X_HEAD_DIM = 128
X_WIDTH = X_HEADS * X_HEAD_DIM

D_FF = 256 * (-(-8 * D_MODEL // (3 * 256)))

EPS = 1e-6

kernel_name = "hybrid_chunked_relpos_mla_memory_encoder"


def rms_norm(x, g):
    xf = x.astype(jnp.float32)
    y = xf * lax.rsqrt(jnp.mean(xf * xf, axis=-1, keepdims=True) + EPS)
    return y.astype(x.dtype) * g


def softmax_f32(s):
    return jax.nn.softmax(s.astype(jnp.float32), axis=-1)


def rope_tables(positions):
    half = B_ROPE // 2
    inv = ROPE_THETA ** (-jnp.arange(half, dtype=jnp.float32) / half)
    ang = positions.astype(jnp.float32)[..., None] * inv
    return jnp.cos(ang), jnp.sin(ang)


def apply_rope(x, cos, sin):
    half = x.shape[-1] // 2
    x1, x2 = x[..., :half], x[..., half:]
    c = cos.astype(x.dtype)
    s = sin.astype(x.dtype)
    return jnp.concatenate([x1 * c - x2 * s, x1 * s + x2 * c], axis=-1)


def chunked_relpos_attention(q, k, v, rel_bias):
    b, s, h, dh = q.shape
    nc = s // CHUNK
    qc = q.reshape(b, nc, CHUNK, h, dh)
    pad = ((0, 0), (LEFT_CHUNKS * CHUNK, 0), (0, 0), (0, 0))
    kp = jnp.pad(k, pad).reshape(b, nc + LEFT_CHUNKS, CHUNK, h, dh)
    vp = jnp.pad(v, pad).reshape(b, nc + LEFT_CHUNKS, CHUNK, h, dh)
    kb = jnp.concatenate([kp[:, j:j + nc] for j in range(BAND_CHUNKS)], axis=2)
    vb = jnp.concatenate([vp[:, j:j + nc] for j in range(BAND_CHUNKS)], axis=2)
    scores = jnp.einsum('bnqhd,bnkhd->bhnqk', qc, kb).astype(jnp.float32) * (dh ** -0.5)
    qpos = LEFT_CHUNKS * CHUNK + jnp.arange(CHUNK)
    kpos = jnp.arange(BAND_CHUNKS * CHUNK)
    rel = jnp.clip(qpos[:, None] - kpos[None, :], -REL_CLIP, REL_CLIP) + REL_CLIP
    bias = rel_bias[:, rel].astype(jnp.float32)
    kglob = jnp.arange(nc)[:, None] * CHUNK + kpos[None, :] - LEFT_CHUNKS * CHUNK
    valid = kglob >= 0
    scores = scores + bias[None, :, None]
    scores = jnp.where(valid[None, None, :, None, :], scores, -jnp.inf)
    p = softmax_f32(scores).astype(v.dtype)
    o = jnp.einsum('bhnqk,bnkhd->bnqhd', p, vb)
    return o.reshape(b, s, h * dh)


def mla_attention(q_lat, kv_lat, k_rope, cos, sin, q_norm, kv_norm, w_uq, w_ukv):
    b, s, _ = q_lat.shape
    q = (rms_norm(q_lat, q_norm) @ w_uq).reshape(b, s, B_HEADS, B_NOPE + B_ROPE)
    q_nope = q[..., :B_NOPE]
    q_pe = apply_rope(q[..., B_NOPE:], cos[:, :, None], sin[:, :, None])
    kv = (rms_norm(kv_lat, kv_norm) @ w_ukv).reshape(b, s, B_HEADS, B_NOPE + B_V)
    k_nope, v = kv[..., :B_NOPE], kv[..., B_NOPE:]
    k_pe = apply_rope(k_rope, cos, sin)
    nb = s // Q_BLOCK
    qn_blocks = q_nope.reshape(b, nb, Q_BLOCK, B_HEADS, B_NOPE).transpose(1, 0, 2, 3, 4)
    qr_blocks = q_pe.reshape(b, nb, Q_BLOCK, B_HEADS, B_ROPE).transpose(1, 0, 2, 3, 4)
    k_chunk = jnp.arange(s) // CHUNK
    scale = (B_NOPE + B_ROPE) ** -0.5

    def block(args):
        i, qn, qr = args
        sc = (jnp.einsum('bqhd,bkhd->bhqk', qn, k_nope)
              + jnp.einsum('bqhr,bkr->bhqk', qr, k_pe)).astype(jnp.float32) * scale
        q_chunk = (i * Q_BLOCK + jnp.arange(Q_BLOCK)) // CHUNK
        allowed = k_chunk[None, :] <= q_chunk[:, None]
        sc = jnp.where(allowed[None, None], sc, -jnp.inf)
        p = softmax_f32(sc).astype(v.dtype)
        return jnp.einsum('bhqk,bkhd->bqhd', p, v)

    o = lax.map(block, (jnp.arange(nb), qn_blocks, qr_blocks))
    return o.transpose(1, 0, 2, 3, 4).reshape(b, s, B_WIDTH)


def memory_cross_attention(h, mem_n, w_xq, w_xkv, w_xo):
    b, s, _ = h.shape
    m = mem_n.shape[1]
    q = (h @ w_xq).reshape(b, s, X_HEADS, X_HEAD_DIM)
    kv = (mem_n @ w_xkv).reshape(b, m, 2, X_HEADS, X_HEAD_DIM)
    k, v = kv[:, :, 0], kv[:, :, 1]
    sc = jnp.einsum('bqhd,bmhd->bhqm', q, k).astype(jnp.float32) * (X_HEAD_DIM ** -0.5)
    p = softmax_f32(sc).astype(v.dtype)
    o = jnp.einsum('bhqm,bmhd->bqhd', p, v).reshape(b, s, X_WIDTH)
    return o @ w_xo


def swiglu(h, w_gate, w_up, w_down):
    return (jax.nn.silu(h @ w_gate) * (h @ w_up)) @ w_down


def setup_inputs(seed: int = 0) -> dict:
    key = jax.random.key(seed)
    ks = jax.random.split(key, 24)
    f32 = jnp.float32

    def dense(k, shape):
        return jax.random.normal(k, shape, f32) * (shape[-2] ** -0.5)

    def gain(k, shape):
        return 1.0 + 0.02 * jax.random.normal(k, shape, f32)

    x = jax.random.normal(ks[0], (BATCH, SEQ, D_MODEL), f32)
    mem = jax.random.normal(ks[1], (BATCH, N_MEM, D_MODEL), f32)
    offset = jax.random.randint(ks[2], (BATCH, 1), 0, 4096, dtype=jnp.int32)
    positions = (offset + jnp.arange(SEQ, dtype=jnp.int32)[None, :]).astype(jnp.int32)
    return {
        "x": x,
        "mem": mem,
        "positions": positions,
        "norm_mix": gain(ks[3], (DEPTH, D_MODEL)),
        "w_in": dense(ks[4], (DEPTH, D_MODEL, IN_WIDTH)),
        "rel_bias": 0.1 * jax.random.normal(ks[5], (DEPTH, A_HEADS, 2 * REL_CLIP + 1), f32),
        "q_norm": gain(ks[6], (DEPTH, Q_LORA)),
        "kv_norm": gain(ks[7], (DEPTH, KV_LORA)),
        "w_uq": dense(ks[8], (DEPTH, Q_LORA, B_HEADS * (B_NOPE + B_ROPE))),
        "w_ukv": dense(ks[9], (DEPTH, KV_LORA, B_HEADS * (B_NOPE + B_V))),
        "w_out": dense(ks[10], (DEPTH, MIX_WIDTH, D_MODEL)),
        "norm_mem": gain(ks[11], (DEPTH, D_MODEL)),
        "mem_norm": gain(ks[12], (D_MODEL,)),
        "w_xq": dense(ks[13], (DEPTH, D_MODEL, X_WIDTH)),
        "w_xkv": dense(ks[14], (DEPTH, D_MODEL, 2 * X_WIDTH)),
        "w_xo": dense(ks[15], (DEPTH, X_WIDTH, D_MODEL)),
        "norm_ffn": gain(ks[16], (DEPTH, D_MODEL)),
        "w_gate": dense(ks[17], (DEPTH, D_MODEL, D_FF)),
        "w_up": dense(ks[18], (DEPTH, D_MODEL, D_FF)),
        "w_down": dense(ks[19], (DEPTH, D_FF, D_MODEL)),
        "norm_final": gain(ks[20], (D_MODEL,)),
    }


def reference(x, mem, positions, norm_mix, w_in, rel_bias, q_norm, kv_norm, w_uq, w_ukv,
              w_out, norm_mem, mem_norm, w_xq, w_xkv, w_xo, norm_ffn, w_gate, w_up,
              w_down, norm_final):
    b, s, _ = x.shape
    cos, sin = rope_tables(positions)
    mem_n = rms_norm(mem, mem_norm)
    for l in range(DEPTH):
        h = rms_norm(x, norm_mix[l])
        proj = h @ w_in[l]
        qa, ka, va, q_lat, kv_lat, k_rope = jnp.split(proj, IN_SPLITS, axis=-1)
        shp = (b, s, A_HEADS, A_HEAD_DIM)
        oa = chunked_relpos_attention(qa.reshape(shp), ka.reshape(shp), va.reshape(shp),
                                      rel_bias[l])
        ob = mla_attention(q_lat, kv_lat, k_rope, cos, sin, q_norm[l], kv_norm[l],
                           w_uq[l], w_ukv[l])
        x = x + jnp.concatenate([oa, ob], axis=-1) @ w_out[l]
        h = rms_norm(x, norm_mem[l])
        x = x + memory_cross_attention(h, mem_n, w_xq[l], w_xkv[l], w_xo[l])
        h = rms_norm(x, norm_ffn[l])
        x = x + swiglu(h, w_gate[l], w_up[l], w_down[l])
    return rms_norm(x, norm_final)
```

```python
import functools
import math

import jax
import jax.numpy as jnp
from jax import lax
from jax.experimental import pallas as pl
from jax.experimental.pallas import tpu as pltpu

D_MODEL = 2048
DEPTH = 4
CHUNK = 64
LEFT_CHUNKS = 8
A_HEAD_DIM = 128
A_HEADS = 8
A_WIDTH = A_HEADS * A_HEAD_DIM
REL_CLIP = 128
B_NOPE = 128
B_ROPE = 64
B_V = 128
B_HEADS = 8
B_WIDTH = B_HEADS * B_V
Q_LORA = 768
KV_LORA = 512
ROPE_THETA = 10000.0
N_MEM = 256
X_HEADS = 4
X_HEAD_DIM = 128
X_WIDTH = X_HEADS * X_HEAD_DIM
D_FF = 5632
EPS = 1e-6

V7X_LANES = 128
V7X_VMEM_BYTES = 64 * 1024 * 1024
V7X_VMEM_REQUEST_CAP = 56 * 1024 * 1024

BF16 = jnp.bfloat16
F32 = jnp.float32

IN_QLAT_OFF = 3 * A_WIDTH
IN_ROPE_OFF = IN_QLAT_OFF + Q_LORA
IN_KVLAT_OFF = IN_ROPE_OFF + 2 * V7X_LANES
IN_PAD_WIDTH = IN_KVLAT_OFF + KV_LORA

A_TQ = 4 * CHUNK
A_WIN = A_TQ + LEFT_CHUNKS * CHUNK
A_KBLKS = A_WIN // A_TQ

B_TQ = 512
B_TK = 512
B_HEADS_PER_STEP = 2
B_QK = B_NOPE + 2 * B_ROPE

LOG2E = math.log2(math.e)


def _cparams(sem, vmem_bytes):
    limit = int(min(max(vmem_bytes, 32 * 1024 * 1024), V7X_VMEM_REQUEST_CAP))
    return pltpu.CompilerParams(dimension_semantics=sem, vmem_limit_bytes=limit)


def _rms_rows(x, g):
    ms = jnp.mean(x * x, axis=-1, keepdims=True)
    return (x * lax.rsqrt(ms + EPS)) * g


def _norm_into(h_ref, x_ref, g_ref, rows=128):
    n = x_ref.shape[0] // rows

    def body(c, carry):
        r0 = pl.multiple_of(c * rows, rows)
        x = x_ref[pl.ds(r0, rows), :].astype(F32)
        h_ref[pl.ds(r0, rows), :] = _rms_rows(x, g_ref[...]).astype(h_ref.dtype)
        return carry

    lax.fori_loop(0, n, body, 0)


def _norm_matmul_kernel(x_ref, g_ref, w_ref, o_ref, h_ref):
    @pl.when(pl.program_id(1) == 0)
    def _():
        _norm_into(h_ref, x_ref, g_ref)

    o_ref[...] = jnp.dot(h_ref[...], w_ref[...], preferred_element_type=F32).astype(o_ref.dtype)


def _norm_matmul(x, g, w, *, tm, tn, name):
    m, k = x.shape
    n = w.shape[1]
    xb = x.dtype.itemsize
    est = 2 * tm * k * xb + tm * k * 2 + 2 * k * tn * 2 + 2 * tm * tn * 2 + tm * tn * 4 + (4 << 20)
    return pl.pallas_call(
        _norm_matmul_kernel,
        out_shape=jax.ShapeDtypeStruct((m, n), BF16),
        grid=(m // tm, n // tn),
        in_specs=[
            pl.BlockSpec((tm, k), lambda i, j: (i, 0)),
            pl.BlockSpec((1, k), lambda i, j: (0, 0)),
            pl.BlockSpec((k, tn), lambda i, j: (0, j)),
        ],
        out_specs=pl.BlockSpec((tm, tn), lambda i, j: (i, j)),
        scratch_shapes=[pltpu.VMEM((tm, k), BF16)],
        compiler_params=_cparams(("arbitrary", "arbitrary"), est),
        name=name,
    )(x, g, w)


def _mm_res_kernel(*refs, n_in):
    a_refs = refs[:n_in]
    w_refs = refs[n_in:2 * n_in]
    r_ref = refs[2 * n_in]
    o_ref = refs[2 * n_in + 1]
    acc = r_ref[...]
    for a_ref, w_ref in zip(a_refs, w_refs):
        acc = acc + jnp.dot(a_ref[...], w_ref[...], preferred_element_type=F32)
    o_ref[...] = acc


def _mm_res(a_list, w_list, res, *, tm, tn, name):
    m, n = res.shape
    n_in = len(a_list)
    in_specs = []
    est = (4 + 2 * n_in) * tm * tn * 4 + (4 << 20)
    for a in a_list:
        k = a.shape[1]
        in_specs.append(pl.BlockSpec((tm, k), lambda i, j: (i, 0)))
        est += 2 * tm * k * 2
    for w in w_list:
        k = w.shape[0]
        in_specs.append(pl.BlockSpec((k, tn), lambda i, j: (0, j)))
        est += 2 * k * tn * 2
    in_specs.append(pl.BlockSpec((tm, tn), lambda i, j: (i, j)))
    return pl.pallas_call(
        functools.partial(_mm_res_kernel, n_in=n_in),
        out_shape=jax.ShapeDtypeStruct((m, n), F32),
        grid=(m // tm, n // tn),
        in_specs=in_specs,
        out_specs=pl.BlockSpec((tm, tn), lambda i, j: (i, j)),
        compiler_params=_cparams(("arbitrary", "arbitrary"), est),
        name=name,
    )(*a_list, *w_list, res)


def _swiglu_gu_kernel(x_ref, g_ref, wg_ref, wu_ref, o_ref, h_ref):
    @pl.when(pl.program_id(1) == 0)
    def _():
        _norm_into(h_ref, x_ref, g_ref)

    h = h_ref[...]
    gate = jnp.dot(h, wg_ref[...], preferred_element_type=F32)
    up = jnp.dot(h, wu_ref[...], preferred_element_type=F32)
    o_ref[...] = (gate * (1.0 / (1.0 + jnp.exp(-gate))) * up).astype(o_ref.dtype)


def _swiglu_gu(x, g, wg, wu, *, tm, tn, name):
    m, k = x.shape
    n = wg.shape[1]
    est = 2 * tm * k * 4 + tm * k * 2 + 4 * k * tn * 2 + 2 * tm * tn * 2 + 3 * tm * tn * 4 + (4 << 20)
    return pl.pallas_call(
        _swiglu_gu_kernel,
        out_shape=jax.ShapeDtypeStruct((m, n), BF16),
        grid=(m // tm, n // tn),
        in_specs=[
            pl.BlockSpec((tm, k), lambda i, j: (i, 0)),
            pl.BlockSpec((1, k), lambda i, j: (0, 0)),
            pl.BlockSpec((k, tn), lambda i, j: (0, j)),
            pl.BlockSpec((k, tn), lambda i, j: (0, j)),
        ],
        out_specs=pl.BlockSpec((tm, tn), lambda i, j: (i, j)),
        scratch_shapes=[pltpu.VMEM((tm, k), BF16)],
        compiler_params=_cparams(("arbitrary", "arbitrary"), est),
        name=name,
    )(x, g, wg, wu)


def _rope_table_kernel(pos_ref, inv_ref, sign_ref, cs_ref):
    ang = pos_ref[...].astype(F32) * inv_ref[...]
    lane = lax.broadcasted_iota(jnp.int32, ang.shape, 1)
    cs_ref[...] = jnp.where(lane < B_ROPE, jnp.cos(ang), jnp.sin(ang) * sign_ref[...])


def _rope_table(pos_col, inv_row, sign_row, *, tm):
    m = pos_col.shape[0]
    return pl.pallas_call(
        _rope_table_kernel,
        out_shape=jax.ShapeDtypeStruct((m, V7X_LANES), F32),
        grid=(m // tm,),
        in_specs=[
            pl.BlockSpec((tm, 1), lambda i: (i, 0)),
            pl.BlockSpec((1, V7X_LANES), lambda i: (0, 0)),
            pl.BlockSpec((1, V7X_LANES), lambda i: (0, 0)),
        ],
        out_specs=pl.BlockSpec((tm, V7X_LANES), lambda i: (i, 0)),
        compiler_params=_cparams(("arbitrary",), 32 << 20),
        name="rope_table",
    )(pos_col, inv_row, sign_row)


def _rope_pair_sum(slot, cs):
    t = slot * cs
    return t + pltpu.roll(t, B_ROPE, axis=1)


def _mla_prep_kernel(ql_ref, rope_ref, kvl_ref, cs_ref, gq_ref, gkv_ref, wq_ref, wkv_ref,
                     qb_ref, kn_ref, vb_ref, kpe_ref, *, q_scale):
    cs = cs_ref[...]
    hq = _rms_rows(ql_ref[...].astype(F32), gq_ref[...]).astype(BF16)
    qf = jnp.dot(hq, wq_ref[...], preferred_element_type=F32)
    for h in range(B_HEADS):
        base = h * B_QK
        nope = qf[:, base:base + B_NOPE]
        pe = _rope_pair_sum(qf[:, base + B_NOPE:base + B_QK], cs)
        qb_ref[:, base:base + B_NOPE] = (nope * q_scale).astype(BF16)
        qb_ref[:, base + B_NOPE:base + B_QK] = (pe * q_scale).astype(BF16)

    hkv = _rms_rows(kvl_ref[...].astype(F32), gkv_ref[...]).astype(BF16)
    kvf = jnp.dot(hkv, wkv_ref[...], preferred_element_type=F32)
    kn_ref[...] = kvf[:, :B_HEADS * B_NOPE].astype(BF16)
    vb_ref[...] = kvf[:, B_HEADS * B_NOPE:].astype(BF16)

    kpe = _rope_pair_sum(rope_ref[...].astype(F32), cs)
    lane = lax.broadcasted_iota(jnp.int32, kpe.shape, 1)
    kpe_ref[...] = jnp.where(lane < B_ROPE, kpe, 0.0).astype(BF16)


def _mla_prep(proj, cs, gq, gkv, wq, wkv, *, tm, name):
    m = proj.shape[0]
    q_scale = (B_NOPE + B_ROPE) ** -0.5 * LOG2E
    est = (2 * tm * (Q_LORA + KV_LORA + 128) * 2 + 2 * tm * 128 * 4
           + 2 * (Q_LORA + KV_LORA) * 2048 * 2
           + 2 * tm * (2048 + 1024 + 1024 + 128) * 2 + 3 * tm * 2048 * 4 + (4 << 20))
    return pl.pallas_call(
        functools.partial(_mla_prep_kernel, q_scale=q_scale),
        out_shape=(
            jax.ShapeDtypeStruct((m, B_HEADS * B_QK), BF16),
            jax.ShapeDtypeStruct((m, B_HEADS * B_NOPE), BF16),
            jax.ShapeDtypeStruct((m, B_WIDTH), BF16),
            jax.ShapeDtypeStruct((m, V7X_LANES), BF16),
        ),
        grid=(m // tm,),
        in_specs=[
            pl.BlockSpec((tm, Q_LORA), lambda i: (i, IN_QLAT_OFF // Q_LORA)),
            pl.BlockSpec((tm, V7X_LANES), lambda i: (i, IN_ROPE_OFF // V7X_LANES)),
            pl.BlockSpec((tm, KV_LORA), lambda i: (i, IN_KVLAT_OFF // KV_LORA)),
            pl.BlockSpec((tm, V7X_LANES), lambda i: (i, 0)),
            pl.BlockSpec((1, Q_LORA), lambda i: (0, 0)),
            pl.BlockSpec((1, KV_LORA), lambda i: (0, 0)),
            pl.BlockSpec((Q_LORA, B_HEADS * B_QK), lambda i: (0, 0)),
            pl.BlockSpec((KV_LORA, 2 * B_WIDTH), lambda i: (0, 0)),
        ],
        out_specs=(
            pl.BlockSpec((tm, B_HEADS * B_QK), lambda i: (i, 0)),
            pl.BlockSpec((tm, B_HEADS * B_NOPE), lambda i: (i, 0)),
            pl.BlockSpec((tm, B_WIDTH), lambda i: (i, 0)),
            pl.BlockSpec((tm, V7X_LANES), lambda i: (i, 0)),
        ),
        compiler_params=_cparams(("arbitrary",), est),
        name=name,
    )(proj, proj, proj, cs, gq, gkv, wq, wkv)


def _attn_a_kernel(q_ref, k0_ref, k1_ref, k2_ref, v0_ref, v1_ref, v2_ref, bias_ref, o_ref):
    i = pl.program_id(1)
    col = lax.broadcasted_iota(jnp.int32, (1, A_WIN), 1)
    pad_mask = jnp.where(i - (A_KBLKS - 1) + col // A_TQ < 0, -jnp.inf, 0.0)
    scale = A_HEAD_DIM ** -0.5
    for h in range(A_HEADS):
        hs = slice(h * A_HEAD_DIM, (h + 1) * A_HEAD_DIM)
        q = q_ref[:, hs]
        k = jnp.concatenate([k0_ref[:, hs], k1_ref[:, hs], k2_ref[:, hs]], axis=0)
        v = jnp.concatenate([v0_ref[:, hs], v1_ref[:, hs], v2_ref[:, hs]], axis=0)
        s = lax.dot_general(q, k, (((1,), (1,)), ((), ())), preferred_element_type=F32)
        s = s * scale + bias_ref[h] + pad_mask
        m = jnp.max(s, axis=1, keepdims=True)
        p = jnp.exp(s - m)
        l = jnp.sum(p, axis=1, keepdims=True)
        o = jnp.dot(p.astype(BF16), v, preferred_element_type=F32)
        o_ref[:, hs] = (o / l).astype(o_ref.dtype)


def _attn_a(proj, bias, *, batch, seq, name):
    m = proj.shape[0]
    nq = seq // A_TQ
    qspec = pl.BlockSpec((A_TQ, A_WIDTH), lambda b, i: (b * nq + i, 0))

    def kv_spec(col_blk, c):
        back = A_KBLKS - 1 - c
        return pl.BlockSpec((A_TQ, A_WIDTH),
                            lambda b, i: (b * nq + jnp.maximum(i - back, 0), col_blk))

    est = 2 * 7 * A_TQ * A_WIDTH * 2 + 2 * A_HEADS * A_TQ * A_WIN * 4 + 2 * A_TQ * A_WIDTH * 2 + (8 << 20)
    return pl.pallas_call(
        _attn_a_kernel,
        out_shape=jax.ShapeDtypeStruct((m, A_WIDTH), BF16),
        grid=(batch, nq),
        in_specs=[qspec,
                  kv_spec(1, 0), kv_spec(1, 1), kv_spec(1, 2),
                  kv_spec(2, 0), kv_spec(2, 1), kv_spec(2, 2),
                  pl.BlockSpec((A_HEADS, A_TQ, A_WIN), lambda b, i: (0, 0, 0))],
        out_specs=pl.BlockSpec((A_TQ, A_WIDTH), lambda b, i: (b * nq + i, 0)),
        compiler_params=_cparams(("arbitrary", "arbitrary"), est),
        name=name,
    )(proj, proj, proj, proj, proj, proj, proj, bias)


def _attn_b_kernel(q_ref, kn_ref, kpe_ref, v_ref, o_ref, m_ref, l_ref, acc_ref):
    i = pl.program_id(2)
    hp = B_HEADS_PER_STEP
    m_ref[...] = jnp.full(m_ref.shape, -jnp.inf, F32)
    l_ref[...] = jnp.zeros(l_ref.shape, F32)
    acc_ref[...] = jnp.zeros(acc_ref.shape, F32)

    def step(j, masked):
        r0 = pl.multiple_of(j * B_TK, B_TK)
        kpe = kpe_ref[pl.ds(r0, B_TK), :]
        if masked:
            row = lax.broadcasted_iota(jnp.int32, (B_TQ, B_TK), 0)
            colk = lax.broadcasted_iota(jnp.int32, (B_TQ, B_TK), 1)
            allowed = (colk // CHUNK) <= (row // CHUNK)
        for h in range(hp):
            q = q_ref[:, h * B_QK:(h + 1) * B_QK]
            k = jnp.concatenate([kn_ref[pl.ds(r0, B_TK), h * B_NOPE:(h + 1) * B_NOPE], kpe], axis=1)
            s = lax.dot_general(q, k, (((1,), (1,)), ((), ())), preferred_element_type=F32)
            if masked:
                s = jnp.where(allowed, s, -jnp.inf)
            m_prev = m_ref[h]
            m_new = jnp.maximum(m_prev, jnp.max(s, axis=1, keepdims=True))
            alpha = jnp.exp2(m_prev - m_new)
            p = jnp.exp2(s - jnp.concatenate([m_new] * (B_TK // V7X_LANES), axis=1))
            l_ref[h] = alpha * l_ref[h] + jnp.sum(p, axis=1, keepdims=True)
            pv = jnp.dot(p.astype(BF16), v_ref[pl.ds(r0, B_TK), h * B_V:(h + 1) * B_V],
                         preferred_element_type=F32)
            acc_ref[h] = alpha * acc_ref[h] + pv
            m_ref[h] = m_new

    def body(j, carry):
        step(j, False)
        return carry

    lax.fori_loop(0, i, body, 0)
    step(i, True)

    for h in range(hp):
        o_ref[:, h * B_V:(h + 1) * B_V] = (acc_ref[h] / l_ref[h]).astype(o_ref.dtype)


def _attn_b(qb, kn, kpe, vb, *, batch, seq, name):
    m = qb.shape[0]
    nq = seq // B_TQ
    hp = B_HEADS_PER_STEP
    est = (2 * B_TQ * hp * B_QK * 2 + 2 * seq * hp * (B_NOPE + B_V) * 2 + 2 * seq * 128 * 2
           + 2 * B_TQ * hp * B_V * 2 + 3 * hp * B_TQ * 128 * 4 + 6 * B_TQ * B_TK * 4 + (4 << 20))
    return pl.pallas_call(
        _attn_b_kernel,
        out_shape=jax.ShapeDtypeStruct((m, B_WIDTH), BF16),
        grid=(batch, B_HEADS // hp, nq),
        in_specs=[
            pl.BlockSpec((B_TQ, hp * B_QK), lambda b, g, i: (b * nq + i, g)),
            pl.BlockSpec((seq, hp * B_NOPE), lambda b, g, i: (b, g)),
            pl.BlockSpec((seq, V7X_LANES), lambda b, g, i: (b, 0)),
            pl.BlockSpec((seq, hp * B_V), lambda b, g, i: (b, g)),
        ],
        out_specs=pl.BlockSpec((B_TQ, hp * B_V), lambda b, g, i: (b * nq + i, g)),
        scratch_shapes=[
            pltpu.VMEM((hp, B_TQ, V7X_LANES), F32),
            pltpu.VMEM((hp, B_TQ, V7X_LANES), F32),
            pltpu.VMEM((hp, B_TQ, B_V), F32),
        ],
        compiler_params=_cparams(("arbitrary", "arbitrary", "arbitrary"), est),
        name=name,
    )(qb, kn, kpe, vb)


def _cross_kernel(x_ref, g_ref, wq_ref, k_ref, v_ref, wo_ref, o_ref, h_ref):
    _norm_into(h_ref, x_ref, g_ref)
    scale = X_HEAD_DIM ** -0.5
    q = (jnp.dot(h_ref[...], wq_ref[...], preferred_element_type=F32) * scale).astype(BF16)
    outs = []
    for h in range(X_HEADS):
        hs = slice(h * X_HEAD_DIM, (h + 1) * X_HEAD_DIM)
        s = lax.dot_general(q[:, hs], k_ref[:, hs], (((1,), (1,)), ((), ())),
                            preferred_element_type=F32)
        m = jnp.max(s, axis=1, keepdims=True)
        p = jnp.exp(s - m)
        l = jnp.sum(p, axis=1, keepdims=True)
        o = jnp.dot(p.astype(BF16), v_ref[:, hs], preferred_element_type=F32)
        outs.append((o / l).astype(BF16))
    o_all = jnp.concatenate(outs, axis=1)
    o_ref[...] = x_ref[...] + jnp.dot(o_all, wo_ref[...], preferred_element_type=F32)


def _cross(x, g, wq, kvm, wo, layer, *, seq, tm, name):
    m, d = x.shape
    blocks_per_batch = seq // tm
    est = (4 * tm * d * 4 + tm * d * 2 + 2 * d * X_WIDTH * 2 + 2 * X_WIDTH * d * 2
           + 4 * N_MEM * X_WIDTH * 2 + 4 * tm * N_MEM * 4 + 2 * tm * d * 4 + (4 << 20))
    return pl.pallas_call(
        _cross_kernel,
        out_shape=jax.ShapeDtypeStruct((m, d), F32),
        grid=(m // tm,),
        in_specs=[
            pl.BlockSpec((tm, d), lambda i: (i, 0)),
            pl.BlockSpec((1, d), lambda i: (0, 0)),
            pl.BlockSpec((d, X_WIDTH), lambda i: (0, 0)),
            pl.BlockSpec((N_MEM, X_WIDTH), lambda i: (i // blocks_per_batch, 2 * layer)),
            pl.BlockSpec((N_MEM, X_WIDTH), lambda i: (i // blocks_per_batch, 2 * layer + 1)),
            pl.BlockSpec((X_WIDTH, d), lambda i: (0, 0)),
        ],
        out_specs=pl.BlockSpec((tm, d), lambda i: (i, 0)),
        scratch_shapes=[pltpu.VMEM((tm, d), BF16)],
        compiler_params=_cparams(("arbitrary",), est),
        name=name,
    )(x, g, wq, kvm, kvm, wo)


def _final_norm_kernel(x_ref, g_ref, o_ref):
    o_ref[...] = _rms_rows(x_ref[...], g_ref[...])


def _final_norm(x, g, *, tm):
    m, d = x.shape
    return pl.pallas_call(
        _final_norm_kernel,
        out_shape=jax.ShapeDtypeStruct((m, d), F32),
        grid=(m // tm,),
        in_specs=[pl.BlockSpec((tm, d), lambda i: (i, 0)), pl.BlockSpec((1, d), lambda i: (0, 0))],
        out_specs=pl.BlockSpec((tm, d), lambda i: (i, 0)),
        compiler_params=_cparams(("arbitrary",), 6 * tm * d * 4 + (4 << 20)),
        name="final_norm",
    )(x, g)


def _prep_weights(w_in, w_uq, w_ukv, w_out, w_xkv):
    swap = (jnp.arange(B_ROPE) + B_ROPE // 2) % B_ROPE
    a3 = 3 * A_WIDTH
    qlat = w_in[:, :, a3:a3 + Q_LORA]
    kvlat = w_in[:, :, a3 + Q_LORA:a3 + Q_LORA + KV_LORA]
    krope = w_in[:, :, a3 + Q_LORA + KV_LORA:]
    pad = jnp.zeros((DEPTH, D_MODEL, IN_KVLAT_OFF - IN_ROPE_OFF - 2 * B_ROPE), w_in.dtype)
    w_in_p = jnp.concatenate([w_in[:, :, :a3], qlat, krope, krope[:, :, swap], pad, kvlat],
                             axis=-1).astype(BF16)

    uq = w_uq.reshape(DEPTH, Q_LORA, B_HEADS, B_NOPE + B_ROPE)
    uq_pe = uq[..., B_NOPE:]
    wq_p = jnp.concatenate([uq[..., :B_NOPE], uq_pe, uq_pe[..., swap]], axis=-1)
    wq_p = wq_p.reshape(DEPTH, Q_LORA, B_HEADS * B_QK).astype(BF16)

    ukv = w_ukv.reshape(DEPTH, KV_LORA, B_HEADS, B_NOPE + B_V)
    wkv_p = jnp.concatenate([ukv[..., :B_NOPE].reshape(DEPTH, KV_LORA, B_HEADS * B_NOPE),
                             ukv[..., B_NOPE:].reshape(DEPTH, KV_LORA, B_WIDTH)],
                            axis=-1).astype(BF16)

    wo_a = w_out[:, :A_WIDTH].astype(BF16)
    wo_b = w_out[:, A_WIDTH:].astype(BF16)
    wxkv_all = jnp.transpose(w_xkv, (1, 0, 2)).reshape(D_MODEL, DEPTH * 2 * X_WIDTH).astype(BF16)
    return w_in_p, wq_p, wkv_p, wo_a, wo_b, wxkv_all


def _relpos_bias_tables(rel_bias):
    r = jnp.arange(A_TQ)[:, None]
    t = jnp.arange(A_WIN)[None, :]
    rel = jnp.clip(LEFT_CHUNKS * CHUNK + r - t, -REL_CLIP, REL_CLIP) + REL_CLIP
    dchunk = t // CHUNK - r // CHUNK
    band = (dchunk >= 0) & (dchunk <= LEFT_CHUNKS)
    return jnp.where(band[None, None], rel_bias[:, :, rel], -jnp.inf).astype(F32)


def kernel(x, mem, positions, norm_mix, w_in, rel_bias, q_norm, kv_norm, w_uq, w_ukv, w_out,
           norm_mem, mem_norm, w_xq, w_xkv, w_xo, norm_ffn, w_gate, w_up, w_down, norm_final):
    b, s, d = x.shape
    m = b * s
    assert (b, s, d) == (4, 4096, D_MODEL) and mem.shape == (b, N_MEM, d)

    w_in_p, wq_p, wkv_p, wo_a, wo_b, wxkv_all = _prep_weights(w_in, w_uq, w_ukv, w_out, w_xkv)
    w_xq_b = w_xq.astype(BF16)
    w_xo_b = w_xo.astype(BF16)
    w_gate_b = w_gate.astype(BF16)
    w_up_b = w_up.astype(BF16)
    w_down_b = w_down.astype(BF16)
    bias_tab = _relpos_bias_tables(rel_bias)

    half = B_ROPE // 2
    inv = ROPE_THETA ** (-jnp.arange(half, dtype=F32) / half)
    inv_row = jnp.tile(inv, 4)[None, :]
    sign_row = jnp.concatenate([jnp.ones((B_ROPE,), F32), -jnp.ones((half,), F32),
                                jnp.ones((half,), F32)])[None, :]
    cs = _rope_table(positions.reshape(m, 1), inv_row, sign_row, tm=2048)

    kvm = _norm_matmul(mem.reshape(b * N_MEM, d), mem_norm[None, :], wxkv_all,
                       tm=b * N_MEM, tn=1024, name="mem_kv")

    xf = x.reshape(m, d)
    for l in range(DEPTH):
        proj = _norm_matmul(xf, norm_mix[l][None, :], w_in_p[l], tm=1024, tn=512, name=f"in_proj_{l}")
        oa = _attn_a(proj, bias_tab[l], batch=b, seq=s, name=f"attn_a_{l}")
        qb, kn, vb, kpe = _mla_prep(proj, cs, q_norm[l][None, :], kv_norm[l][None, :],
                                    wq_p[l], wkv_p[l], tm=512, name=f"mla_prep_{l}")
        ob = _attn_b(qb, kn, kpe, vb, batch=b, seq=s, name=f"attn_b_{l}")
        xf = _mm_res([oa, ob], [wo_a[l], wo_b[l]], xf, tm=1024, tn=1024, name=f"out_proj_{l}")
        xf = _cross(xf, norm_mem[l][None, :], w_xq_b[l], kvm, w_xo_b[l], l, seq=s, tm=512,
                    name=f"cross_{l}")
        act = _swiglu_gu(xf, norm_ffn[l][None, :], w_gate_b[l], w_up_b[l], tm=1024, tn=512,
                         name=f"swiglu_gu_{l}")
        xf = _mm_res([act], [w_down_b[l]], xf, tm=1024, tn=512, name=f"swiglu_down_{l}")
    out = _final_norm(xf, norm_final[None, :], tm=512)
    return out.reshape(b, s, d)
```

```python
import functools
import math

import jax
import jax.numpy as jnp
from jax import lax
from jax.experimental import pallas as pl
from jax.experimental.pallas import tpu as pltpu

D_MODEL = 2048
DEPTH = 4
CHUNK = 64
LEFT_CHUNKS = 8
A_HEAD_DIM = 128
A_HEADS = 8
A_WIDTH = A_HEADS * A_HEAD_DIM
REL_CLIP = 128
B_NOPE = 128
B_ROPE = 64
B_V = 128
B_HEADS = 8
B_WIDTH = B_HEADS * B_V
Q_LORA = 768
KV_LORA = 512
ROPE_THETA = 10000.0
N_MEM = 256
X_HEADS = 4
X_HEAD_DIM = 128
X_WIDTH = X_HEADS * X_HEAD_DIM
D_FF = 5632
EPS = 1e-6

V7X_LANES = 128
V7X_VMEM_BYTES = 64 * 1024 * 1024
V7X_VMEM_REQUEST_CAP = 56 * 1024 * 1024

BF16 = jnp.bfloat16
F32 = jnp.float32

IN_QLAT_OFF = 3 * A_WIDTH
IN_ROPE_OFF = IN_QLAT_OFF + Q_LORA
IN_KVLAT_OFF = IN_ROPE_OFF + 2 * V7X_LANES
IN_PAD_WIDTH = IN_KVLAT_OFF + KV_LORA

A_TQ = 4 * CHUNK
A_WIN = A_TQ + LEFT_CHUNKS * CHUNK
A_KBLKS = A_WIN // A_TQ
A_BIAS_VEC = 1024

B_TQ = 512
B_TK = 512
B_HEADS_PER_STEP = 2
B_QK = B_NOPE + 2 * B_ROPE

LOG2E = math.log2(math.e)


def _cparams(sem, vmem_bytes):
    limit = int(min(max(vmem_bytes, 32 * 1024 * 1024), V7X_VMEM_REQUEST_CAP))
    return pltpu.CompilerParams(dimension_semantics=sem, vmem_limit_bytes=limit)


def _rms_rows(x, g):
    ms = jnp.mean(x * x, axis=-1, keepdims=True)
    return (x * lax.rsqrt(ms + EPS)) * g


def _norm_into(h_ref, x_ref, g_ref, rows=128):
    n = x_ref.shape[0] // rows

    def body(c, carry):
        r0 = pl.multiple_of(c * rows, rows)
        x = x_ref[pl.ds(r0, rows), :].astype(F32)
        h_ref[pl.ds(r0, rows), :] = _rms_rows(x, g_ref[...]).astype(h_ref.dtype)
        return carry

    lax.fori_loop(0, n, body, 0)


def _norm_matmul_kernel(x_ref, g_ref, w_ref, o_ref, h_ref):
    @pl.when(pl.program_id(1) == 0)
    def _():
        _norm_into(h_ref, x_ref, g_ref)

    o_ref[...] = jnp.dot(h_ref[...], w_ref[...], preferred_element_type=F32).astype(o_ref.dtype)


def _norm_matmul(x, g, w, layer, *, tm, tn, name):
    m, k = x.shape
    n = w.shape[2]
    xb = x.dtype.itemsize
    est = 2 * tm * k * xb + tm * k * 2 + 2 * k * tn * 2 + 2 * tm * tn * 2 + tm * tn * 4 + (4 << 20)
    return pl.pallas_call(
        _norm_matmul_kernel,
        out_shape=jax.ShapeDtypeStruct((m, n), BF16),
        grid=(m // tm, n // tn),
        in_specs=[
            pl.BlockSpec((tm, k), lambda i, j: (i, 0)),
            pl.BlockSpec((None, 1, k), lambda i, j: (layer, 0, 0)),
            pl.BlockSpec((None, k, tn), lambda i, j: (layer, 0, j)),
        ],
        out_specs=pl.BlockSpec((tm, tn), lambda i, j: (i, j)),
        scratch_shapes=[pltpu.VMEM((tm, k), BF16)],
        compiler_params=_cparams(("arbitrary", "arbitrary"), est),
        name=name,
    )(x, g, w)


def _mm_res_kernel(*refs, n_in):
    a_refs = refs[:n_in]
    w_refs = refs[n_in:2 * n_in]
    r_ref = refs[2 * n_in]
    o_ref = refs[2 * n_in + 1]
    acc = r_ref[...]
    for a_ref, w_ref in zip(a_refs, w_refs):
        acc = acc + jnp.dot(a_ref[...], w_ref[...], preferred_element_type=F32)
    o_ref[...] = acc


def _mm_res(a_list, w, res, layer, *, tm, tn, name):
    m, n = res.shape
    n_in = len(a_list)
    in_specs = []
    w_specs = []
    est = (4 + 2 * n_in) * tm * tn * 4 + (4 << 20)
    for idx, a in enumerate(a_list):
        k = a.shape[1]
        assert w.shape[1] == n_in * k
        in_specs.append(pl.BlockSpec((tm, k), lambda i, j: (i, 0)))
        w_specs.append(pl.BlockSpec((None, k, tn), lambda i, j, idx=idx: (layer, idx, j)))
        est += 2 * tm * k * 2 + 2 * k * tn * 2
    in_specs += w_specs
    in_specs.append(pl.BlockSpec((tm, tn), lambda i, j: (i, j)))
    return pl.pallas_call(
        functools.partial(_mm_res_kernel, n_in=n_in),
        out_shape=jax.ShapeDtypeStruct((m, n), F32),
        grid=(m // tm, n // tn),
        in_specs=in_specs,
        out_specs=pl.BlockSpec((tm, tn), lambda i, j: (i, j)),
        compiler_params=_cparams(("arbitrary", "arbitrary"), est),
        name=name,
    )(*a_list, *([w] * n_in), res)


def _swiglu_gu_kernel(x_ref, g_ref, wg_ref, wu_ref, o_ref, h_ref):
    @pl.when(pl.program_id(1) == 0)
    def _():
        _norm_into(h_ref, x_ref, g_ref)

    h = h_ref[...]
    gate = jnp.dot(h, wg_ref[...], preferred_element_type=F32)
    up = jnp.dot(h, wu_ref[...], preferred_element_type=F32)
    o_ref[...] = (gate * (1.0 / (1.0 + jnp.exp(-gate))) * up).astype(o_ref.dtype)


def _swiglu_gu(x, g, wg, wu, layer, *, tm, tn, name):
    m, k = x.shape
    n = wg.shape[2]
    est = 2 * tm * k * 4 + tm * k * 2 + 4 * k * tn * 2 + 2 * tm * tn * 2 + 3 * tm * tn * 4 + (4 << 20)
    return pl.pallas_call(
        _swiglu_gu_kernel,
        out_shape=jax.ShapeDtypeStruct((m, n), BF16),
        grid=(m // tm, n // tn),
        in_specs=[
            pl.BlockSpec((tm, k), lambda i, j: (i, 0)),
            pl.BlockSpec((None, 1, k), lambda i, j: (layer, 0, 0)),
            pl.BlockSpec((None, k, tn), lambda i, j: (layer, 0, j)),
            pl.BlockSpec((None, k, tn), lambda i, j: (layer, 0, j)),
        ],
        out_specs=pl.BlockSpec((tm, tn), lambda i, j: (i, j)),
        scratch_shapes=[pltpu.VMEM((tm, k), BF16)],
        compiler_params=_cparams(("arbitrary", "arbitrary"), est),
        name=name,
    )(x, g, wg, wu)


def _rope_table_kernel(pos_ref, inv_ref, sign_ref, cs_ref):
    ang = pos_ref[...].astype(F32) * inv_ref[...]
    lane = lax.broadcasted_iota(jnp.int32, ang.shape, 1)
    cs_ref[...] = jnp.where(lane < B_ROPE, jnp.cos(ang), jnp.sin(ang) * sign_ref[...])


def _rope_table(pos_col, inv_row, sign_row, *, tm):
    m = pos_col.shape[0]
    return pl.pallas_call(
        _rope_table_kernel,
        out_shape=jax.ShapeDtypeStruct((m, V7X_LANES), F32),
        grid=(m // tm,),
        in_specs=[
            pl.BlockSpec((tm, 1), lambda i: (i, 0)),
            pl.BlockSpec((1, V7X_LANES), lambda i: (0, 0)),
            pl.BlockSpec((1, V7X_LANES), lambda i: (0, 0)),
        ],
        out_specs=pl.BlockSpec((tm, V7X_LANES), lambda i: (i, 0)),
        compiler_params=_cparams(("arbitrary",), 32 << 20),
        name="rope_table",
    )(pos_col, inv_row, sign_row)


def _rope_pair_sum(slot, cs):
    t = slot * cs
    return t + pltpu.roll(t, B_ROPE, axis=1)


def _mla_prep_kernel(ql_ref, rope_ref, kvl_ref, cs_ref, gq_ref, gkv_ref, wq_ref, wkv_ref,
                     qb_ref, kn_ref, vb_ref, kpe_ref, *, q_scale):
    cs = cs_ref[...]
    hq = _rms_rows(ql_ref[...].astype(F32), gq_ref[...]).astype(BF16)
    qf = jnp.dot(hq, wq_ref[...], preferred_element_type=F32)
    for h in range(B_HEADS):
        base = h * B_QK
        nope = qf[:, base:base + B_NOPE]
        pe = _rope_pair_sum(qf[:, base + B_NOPE:base + B_QK], cs)
        qb_ref[:, base:base + B_NOPE] = (nope * q_scale).astype(BF16)
        qb_ref[:, base + B_NOPE:base + B_QK] = (pe * q_scale).astype(BF16)

    hkv = _rms_rows(kvl_ref[...].astype(F32), gkv_ref[...]).astype(BF16)
    kvf = jnp.dot(hkv, wkv_ref[...], preferred_element_type=F32)
    kn_ref[...] = kvf[:, :B_HEADS * B_NOPE].astype(BF16)
    vb_ref[...] = kvf[:, B_HEADS * B_NOPE:].astype(BF16)

    kpe = _rope_pair_sum(rope_ref[...].astype(F32), cs)
    lane = lax.broadcasted_iota(jnp.int32, kpe.shape, 1)
    kpe_ref[...] = jnp.where(lane < B_ROPE, kpe, 0.0).astype(BF16)


def _mla_prep(proj, cs, gq, gkv, wq, wkv, layer, *, tm, name):
    m = proj.shape[0]
    q_scale = (B_NOPE + B_ROPE) ** -0.5 * LOG2E
    est = (2 * tm * (Q_LORA + KV_LORA + 128) * 2 + 2 * tm * 128 * 4
           + 2 * (Q_LORA + KV_LORA) * 2048 * 2
           + 2 * tm * (2048 + 1024 + 1024 + 128) * 2 + 3 * tm * 2048 * 4 + (4 << 20))
    return pl.pallas_call(
        functools.partial(_mla_prep_kernel, q_scale=q_scale),
        out_shape=(
            jax.ShapeDtypeStruct((m, B_HEADS * B_QK), BF16),
            jax.ShapeDtypeStruct((m, B_HEADS * B_NOPE), BF16),
            jax.ShapeDtypeStruct((m, B_WIDTH), BF16),
            jax.ShapeDtypeStruct((m, V7X_LANES), BF16),
        ),
        grid=(m // tm,),
        in_specs=[
            pl.BlockSpec((tm, Q_LORA), lambda i: (i, IN_QLAT_OFF // Q_LORA)),
            pl.BlockSpec((tm, V7X_LANES), lambda i: (i, IN_ROPE_OFF // V7X_LANES)),
            pl.BlockSpec((tm, KV_LORA), lambda i: (i, IN_KVLAT_OFF // KV_LORA)),
            pl.BlockSpec((tm, V7X_LANES), lambda i: (i, 0)),
            pl.BlockSpec((None, 1, Q_LORA), lambda i: (layer, 0, 0)),
            pl.BlockSpec((None, 1, KV_LORA), lambda i: (layer, 0, 0)),
            pl.BlockSpec((None, Q_LORA, B_HEADS * B_QK), lambda i: (layer, 0, 0)),
            pl.BlockSpec((None, KV_LORA, 2 * B_WIDTH), lambda i: (layer, 0, 0)),
        ],
        out_specs=(
            pl.BlockSpec((tm, B_HEADS * B_QK), lambda i: (i, 0)),
            pl.BlockSpec((tm, B_HEADS * B_NOPE), lambda i: (i, 0)),
            pl.BlockSpec((tm, B_WIDTH), lambda i: (i, 0)),
            pl.BlockSpec((tm, V7X_LANES), lambda i: (i, 0)),
        ),
        compiler_params=_cparams(("arbitrary",), est),
        name=name,
    )(proj, proj, proj, cs, gq, gkv, wq, wkv)


def _attn_a_kernel(q_ref, k0_ref, k1_ref, k2_ref, v0_ref, v1_ref, v2_ref, bias_ref, o_ref):
    i = pl.program_id(1)
    col = lax.broadcasted_iota(jnp.int32, (1, A_WIN), 1)
    pad_mask = jnp.where(i - (A_KBLKS - 1) + col // A_TQ < 0, -jnp.inf, 0.0)
    scale = A_HEAD_DIM ** -0.5
    for h in range(A_HEADS):
        hs = slice(h * A_HEAD_DIM, (h + 1) * A_HEAD_DIM)
        q = q_ref[:, hs]
        k = jnp.concatenate([k0_ref[:, hs], k1_ref[:, hs], k2_ref[:, hs]], axis=0)
        v = jnp.concatenate([v0_ref[:, hs], v1_ref[:, hs], v2_ref[:, hs]], axis=0)
        s = lax.dot_general(q, k, (((1,), (1,)), ((), ())), preferred_element_type=F32)
        s = s * scale + bias_ref[h] + pad_mask
        m = jnp.max(s, axis=1, keepdims=True)
        p = jnp.exp(s - m)
        l = jnp.sum(p, axis=1, keepdims=True)
        o = jnp.dot(p.astype(BF16), v, preferred_element_type=F32)
        o_ref[:, hs] = (o / l).astype(o_ref.dtype)


def _attn_a(proj, bias, layer, *, batch, seq, name):
    m = proj.shape[0]
    nq = seq // A_TQ
    qspec = pl.BlockSpec((A_TQ, A_WIDTH), lambda b, i: (b * nq + i, 0))

    def kv_spec(col_blk, c):
        back = A_KBLKS - 1 - c
        return pl.BlockSpec((A_TQ, A_WIDTH),
                            lambda b, i: (b * nq + jnp.maximum(i - back, 0), col_blk))

    est = 2 * 7 * A_TQ * A_WIDTH * 2 + 2 * A_HEADS * A_TQ * A_WIN * 4 + 2 * A_TQ * A_WIDTH * 2 + (8 << 20)
    return pl.pallas_call(
        _attn_a_kernel,
        out_shape=jax.ShapeDtypeStruct((m, A_WIDTH), BF16),
        grid=(batch, nq),
        in_specs=[qspec,
                  kv_spec(1, 0), kv_spec(1, 1), kv_spec(1, 2),
                  kv_spec(2, 0), kv_spec(2, 1), kv_spec(2, 2),
                  pl.BlockSpec((A_HEADS, A_TQ, A_WIN), lambda b, i: (layer, 0, 0))],
        out_specs=pl.BlockSpec((A_TQ, A_WIDTH), lambda b, i: (b * nq + i, 0)),
        compiler_params=_cparams(("arbitrary", "arbitrary"), est),
        name=name,
    )(proj, proj, proj, proj, proj, proj, proj, bias)


def _attn_b_kernel(q_ref, kn_ref, kpe_ref, v_ref, o_ref, m_ref, l_ref, acc_ref):
    i = pl.program_id(2)
    hp = B_HEADS_PER_STEP
    m_ref[...] = jnp.full(m_ref.shape, -jnp.inf, F32)
    l_ref[...] = jnp.zeros(l_ref.shape, F32)
    acc_ref[...] = jnp.zeros(acc_ref.shape, F32)

    def step(j, masked):
        r0 = pl.multiple_of(j * B_TK, B_TK)
        kpe = kpe_ref[pl.ds(r0, B_TK), :]
        if masked:
            row = lax.broadcasted_iota(jnp.int32, (B_TQ, B_TK), 0)
            colk = lax.broadcasted_iota(jnp.int32, (B_TQ, B_TK), 1)
            allowed = (colk // CHUNK) <= (row // CHUNK)
        for h in range(hp):
            q = q_ref[:, h * B_QK:(h + 1) * B_QK]
            k = jnp.concatenate([kn_ref[pl.ds(r0, B_TK), h * B_NOPE:(h + 1) * B_NOPE], kpe], axis=1)
            s = lax.dot_general(q, k, (((1,), (1,)), ((), ())), preferred_element_type=F32)
            if masked:
                s = jnp.where(allowed, s, -jnp.inf)
            m_prev = m_ref[h]
            m_new = jnp.maximum(m_prev, jnp.max(s, axis=1, keepdims=True))
            alpha = jnp.exp2(m_prev - m_new)
            p = jnp.exp2(s - jnp.concatenate([m_new] * (B_TK // V7X_LANES), axis=1))
            l_ref[h] = alpha * l_ref[h] + jnp.sum(p, axis=1, keepdims=True)
            pv = jnp.dot(p.astype(BF16), v_ref[pl.ds(r0, B_TK), h * B_V:(h + 1) * B_V],
                         preferred_element_type=F32)
            acc_ref[h] = alpha * acc_ref[h] + pv
            m_ref[h] = m_new

    def body(j, carry):
        step(j, False)
        return carry

    lax.fori_loop(0, i, body, 0)
    step(i, True)

    for h in range(hp):
        o_ref[:, h * B_V:(h + 1) * B_V] = (acc_ref[h] / l_ref[h]).astype(o_ref.dtype)


def _attn_b(qb, kn, kpe, vb, *, batch, seq, name):
    m = qb.shape[0]
    nq = seq // B_TQ
    hp = B_HEADS_PER_STEP
    est = (2 * B_TQ * hp * B_QK * 2 + 2 * seq * hp * (B_NOPE + B_V) * 2 + 2 * seq * 128 * 2
           + 2 * B_TQ * hp * B_V * 2 + 3 * hp * B_TQ * 128 * 4 + 6 * B_TQ * B_TK * 4 + (4 << 20))
    return pl.pallas_call(
        _attn_b_kernel,
        out_shape=jax.ShapeDtypeStruct((m, B_WIDTH), BF16),
        grid=(batch, B_HEADS // hp, nq),
        in_specs=[
            pl.BlockSpec((B_TQ, hp * B_QK), lambda b, g, i: (b * nq + i, g)),
            pl.BlockSpec((seq, hp * B_NOPE), lambda b, g, i: (b, g)),
            pl.BlockSpec((seq, V7X_LANES), lambda b, g, i: (b, 0)),
            pl.BlockSpec((seq, hp * B_V), lambda b, g, i: (b, g)),
        ],
        out_specs=pl.BlockSpec((B_TQ, hp * B_V), lambda b, g, i: (b * nq + i, g)),
        scratch_shapes=[
            pltpu.VMEM((hp, B_TQ, V7X_LANES), F32),
            pltpu.VMEM((hp, B_TQ, V7X_LANES), F32),
            pltpu.VMEM((hp, B_TQ, B_V), F32),
        ],
        compiler_params=_cparams(("arbitrary", "arbitrary", "arbitrary"), est),
        name=name,
    )(qb, kn, kpe, vb)


def _cross_kernel(x_ref, g_ref, wq_ref, k_ref, v_ref, wo_ref, o_ref, h_ref):
    _norm_into(h_ref, x_ref, g_ref)
    scale = X_HEAD_DIM ** -0.5
    q = (jnp.dot(h_ref[...], wq_ref[...], preferred_element_type=F32) * scale).astype(BF16)
    outs = []
    for h in range(X_HEADS):
        hs = slice(h * X_HEAD_DIM, (h + 1) * X_HEAD_DIM)
        s = lax.dot_general(q[:, hs], k_ref[:, hs], (((1,), (1,)), ((), ())),
                            preferred_element_type=F32)
        m = jnp.max(s, axis=1, keepdims=True)
        p = jnp.exp(s - m)
        l = jnp.sum(p, axis=1, keepdims=True)
        o = jnp.dot(p.astype(BF16), v_ref[:, hs], preferred_element_type=F32)
        outs.append((o / l).astype(BF16))
    o_all = jnp.concatenate(outs, axis=1)
    o_ref[...] = x_ref[...] + jnp.dot(o_all, wo_ref[...], preferred_element_type=F32)


def _cross(x, g, wq, kvm, wo, layer, *, seq, tm, name):
    m, d = x.shape
    blocks_per_batch = seq // tm
    est = (4 * tm * d * 4 + tm * d * 2 + 2 * d * X_WIDTH * 2 + 2 * X_WIDTH * d * 2
           + 4 * N_MEM * X_WIDTH * 2 + 4 * tm * N_MEM * 4 + 2 * tm * d * 4 + (4 << 20))
    return pl.pallas_call(
        _cross_kernel,
        out_shape=jax.ShapeDtypeStruct((m, d), F32),
        grid=(m // tm,),
        in_specs=[
            pl.BlockSpec((tm, d), lambda i: (i, 0)),
            pl.BlockSpec((None, 1, d), lambda i: (layer, 0, 0)),
            pl.BlockSpec((None, d, X_WIDTH), lambda i: (layer, 0, 0)),
            pl.BlockSpec((N_MEM, X_WIDTH), lambda i: (i // blocks_per_batch, 2 * layer)),
            pl.BlockSpec((N_MEM, X_WIDTH), lambda i: (i // blocks_per_batch, 2 * layer + 1)),
            pl.BlockSpec((None, X_WIDTH, d), lambda i: (layer, 0, 0)),
        ],
        out_specs=pl.BlockSpec((tm, d), lambda i: (i, 0)),
        scratch_shapes=[pltpu.VMEM((tm, d), BF16)],
        compiler_params=_cparams(("arbitrary",), est),
        name=name,
    )(x, g, wq, kvm, kvm, wo)


def _final_norm_kernel(x_ref, g_ref, o_ref):
    o_ref[...] = _rms_rows(x_ref[...], g_ref[...])


def _final_norm(x, g, *, tm):
    m, d = x.shape
    return pl.pallas_call(
        _final_norm_kernel,
        out_shape=jax.ShapeDtypeStruct((m, d), F32),
        grid=(m // tm,),
        in_specs=[pl.BlockSpec((tm, d), lambda i: (i, 0)), pl.BlockSpec((1, d), lambda i: (0, 0))],
        out_specs=pl.BlockSpec((tm, d), lambda i: (i, 0)),
        compiler_params=_cparams(("arbitrary",), 6 * tm * d * 4 + (4 << 20)),
        name="final_norm",
    )(x, g)


def _prep_weights(w_in, w_uq, w_ukv, w_xkv):
    swap = (jnp.arange(B_ROPE) + B_ROPE // 2) % B_ROPE
    lat_end = IN_ROPE_OFF + KV_LORA
    krope = w_in[:, :, lat_end:].astype(BF16)
    pad = jnp.zeros((DEPTH, D_MODEL, IN_KVLAT_OFF - IN_ROPE_OFF - 2 * B_ROPE), BF16)
    w_in_p = jnp.concatenate([w_in[:, :, :IN_ROPE_OFF].astype(BF16), krope, krope[:, :, swap], pad,
                              w_in[:, :, IN_ROPE_OFF:lat_end].astype(BF16)], axis=-1)

    uq = w_uq.reshape(DEPTH, Q_LORA, B_HEADS, B_NOPE + B_ROPE)
    uq_pe = uq[..., B_NOPE:]
    wq_p = jnp.concatenate([uq[..., :B_NOPE], uq_pe, uq_pe[..., swap]], axis=-1)
    wq_p = wq_p.reshape(DEPTH, Q_LORA, B_HEADS * B_QK).astype(BF16)

    ukv = w_ukv.reshape(DEPTH, KV_LORA, B_HEADS, B_NOPE + B_V)
    wkv_p = jnp.concatenate([ukv[..., :B_NOPE].reshape(DEPTH, KV_LORA, B_HEADS * B_NOPE),
                             ukv[..., B_NOPE:].reshape(DEPTH, KV_LORA, B_WIDTH)],
                            axis=-1).astype(BF16)

    wxkv_all = jnp.transpose(w_xkv.astype(BF16), (1, 0, 2)).reshape(1, D_MODEL, DEPTH * 2 * X_WIDTH)
    return w_in_p, wq_p, wkv_p, wxkv_all


def _bias_table_kernel(w_ref, o_ref):
    w = jnp.broadcast_to(w_ref[0], (A_TQ, A_BIAS_VEC))
    tab = pltpu.roll(w, 0, axis=1, stride=1, stride_axis=0)[:, :A_WIN]
    r = lax.broadcasted_iota(jnp.int32, (A_TQ, A_WIN), 0)
    t = lax.broadcasted_iota(jnp.int32, (A_TQ, A_WIN), 1)
    dchunk = t // CHUNK - r // CHUNK
    band = (dchunk >= 0) & (dchunk <= LEFT_CHUNKS)
    o_ref[0] = jnp.where(band, tab, -jnp.inf)


def _relpos_bias_tables(rel_bias):
    u = jnp.arange(A_BIAS_VEC)
    delta = jnp.where(u <= A_WIN, u, u - A_BIAS_VEC)
    idx = jnp.clip(LEFT_CHUNKS * CHUNK - delta, -REL_CLIP, REL_CLIP) + REL_CLIP
    vec = rel_bias[:, :, idx].reshape(DEPTH * A_HEADS, 1, A_BIAS_VEC).astype(F32)
    return pl.pallas_call(
        _bias_table_kernel,
        out_shape=jax.ShapeDtypeStruct((DEPTH * A_HEADS, A_TQ, A_WIN), F32),
        grid=(DEPTH * A_HEADS,),
        in_specs=[pl.BlockSpec((1, 1, A_BIAS_VEC), lambda i: (i, 0, 0))],
        out_specs=pl.BlockSpec((1, A_TQ, A_WIN), lambda i: (i, 0, 0)),
        compiler_params=_cparams(("arbitrary",), 32 << 20),
        name="bias_table",
    )(vec)


def kernel(x, mem, positions, norm_mix, w_in, rel_bias, q_norm, kv_norm, w_uq, w_ukv, w_out,
           norm_mem, mem_norm, w_xq, w_xkv, w_xo, norm_ffn, w_gate, w_up, w_down, norm_final):
    b, s, d = x.shape
    m = b * s
    assert (b, s, d) == (4, 4096, D_MODEL) and mem.shape == (b, N_MEM, d)

    w_in_p, wq_p, wkv_p, wxkv_all = _prep_weights(w_in, w_uq, w_ukv, w_xkv)
    w_out_b = w_out.astype(BF16)
    w_xq_b = w_xq.astype(BF16)
    w_xo_b = w_xo.astype(BF16)
    w_gate_b = w_gate.astype(BF16)
    w_up_b = w_up.astype(BF16)
    w_down_b = w_down.astype(BF16)
    bias_tab = _relpos_bias_tables(rel_bias)

    half = B_ROPE // 2
    inv = ROPE_THETA ** (-jnp.arange(half, dtype=F32) / half)
    inv_row = jnp.tile(inv, 4)[None, :]
    sign_row = jnp.concatenate([jnp.ones((B_ROPE,), F32), -jnp.ones((half,), F32),
                                jnp.ones((half,), F32)])[None, :]
    cs = _rope_table(positions.reshape(m, 1), inv_row, sign_row, tm=2048)

    kvm = _norm_matmul(mem.reshape(b * N_MEM, d), mem_norm[None, None, :], wxkv_all, 0,
                       tm=b * N_MEM, tn=1024, name="mem_kv")

    norm_mix, norm_mem, norm_ffn, q_norm, kv_norm = (
        g[:, None, :] for g in (norm_mix, norm_mem, norm_ffn, q_norm, kv_norm))
    xf = x.reshape(m, d)
    for l in range(DEPTH):
        proj = _norm_matmul(xf, norm_mix, w_in_p, l, tm=1024, tn=512, name=f"in_proj_{l}")
        oa = _attn_a(proj, bias_tab, l, batch=b, seq=s, name=f"attn_a_{l}")
        qb, kn, vb, kpe = _mla_prep(proj, cs, q_norm, kv_norm, wq_p, wkv_p, l, tm=512,
                                    name=f"mla_prep_{l}")
        ob = _attn_b(qb, kn, kpe, vb, batch=b, seq=s, name=f"attn_b_{l}")
        xf = _mm_res([oa, ob], w_out_b, xf, l, tm=1024, tn=1024, name=f"out_proj_{l}")
        xf = _cross(xf, norm_mem, w_xq_b, kvm, w_xo_b, l, seq=s, tm=512, name=f"cross_{l}")
        act = _swiglu_gu(xf, norm_ffn, w_gate_b, w_up_b, l, tm=1024, tn=512, name=f"swiglu_gu_{l}")
        xf = _mm_res([act], w_down_b, xf, l, tm=1024, tn=512, name=f"swiglu_down_{l}")
    out = _final_norm(xf, norm_final[None, :], tm=512)
    return out.reshape(b, s, d)
```

```python
import functools
import math

import jax
import jax.numpy as jnp
from jax import lax
from jax.experimental import pallas as pl
from jax.experimental.pallas import tpu as pltpu

D_MODEL = 2048
DEPTH = 4
CHUNK = 64
LEFT_CHUNKS = 8
A_HEAD_DIM = 128
A_HEADS = 8
A_WIDTH = A_HEADS * A_HEAD_DIM
REL_CLIP = 128
B_NOPE = 128
B_ROPE = 64
B_V = 128
B_HEADS = 8
B_WIDTH = B_HEADS * B_V
Q_LORA = 768
KV_LORA = 512
ROPE_THETA = 10000.0
N_MEM = 256
X_HEADS = 4
X_HEAD_DIM = 128
X_WIDTH = X_HEADS * X_HEAD_DIM
D_FF = 5632
EPS = 1e-6

V7X_LANES = 128
V7X_VMEM_BYTES = 64 * 1024 * 1024
V7X_VMEM_REQUEST_CAP = 56 * 1024 * 1024

BF16 = jnp.bfloat16
F32 = jnp.float32

IN_QLAT_OFF = 3 * A_WIDTH
IN_ROPE_OFF = IN_QLAT_OFF + Q_LORA
IN_KVLAT_OFF = IN_ROPE_OFF + 2 * V7X_LANES
IN_PAD_WIDTH = IN_KVLAT_OFF + KV_LORA

A_TQ = 4 * CHUNK
A_WIN = A_TQ + LEFT_CHUNKS * CHUNK
A_KBLKS = A_WIN // A_TQ
A_BIAS_VEC = 1024

B_TQ = 512
B_TK = 512
B_HEADS_PER_STEP = 4
B_RED_SLABS = 8
B_QK = B_NOPE + 2 * B_ROPE

LOG2E = math.log2(math.e)


def _cparams(sem, vmem_bytes):
    limit = int(min(max(vmem_bytes, 32 * 1024 * 1024), V7X_VMEM_REQUEST_CAP))
    return pltpu.CompilerParams(dimension_semantics=sem, vmem_limit_bytes=limit)


def _rms_rows(x, g):
    ms = jnp.mean(x * x, axis=-1, keepdims=True)
    return (x * lax.rsqrt(ms + EPS)) * g


def _norm_into(h_ref, x_ref, g_ref, rows=128):
    n = x_ref.shape[0] // rows

    def body(c, carry):
        r0 = pl.multiple_of(c * rows, rows)
        x = x_ref[pl.ds(r0, rows), :].astype(F32)
        h_ref[pl.ds(r0, rows), :] = _rms_rows(x, g_ref[...]).astype(h_ref.dtype)
        return carry

    lax.fori_loop(0, n, body, 0)


def _norm_matmul_kernel(x_ref, g_ref, w_ref, o_ref, h_ref):
    @pl.when(pl.program_id(1) == 0)
    def _():
        _norm_into(h_ref, x_ref, g_ref)

    o_ref[...] = jnp.dot(h_ref[...], w_ref[...], preferred_element_type=F32).astype(o_ref.dtype)


def _norm_matmul(x, g, w, layer, *, tm, tn, name):
    m, k = x.shape
    n = w.shape[2]
    xb = x.dtype.itemsize
    est = 2 * tm * k * xb + tm * k * 2 + 2 * k * tn * 2 + 2 * tm * tn * 2 + tm * tn * 4 + (4 << 20)
    return pl.pallas_call(
        _norm_matmul_kernel,
        out_shape=jax.ShapeDtypeStruct((m, n), BF16),
        grid=(m // tm, n // tn),
        in_specs=[
            pl.BlockSpec((tm, k), lambda i, j: (i, 0)),
            pl.BlockSpec((None, 1, k), lambda i, j: (layer, 0, 0)),
            pl.BlockSpec((None, k, tn), lambda i, j: (layer, 0, j)),
        ],
        out_specs=pl.BlockSpec((tm, tn), lambda i, j: (i, j)),
        scratch_shapes=[pltpu.VMEM((tm, k), BF16)],
        compiler_params=_cparams(("arbitrary", "arbitrary"), est),
        name=name,
    )(x, g, w)


def _mm_res_kernel(*refs, n_in):
    a_refs = refs[:n_in]
    w_refs = refs[n_in:2 * n_in]
    r_ref = refs[2 * n_in]
    o_ref = refs[2 * n_in + 1]
    acc = r_ref[...]
    for a_ref, w_ref in zip(a_refs, w_refs):
        acc = acc + jnp.dot(a_ref[...], w_ref[...], preferred_element_type=F32)
    o_ref[...] = acc


def _mm_res(a_list, w, res, layer, *, tm, tn, name):
    m, n = res.shape
    n_in = len(a_list)
    in_specs = []
    w_specs = []
    est = (4 + 2 * n_in) * tm * tn * 4 + (4 << 20)
    for idx, a in enumerate(a_list):
        k = a.shape[1]
        assert w.shape[1] == n_in * k
        in_specs.append(pl.BlockSpec((tm, k), lambda i, j: (i, 0)))
        w_specs.append(pl.BlockSpec((None, k, tn), lambda i, j, idx=idx: (layer, idx, j)))
        est += 2 * tm * k * 2 + 2 * k * tn * 2
    in_specs += w_specs
    in_specs.append(pl.BlockSpec((tm, tn), lambda i, j: (i, j)))
    return pl.pallas_call(
        functools.partial(_mm_res_kernel, n_in=n_in),
        out_shape=jax.ShapeDtypeStruct((m, n), F32),
        grid=(m // tm, n // tn),
        in_specs=in_specs,
        out_specs=pl.BlockSpec((tm, tn), lambda i, j: (i, j)),
        compiler_params=_cparams(("arbitrary", "arbitrary"), est),
        name=name,
    )(*a_list, *([w] * n_in), res)


def _swiglu_gu_kernel(x_ref, g_ref, wg_ref, wu_ref, o_ref, h_ref):
    @pl.when(pl.program_id(1) == 0)
    def _():
        _norm_into(h_ref, x_ref, g_ref)

    h = h_ref[...]
    gate = jnp.dot(h, wg_ref[...], preferred_element_type=F32)
    up = jnp.dot(h, wu_ref[...], preferred_element_type=F32)
    o_ref[...] = (gate * (1.0 / (1.0 + jnp.exp(-gate))) * up).astype(o_ref.dtype)


def _swiglu_gu(x, g, wg, wu, layer, *, tm, tn, name):
    m, k = x.shape
    n = wg.shape[2]
    est = 2 * tm * k * 4 + tm * k * 2 + 4 * k * tn * 2 + 2 * tm * tn * 2 + 3 * tm * tn * 4 + (4 << 20)
    return pl.pallas_call(
        _swiglu_gu_kernel,
        out_shape=jax.ShapeDtypeStruct((m, n), BF16),
        grid=(m // tm, n // tn),
        in_specs=[
            pl.BlockSpec((tm, k), lambda i, j: (i, 0)),
            pl.BlockSpec((None, 1, k), lambda i, j: (layer, 0, 0)),
            pl.BlockSpec((None, k, tn), lambda i, j: (layer, 0, j)),
            pl.BlockSpec((None, k, tn), lambda i, j: (layer, 0, j)),
        ],
        out_specs=pl.BlockSpec((tm, tn), lambda i, j: (i, j)),
        scratch_shapes=[pltpu.VMEM((tm, k), BF16)],
        compiler_params=_cparams(("arbitrary", "arbitrary"), est),
        name=name,
    )(x, g, wg, wu)


def _rope_table_kernel(pos_ref, inv_ref, sign_ref, cs_ref):
    ang = pos_ref[...].astype(F32) * inv_ref[...]
    lane = lax.broadcasted_iota(jnp.int32, ang.shape, 1)
    cs_ref[...] = jnp.where(lane < B_ROPE, jnp.cos(ang), jnp.sin(ang) * sign_ref[...])


def _rope_table(pos_col, inv_row, sign_row, *, tm):
    m = pos_col.shape[0]
    return pl.pallas_call(
        _rope_table_kernel,
        out_shape=jax.ShapeDtypeStruct((m, V7X_LANES), F32),
        grid=(m // tm,),
        in_specs=[
            pl.BlockSpec((tm, 1), lambda i: (i, 0)),
            pl.BlockSpec((1, V7X_LANES), lambda i: (0, 0)),
            pl.BlockSpec((1, V7X_LANES), lambda i: (0, 0)),
        ],
        out_specs=pl.BlockSpec((tm, V7X_LANES), lambda i: (i, 0)),
        compiler_params=_cparams(("arbitrary",), 32 << 20),
        name="rope_table",
    )(pos_col, inv_row, sign_row)


def _rope_pair_sum(slot, cs):
    t = slot * cs
    return t + pltpu.roll(t, B_ROPE, axis=1)


def _mla_prep_kernel(ql_ref, rope_ref, kvl_ref, cs_ref, gq_ref, gkv_ref, wq_ref, wkv_ref,
                     qb_ref, kn_ref, vt_ref, kpe_ref, *, q_scale):
    cs = cs_ref[...]
    hq = _rms_rows(ql_ref[...].astype(F32), gq_ref[...]).astype(BF16)
    qf = jnp.dot(hq, wq_ref[...], preferred_element_type=F32)
    for h in range(B_HEADS):
        base = h * B_QK
        nope = qf[:, base:base + B_NOPE]
        pe = _rope_pair_sum(qf[:, base + B_NOPE:base + B_QK], cs)
        qb_ref[:, base:base + B_NOPE] = (nope * q_scale).astype(BF16)
        qb_ref[:, base + B_NOPE:base + B_QK] = (pe * q_scale).astype(BF16)

    hkv = _rms_rows(kvl_ref[...].astype(F32), gkv_ref[...]).astype(BF16)
    kvf = jnp.dot(hkv, wkv_ref[...], preferred_element_type=F32)
    kn_ref[...] = kvf[:, :B_HEADS * B_NOPE].astype(BF16)
    vt_ref[...] = kvf[:, B_HEADS * B_NOPE:].T.astype(BF16)

    kpe = _rope_pair_sum(rope_ref[...].astype(F32), cs)
    lane = lax.broadcasted_iota(jnp.int32, kpe.shape, 1)
    kpe_ref[...] = jnp.where(lane < B_ROPE, kpe, 0.0).astype(BF16)


def _mla_prep(proj, cs, gq, gkv, wq, wkv, layer, *, seq, tm, name):
    m = proj.shape[0]
    tiles_per_seq = seq // tm
    q_scale = (B_NOPE + B_ROPE) ** -0.5 * LOG2E
    est = (2 * tm * (Q_LORA + KV_LORA + 128) * 2 + 2 * tm * 128 * 4
           + 2 * (Q_LORA + KV_LORA) * 2048 * 2
           + 2 * tm * (2048 + 1024 + 1024 + 128) * 2 + 3 * tm * 2048 * 4 + (4 << 20))
    return pl.pallas_call(
        functools.partial(_mla_prep_kernel, q_scale=q_scale),
        out_shape=(
            jax.ShapeDtypeStruct((m, B_HEADS * B_QK), BF16),
            jax.ShapeDtypeStruct((m, B_HEADS * B_NOPE), BF16),
            jax.ShapeDtypeStruct((m // seq * B_WIDTH, seq), BF16),
            jax.ShapeDtypeStruct((m, V7X_LANES), BF16),
        ),
        grid=(m // tm,),
        in_specs=[
            pl.BlockSpec((tm, Q_LORA), lambda i: (i, IN_QLAT_OFF // Q_LORA)),
            pl.BlockSpec((tm, V7X_LANES), lambda i: (i, IN_ROPE_OFF // V7X_LANES)),
            pl.BlockSpec((tm, KV_LORA), lambda i: (i, IN_KVLAT_OFF // KV_LORA)),
            pl.BlockSpec((tm, V7X_LANES), lambda i: (i, 0)),
            pl.BlockSpec((None, 1, Q_LORA), lambda i: (layer, 0, 0)),
            pl.BlockSpec((None, 1, KV_LORA), lambda i: (layer, 0, 0)),
            pl.BlockSpec((None, Q_LORA, B_HEADS * B_QK), lambda i: (layer, 0, 0)),
            pl.BlockSpec((None, KV_LORA, 2 * B_WIDTH), lambda i: (layer, 0, 0)),
        ],
        out_specs=(
            pl.BlockSpec((tm, B_HEADS * B_QK), lambda i: (i, 0)),
            pl.BlockSpec((tm, B_HEADS * B_NOPE), lambda i: (i, 0)),
            pl.BlockSpec((B_WIDTH, tm), lambda i: (i // tiles_per_seq, i % tiles_per_seq)),
            pl.BlockSpec((tm, V7X_LANES), lambda i: (i, 0)),
        ),
        compiler_params=_cparams(("arbitrary",), est),
        name=name,
    )(proj, proj, proj, cs, gq, gkv, wq, wkv)


def _attn_a_kernel(q_ref, k0_ref, k1_ref, k2_ref, v0_ref, v1_ref, v2_ref, bias_ref, o_ref):
    scale = A_HEAD_DIM ** -0.5 * LOG2E

    def scores(h):
        hs = slice(h * A_HEAD_DIM, (h + 1) * A_HEAD_DIM)
        k = jnp.concatenate([k0_ref[:, hs], k1_ref[:, hs], k2_ref[:, hs]], axis=0)
        return lax.dot_general(k, q_ref[:, hs], (((1,), (1,)), ((), ())),
                               preferred_element_type=F32)

    st_next = scores(0)
    for h in range(A_HEADS):
        hs = slice(h * A_HEAD_DIM, (h + 1) * A_HEAD_DIM)
        st = st_next
        if h + 1 < A_HEADS:
            st_next = scores(h + 1)
        v = jnp.concatenate([v0_ref[:, hs], v1_ref[:, hs], v2_ref[:, hs]], axis=0)
        st = st * scale + bias_ref[h]
        m = _col_reduce(st, jnp.max)
        p = jnp.exp2(st - m)
        l = _col_reduce(p, jnp.sum)
        ot = lax.dot_general(v, p.astype(BF16), (((0,), (0,)), ((), ())),
                             preferred_element_type=F32)
        o_ref[:, hs] = (ot / l).T.astype(o_ref.dtype)


def _attn_a(proj, bias, layer, *, batch, seq, name):
    m = proj.shape[0]
    nq = seq // A_TQ
    qspec = pl.BlockSpec((A_TQ, A_WIDTH), lambda b, i: (b * nq + i, 0))

    def kv_spec(col_blk, c):
        back = A_KBLKS - 1 - c
        return pl.BlockSpec((A_TQ, A_WIDTH),
                            lambda b, i: (b * nq + jnp.maximum(i - back, 0), col_blk))

    est = 2 * 7 * A_TQ * A_WIDTH * 2 + 2 * A_HEADS * A_TQ * A_WIN * 4 + 2 * A_TQ * A_WIDTH * 2 + (8 << 20)
    return pl.pallas_call(
        _attn_a_kernel,
        out_shape=jax.ShapeDtypeStruct((m, A_WIDTH), BF16),
        grid=(batch, nq),
        in_specs=[qspec,
                  kv_spec(1, 0), kv_spec(1, 1), kv_spec(1, 2),
                  kv_spec(2, 0), kv_spec(2, 1), kv_spec(2, 2),
                  pl.BlockSpec((A_HEADS, None, A_WIN, A_TQ),
                               lambda b, i: (layer, jnp.minimum(i, A_KBLKS - 1), 0, 0))],
        out_specs=pl.BlockSpec((A_TQ, A_WIDTH), lambda b, i: (b * nq + i, 0)),
        compiler_params=_cparams(("arbitrary", "arbitrary"), est),
        name=name,
    )(proj, proj, proj, proj, proj, proj, proj, bias)


def _col_reduce(x, op):
    rows, cols = x.shape
    slabs = op(x.reshape(B_RED_SLABS, rows // B_RED_SLABS, cols), axis=0)
    return op(slabs, axis=0, keepdims=True)


def _attn_b_kernel(q_ref, kn_ref, kpe_ref, vt_ref, o_ref, m_ref, l_ref, acc_ref, st_ref):
    i = pl.program_id(2)
    hp = B_HEADS_PER_STEP
    m_ref[...] = jnp.full(m_ref.shape, -jnp.inf, F32)
    l_ref[...] = jnp.zeros(l_ref.shape, F32)
    acc_ref[...] = jnp.zeros(acc_ref.shape, F32)

    def scores(j, h):
        r0 = pl.multiple_of(j * B_TK, B_TK)
        k = jnp.concatenate([kn_ref[pl.ds(r0, B_TK), h * B_NOPE:(h + 1) * B_NOPE],
                             kpe_ref[pl.ds(r0, B_TK), :]], axis=1)
        q = q_ref[:, h * B_QK:(h + 1) * B_QK]
        return lax.dot_general(k, q, (((1,), (1,)), ((), ())), preferred_element_type=F32)

    def softmax_pv(j, h, st, allowed):
        r0 = pl.multiple_of(j * B_TK, B_TK)
        if allowed is not None:
            st = jnp.where(allowed, st, -jnp.inf)
        m_prev = m_ref[h]
        m_new = jnp.maximum(m_prev, _col_reduce(st, jnp.max))
        alpha = jnp.exp2(m_prev - m_new)
        p = jnp.exp2(st - m_new)
        l_ref[h] = alpha * l_ref[h] + _col_reduce(p, jnp.sum)
        pv = jnp.dot(vt_ref[h * B_V:(h + 1) * B_V, pl.ds(r0, B_TK)], p.astype(BF16),
                     preferred_element_type=F32)
        acc_ref[h] = alpha * acc_ref[h] + pv
        m_ref[h] = m_new

    def key_block(j, masked):
        allowed = None
        if masked:
            krow = lax.broadcasted_iota(jnp.int32, (B_TK, B_TQ), 0)
            qcol = lax.broadcasted_iota(jnp.int32, (B_TK, B_TQ), 1)
            allowed = (krow // CHUNK) <= (qcol // CHUNK)
        st = st_ref[...]
        for h in range(hp):
            if h + 1 < hp:
                st_next = scores(j, h + 1)
            elif not masked:
                st_ref[...] = scores(j + 1, 0)
            softmax_pv(j, h, st, allowed)
            if h + 1 < hp:
                st = st_next

    def body(j, carry):
        key_block(j, False)
        return carry

    st_ref[...] = scores(0, 0)
    lax.fori_loop(0, i, body, 0)
    key_block(i, True)

    for h in range(hp):
        o_ref[:, h * B_V:(h + 1) * B_V] = (acc_ref[h] / l_ref[h]).T.astype(o_ref.dtype)


def _attn_b(qb, kn, kpe, vt, *, batch, seq, name):
    m = qb.shape[0]
    nq = seq // B_TQ
    hp = B_HEADS_PER_STEP
    est = (2 * B_TQ * hp * B_QK * 2 + 2 * seq * hp * (B_NOPE + B_V) * 2 + 2 * seq * 128 * 2
           + 2 * B_TQ * hp * B_V * 2 + hp * (B_V + 16) * B_TQ * 4 + 8 * B_TQ * B_TK * 4 + (4 << 20))
    return pl.pallas_call(
        _attn_b_kernel,
        out_shape=jax.ShapeDtypeStruct((m, B_WIDTH), BF16),
        grid=(batch, B_HEADS // hp, nq),
        in_specs=[
            pl.BlockSpec((B_TQ, hp * B_QK), lambda b, g, i: (b * nq + i, g)),
            pl.BlockSpec((seq, hp * B_NOPE), lambda b, g, i: (b, g)),
            pl.BlockSpec((seq, V7X_LANES), lambda b, g, i: (b, 0)),
            pl.BlockSpec((hp * B_V, seq), lambda b, g, i: (b * (B_HEADS // hp) + g, 0)),
        ],
        out_specs=pl.BlockSpec((B_TQ, hp * B_V), lambda b, g, i: (b * nq + i, g)),
        scratch_shapes=[
            pltpu.VMEM((hp, 1, B_TQ), F32),
            pltpu.VMEM((hp, 1, B_TQ), F32),
            pltpu.VMEM((hp, B_V, B_TQ), F32),
            pltpu.VMEM((B_TK, B_TQ), F32),
        ],
        compiler_params=_cparams(("arbitrary", "arbitrary", "arbitrary"), est),
        name=name,
    )(qb, kn, kpe, vt)


def _cross_kernel(x_ref, g_ref, wq_ref, k_ref, v_ref, wo_ref, o_ref, h_ref):
    _norm_into(h_ref, x_ref, g_ref)
    scale = X_HEAD_DIM ** -0.5
    q = (jnp.dot(h_ref[...], wq_ref[...], preferred_element_type=F32) * scale).astype(BF16)
    outs = []
    for h in range(X_HEADS):
        hs = slice(h * X_HEAD_DIM, (h + 1) * X_HEAD_DIM)
        s = lax.dot_general(q[:, hs], k_ref[:, hs], (((1,), (1,)), ((), ())),
                            preferred_element_type=F32)
        m = jnp.max(s, axis=1, keepdims=True)
        p = jnp.exp(s - m)
        l = jnp.sum(p, axis=1, keepdims=True)
        o = jnp.dot(p.astype(BF16), v_ref[:, hs], preferred_element_type=F32)
        outs.append((o / l).astype(BF16))
    o_all = jnp.concatenate(outs, axis=1)
    o_ref[...] = x_ref[...] + jnp.dot(o_all, wo_ref[...], preferred_element_type=F32)


def _cross(x, g, wq, kvm, wo, layer, *, seq, tm, name):
    m, d = x.shape
    blocks_per_batch = seq // tm
    est = (4 * tm * d * 4 + tm * d * 2 + 2 * d * X_WIDTH * 2 + 2 * X_WIDTH * d * 2
           + 4 * N_MEM * X_WIDTH * 2 + 4 * tm * N_MEM * 4 + 2 * tm * d * 4 + (4 << 20))
    return pl.pallas_call(
        _cross_kernel,
        out_shape=jax.ShapeDtypeStruct((m, d), F32),
        grid=(m // tm,),
        in_specs=[
            pl.BlockSpec((tm, d), lambda i: (i, 0)),
            pl.BlockSpec((None, 1, d), lambda i: (layer, 0, 0)),
            pl.BlockSpec((None, d, X_WIDTH), lambda i: (layer, 0, 0)),
            pl.BlockSpec((N_MEM, X_WIDTH), lambda i: (i // blocks_per_batch, 2 * layer)),
            pl.BlockSpec((N_MEM, X_WIDTH), lambda i: (i // blocks_per_batch, 2 * layer + 1)),
            pl.BlockSpec((None, X_WIDTH, d), lambda i: (layer, 0, 0)),
        ],
        out_specs=pl.BlockSpec((tm, d), lambda i: (i, 0)),
        scratch_shapes=[pltpu.VMEM((tm, d), BF16)],
        compiler_params=_cparams(("arbitrary",), est),
        name=name,
    )(x, g, wq, kvm, kvm, wo)


def _final_norm_kernel(x_ref, g_ref, o_ref):
    o_ref[...] = _rms_rows(x_ref[...], g_ref[...])


def _final_norm(x, g, *, tm):
    m, d = x.shape
    return pl.pallas_call(
        _final_norm_kernel,
        out_shape=jax.ShapeDtypeStruct((m, d), F32),
        grid=(m // tm,),
        in_specs=[pl.BlockSpec((tm, d), lambda i: (i, 0)), pl.BlockSpec((1, d), lambda i: (0, 0))],
        out_specs=pl.BlockSpec((tm, d), lambda i: (i, 0)),
        compiler_params=_cparams(("arbitrary",), 6 * tm * d * 4 + (4 << 20)),
        name="final_norm",
    )(x, g)


def _prep_weights(w_in, w_uq, w_ukv, w_xkv):
    swap = (jnp.arange(B_ROPE) + B_ROPE // 2) % B_ROPE
    lat_end = IN_ROPE_OFF + KV_LORA
    krope = w_in[:, :, lat_end:].astype(BF16)
    pad = jnp.zeros((DEPTH, D_MODEL, IN_KVLAT_OFF - IN_ROPE_OFF - 2 * B_ROPE), BF16)
    w_in_p = jnp.concatenate([w_in[:, :, :IN_ROPE_OFF].astype(BF16), krope, krope[:, :, swap], pad,
                              w_in[:, :, IN_ROPE_OFF:lat_end].astype(BF16)], axis=-1)

    uq = w_uq.reshape(DEPTH, Q_LORA, B_HEADS, B_NOPE + B_ROPE)
    uq_pe = uq[..., B_NOPE:]
    wq_p = jnp.concatenate([uq[..., :B_NOPE], uq_pe, uq_pe[..., swap]], axis=-1)
    wq_p = wq_p.reshape(DEPTH, Q_LORA, B_HEADS * B_QK).astype(BF16)

    ukv = w_ukv.reshape(DEPTH, KV_LORA, B_HEADS, B_NOPE + B_V)
    wkv_p = jnp.concatenate([ukv[..., :B_NOPE].reshape(DEPTH, KV_LORA, B_HEADS * B_NOPE),
                             ukv[..., B_NOPE:].reshape(DEPTH, KV_LORA, B_WIDTH)],
                            axis=-1).astype(BF16)

    wxkv_all = jnp.transpose(w_xkv.astype(BF16), (1, 0, 2)).reshape(1, D_MODEL, DEPTH * 2 * X_WIDTH)
    return w_in_p, wq_p, wkv_p, wxkv_all


def _bias_table_kernel(w_ref, o_ref):
    v = pl.program_id(1)
    w = jnp.broadcast_to(w_ref[0], (A_TQ, A_BIAS_VEC))
    tab = pltpu.roll(w, 0, axis=1, stride=1, stride_axis=0)[:, :A_WIN]
    r = lax.broadcasted_iota(jnp.int32, (A_TQ, A_WIN), 0)
    t = lax.broadcasted_iota(jnp.int32, (A_TQ, A_WIN), 1)
    dchunk = t // CHUNK - r // CHUNK
    valid = (dchunk >= 0) & (dchunk <= LEFT_CHUNKS) & (v - (A_KBLKS - 1) + t // A_TQ >= 0)
    o_ref[0, 0] = jnp.where(valid, tab * LOG2E, -jnp.inf).T


def _relpos_bias_tables(rel_bias):
    u = jnp.arange(A_BIAS_VEC)
    delta = jnp.where(u <= A_WIN, u, u - A_BIAS_VEC)
    idx = jnp.clip(LEFT_CHUNKS * CHUNK - delta, -REL_CLIP, REL_CLIP) + REL_CLIP
    vec = rel_bias[:, :, idx].reshape(DEPTH * A_HEADS, 1, A_BIAS_VEC).astype(F32)
    return pl.pallas_call(
        _bias_table_kernel,
        out_shape=jax.ShapeDtypeStruct((DEPTH * A_HEADS, A_KBLKS, A_WIN, A_TQ), F32),
        grid=(DEPTH * A_HEADS, A_KBLKS),
        in_specs=[pl.BlockSpec((1, 1, A_BIAS_VEC), lambda i, v: (i, 0, 0))],
        out_specs=pl.BlockSpec((1, 1, A_WIN, A_TQ), lambda i, v: (i, v, 0, 0)),
        compiler_params=_cparams(("arbitrary", "arbitrary"), 32 << 20),
        name="bias_table",
    )(vec)


def kernel(x, mem, positions, norm_mix, w_in, rel_bias, q_norm, kv_norm, w_uq, w_ukv, w_out,
           norm_mem, mem_norm, w_xq, w_xkv, w_xo, norm_ffn, w_gate, w_up, w_down, norm_final):
    b, s, d = x.shape
    m = b * s
    assert (b, s, d) == (4, 4096, D_MODEL) and mem.shape == (b, N_MEM, d)

    w_in_p, wq_p, wkv_p, wxkv_all = _prep_weights(w_in, w_uq, w_ukv, w_xkv)
    w_out_b = w_out.astype(BF16)
    w_xq_b = w_xq.astype(BF16)
    w_xo_b = w_xo.astype(BF16)
    w_gate_b = w_gate.astype(BF16)
    w_up_b = w_up.astype(BF16)
    w_down_b = w_down.astype(BF16)
    bias_tab = _relpos_bias_tables(rel_bias)

    half = B_ROPE // 2
    inv = ROPE_THETA ** (-jnp.arange(half, dtype=F32) / half)
    inv_row = jnp.tile(inv, 4)[None, :]
    sign_row = jnp.concatenate([jnp.ones((B_ROPE,), F32), -jnp.ones((half,), F32),
                                jnp.ones((half,), F32)])[None, :]
    cs = _rope_table(positions.reshape(m, 1), inv_row, sign_row, tm=2048)

    kvm = _norm_matmul(mem.reshape(b * N_MEM, d), mem_norm[None, None, :], wxkv_all, 0,
                       tm=b * N_MEM, tn=1024, name="mem_kv")

    norm_mix, norm_mem, norm_ffn, q_norm, kv_norm = (
        g[:, None, :] for g in (norm_mix, norm_mem, norm_ffn, q_norm, kv_norm))
    xf = x.reshape(m, d)
    for l in range(DEPTH):
        proj = _norm_matmul(xf, norm_mix, w_in_p, l, tm=1024, tn=512, name=f"in_proj_{l}")
        oa = _attn_a(proj, bias_tab, l, batch=b, seq=s, name=f"attn_a_{l}")
        qb, kn, vt, kpe = _mla_prep(proj, cs, q_norm, kv_norm, wq_p, wkv_p, l, seq=s, tm=512,
                                    name=f"mla_prep_{l}")
        ob = _attn_b(qb, kn, kpe, vt, batch=b, seq=s, name=f"attn_b_{l}")
        xf = _mm_res([oa, ob], w_out_b, xf, l, tm=1024, tn=1024, name=f"out_proj_{l}")
        xf = _cross(xf, norm_mem, w_xq_b, kvm, w_xo_b, l, seq=s, tm=512, name=f"cross_{l}")
        act = _swiglu_gu(xf, norm_ffn, w_gate_b, w_up_b, l, tm=1024, tn=512, name=f"swiglu_gu_{l}")
        xf = _mm_res([act], w_down_b, xf, l, tm=1024, tn=512, name=f"swiglu_down_{l}")
    out = _final_norm(xf, norm_final[None, :], tm=512)
    return out.reshape(b, s, d)
```

```python
import functools
import math

import jax
import jax.numpy as jnp
from jax import lax
from jax.experimental import pallas as pl
from jax.experimental.pallas import tpu as pltpu

D_MODEL = 2048
DEPTH = 4
CHUNK = 64
LEFT_CHUNKS = 8
A_HEAD_DIM = 128
A_HEADS = 8
A_WIDTH = A_HEADS * A_HEAD_DIM
REL_CLIP = 128
B_NOPE = 128
B_ROPE = 64
B_V = 128
B_HEADS = 8
B_WIDTH = B_HEADS * B_V
Q_LORA = 768
KV_LORA = 512
ROPE_THETA = 10000.0
N_MEM = 256
X_HEADS = 4
X_HEAD_DIM = 128
X_WIDTH = X_HEADS * X_HEAD_DIM
D_FF = 5632
EPS = 1e-6

V7X_LANES = 128
V7X_VMEM_BYTES = 64 * 1024 * 1024
V7X_VMEM_REQUEST_CAP = 56 * 1024 * 1024

BF16 = jnp.bfloat16
F32 = jnp.float32

IN_QLAT_OFF = 3 * A_WIDTH
IN_ROPE_OFF = IN_QLAT_OFF + Q_LORA
IN_KVLAT_OFF = IN_ROPE_OFF + 2 * V7X_LANES
IN_PAD_WIDTH = IN_KVLAT_OFF + KV_LORA

A_TQ = 4 * CHUNK
A_WIN = A_TQ + LEFT_CHUNKS * CHUNK
A_KBLKS = A_WIN // A_TQ
A_BIAS_VEC = 1024

B_TQ = 512
B_TK = 512
B_HEADS_PER_STEP = 4
B_RED_SLABS = 8
B_QK = B_NOPE + 2 * B_ROPE

LOG2E = math.log2(math.e)


def _cparams(sem, vmem_bytes):
    limit = int(min(max(vmem_bytes, 32 * 1024 * 1024), V7X_VMEM_REQUEST_CAP))
    return pltpu.CompilerParams(dimension_semantics=sem, vmem_limit_bytes=limit)


def _rms_rows(x, g):
    ms = jnp.mean(x * x, axis=-1, keepdims=True)
    return (x * lax.rsqrt(ms + EPS)) * g


def _norm_into(h_ref, x_ref, g_ref, rows=128):
    n = x_ref.shape[0] // rows

    def body(c, carry):
        r0 = pl.multiple_of(c * rows, rows)
        x = x_ref[pl.ds(r0, rows), :].astype(F32)
        h_ref[pl.ds(r0, rows), :] = _rms_rows(x, g_ref[...]).astype(h_ref.dtype)
        return carry

    lax.fori_loop(0, n, body, 0)


def _norm_matmul_kernel(x_ref, g_ref, w_ref, o_ref, h_ref):
    @pl.when(pl.program_id(1) == 0)
    def _():
        _norm_into(h_ref, x_ref, g_ref)

    o_ref[...] = jnp.dot(h_ref[...], w_ref[...], preferred_element_type=F32).astype(o_ref.dtype)


def _norm_matmul(x, g, w, layer, *, tm, tn, name):
    m, k = x.shape
    n = w.shape[2]
    xb = x.dtype.itemsize
    est = 2 * tm * k * xb + tm * k * 2 + 2 * k * tn * 2 + 2 * tm * tn * 2 + tm * tn * 4 + (4 << 20)
    return pl.pallas_call(
        _norm_matmul_kernel,
        out_shape=jax.ShapeDtypeStruct((m, n), BF16),
        grid=(m // tm, n // tn),
        in_specs=[
            pl.BlockSpec((tm, k), lambda i, j: (i, 0)),
            pl.BlockSpec((None, 1, k), lambda i, j: (layer, 0, 0)),
            pl.BlockSpec((None, k, tn), lambda i, j: (layer, 0, j)),
        ],
        out_specs=pl.BlockSpec((tm, tn), lambda i, j: (i, j)),
        scratch_shapes=[pltpu.VMEM((tm, k), BF16)],
        compiler_params=_cparams(("arbitrary", "arbitrary"), est),
        name=name,
    )(x, g, w)


def _mm_res_kernel(*refs, n_in):
    a_refs = refs[:n_in]
    w_refs = refs[n_in:2 * n_in]
    r_ref = refs[2 * n_in]
    o_ref = refs[2 * n_in + 1]
    acc = r_ref[...]
    for a_ref, w_ref in zip(a_refs, w_refs):
        acc = acc + jnp.dot(a_ref[...], w_ref[...], preferred_element_type=F32)
    o_ref[...] = acc


def _mm_res(a_list, w, res, layer, *, tm, tn, name):
    m, n = res.shape
    n_in = len(a_list)
    in_specs = []
    w_specs = []
    est = (4 + 2 * n_in) * tm * tn * 4 + (4 << 20)
    for idx, a in enumerate(a_list):
        k = a.shape[1]
        assert w.shape[1] == n_in * k
        in_specs.append(pl.BlockSpec((tm, k), lambda i, j: (i, 0)))
        w_specs.append(pl.BlockSpec((None, k, tn), lambda i, j, idx=idx: (layer, idx, j)))
        est += 2 * tm * k * 2 + 2 * k * tn * 2
    in_specs += w_specs
    in_specs.append(pl.BlockSpec((tm, tn), lambda i, j: (i, j)))
    return pl.pallas_call(
        functools.partial(_mm_res_kernel, n_in=n_in),
        out_shape=jax.ShapeDtypeStruct((m, n), F32),
        grid=(m // tm, n // tn),
        in_specs=in_specs,
        out_specs=pl.BlockSpec((tm, tn), lambda i, j: (i, j)),
        compiler_params=_cparams(("arbitrary", "arbitrary"), est),
        name=name,
    )(*a_list, *([w] * n_in), res)


def _swiglu_gu_kernel(x_ref, g_ref, wg_ref, wu_ref, o_ref, h_ref):
    @pl.when(pl.program_id(1) == 0)
    def _():
        _norm_into(h_ref, x_ref, g_ref)

    h = h_ref[...]
    gate = jnp.dot(h, wg_ref[...], preferred_element_type=F32)
    up = jnp.dot(h, wu_ref[...], preferred_element_type=F32)
    o_ref[...] = (gate * (1.0 / (1.0 + jnp.exp(-gate))) * up).astype(o_ref.dtype)


def _swiglu_gu(x, g, wg, wu, layer, *, tm, tn, name):
    m, k = x.shape
    n = wg.shape[2]
    est = 2 * tm * k * 4 + tm * k * 2 + 4 * k * tn * 2 + 2 * tm * tn * 2 + 3 * tm * tn * 4 + (4 << 20)
    return pl.pallas_call(
        _swiglu_gu_kernel,
        out_shape=jax.ShapeDtypeStruct((m, n), BF16),
        grid=(m // tm, n // tn),
        in_specs=[
            pl.BlockSpec((tm, k), lambda i, j: (i, 0)),
            pl.BlockSpec((None, 1, k), lambda i, j: (layer, 0, 0)),
            pl.BlockSpec((None, k, tn), lambda i, j: (layer, 0, j)),
            pl.BlockSpec((None, k, tn), lambda i, j: (layer, 0, j)),
        ],
        out_specs=pl.BlockSpec((tm, tn), lambda i, j: (i, j)),
        scratch_shapes=[pltpu.VMEM((tm, k), BF16)],
        compiler_params=_cparams(("arbitrary", "arbitrary"), est),
        name=name,
    )(x, g, wg, wu)


def _rope_table_kernel(pos_ref, inv_ref, sign_ref, cs_ref):
    ang = pos_ref[...].astype(F32) * inv_ref[...]
    lane = lax.broadcasted_iota(jnp.int32, ang.shape, 1)
    cs_ref[...] = jnp.where(lane < B_ROPE, jnp.cos(ang), jnp.sin(ang) * sign_ref[...])


def _rope_table(pos_col, inv_row, sign_row, *, tm):
    m = pos_col.shape[0]
    return pl.pallas_call(
        _rope_table_kernel,
        out_shape=jax.ShapeDtypeStruct((m, V7X_LANES), F32),
        grid=(m // tm,),
        in_specs=[
            pl.BlockSpec((tm, 1), lambda i: (i, 0)),
            pl.BlockSpec((1, V7X_LANES), lambda i: (0, 0)),
            pl.BlockSpec((1, V7X_LANES), lambda i: (0, 0)),
        ],
        out_specs=pl.BlockSpec((tm, V7X_LANES), lambda i: (i, 0)),
        compiler_params=_cparams(("arbitrary",), 32 << 20),
        name="rope_table",
    )(pos_col, inv_row, sign_row)


def _rope_pair_sum(slot, cs):
    t = slot * cs
    return t + pltpu.roll(t, B_ROPE, axis=1)


def _mla_prep_kernel(ql_ref, rope_ref, kvl_ref, cs_ref, gq_ref, gkv_ref, wq_ref, wkv_ref,
                     qb_ref, kn_ref, vt_ref, kpe_ref, *, q_scale):
    cs = cs_ref[...]
    hq = _rms_rows(ql_ref[...].astype(F32), gq_ref[...]).astype(BF16)
    qf = jnp.dot(hq, wq_ref[...], preferred_element_type=F32)
    for h in range(B_HEADS):
        base = h * B_QK
        nope = qf[:, base:base + B_NOPE]
        pe = _rope_pair_sum(qf[:, base + B_NOPE:base + B_QK], cs)
        qb_ref[:, base:base + B_NOPE] = (nope * q_scale).astype(BF16)
        qb_ref[:, base + B_NOPE:base + B_QK] = (pe * q_scale).astype(BF16)

    hkv = _rms_rows(kvl_ref[...].astype(F32), gkv_ref[...]).astype(BF16)
    kvf = jnp.dot(hkv, wkv_ref[...], preferred_element_type=F32)
    kn_ref[...] = kvf[:, :B_HEADS * B_NOPE].astype(BF16)
    vt_ref[...] = kvf[:, B_HEADS * B_NOPE:].T.astype(BF16)

    kpe = _rope_pair_sum(rope_ref[...].astype(F32), cs)
    lane = lax.broadcasted_iota(jnp.int32, kpe.shape, 1)
    kpe_ref[...] = jnp.where(lane < B_ROPE, kpe, 0.0).astype(BF16)


def _mla_prep(proj, cs, gq, gkv, wq, wkv, layer, *, seq, tm, name):
    m = proj.shape[0]
    tiles_per_seq = seq // tm
    q_scale = (B_NOPE + B_ROPE) ** -0.5 * LOG2E
    est = (2 * tm * (Q_LORA + KV_LORA + 128) * 2 + 2 * tm * 128 * 4
           + 2 * (Q_LORA + KV_LORA) * 2048 * 2
           + 2 * tm * (2048 + 1024 + 1024 + 128) * 2 + 3 * tm * 2048 * 4 + (4 << 20))
    return pl.pallas_call(
        functools.partial(_mla_prep_kernel, q_scale=q_scale),
        out_shape=(
            jax.ShapeDtypeStruct((m, B_HEADS * B_QK), BF16),
            jax.ShapeDtypeStruct((m, B_HEADS * B_NOPE), BF16),
            jax.ShapeDtypeStruct((m // seq * B_WIDTH, seq), BF16),
            jax.ShapeDtypeStruct((m, V7X_LANES), BF16),
        ),
        grid=(m // tm,),
        in_specs=[
            pl.BlockSpec((tm, Q_LORA), lambda i: (i, IN_QLAT_OFF // Q_LORA)),
            pl.BlockSpec((tm, V7X_LANES), lambda i: (i, IN_ROPE_OFF // V7X_LANES)),
            pl.BlockSpec((tm, KV_LORA), lambda i: (i, IN_KVLAT_OFF // KV_LORA)),
            pl.BlockSpec((tm, V7X_LANES), lambda i: (i, 0)),
            pl.BlockSpec((None, 1, Q_LORA), lambda i: (layer, 0, 0)),
            pl.BlockSpec((None, 1, KV_LORA), lambda i: (layer, 0, 0)),
            pl.BlockSpec((None, Q_LORA, B_HEADS * B_QK), lambda i: (layer, 0, 0)),
            pl.BlockSpec((None, KV_LORA, 2 * B_WIDTH), lambda i: (layer, 0, 0)),
        ],
        out_specs=(
            pl.BlockSpec((tm, B_HEADS * B_QK), lambda i: (i, 0)),
            pl.BlockSpec((tm, B_HEADS * B_NOPE), lambda i: (i, 0)),
            pl.BlockSpec((B_WIDTH, tm), lambda i: (i // tiles_per_seq, i % tiles_per_seq)),
            pl.BlockSpec((tm, V7X_LANES), lambda i: (i, 0)),
        ),
        compiler_params=_cparams(("arbitrary",), est),
        name=name,
    )(proj, proj, proj, cs, gq, gkv, wq, wkv)


def _attn_a_kernel(q_ref, k0_ref, k1_ref, k2_ref, v0_ref, v1_ref, v2_ref, bias_ref, o_ref):
    d, tq = A_HEAD_DIM, A_TQ
    lane = lax.broadcasted_iota(jnp.int32, (tq, 2 * d), 1)

    def scores(g):
        ps = slice(2 * g * d, (2 * g + 2) * d)
        k = jnp.concatenate([k0_ref[:, ps], k1_ref[:, ps], k2_ref[:, ps]], axis=0)
        q = q_ref[:, ps]
        zero = jnp.zeros_like(q)
        qd = jnp.concatenate([jnp.where(lane < d, q, zero), jnp.where(lane >= d, q, zero)], axis=0)
        return lax.dot_general(k, qd, (((1,), (1,)), ((), ())),
                               preferred_element_type=F32)

    st_next = scores(0)
    for g in range(A_HEADS // 2):
        ps = slice(2 * g * d, (2 * g + 2) * d)
        st = st_next
        if g + 1 < A_HEADS // 2:
            st_next = scores(g + 1)
        v = jnp.concatenate([v0_ref[:, ps], v1_ref[:, ps], v2_ref[:, ps]], axis=0)
        st = st + bias_ref[g]
        m = _col_reduce(st, jnp.max)
        p = jnp.exp2(st - m)
        l = _col_reduce(p, jnp.sum)
        pb = p.astype(BF16)
        for hh in range(2):
            ot = lax.dot_general(v[:, hh * d:(hh + 1) * d], pb[:, hh * tq:(hh + 1) * tq],
                                 (((0,), (0,)), ((), ())), preferred_element_type=F32)
            ot = ot / l[:, hh * tq:(hh + 1) * tq]
            o_ref[:, (2 * g + hh) * d:(2 * g + hh + 1) * d] = ot.T.astype(o_ref.dtype)


def _attn_a(proj, bias, layer, *, batch, seq, name):
    m = proj.shape[0]
    nq = seq // A_TQ
    qspec = pl.BlockSpec((A_TQ, A_WIDTH), lambda b, i: (b * nq + i, 0))

    def kv_spec(col_blk, c):
        back = A_KBLKS - 1 - c
        return pl.BlockSpec((A_TQ, A_WIDTH),
                            lambda b, i: (b * nq + jnp.maximum(i - back, 0), col_blk))

    est = 2 * 7 * A_TQ * A_WIDTH * 2 + 2 * A_HEADS * A_TQ * A_WIN * 4 + 2 * A_TQ * A_WIDTH * 2 + (8 << 20)
    return pl.pallas_call(
        _attn_a_kernel,
        out_shape=jax.ShapeDtypeStruct((m, A_WIDTH), BF16),
        grid=(batch, nq),
        in_specs=[qspec,
                  kv_spec(1, 0), kv_spec(1, 1), kv_spec(1, 2),
                  kv_spec(2, 0), kv_spec(2, 1), kv_spec(2, 2),
                  pl.BlockSpec((A_HEADS // 2, None, A_WIN, 2 * A_TQ),
                               lambda b, i: (layer, jnp.minimum(i, A_KBLKS - 1), 0, 0))],
        out_specs=pl.BlockSpec((A_TQ, A_WIDTH), lambda b, i: (b * nq + i, 0)),
        compiler_params=_cparams(("arbitrary", "arbitrary"), est),
        name=name,
    )(proj, proj, proj, proj, proj, proj, proj, bias)


def _col_reduce(x, op):
    rows, cols = x.shape
    slabs = op(x.reshape(B_RED_SLABS, rows // B_RED_SLABS, cols), axis=0)
    return op(slabs, axis=0, keepdims=True)


def _attn_b_kernel(q_ref, kn_ref, kpe_ref, vt_ref, o_ref, m_ref, l_ref, acc_ref, st_ref):
    i = pl.program_id(2)
    hp = B_HEADS_PER_STEP
    m_ref[...] = jnp.full(m_ref.shape, -jnp.inf, F32)
    l_ref[...] = jnp.zeros(l_ref.shape, F32)
    acc_ref[...] = jnp.zeros(acc_ref.shape, F32)

    def scores(j, h):
        r0 = pl.multiple_of(j * B_TK, B_TK)
        k = jnp.concatenate([kn_ref[pl.ds(r0, B_TK), h * B_NOPE:(h + 1) * B_NOPE],
                             kpe_ref[pl.ds(r0, B_TK), :]], axis=1)
        q = q_ref[:, h * B_QK:(h + 1) * B_QK]
        return lax.dot_general(k, q, (((1,), (1,)), ((), ())), preferred_element_type=F32)

    def softmax_pv(j, h, st, allowed):
        r0 = pl.multiple_of(j * B_TK, B_TK)
        if allowed is not None:
            st = jnp.where(allowed, st, -jnp.inf)
        m_prev = m_ref[h]
        m_new = jnp.maximum(m_prev, _col_reduce(st, jnp.max))
        alpha = jnp.exp2(m_prev - m_new)
        p = jnp.exp2(st - m_new)
        l_ref[h] = alpha * l_ref[h] + _col_reduce(p, jnp.sum)
        pv = jnp.dot(vt_ref[h * B_V:(h + 1) * B_V, pl.ds(r0, B_TK)], p.astype(BF16),
                     preferred_element_type=F32)
        acc_ref[h] = alpha * acc_ref[h] + pv
        m_ref[h] = m_new

    def key_block(j, masked):
        allowed = None
        if masked:
            krow = lax.broadcasted_iota(jnp.int32, (B_TK, B_TQ), 0)
            qcol = lax.broadcasted_iota(jnp.int32, (B_TK, B_TQ), 1)
            allowed = (krow // CHUNK) <= (qcol // CHUNK)
        st = st_ref[...]
        for h in range(hp):
            if h + 1 < hp:
                st_next = scores(j, h + 1)
            elif not masked:
                st_ref[...] = scores(j + 1, 0)
            softmax_pv(j, h, st, allowed)
            if h + 1 < hp:
                st = st_next

    def body(j, carry):
        key_block(j, False)
        return carry

    st_ref[...] = scores(0, 0)
    lax.fori_loop(0, i, body, 0)
    key_block(i, True)

    for h in range(hp):
        o_ref[:, h * B_V:(h + 1) * B_V] = (acc_ref[h] / l_ref[h]).T.astype(o_ref.dtype)


def _attn_b(qb, kn, kpe, vt, *, batch, seq, name):
    m = qb.shape[0]
    nq = seq // B_TQ
    hp = B_HEADS_PER_STEP
    est = (2 * B_TQ * hp * B_QK * 2 + 2 * seq * hp * (B_NOPE + B_V) * 2 + 2 * seq * 128 * 2
           + 2 * B_TQ * hp * B_V * 2 + hp * (B_V + 16) * B_TQ * 4 + 8 * B_TQ * B_TK * 4 + (4 << 20))
    return pl.pallas_call(
        _attn_b_kernel,
        out_shape=jax.ShapeDtypeStruct((m, B_WIDTH), BF16),
        grid=(batch, B_HEADS // hp, nq),
        in_specs=[
            pl.BlockSpec((B_TQ, hp * B_QK), lambda b, g, i: (b * nq + i, g)),
            pl.BlockSpec((seq, hp * B_NOPE), lambda b, g, i: (b, g)),
            pl.BlockSpec((seq, V7X_LANES), lambda b, g, i: (b, 0)),
            pl.BlockSpec((hp * B_V, seq), lambda b, g, i: (b * (B_HEADS // hp) + g, 0)),
        ],
        out_specs=pl.BlockSpec((B_TQ, hp * B_V), lambda b, g, i: (b * nq + i, g)),
        scratch_shapes=[
            pltpu.VMEM((hp, 1, B_TQ), F32),
            pltpu.VMEM((hp, 1, B_TQ), F32),
            pltpu.VMEM((hp, B_V, B_TQ), F32),
            pltpu.VMEM((B_TK, B_TQ), F32),
        ],
        compiler_params=_cparams(("arbitrary", "arbitrary", "arbitrary"), est),
        name=name,
    )(qb, kn, kpe, vt)


def _cross_kernel(x_ref, g_ref, wq_ref, k_ref, v_ref, wo_ref, o_ref, h_ref):
    _norm_into(h_ref, x_ref, g_ref)
    scale = X_HEAD_DIM ** -0.5
    q = (jnp.dot(h_ref[...], wq_ref[...], preferred_element_type=F32) * scale).astype(BF16)
    outs = []
    for h in range(X_HEADS):
        hs = slice(h * X_HEAD_DIM, (h + 1) * X_HEAD_DIM)
        s = lax.dot_general(q[:, hs], k_ref[:, hs], (((1,), (1,)), ((), ())),
                            preferred_element_type=F32)
        m = jnp.max(s, axis=1, keepdims=True)
        p = jnp.exp(s - m)
        l = jnp.sum(p, axis=1, keepdims=True)
        o = jnp.dot(p.astype(BF16), v_ref[:, hs], preferred_element_type=F32)
        outs.append((o / l).astype(BF16))
    o_all = jnp.concatenate(outs, axis=1)
    o_ref[...] = x_ref[...] + jnp.dot(o_all, wo_ref[...], preferred_element_type=F32)


def _cross(x, g, wq, kvm, wo, layer, *, seq, tm, name):
    m, d = x.shape
    blocks_per_batch = seq // tm
    est = (4 * tm * d * 4 + tm * d * 2 + 2 * d * X_WIDTH * 2 + 2 * X_WIDTH * d * 2
           + 4 * N_MEM * X_WIDTH * 2 + 4 * tm * N_MEM * 4 + 2 * tm * d * 4 + (4 << 20))
    return pl.pallas_call(
        _cross_kernel,
        out_shape=jax.ShapeDtypeStruct((m, d), F32),
        grid=(m // tm,),
        in_specs=[
            pl.BlockSpec((tm, d), lambda i: (i, 0)),
            pl.BlockSpec((None, 1, d), lambda i: (layer, 0, 0)),
            pl.BlockSpec((None, d, X_WIDTH), lambda i: (layer, 0, 0)),
            pl.BlockSpec((N_MEM, X_WIDTH), lambda i: (i // blocks_per_batch, 2 * layer)),
            pl.BlockSpec((N_MEM, X_WIDTH), lambda i: (i // blocks_per_batch, 2 * layer + 1)),
            pl.BlockSpec((None, X_WIDTH, d), lambda i: (layer, 0, 0)),
        ],
        out_specs=pl.BlockSpec((tm, d), lambda i: (i, 0)),
        scratch_shapes=[pltpu.VMEM((tm, d), BF16)],
        compiler_params=_cparams(("arbitrary",), est),
        name=name,
    )(x, g, wq, kvm, kvm, wo)


def _final_norm_kernel(x_ref, g_ref, o_ref):
    o_ref[...] = _rms_rows(x_ref[...], g_ref[...])


def _final_norm(x, g, *, tm):
    m, d = x.shape
    return pl.pallas_call(
        _final_norm_kernel,
        out_shape=jax.ShapeDtypeStruct((m, d), F32),
        grid=(m // tm,),
        in_specs=[pl.BlockSpec((tm, d), lambda i: (i, 0)), pl.BlockSpec((1, d), lambda i: (0, 0))],
        out_specs=pl.BlockSpec((tm, d), lambda i: (i, 0)),
        compiler_params=_cparams(("arbitrary",), 6 * tm * d * 4 + (4 << 20)),
        name="final_norm",
    )(x, g)


def _prep_weights(w_in, w_uq, w_ukv, w_xkv):
    swap = (jnp.arange(B_ROPE) + B_ROPE // 2) % B_ROPE
    lat_end = IN_ROPE_OFF + KV_LORA
    krope = w_in[:, :, lat_end:].astype(BF16)
    pad = jnp.zeros((DEPTH, D_MODEL, IN_KVLAT_OFF - IN_ROPE_OFF - 2 * B_ROPE), BF16)
    qa = (w_in[:, :, :A_WIDTH] * (A_HEAD_DIM ** -0.5 * LOG2E)).astype(BF16)
    w_in_p = jnp.concatenate([qa, w_in[:, :, A_WIDTH:IN_ROPE_OFF].astype(BF16), krope,
                              krope[:, :, swap], pad,
                              w_in[:, :, IN_ROPE_OFF:lat_end].astype(BF16)], axis=-1)

    uq = w_uq.reshape(DEPTH, Q_LORA, B_HEADS, B_NOPE + B_ROPE)
    uq_pe = uq[..., B_NOPE:]
    wq_p = jnp.concatenate([uq[..., :B_NOPE], uq_pe, uq_pe[..., swap]], axis=-1)
    wq_p = wq_p.reshape(DEPTH, Q_LORA, B_HEADS * B_QK).astype(BF16)

    ukv = w_ukv.reshape(DEPTH, KV_LORA, B_HEADS, B_NOPE + B_V)
    wkv_p = jnp.concatenate([ukv[..., :B_NOPE].reshape(DEPTH, KV_LORA, B_HEADS * B_NOPE),
                             ukv[..., B_NOPE:].reshape(DEPTH, KV_LORA, B_WIDTH)],
                            axis=-1).astype(BF16)

    wxkv_all = jnp.transpose(w_xkv.astype(BF16), (1, 0, 2)).reshape(1, D_MODEL, DEPTH * 2 * X_WIDTH)
    return w_in_p, wq_p, wkv_p, wxkv_all


def _bias_table_kernel(w_ref, o_ref):
    v = pl.program_id(1)
    r = lax.broadcasted_iota(jnp.int32, (A_TQ, A_WIN), 0)
    t = lax.broadcasted_iota(jnp.int32, (A_TQ, A_WIN), 1)
    dchunk = t // CHUNK - r // CHUNK
    valid = (dchunk >= 0) & (dchunk <= LEFT_CHUNKS) & (v - (A_KBLKS - 1) + t // A_TQ >= 0)
    for hh in range(2):
        w = jnp.broadcast_to(w_ref[hh], (A_TQ, A_BIAS_VEC))
        tab = pltpu.roll(w, 0, axis=1, stride=1, stride_axis=0)[:, :A_WIN]
        o_ref[0, 0, :, hh * A_TQ:(hh + 1) * A_TQ] = jnp.where(valid, tab * LOG2E, -jnp.inf).T


def _relpos_bias_tables(rel_bias):
    u = jnp.arange(A_BIAS_VEC)
    delta = jnp.where(u <= A_WIN, u, u - A_BIAS_VEC)
    idx = jnp.clip(LEFT_CHUNKS * CHUNK - delta, -REL_CLIP, REL_CLIP) + REL_CLIP
    vec = rel_bias[:, :, idx].reshape(DEPTH * A_HEADS, 1, A_BIAS_VEC).astype(F32)
    return pl.pallas_call(
        _bias_table_kernel,
        out_shape=jax.ShapeDtypeStruct((DEPTH * A_HEADS // 2, A_KBLKS, A_WIN, 2 * A_TQ), F32),
        grid=(DEPTH * A_HEADS // 2, A_KBLKS),
        in_specs=[pl.BlockSpec((2, 1, A_BIAS_VEC), lambda i, v: (i, 0, 0))],
        out_specs=pl.BlockSpec((1, 1, A_WIN, 2 * A_TQ), lambda i, v: (i, v, 0, 0)),
        compiler_params=_cparams(("arbitrary", "arbitrary"), 32 << 20),
        name="bias_table",
    )(vec)


def kernel(x, mem, positions, norm_mix, w_in, rel_bias, q_norm, kv_norm, w_uq, w_ukv, w_out,
           norm_mem, mem_norm, w_xq, w_xkv, w_xo, norm_ffn, w_gate, w_up, w_down, norm_final):
    b, s, d = x.shape
    m = b * s
    assert (b, s, d) == (4, 4096, D_MODEL) and mem.shape == (b, N_MEM, d)

    w_in_p, wq_p, wkv_p, wxkv_all = _prep_weights(w_in, w_uq, w_ukv, w_xkv)
    w_out_b = w_out.astype(BF16)
    w_xq_b = w_xq.astype(BF16)
    w_xo_b = w_xo.astype(BF16)
    w_gate_b = w_gate.astype(BF16)
    w_up_b = w_up.astype(BF16)
    w_down_b = w_down.astype(BF16)
    bias_tab = _relpos_bias_tables(rel_bias)

    half = B_ROPE // 2
    inv = ROPE_THETA ** (-jnp.arange(half, dtype=F32) / half)
    inv_row = jnp.tile(inv, 4)[None, :]
    sign_row = jnp.concatenate([jnp.ones((B_ROPE,), F32), -jnp.ones((half,), F32),
                                jnp.ones((half,), F32)])[None, :]
    cs = _rope_table(positions.reshape(m, 1), inv_row, sign_row, tm=2048)

    kvm = _norm_matmul(mem.reshape(b * N_MEM, d), mem_norm[None, None, :], wxkv_all, 0,
                       tm=b * N_MEM, tn=1024, name="mem_kv")

    norm_mix, norm_mem, norm_ffn, q_norm, kv_norm = (
        g[:, None, :] for g in (norm_mix, norm_mem, norm_ffn, q_norm, kv_norm))
    xf = x.reshape(m, d)
    for l in range(DEPTH):
        proj = _norm_matmul(xf, norm_mix, w_in_p, l, tm=1024, tn=1536, name=f"in_proj_{l}")
        oa = _attn_a(proj, bias_tab, l, batch=b, seq=s, name=f"attn_a_{l}")
        qb, kn, vt, kpe = _mla_prep(proj, cs, q_norm, kv_norm, wq_p, wkv_p, l, seq=s, tm=512,
                                    name=f"mla_prep_{l}")
        ob = _attn_b(qb, kn, kpe, vt, batch=b, seq=s, name=f"attn_b_{l}")
        xf = _mm_res([oa, ob], w_out_b, xf, l, tm=1024, tn=1024, name=f"out_proj_{l}")
        xf = _cross(xf, norm_mem, w_xq_b, kvm, w_xo_b, l, seq=s, tm=512, name=f"cross_{l}")
        act = _swiglu_gu(xf, norm_ffn, w_gate_b, w_up_b, l, tm=1024, tn=512, name=f"swiglu_gu_{l}")
        xf = _mm_res([act], w_down_b, xf, l, tm=1024, tn=512, name=f"swiglu_down_{l}")
    out = _final_norm(xf, norm_final[None, :], tm=512)
    return out.reshape(b, s, d)
```

```python
import functools
import math

import jax
import jax.numpy as jnp
from jax import lax
from jax.experimental import pallas as pl
from jax.experimental.pallas import tpu as pltpu

D_MODEL = 2048
DEPTH = 4
CHUNK = 64
LEFT_CHUNKS = 8
A_HEAD_DIM = 128
A_HEADS = 8
A_WIDTH = A_HEADS * A_HEAD_DIM
REL_CLIP = 128
B_NOPE = 128
B_ROPE = 64
B_V = 128
B_HEADS = 8
B_WIDTH = B_HEADS * B_V
Q_LORA = 768
KV_LORA = 512
ROPE_THETA = 10000.0
N_MEM = 256
X_HEADS = 4
X_HEAD_DIM = 128
X_WIDTH = X_HEADS * X_HEAD_DIM
D_FF = 5632
EPS = 1e-6

V7X_LANES = 128
V7X_VMEM_BYTES = 64 * 1024 * 1024
V7X_VMEM_REQUEST_CAP = 56 * 1024 * 1024

BF16 = jnp.bfloat16
F32 = jnp.float32

IN_QLAT_OFF = 3 * A_WIDTH
IN_ROPE_OFF = IN_QLAT_OFF + Q_LORA
IN_KVLAT_OFF = IN_ROPE_OFF + 2 * V7X_LANES
IN_PAD_WIDTH = IN_KVLAT_OFF + KV_LORA

A_TQ = 4 * CHUNK
A_WIN = A_TQ + LEFT_CHUNKS * CHUNK
A_KBLKS = A_WIN // A_TQ
A_BIAS_VEC = 1024

B_TQ = 512
B_TK = 512
B_HEADS_PER_STEP = 4
B_RED_SLABS = 8
B_QK = B_NOPE + 2 * B_ROPE

LOG2E = math.log2(math.e)


def _cparams(sem, vmem_bytes):
    limit = int(min(max(vmem_bytes, 32 * 1024 * 1024), V7X_VMEM_REQUEST_CAP))
    return pltpu.CompilerParams(dimension_semantics=sem, vmem_limit_bytes=limit)


def _rms_rows(x, g):
    ms = jnp.mean(x * x, axis=-1, keepdims=True)
    return (x * lax.rsqrt(ms + EPS)) * g


def _norm_into(h_ref, x_ref, g_ref, rows=128):
    n = x_ref.shape[0] // rows

    def body(c, carry):
        r0 = pl.multiple_of(c * rows, rows)
        x = x_ref[pl.ds(r0, rows), :].astype(F32)
        h_ref[pl.ds(r0, rows), :] = _rms_rows(x, g_ref[...]).astype(h_ref.dtype)
        return carry

    lax.fori_loop(0, n, body, 0)


def _norm_matmul_kernel(x_ref, g_ref, w_ref, o_ref, h_ref):
    @pl.when(pl.program_id(1) == 0)
    def _():
        _norm_into(h_ref, x_ref, g_ref)

    o_ref[...] = jnp.dot(h_ref[...], w_ref[...], preferred_element_type=F32).astype(o_ref.dtype)


def _norm_matmul(x, g, w, layer, *, tm, tn, name):
    m, k = x.shape
    n = w.shape[2]
    xb = x.dtype.itemsize
    est = 2 * tm * k * xb + tm * k * 2 + 2 * k * tn * 2 + 2 * tm * tn * 2 + tm * tn * 4 + (4 << 20)
    return pl.pallas_call(
        _norm_matmul_kernel,
        out_shape=jax.ShapeDtypeStruct((m, n), BF16),
        grid=(m // tm, n // tn),
        in_specs=[
            pl.BlockSpec((tm, k), lambda i, j: (i, 0)),
            pl.BlockSpec((None, 1, k), lambda i, j: (layer, 0, 0)),
            pl.BlockSpec((None, k, tn), lambda i, j: (layer, 0, j)),
        ],
        out_specs=pl.BlockSpec((tm, tn), lambda i, j: (i, j)),
        scratch_shapes=[pltpu.VMEM((tm, k), BF16)],
        compiler_params=_cparams(("arbitrary", "arbitrary"), est),
        name=name,
    )(x, g, w)


def _mm_res_kernel(*refs, n_in):
    a_refs = refs[:n_in]
    w_refs = refs[n_in:2 * n_in]
    r_ref = refs[2 * n_in]
    o_ref = refs[2 * n_in + 1]
    acc = r_ref[...]
    for a_ref, w_ref in zip(a_refs, w_refs):
        acc = acc + jnp.dot(a_ref[...], w_ref[...], preferred_element_type=F32)
    o_ref[...] = acc


def _mm_res(a_list, w, res, layer, *, tm, tn, name):
    m, n = res.shape
    n_in = len(a_list)
    in_specs = []
    w_specs = []
    est = (4 + 2 * n_in) * tm * tn * 4 + (4 << 20)
    for idx, a in enumerate(a_list):
        k = a.shape[1]
        assert w.shape[1] == n_in * k
        in_specs.append(pl.BlockSpec((tm, k), lambda i, j: (i, 0)))
        w_specs.append(pl.BlockSpec((None, k, tn), lambda i, j, idx=idx: (layer, idx, j)))
        est += 2 * tm * k * 2 + 2 * k * tn * 2
    in_specs += w_specs
    in_specs.append(pl.BlockSpec((tm, tn), lambda i, j: (i, j)))
    return pl.pallas_call(
        functools.partial(_mm_res_kernel, n_in=n_in),
        out_shape=jax.ShapeDtypeStruct((m, n), F32),
        grid=(m // tm, n // tn),
        in_specs=in_specs,
        out_specs=pl.BlockSpec((tm, tn), lambda i, j: (i, j)),
        compiler_params=_cparams(("arbitrary", "arbitrary"), est),
        name=name,
    )(*a_list, *([w] * n_in), res)


def _swiglu_gu_kernel(x_ref, g_ref, wg_ref, wu_ref, o_ref, h_ref):
    @pl.when(pl.program_id(1) == 0)
    def _():
        _norm_into(h_ref, x_ref, g_ref)

    h = h_ref[...]
    gate = jnp.dot(h, wg_ref[...].astype(BF16), preferred_element_type=F32)
    up = jnp.dot(h, wu_ref[...].astype(BF16), preferred_element_type=F32)
    o_ref[...] = (gate * (1.0 / (1.0 + jnp.exp(-gate))) * up).astype(o_ref.dtype)


def _swiglu_gu(x, g, wg, wu, layer, *, tm, tn, name):
    m, k = x.shape
    n = wg.shape[2]
    wbytes = wg.dtype.itemsize
    est = (2 * tm * k * 4 + tm * k * 2 + 4 * k * tn * wbytes + 2 * k * tn * 2 + 2 * tm * tn * 2
           + 3 * tm * tn * 4 + (4 << 20))
    return pl.pallas_call(
        _swiglu_gu_kernel,
        out_shape=jax.ShapeDtypeStruct((m, n), BF16),
        grid=(m // tm, n // tn),
        in_specs=[
            pl.BlockSpec((tm, k), lambda i, j: (i, 0)),
            pl.BlockSpec((None, 1, k), lambda i, j: (layer, 0, 0)),
            pl.BlockSpec((None, k, tn), lambda i, j: (layer, 0, j)),
            pl.BlockSpec((None, k, tn), lambda i, j: (layer, 0, j)),
        ],
        out_specs=pl.BlockSpec((tm, tn), lambda i, j: (i, j)),
        scratch_shapes=[pltpu.VMEM((tm, k), BF16)],
        compiler_params=_cparams(("arbitrary", "arbitrary"), est),
        name=name,
    )(x, g, wg, wu)


def _rope_table_kernel(pos_ref, inv_ref, sign_ref, cs_ref):
    ang = pos_ref[...].astype(F32) * inv_ref[...]
    lane = lax.broadcasted_iota(jnp.int32, ang.shape, 1)
    cs_ref[...] = jnp.where(lane < B_ROPE, jnp.cos(ang), jnp.sin(ang) * sign_ref[...])


def _rope_table(pos_col, inv_row, sign_row, *, tm):
    m = pos_col.shape[0]
    return pl.pallas_call(
        _rope_table_kernel,
        out_shape=jax.ShapeDtypeStruct((m, V7X_LANES), F32),
        grid=(m // tm,),
        in_specs=[
            pl.BlockSpec((tm, 1), lambda i: (i, 0)),
            pl.BlockSpec((1, V7X_LANES), lambda i: (0, 0)),
            pl.BlockSpec((1, V7X_LANES), lambda i: (0, 0)),
        ],
        out_specs=pl.BlockSpec((tm, V7X_LANES), lambda i: (i, 0)),
        compiler_params=_cparams(("arbitrary",), 32 << 20),
        name="rope_table",
    )(pos_col, inv_row, sign_row)


def _rope_pair_sum(slot, cs):
    t = slot * cs
    return t + pltpu.roll(t, B_ROPE, axis=1)


def _mla_prep_kernel(ql_ref, rope_ref, kvl_ref, cs_ref, gq_ref, gkv_ref, wq_ref, wkv_ref,
                     qb_ref, kn_ref, vt_ref, kpe_ref, *, q_scale):
    cs = cs_ref[...]
    hq = _rms_rows(ql_ref[...].astype(F32), gq_ref[...]).astype(BF16)
    qf = jnp.dot(hq, wq_ref[...], preferred_element_type=F32)
    for h in range(B_HEADS):
        base = h * B_QK
        nope = qf[:, base:base + B_NOPE]
        pe = _rope_pair_sum(qf[:, base + B_NOPE:base + B_QK], cs)
        qb_ref[:, base:base + B_NOPE] = (nope * q_scale).astype(BF16)
        qb_ref[:, base + B_NOPE:base + B_QK] = (pe * q_scale).astype(BF16)

    hkv = _rms_rows(kvl_ref[...].astype(F32), gkv_ref[...]).astype(BF16)
    kvf = jnp.dot(hkv, wkv_ref[...], preferred_element_type=F32)
    kn_ref[...] = kvf[:, :B_HEADS * B_NOPE].astype(BF16)
    vt_ref[...] = kvf[:, B_HEADS * B_NOPE:].T.astype(BF16)

    kpe = _rope_pair_sum(rope_ref[...].astype(F32), cs)
    lane = lax.broadcasted_iota(jnp.int32, kpe.shape, 1)
    kpe_ref[...] = jnp.where(lane < B_ROPE, kpe, 0.0).astype(BF16)


def _mla_prep(proj, cs, gq, gkv, wq, wkv, layer, *, seq, tm, name):
    m = proj.shape[0]
    tiles_per_seq = seq // tm
    q_scale = (B_NOPE + B_ROPE) ** -0.5 * LOG2E
    est = (2 * tm * (Q_LORA + KV_LORA + 128) * 2 + 2 * tm * 128 * 4
           + 2 * (Q_LORA + KV_LORA) * 2048 * 2
           + 2 * tm * (2048 + 1024 + 1024 + 128) * 2 + 3 * tm * 2048 * 4 + (4 << 20))
    return pl.pallas_call(
        functools.partial(_mla_prep_kernel, q_scale=q_scale),
        out_shape=(
            jax.ShapeDtypeStruct((m, B_HEADS * B_QK), BF16),
            jax.ShapeDtypeStruct((m, B_HEADS * B_NOPE), BF16),
            jax.ShapeDtypeStruct((m // seq * B_WIDTH, seq), BF16),
            jax.ShapeDtypeStruct((m, V7X_LANES), BF16),
        ),
        grid=(m // tm,),
        in_specs=[
            pl.BlockSpec((tm, Q_LORA), lambda i: (i, IN_QLAT_OFF // Q_LORA)),
            pl.BlockSpec((tm, V7X_LANES), lambda i: (i, IN_ROPE_OFF // V7X_LANES)),
            pl.BlockSpec((tm, KV_LORA), lambda i: (i, IN_KVLAT_OFF // KV_LORA)),
            pl.BlockSpec((tm, V7X_LANES), lambda i: (i, 0)),
            pl.BlockSpec((None, 1, Q_LORA), lambda i: (layer, 0, 0)),
            pl.BlockSpec((None, 1, KV_LORA), lambda i: (layer, 0, 0)),
            pl.BlockSpec((None, Q_LORA, B_HEADS * B_QK), lambda i: (layer, 0, 0)),
            pl.BlockSpec((None, KV_LORA, 2 * B_WIDTH), lambda i: (layer, 0, 0)),
        ],
        out_specs=(
            pl.BlockSpec((tm, B_HEADS * B_QK), lambda i: (i, 0)),
            pl.BlockSpec((tm, B_HEADS * B_NOPE), lambda i: (i, 0)),
            pl.BlockSpec((B_WIDTH, tm), lambda i: (i // tiles_per_seq, i % tiles_per_seq)),
            pl.BlockSpec((tm, V7X_LANES), lambda i: (i, 0)),
        ),
        compiler_params=_cparams(("arbitrary",), est),
        name=name,
    )(proj, proj, proj, cs, gq, gkv, wq, wkv)


def _attn_a_kernel(q_ref, k0_ref, k1_ref, k2_ref, v0_ref, v1_ref, v2_ref, bias_ref, o_ref):
    d, tq = A_HEAD_DIM, A_TQ
    scale = d ** -0.5 * LOG2E
    lane = lax.broadcasted_iota(jnp.int32, (tq, 2 * d), 1)

    def scores(g):
        ps = slice(2 * g * d, (2 * g + 2) * d)
        k = jnp.concatenate([k0_ref[:, ps], k1_ref[:, ps], k2_ref[:, ps]], axis=0)
        q = q_ref[:, ps]
        zero = jnp.zeros_like(q)
        qd = jnp.concatenate([jnp.where(lane < d, q, zero), jnp.where(lane >= d, q, zero)], axis=0)
        return lax.dot_general(k, qd, (((1,), (1,)), ((), ())),
                               preferred_element_type=F32)

    st_next = scores(0)
    for g in range(A_HEADS // 2):
        ps = slice(2 * g * d, (2 * g + 2) * d)
        st = st_next
        if g + 1 < A_HEADS // 2:
            st_next = scores(g + 1)
        v = jnp.concatenate([v0_ref[:, ps], v1_ref[:, ps], v2_ref[:, ps]], axis=0)
        st = st * scale + bias_ref[g]
        m = _col_reduce(st, jnp.max)
        p = jnp.exp2(st - m)
        l = _col_reduce(p, jnp.sum)
        pb = p.astype(BF16)
        for hh in range(2):
            ot = lax.dot_general(v[:, hh * d:(hh + 1) * d], pb[:, hh * tq:(hh + 1) * tq],
                                 (((0,), (0,)), ((), ())), preferred_element_type=F32)
            ot = ot / l[:, hh * tq:(hh + 1) * tq]
            o_ref[:, (2 * g + hh) * d:(2 * g + hh + 1) * d] = ot.T.astype(o_ref.dtype)


def _attn_a(proj, bias, layer, *, batch, seq, name):
    m = proj.shape[0]
    nq = seq // A_TQ
    qspec = pl.BlockSpec((A_TQ, A_WIDTH), lambda b, i: (b * nq + i, 0))

    def kv_spec(col_blk, c):
        back = A_KBLKS - 1 - c
        return pl.BlockSpec((A_TQ, A_WIDTH),
                            lambda b, i: (b * nq + jnp.maximum(i - back, 0), col_blk))

    est = 2 * 7 * A_TQ * A_WIDTH * 2 + 2 * A_HEADS * A_TQ * A_WIN * 4 + 2 * A_TQ * A_WIDTH * 2 + (8 << 20)
    return pl.pallas_call(
        _attn_a_kernel,
        out_shape=jax.ShapeDtypeStruct((m, A_WIDTH), BF16),
        grid=(batch, nq),
        in_specs=[qspec,
                  kv_spec(1, 0), kv_spec(1, 1), kv_spec(1, 2),
                  kv_spec(2, 0), kv_spec(2, 1), kv_spec(2, 2),
                  pl.BlockSpec((A_HEADS // 2, None, A_WIN, 2 * A_TQ),
                               lambda b, i: (layer, jnp.minimum(i, A_KBLKS - 1), 0, 0))],
        out_specs=pl.BlockSpec((A_TQ, A_WIDTH), lambda b, i: (b * nq + i, 0)),
        compiler_params=_cparams(("arbitrary", "arbitrary"), est),
        name=name,
    )(proj, proj, proj, proj, proj, proj, proj, bias)


def _col_reduce(x, op):
    rows, cols = x.shape
    slabs = op(x.reshape(B_RED_SLABS, rows // B_RED_SLABS, cols), axis=0)
    return op(slabs, axis=0, keepdims=True)


def _attn_b_kernel(q_ref, kn_ref, kpe_ref, vt_ref, o_ref, m_ref, l_ref, acc_ref, st_ref):
    i = pl.program_id(2)
    hp = B_HEADS_PER_STEP
    m_ref[...] = jnp.full(m_ref.shape, -jnp.inf, F32)
    l_ref[...] = jnp.zeros(l_ref.shape, F32)
    acc_ref[...] = jnp.zeros(acc_ref.shape, F32)

    def scores(j, h):
        r0 = pl.multiple_of(j * B_TK, B_TK)
        k = jnp.concatenate([kn_ref[pl.ds(r0, B_TK), h * B_NOPE:(h + 1) * B_NOPE],
                             kpe_ref[pl.ds(r0, B_TK), :]], axis=1)
        q = q_ref[:, h * B_QK:(h + 1) * B_QK]
        return lax.dot_general(k, q, (((1,), (1,)), ((), ())), preferred_element_type=F32)

    def softmax_pv(j, h, st, allowed):
        r0 = pl.multiple_of(j * B_TK, B_TK)
        if allowed is not None:
            st = jnp.where(allowed, st, -jnp.inf)
        m_prev = m_ref[h]
        m_new = jnp.maximum(m_prev, _col_reduce(st, jnp.max))
        alpha = jnp.exp2(m_prev - m_new)
        p = jnp.exp2(st - m_new)
        l_ref[h] = alpha * l_ref[h] + _col_reduce(p, jnp.sum)
        pv = jnp.dot(vt_ref[h * B_V:(h + 1) * B_V, pl.ds(r0, B_TK)], p.astype(BF16),
                     preferred_element_type=F32)
        acc_ref[h] = alpha * acc_ref[h] + pv
        m_ref[h] = m_new

    def key_block(j, masked):
        allowed = None
        if masked:
            krow = lax.broadcasted_iota(jnp.int32, (B_TK, B_TQ), 0)
            qcol = lax.broadcasted_iota(jnp.int32, (B_TK, B_TQ), 1)
            allowed = (krow // CHUNK) <= (qcol // CHUNK)
        st = st_ref[...]
        for h in range(hp):
            if h + 1 < hp:
                st_next = scores(j, h + 1)
            elif not masked:
                st_ref[...] = scores(j + 1, 0)
            softmax_pv(j, h, st, allowed)
            if h + 1 < hp:
                st = st_next

    def body(j, carry):
        key_block(j, False)
        return carry

    st_ref[...] = scores(0, 0)
    lax.fori_loop(0, i, body, 0)
    key_block(i, True)

    for h in range(hp):
        o_ref[:, h * B_V:(h + 1) * B_V] = (acc_ref[h] / l_ref[h]).T.astype(o_ref.dtype)


def _attn_b(qb, kn, kpe, vt, *, batch, seq, name):
    m = qb.shape[0]
    nq = seq // B_TQ
    hp = B_HEADS_PER_STEP
    est = (2 * B_TQ * hp * B_QK * 2 + 2 * seq * hp * (B_NOPE + B_V) * 2 + 2 * seq * 128 * 2
           + 2 * B_TQ * hp * B_V * 2 + hp * (B_V + 16) * B_TQ * 4 + 8 * B_TQ * B_TK * 4 + (4 << 20))
    return pl.pallas_call(
        _attn_b_kernel,
        out_shape=jax.ShapeDtypeStruct((m, B_WIDTH), BF16),
        grid=(batch, B_HEADS // hp, nq),
        in_specs=[
            pl.BlockSpec((B_TQ, hp * B_QK), lambda b, g, i: (b * nq + i, g)),
            pl.BlockSpec((seq, hp * B_NOPE), lambda b, g, i: (b, g)),
            pl.BlockSpec((seq, V7X_LANES), lambda b, g, i: (b, 0)),
            pl.BlockSpec((hp * B_V, seq), lambda b, g, i: (b * (B_HEADS // hp) + g, 0)),
        ],
        out_specs=pl.BlockSpec((B_TQ, hp * B_V), lambda b, g, i: (b * nq + i, g)),
        scratch_shapes=[
            pltpu.VMEM((hp, 1, B_TQ), F32),
            pltpu.VMEM((hp, 1, B_TQ), F32),
            pltpu.VMEM((hp, B_V, B_TQ), F32),
            pltpu.VMEM((B_TK, B_TQ), F32),
        ],
        compiler_params=_cparams(("arbitrary", "arbitrary", "arbitrary"), est),
        name=name,
    )(qb, kn, kpe, vt)


def _cross_kernel(x_ref, g_ref, wq_ref, k_ref, v_ref, wo_ref, o_ref, h_ref):
    _norm_into(h_ref, x_ref, g_ref)
    scale = X_HEAD_DIM ** -0.5
    q = (jnp.dot(h_ref[...], wq_ref[...], preferred_element_type=F32) * scale).astype(BF16)
    outs = []
    for h in range(X_HEADS):
        hs = slice(h * X_HEAD_DIM, (h + 1) * X_HEAD_DIM)
        s = lax.dot_general(q[:, hs], k_ref[:, hs], (((1,), (1,)), ((), ())),
                            preferred_element_type=F32)
        m = jnp.max(s, axis=1, keepdims=True)
        p = jnp.exp(s - m)
        l = jnp.sum(p, axis=1, keepdims=True)
        o = jnp.dot(p.astype(BF16), v_ref[:, hs], preferred_element_type=F32)
        outs.append((o / l).astype(BF16))
    o_all = jnp.concatenate(outs, axis=1)
    o_ref[...] = x_ref[...] + jnp.dot(o_all, wo_ref[...], preferred_element_type=F32)


def _cross(x, g, wq, kvm, wo, layer, *, seq, tm, name):
    m, d = x.shape
    blocks_per_batch = seq // tm
    est = (4 * tm * d * 4 + tm * d * 2 + 2 * d * X_WIDTH * 2 + 2 * X_WIDTH * d * 2
           + 4 * N_MEM * X_WIDTH * 2 + 4 * tm * N_MEM * 4 + 2 * tm * d * 4 + (4 << 20))
    return pl.pallas_call(
        _cross_kernel,
        out_shape=jax.ShapeDtypeStruct((m, d), F32),
        grid=(m // tm,),
        in_specs=[
            pl.BlockSpec((tm, d), lambda i: (i, 0)),
            pl.BlockSpec((None, 1, d), lambda i: (layer, 0, 0)),
            pl.BlockSpec((None, d, X_WIDTH), lambda i: (layer, 0, 0)),
            pl.BlockSpec((N_MEM, X_WIDTH), lambda i: (i // blocks_per_batch, 2 * layer)),
            pl.BlockSpec((N_MEM, X_WIDTH), lambda i: (i // blocks_per_batch, 2 * layer + 1)),
            pl.BlockSpec((None, X_WIDTH, d), lambda i: (layer, 0, 0)),
        ],
        out_specs=pl.BlockSpec((tm, d), lambda i: (i, 0)),
        scratch_shapes=[pltpu.VMEM((tm, d), BF16)],
        compiler_params=_cparams(("arbitrary",), est),
        name=name,
    )(x, g, wq, kvm, kvm, wo)


def _final_norm_kernel(x_ref, g_ref, o_ref):
    o_ref[...] = _rms_rows(x_ref[...], g_ref[...])


def _final_norm(x, g, *, tm):
    m, d = x.shape
    return pl.pallas_call(
        _final_norm_kernel,
        out_shape=jax.ShapeDtypeStruct((m, d), F32),
        grid=(m // tm,),
        in_specs=[pl.BlockSpec((tm, d), lambda i: (i, 0)), pl.BlockSpec((1, d), lambda i: (0, 0))],
        out_specs=pl.BlockSpec((tm, d), lambda i: (i, 0)),
        compiler_params=_cparams(("arbitrary",), 6 * tm * d * 4 + (4 << 20)),
        name="final_norm",
    )(x, g)


def _prep_weights(w_in, w_uq, w_ukv, w_xkv):
    swap = (jnp.arange(B_ROPE) + B_ROPE // 2) % B_ROPE
    lat_end = IN_ROPE_OFF + KV_LORA
    w_in_b = w_in.astype(BF16)
    krope = w_in_b[:, :, lat_end:]
    pad = jnp.zeros((DEPTH, D_MODEL, IN_KVLAT_OFF - IN_ROPE_OFF - 2 * B_ROPE), BF16)
    w_in_p = jnp.concatenate([w_in_b[:, :, :IN_ROPE_OFF], krope, krope[:, :, swap], pad,
                              w_in_b[:, :, IN_ROPE_OFF:lat_end]], axis=-1)

    uq = w_uq.reshape(DEPTH, Q_LORA, B_HEADS, B_NOPE + B_ROPE)
    uq_pe = uq[..., B_NOPE:]
    wq_p = jnp.concatenate([uq[..., :B_NOPE], uq_pe, uq_pe[..., swap]], axis=-1)
    wq_p = wq_p.reshape(DEPTH, Q_LORA, B_HEADS * B_QK).astype(BF16)

    ukv = w_ukv.reshape(DEPTH, KV_LORA, B_HEADS, B_NOPE + B_V)
    wkv_p = jnp.concatenate([ukv[..., :B_NOPE].reshape(DEPTH, KV_LORA, B_HEADS * B_NOPE),
                             ukv[..., B_NOPE:].reshape(DEPTH, KV_LORA, B_WIDTH)],
                            axis=-1).astype(BF16)

    wxkv_all = jnp.transpose(w_xkv.astype(BF16), (1, 0, 2)).reshape(1, D_MODEL, DEPTH * 2 * X_WIDTH)
    return w_in_p, wq_p, wkv_p, wxkv_all


def _bias_table_kernel(w_ref, o_ref):
    v = pl.program_id(1)
    r = lax.broadcasted_iota(jnp.int32, (A_TQ, A_WIN), 0)
    t = lax.broadcasted_iota(jnp.int32, (A_TQ, A_WIN), 1)
    dchunk = t // CHUNK - r // CHUNK
    valid = (dchunk >= 0) & (dchunk <= LEFT_CHUNKS) & (v - (A_KBLKS - 1) + t // A_TQ >= 0)
    for hh in range(2):
        w = jnp.broadcast_to(w_ref[hh], (A_TQ, A_BIAS_VEC))
        tab = pltpu.roll(w, 0, axis=1, stride=1, stride_axis=0)[:, :A_WIN]
        o_ref[0, 0, :, hh * A_TQ:(hh + 1) * A_TQ] = jnp.where(valid, tab * LOG2E, -jnp.inf).T


def _relpos_bias_tables(rel_bias):
    u = jnp.arange(A_BIAS_VEC)
    delta = jnp.where(u <= A_WIN, u, u - A_BIAS_VEC)
    idx = jnp.clip(LEFT_CHUNKS * CHUNK - delta, -REL_CLIP, REL_CLIP) + REL_CLIP
    vec = rel_bias[:, :, idx].reshape(DEPTH * A_HEADS, 1, A_BIAS_VEC).astype(F32)
    return pl.pallas_call(
        _bias_table_kernel,
        out_shape=jax.ShapeDtypeStruct((DEPTH * A_HEADS // 2, A_KBLKS, A_WIN, 2 * A_TQ), F32),
        grid=(DEPTH * A_HEADS // 2, A_KBLKS),
        in_specs=[pl.BlockSpec((2, 1, A_BIAS_VEC), lambda i, v: (i, 0, 0))],
        out_specs=pl.BlockSpec((1, 1, A_WIN, 2 * A_TQ), lambda i, v: (i, v, 0, 0)),
        compiler_params=_cparams(("arbitrary", "arbitrary"), 32 << 20),
        name="bias_table",
    )(vec)


def kernel(x, mem, positions, norm_mix, w_in, rel_bias, q_norm, kv_norm, w_uq, w_ukv, w_out,
           norm_mem, mem_norm, w_xq, w_xkv, w_xo, norm_ffn, w_gate, w_up, w_down, norm_final):
    b, s, d = x.shape
    m = b * s
    assert (b, s, d) == (4, 4096, D_MODEL) and mem.shape == (b, N_MEM, d)

    w_in_p, wq_p, wkv_p, wxkv_all = _prep_weights(w_in, w_uq, w_ukv, w_xkv)
    w_out_b = w_out.astype(BF16)
    w_xq_b = w_xq.astype(BF16)
    w_xo_b = w_xo.astype(BF16)
    w_down_b = w_down.astype(BF16)
    bias_tab = _relpos_bias_tables(rel_bias)

    half = B_ROPE // 2
    inv = ROPE_THETA ** (-jnp.arange(half, dtype=F32) / half)
    inv_row = jnp.tile(inv, 4)[None, :]
    sign_row = jnp.concatenate([jnp.ones((B_ROPE,), F32), -jnp.ones((half,), F32),
                                jnp.ones((half,), F32)])[None, :]
    cs = _rope_table(positions.reshape(m, 1), inv_row, sign_row, tm=2048)

    kvm = _norm_matmul(mem.reshape(b * N_MEM, d), mem_norm[None, None, :], wxkv_all, 0,
                       tm=b * N_MEM, tn=1024, name="mem_kv")

    norm_mix, norm_mem, norm_ffn, q_norm, kv_norm = (
        g[:, None, :] for g in (norm_mix, norm_mem, norm_ffn, q_norm, kv_norm))
    xf = x.reshape(m, d)
    for l in range(DEPTH):
        proj = _norm_matmul(xf, norm_mix, w_in_p, l, tm=1024, tn=1536, name=f"in_proj_{l}")
        oa = _attn_a(proj, bias_tab, l, batch=b, seq=s, name=f"attn_a_{l}")
        qb, kn, vt, kpe = _mla_prep(proj, cs, q_norm, kv_norm, wq_p, wkv_p, l, seq=s, tm=512,
                                    name=f"mla_prep_{l}")
        ob = _attn_b(qb, kn, kpe, vt, batch=b, seq=s, name=f"attn_b_{l}")
        xf = _mm_res([oa, ob], w_out_b, xf, l, tm=1024, tn=1024, name=f"out_proj_{l}")
        xf = _cross(xf, norm_mem, w_xq_b, kvm, w_xo_b, l, seq=s, tm=512, name=f"cross_{l}")
        act = _swiglu_gu(xf, norm_ffn, w_gate, w_up, l, tm=1024, tn=512, name=f"swiglu_gu_{l}")
        xf = _mm_res([act], w_down_b, xf, l, tm=1024, tn=512, name=f"swiglu_down_{l}")
    out = _final_norm(xf, norm_final[None, :], tm=512)
    return out.reshape(b, s, d)
```

```python
import functools
import math

import jax
import jax.numpy as jnp
from jax import lax
from jax.experimental import pallas as pl
from jax.experimental.pallas import tpu as pltpu

D_MODEL = 2048
DEPTH = 4
CHUNK = 64
LEFT_CHUNKS = 8
A_HEAD_DIM = 128
A_HEADS = 8
A_WIDTH = A_HEADS * A_HEAD_DIM
REL_CLIP = 128
B_NOPE = 128
B_ROPE = 64
B_V = 128
B_HEADS = 8
B_WIDTH = B_HEADS * B_V
Q_LORA = 768
KV_LORA = 512
ROPE_THETA = 10000.0
N_MEM = 256
X_HEADS = 4
X_HEAD_DIM = 128
X_WIDTH = X_HEADS * X_HEAD_DIM
D_FF = 5632
EPS = 1e-6

V7X_LANES = 128
V7X_VMEM_BYTES = 64 * 1024 * 1024
V7X_VMEM_REQUEST_CAP = 56 * 1024 * 1024

BF16 = jnp.bfloat16
F32 = jnp.float32

IN_QLAT_OFF = 3 * A_WIDTH
IN_ROPE_OFF = IN_QLAT_OFF + Q_LORA
IN_KVLAT_OFF = IN_ROPE_OFF + 2 * V7X_LANES
IN_PAD_WIDTH = IN_KVLAT_OFF + KV_LORA

A_TQ = 4 * CHUNK
A_WIN = A_TQ + LEFT_CHUNKS * CHUNK
A_KBLKS = A_WIN // A_TQ
A_BIAS_VEC = 1024

B_TQ = 512
B_TK = 512
B_HEADS_PER_STEP = 4
B_RED_SLABS = 8
B_QK = B_NOPE + 2 * B_ROPE

LOG2E = math.log2(math.e)


def _cparams(sem, vmem_bytes):
    limit = int(min(max(vmem_bytes, 32 * 1024 * 1024), V7X_VMEM_REQUEST_CAP))
    return pltpu.CompilerParams(dimension_semantics=sem, vmem_limit_bytes=limit)


def _rms_rows(x, g):
    ms = jnp.mean(x * x, axis=-1, keepdims=True)
    return (x * lax.rsqrt(ms + EPS)) * g


def _norm_into(h_ref, x_ref, g_ref, rows=128):
    n = x_ref.shape[0] // rows

    def body(c, carry):
        r0 = pl.multiple_of(c * rows, rows)
        x = x_ref[pl.ds(r0, rows), :].astype(F32)
        h_ref[pl.ds(r0, rows), :] = _rms_rows(x, g_ref[...]).astype(h_ref.dtype)
        return carry

    lax.fori_loop(0, n, body, 0)


def _norm_matmul_kernel(x_ref, g_ref, w_ref, o_ref, h_ref):
    @pl.when(pl.program_id(1) == 0)
    def _():
        _norm_into(h_ref, x_ref, g_ref)

    o_ref[...] = jnp.dot(h_ref[...], w_ref[...], preferred_element_type=F32).astype(o_ref.dtype)


def _norm_matmul(x, g, w, layer, *, tm, tn, name):
    m, k = x.shape
    n = w.shape[2]
    xb = x.dtype.itemsize
    est = 2 * tm * k * xb + tm * k * 2 + 2 * k * tn * 2 + 2 * tm * tn * 2 + tm * tn * 4 + (4 << 20)
    return pl.pallas_call(
        _norm_matmul_kernel,
        out_shape=jax.ShapeDtypeStruct((m, n), BF16),
        grid=(m // tm, n // tn),
        in_specs=[
            pl.BlockSpec((tm, k), lambda i, j: (i, 0)),
            pl.BlockSpec((None, 1, k), lambda i, j: (layer, 0, 0)),
            pl.BlockSpec((None, k, tn), lambda i, j: (layer, 0, j)),
        ],
        out_specs=pl.BlockSpec((tm, tn), lambda i, j: (i, j)),
        scratch_shapes=[pltpu.VMEM((tm, k), BF16)],
        compiler_params=_cparams(("arbitrary", "arbitrary"), est),
        name=name,
    )(x, g, w)


def _mm_res_kernel(*refs, n_in):
    a_refs = refs[:n_in]
    w_refs = refs[n_in:2 * n_in]
    r_ref = refs[2 * n_in]
    o_ref = refs[2 * n_in + 1]
    acc = r_ref[...]
    for a_ref, w_ref in zip(a_refs, w_refs):
        acc = acc + jnp.dot(a_ref[...], w_ref[...], preferred_element_type=F32)
    o_ref[...] = acc


def _mm_res(a_list, w, res, layer, *, tm, tn, name):
    m, n = res.shape
    n_in = len(a_list)
    in_specs = []
    w_specs = []
    est = (4 + 2 * n_in) * tm * tn * 4 + (4 << 20)
    for idx, a in enumerate(a_list):
        k = a.shape[1]
        assert w.shape[1] == n_in * k
        in_specs.append(pl.BlockSpec((tm, k), lambda i, j: (i, 0)))
        w_specs.append(pl.BlockSpec((None, k, tn), lambda i, j, idx=idx: (layer, idx, j)))
        est += 2 * tm * k * 2 + 2 * k * tn * 2
    in_specs += w_specs
    in_specs.append(pl.BlockSpec((tm, tn), lambda i, j: (i, j)))
    return pl.pallas_call(
        functools.partial(_mm_res_kernel, n_in=n_in),
        out_shape=jax.ShapeDtypeStruct((m, n), F32),
        grid=(m // tm, n // tn),
        in_specs=in_specs,
        out_specs=pl.BlockSpec((tm, tn), lambda i, j: (i, j)),
        compiler_params=_cparams(("arbitrary", "arbitrary"), est),
        name=name,
    )(*a_list, *([w] * n_in), res)


def _swiglu_gu_kernel(x_ref, g_ref, wg_ref, wu_ref, o_ref, h_ref):
    @pl.when(pl.program_id(1) == 0)
    def _():
        _norm_into(h_ref, x_ref, g_ref)

    h = h_ref[...]
    gate = jnp.dot(h, wg_ref[...].astype(BF16), preferred_element_type=F32)
    up = jnp.dot(h, wu_ref[...].astype(BF16), preferred_element_type=F32)
    o_ref[...] = (gate * (1.0 / (1.0 + jnp.exp(-gate))) * up).astype(o_ref.dtype)


def _swiglu_gu(x, g, wg, wu, layer, *, tm, tn, name):
    m, k = x.shape
    n = wg.shape[2]
    wbytes = wg.dtype.itemsize
    est = (2 * tm * k * 4 + tm * k * 2 + 4 * k * tn * wbytes + 2 * k * tn * 2 + 2 * tm * tn * 2
           + 3 * tm * tn * 4 + (4 << 20))
    return pl.pallas_call(
        _swiglu_gu_kernel,
        out_shape=jax.ShapeDtypeStruct((m, n), BF16),
        grid=(m // tm, n // tn),
        in_specs=[
            pl.BlockSpec((tm, k), lambda i, j: (i, 0)),
            pl.BlockSpec((None, 1, k), lambda i, j: (layer, 0, 0)),
            pl.BlockSpec((None, k, tn), lambda i, j: (layer, 0, j)),
            pl.BlockSpec((None, k, tn), lambda i, j: (layer, 0, j)),
        ],
        out_specs=pl.BlockSpec((tm, tn), lambda i, j: (i, j)),
        scratch_shapes=[pltpu.VMEM((tm, k), BF16)],
        compiler_params=_cparams(("arbitrary", "arbitrary"), est),
        name=name,
    )(x, g, wg, wu)


def _rope_table_kernel(pos_ref, inv_ref, sign_ref, cs_ref):
    ang = pos_ref[...].astype(F32) * inv_ref[...]
    lane = lax.broadcasted_iota(jnp.int32, ang.shape, 1)
    cs_ref[...] = jnp.where(lane < B_ROPE, jnp.cos(ang), jnp.sin(ang) * sign_ref[...])


def _rope_table(pos_col, inv_row, sign_row, *, tm):
    m = pos_col.shape[0]
    return pl.pallas_call(
        _rope_table_kernel,
        out_shape=jax.ShapeDtypeStruct((m, V7X_LANES), F32),
        grid=(m // tm,),
        in_specs=[
            pl.BlockSpec((tm, 1), lambda i: (i, 0)),
            pl.BlockSpec((1, V7X_LANES), lambda i: (0, 0)),
            pl.BlockSpec((1, V7X_LANES), lambda i: (0, 0)),
        ],
        out_specs=pl.BlockSpec((tm, V7X_LANES), lambda i: (i, 0)),
        compiler_params=_cparams(("arbitrary",), 32 << 20),
        name="rope_table",
    )(pos_col, inv_row, sign_row)


def _rope_pair_sum(slot, cs):
    t = slot * cs
    return t + pltpu.roll(t, B_ROPE, axis=1)


def _mla_prep_kernel(ql_ref, rope_ref, kvl_ref, cs_ref, gq_ref, gkv_ref, wq_ref, wkv_ref,
                     qt_ref, kn_ref, vt_ref, kpe_ref, *, q_scale):
    cs = cs_ref[...]
    hq = _rms_rows(ql_ref[...].astype(F32), gq_ref[...]).astype(BF16)
    qf = jnp.dot(hq, wq_ref[...], preferred_element_type=F32)
    for h in range(B_HEADS):
        base = h * B_QK
        nope = qf[:, base:base + B_NOPE]
        pe = _rope_pair_sum(qf[:, base + B_NOPE:base + B_QK], cs)
        qt_ref[base:base + B_NOPE, :] = (nope * q_scale).T.astype(BF16)
        qt_ref[base + B_NOPE:base + B_QK, :] = (pe * q_scale).T.astype(BF16)

    hkv = _rms_rows(kvl_ref[...].astype(F32), gkv_ref[...]).astype(BF16)
    kvf = jnp.dot(hkv, wkv_ref[...], preferred_element_type=F32)
    kn_ref[...] = kvf[:, :B_HEADS * B_NOPE].astype(BF16)
    vt_ref[...] = kvf[:, B_HEADS * B_NOPE:].T.astype(BF16)

    kpe = _rope_pair_sum(rope_ref[...].astype(F32), cs)
    lane = lax.broadcasted_iota(jnp.int32, kpe.shape, 1)
    kpe_ref[...] = jnp.where(lane < B_ROPE, kpe, 0.0).astype(BF16)


def _mla_prep(proj, cs, gq, gkv, wq, wkv, layer, *, seq, tm, name):
    m = proj.shape[0]
    tiles_per_seq = seq // tm
    q_scale = (B_NOPE + B_ROPE) ** -0.5 * LOG2E
    est = (2 * tm * (Q_LORA + KV_LORA + 128) * 2 + 2 * tm * 128 * 4
           + 2 * (Q_LORA + KV_LORA) * 2048 * 2
           + 2 * tm * (2048 + 1024 + 1024 + 128) * 2 + 3 * tm * 2048 * 4 + (4 << 20))
    return pl.pallas_call(
        functools.partial(_mla_prep_kernel, q_scale=q_scale),
        out_shape=(
            jax.ShapeDtypeStruct((m // seq * B_HEADS * B_QK, seq), BF16),
            jax.ShapeDtypeStruct((m, B_HEADS * B_NOPE), BF16),
            jax.ShapeDtypeStruct((m // seq * B_WIDTH, seq), BF16),
            jax.ShapeDtypeStruct((m, V7X_LANES), BF16),
        ),
        grid=(m // tm,),
        in_specs=[
            pl.BlockSpec((tm, Q_LORA), lambda i: (i, IN_QLAT_OFF // Q_LORA)),
            pl.BlockSpec((tm, V7X_LANES), lambda i: (i, IN_ROPE_OFF // V7X_LANES)),
            pl.BlockSpec((tm, KV_LORA), lambda i: (i, IN_KVLAT_OFF // KV_LORA)),
            pl.BlockSpec((tm, V7X_LANES), lambda i: (i, 0)),
            pl.BlockSpec((None, 1, Q_LORA), lambda i: (layer, 0, 0)),
            pl.BlockSpec((None, 1, KV_LORA), lambda i: (layer, 0, 0)),
            pl.BlockSpec((None, Q_LORA, B_HEADS * B_QK), lambda i: (layer, 0, 0)),
            pl.BlockSpec((None, KV_LORA, 2 * B_WIDTH), lambda i: (layer, 0, 0)),
        ],
        out_specs=(
            pl.BlockSpec((B_HEADS * B_QK, tm), lambda i: (i // tiles_per_seq, i % tiles_per_seq)),
            pl.BlockSpec((tm, B_HEADS * B_NOPE), lambda i: (i, 0)),
            pl.BlockSpec((B_WIDTH, tm), lambda i: (i // tiles_per_seq, i % tiles_per_seq)),
            pl.BlockSpec((tm, V7X_LANES), lambda i: (i, 0)),
        ),
        compiler_params=_cparams(("arbitrary",), est),
        name=name,
    )(proj, proj, proj, cs, gq, gkv, wq, wkv)


def _attn_a_kernel(q_ref, k0_ref, k1_ref, k2_ref, v0_ref, v1_ref, v2_ref, bias_ref, o_ref):
    d, tq = A_HEAD_DIM, A_TQ
    scale = d ** -0.5 * LOG2E

    def scores(g):
        ps = slice(2 * g * d, (2 * g + 2) * d)
        k = jnp.concatenate([k0_ref[:, ps], k1_ref[:, ps], k2_ref[:, ps]], axis=0)
        qt = q_ref[:, ps].T
        zero = jnp.zeros((d, tq), qt.dtype)
        qd = jnp.concatenate([jnp.concatenate([qt[:d], zero], axis=1),
                              jnp.concatenate([zero, qt[d:]], axis=1)], axis=0)
        return jnp.dot(k, qd, preferred_element_type=F32)

    st_next = scores(0)
    for g in range(A_HEADS // 2):
        ps = slice(2 * g * d, (2 * g + 2) * d)
        st = st_next
        if g + 1 < A_HEADS // 2:
            st_next = scores(g + 1)
        v = jnp.concatenate([v0_ref[:, ps], v1_ref[:, ps], v2_ref[:, ps]], axis=0)
        st = st * scale + bias_ref[g]
        m = _col_reduce(st, jnp.max)
        p = jnp.exp2(st - m)
        l = _col_reduce(p, jnp.sum)
        pb = p.astype(BF16)
        for hh in range(2):
            ot = lax.dot_general(v[:, hh * d:(hh + 1) * d], pb[:, hh * tq:(hh + 1) * tq],
                                 (((0,), (0,)), ((), ())), preferred_element_type=F32)
            ot = ot / l[:, hh * tq:(hh + 1) * tq]
            o_ref[:, (2 * g + hh) * d:(2 * g + hh + 1) * d] = ot.T.astype(o_ref.dtype)


def _attn_a(proj, bias, layer, *, batch, seq, name):
    m = proj.shape[0]
    nq = seq // A_TQ
    qspec = pl.BlockSpec((A_TQ, A_WIDTH), lambda b, i: (b * nq + i, 0))

    def kv_spec(col_blk, c):
        back = A_KBLKS - 1 - c
        return pl.BlockSpec((A_TQ, A_WIDTH),
                            lambda b, i: (b * nq + jnp.maximum(i - back, 0), col_blk))

    est = 2 * 7 * A_TQ * A_WIDTH * 2 + 2 * A_HEADS * A_TQ * A_WIN * 4 + 2 * A_TQ * A_WIDTH * 2 + (8 << 20)
    return pl.pallas_call(
        _attn_a_kernel,
        out_shape=jax.ShapeDtypeStruct((m, A_WIDTH), BF16),
        grid=(batch, nq),
        in_specs=[qspec,
                  kv_spec(1, 0), kv_spec(1, 1), kv_spec(1, 2),
                  kv_spec(2, 0), kv_spec(2, 1), kv_spec(2, 2),
                  pl.BlockSpec((A_HEADS // 2, None, A_WIN, 2 * A_TQ),
                               lambda b, i: (layer, jnp.minimum(i, A_KBLKS - 1), 0, 0))],
        out_specs=pl.BlockSpec((A_TQ, A_WIDTH), lambda b, i: (b * nq + i, 0)),
        compiler_params=_cparams(("arbitrary", "arbitrary"), est),
        name=name,
    )(proj, proj, proj, proj, proj, proj, proj, bias)


def _col_reduce(x, op):
    rows, cols = x.shape
    slabs = op(x.reshape(B_RED_SLABS, rows // B_RED_SLABS, cols), axis=0)
    return op(slabs, axis=0, keepdims=True)


def _attn_b_kernel(q_ref, kn_ref, kpe_ref, vt_ref, o_ref, m_ref, l_ref, acc_ref, st_ref):
    i = pl.program_id(2)
    hp = B_HEADS_PER_STEP
    m_ref[...] = jnp.full(m_ref.shape, -jnp.inf, F32)
    l_ref[...] = jnp.zeros(l_ref.shape, F32)
    acc_ref[...] = jnp.zeros(acc_ref.shape, F32)

    def scores(j, h):
        r0 = pl.multiple_of(j * B_TK, B_TK)
        k = jnp.concatenate([kn_ref[pl.ds(r0, B_TK), h * B_NOPE:(h + 1) * B_NOPE],
                             kpe_ref[pl.ds(r0, B_TK), :]], axis=1)
        return jnp.dot(k, q_ref[h * B_QK:(h + 1) * B_QK, :], preferred_element_type=F32)

    def softmax_pv(j, h, st, allowed):
        r0 = pl.multiple_of(j * B_TK, B_TK)
        if allowed is not None:
            st = jnp.where(allowed, st, -jnp.inf)
        m_prev = m_ref[h]
        m_new = jnp.maximum(m_prev, _col_reduce(st, jnp.max))
        alpha = jnp.exp2(m_prev - m_new)
        p = jnp.exp2(st - m_new)
        l_ref[h] = alpha * l_ref[h] + _col_reduce(p, jnp.sum)
        pv = jnp.dot(vt_ref[h * B_V:(h + 1) * B_V, pl.ds(r0, B_TK)], p.astype(BF16),
                     preferred_element_type=F32)
        acc_ref[h] = alpha * acc_ref[h] + pv
        m_ref[h] = m_new

    def key_block(j, masked):
        allowed = None
        if masked:
            krow = lax.broadcasted_iota(jnp.int32, (B_TK, B_TQ), 0)
            qcol = lax.broadcasted_iota(jnp.int32, (B_TK, B_TQ), 1)
            allowed = (krow // CHUNK) <= (qcol // CHUNK)
        st = st_ref[...]
        for h in range(hp):
            if h + 1 < hp:
                st_next = scores(j, h + 1)
            elif not masked:
                st_ref[...] = scores(j + 1, 0)
            softmax_pv(j, h, st, allowed)
            if h + 1 < hp:
                st = st_next

    def body(j, carry):
        key_block(j, False)
        return carry

    st_ref[...] = scores(0, 0)
    lax.fori_loop(0, i, body, 0)
    key_block(i, True)

    for h in range(hp):
        o_ref[:, h * B_V:(h + 1) * B_V] = (acc_ref[h] / l_ref[h]).T.astype(o_ref.dtype)


def _attn_b(qt, kn, kpe, vt, *, batch, seq, name):
    m = kn.shape[0]
    nq = seq // B_TQ
    hp = B_HEADS_PER_STEP
    est = (2 * B_TQ * hp * B_QK * 2 + 2 * seq * hp * (B_NOPE + B_V) * 2 + 2 * seq * 128 * 2
           + 2 * B_TQ * hp * B_V * 2 + hp * (B_V + 16) * B_TQ * 4 + 8 * B_TQ * B_TK * 4 + (4 << 20))
    return pl.pallas_call(
        _attn_b_kernel,
        out_shape=jax.ShapeDtypeStruct((m, B_WIDTH), BF16),
        grid=(batch, B_HEADS // hp, nq),
        in_specs=[
            pl.BlockSpec((hp * B_QK, B_TQ), lambda b, g, i: (b * (B_HEADS // hp) + g, i)),
            pl.BlockSpec((seq, hp * B_NOPE), lambda b, g, i: (b, g)),
            pl.BlockSpec((seq, V7X_LANES), lambda b, g, i: (b, 0)),
            pl.BlockSpec((hp * B_V, seq), lambda b, g, i: (b * (B_HEADS // hp) + g, 0)),
        ],
        out_specs=pl.BlockSpec((B_TQ, hp * B_V), lambda b, g, i: (b * nq + i, g)),
        scratch_shapes=[
            pltpu.VMEM((hp, 1, B_TQ), F32),
            pltpu.VMEM((hp, 1, B_TQ), F32),
            pltpu.VMEM((hp, B_V, B_TQ), F32),
            pltpu.VMEM((B_TK, B_TQ), F32),
        ],
        compiler_params=_cparams(("arbitrary", "arbitrary", "arbitrary"), est),
        name=name,
    )(qt, kn, kpe, vt)


def _cross_kernel(x_ref, g_ref, wq_ref, k_ref, v_ref, wo_ref, o_ref, h_ref):
    _norm_into(h_ref, x_ref, g_ref)
    scale = X_HEAD_DIM ** -0.5
    q = (jnp.dot(h_ref[...], wq_ref[...], preferred_element_type=F32) * scale).astype(BF16)
    outs = []
    for h in range(X_HEADS):
        hs = slice(h * X_HEAD_DIM, (h + 1) * X_HEAD_DIM)
        s = lax.dot_general(q[:, hs], k_ref[:, hs], (((1,), (1,)), ((), ())),
                            preferred_element_type=F32)
        m = jnp.max(s, axis=1, keepdims=True)
        p = jnp.exp(s - m)
        l = jnp.sum(p, axis=1, keepdims=True)
        o = jnp.dot(p.astype(BF16), v_ref[:, hs], preferred_element_type=F32)
        outs.append((o / l).astype(BF16))
    o_all = jnp.concatenate(outs, axis=1)
    o_ref[...] = x_ref[...] + jnp.dot(o_all, wo_ref[...], preferred_element_type=F32)


def _cross(x, g, wq, kvm, wo, layer, *, seq, tm, name):
    m, d = x.shape
    blocks_per_batch = seq // tm
    est = (4 * tm * d * 4 + tm * d * 2 + 2 * d * X_WIDTH * 2 + 2 * X_WIDTH * d * 2
           + 4 * N_MEM * X_WIDTH * 2 + 4 * tm * N_MEM * 4 + 2 * tm * d * 4 + (4 << 20))
    return pl.pallas_call(
        _cross_kernel,
        out_shape=jax.ShapeDtypeStruct((m, d), F32),
        grid=(m // tm,),
        in_specs=[
            pl.BlockSpec((tm, d), lambda i: (i, 0)),
            pl.BlockSpec((None, 1, d), lambda i: (layer, 0, 0)),
            pl.BlockSpec((None, d, X_WIDTH), lambda i: (layer, 0, 0)),
            pl.BlockSpec((N_MEM, X_WIDTH), lambda i: (i // blocks_per_batch, 2 * layer)),
            pl.BlockSpec((N_MEM, X_WIDTH), lambda i: (i // blocks_per_batch, 2 * layer + 1)),
            pl.BlockSpec((None, X_WIDTH, d), lambda i: (layer, 0, 0)),
        ],
        out_specs=pl.BlockSpec((tm, d), lambda i: (i, 0)),
        scratch_shapes=[pltpu.VMEM((tm, d), BF16)],
        compiler_params=_cparams(("arbitrary",), est),
        name=name,
    )(x, g, wq, kvm, kvm, wo)


def _final_norm_kernel(x_ref, g_ref, o_ref):
    o_ref[...] = _rms_rows(x_ref[...], g_ref[...])


def _final_norm(x, g, *, tm):
    m, d = x.shape
    return pl.pallas_call(
        _final_norm_kernel,
        out_shape=jax.ShapeDtypeStruct((m, d), F32),
        grid=(m // tm,),
        in_specs=[pl.BlockSpec((tm, d), lambda i: (i, 0)), pl.BlockSpec((1, d), lambda i: (0, 0))],
        out_specs=pl.BlockSpec((tm, d), lambda i: (i, 0)),
        compiler_params=_cparams(("arbitrary",), 6 * tm * d * 4 + (4 << 20)),
        name="final_norm",
    )(x, g)


def _prep_weights(w_in, w_uq, w_ukv, w_xkv):
    swap = (jnp.arange(B_ROPE) + B_ROPE // 2) % B_ROPE
    lat_end = IN_ROPE_OFF + KV_LORA
    w_in_b = w_in.astype(BF16)
    krope = w_in_b[:, :, lat_end:]
    pad = jnp.zeros((DEPTH, D_MODEL, IN_KVLAT_OFF - IN_ROPE_OFF - 2 * B_ROPE), BF16)
    w_in_p = jnp.concatenate([w_in_b[:, :, :IN_ROPE_OFF], krope, krope[:, :, swap], pad,
                              w_in_b[:, :, IN_ROPE_OFF:lat_end]], axis=-1)

    uq = w_uq.reshape(DEPTH, Q_LORA, B_HEADS, B_NOPE + B_ROPE)
    uq_pe = uq[..., B_NOPE:]
    wq_p = jnp.concatenate([uq[..., :B_NOPE], uq_pe, uq_pe[..., swap]], axis=-1)
    wq_p = wq_p.reshape(DEPTH, Q_LORA, B_HEADS * B_QK).astype(BF16)

    ukv = w_ukv.reshape(DEPTH, KV_LORA, B_HEADS, B_NOPE + B_V)
    wkv_p = jnp.concatenate([ukv[..., :B_NOPE].reshape(DEPTH, KV_LORA, B_HEADS * B_NOPE),
                             ukv[..., B_NOPE:].reshape(DEPTH, KV_LORA, B_WIDTH)],
                            axis=-1).astype(BF16)

    wxkv_all = jnp.transpose(w_xkv.astype(BF16), (1, 0, 2)).reshape(1, D_MODEL, DEPTH * 2 * X_WIDTH)
    return w_in_p, wq_p, wkv_p, wxkv_all


def _bias_table_kernel(w_ref, o_ref):
    v = pl.program_id(1)
    r = lax.broadcasted_iota(jnp.int32, (A_TQ, A_WIN), 0)
    t = lax.broadcasted_iota(jnp.int32, (A_TQ, A_WIN), 1)
    dchunk = t // CHUNK - r // CHUNK
    valid = (dchunk >= 0) & (dchunk <= LEFT_CHUNKS) & (v - (A_KBLKS - 1) + t // A_TQ >= 0)
    for hh in range(2):
        w = jnp.broadcast_to(w_ref[hh], (A_TQ, A_BIAS_VEC))
        tab = pltpu.roll(w, 0, axis=1, stride=1, stride_axis=0)[:, :A_WIN]
        o_ref[0, 0, :, hh * A_TQ:(hh + 1) * A_TQ] = jnp.where(valid, tab * LOG2E, -jnp.inf).T


def _relpos_bias_tables(rel_bias):
    u = jnp.arange(A_BIAS_VEC)
    delta = jnp.where(u <= A_WIN, u, u - A_BIAS_VEC)
    idx = jnp.clip(LEFT_CHUNKS * CHUNK - delta, -REL_CLIP, REL_CLIP) + REL_CLIP
    vec = rel_bias[:, :, idx].reshape(DEPTH * A_HEADS, 1, A_BIAS_VEC).astype(F32)
    return pl.pallas_call(
        _bias_table_kernel,
        out_shape=jax.ShapeDtypeStruct((DEPTH * A_HEADS // 2, A_KBLKS, A_WIN, 2 * A_TQ), F32),
        grid=(DEPTH * A_HEADS // 2, A_KBLKS),
        in_specs=[pl.BlockSpec((2, 1, A_BIAS_VEC), lambda i, v: (i, 0, 0))],
        out_specs=pl.BlockSpec((1, 1, A_WIN, 2 * A_TQ), lambda i, v: (i, v, 0, 0)),
        compiler_params=_cparams(("arbitrary", "arbitrary"), 32 << 20),
        name="bias_table",
    )(vec)


def kernel(x, mem, positions, norm_mix, w_in, rel_bias, q_norm, kv_norm, w_uq, w_ukv, w_out,
           norm_mem, mem_norm, w_xq, w_xkv, w_xo, norm_ffn, w_gate, w_up, w_down, norm_final):
    b, s, d = x.shape
    m = b * s
    assert (b, s, d) == (4, 4096, D_MODEL) and mem.shape == (b, N_MEM, d)

    w_in_p, wq_p, wkv_p, wxkv_all = _prep_weights(w_in, w_uq, w_ukv, w_xkv)
    w_out_b = w_out.astype(BF16)
    w_xq_b = w_xq.astype(BF16)
    w_xo_b = w_xo.astype(BF16)
    w_down_b = w_down.astype(BF16)
    bias_tab = _relpos_bias_tables(rel_bias)

    half = B_ROPE // 2
    inv = ROPE_THETA ** (-jnp.arange(half, dtype=F32) / half)
    inv_row = jnp.tile(inv, 4)[None, :]
    sign_row = jnp.concatenate([jnp.ones((B_ROPE,), F32), -jnp.ones((half,), F32),
                                jnp.ones((half,), F32)])[None, :]
    cs = _rope_table(positions.reshape(m, 1), inv_row, sign_row, tm=2048)

    kvm = _norm_matmul(mem.reshape(b * N_MEM, d), mem_norm[None, None, :], wxkv_all, 0,
                       tm=b * N_MEM, tn=1024, name="mem_kv")

    norm_mix, norm_mem, norm_ffn, q_norm, kv_norm = (
        g[:, None, :] for g in (norm_mix, norm_mem, norm_ffn, q_norm, kv_norm))
    xf = x.reshape(m, d)
    for l in range(DEPTH):
        proj = _norm_matmul(xf, norm_mix, w_in_p, l, tm=1024, tn=1536, name=f"in_proj_{l}")
        oa = _attn_a(proj, bias_tab, l, batch=b, seq=s, name=f"attn_a_{l}")
        qt, kn, vt, kpe = _mla_prep(proj, cs, q_norm, kv_norm, wq_p, wkv_p, l, seq=s, tm=512,
                                    name=f"mla_prep_{l}")
        ob = _attn_b(qt, kn, kpe, vt, batch=b, seq=s, name=f"attn_b_{l}")
        xf = _mm_res([oa, ob], w_out_b, xf, l, tm=1024, tn=1024, name=f"out_proj_{l}")
        xf = _cross(xf, norm_mem, w_xq_b, kvm, w_xo_b, l, seq=s, tm=512, name=f"cross_{l}")
        act = _swiglu_gu(xf, norm_ffn, w_gate, w_up, l, tm=1024, tn=512, name=f"swiglu_gu_{l}")
        xf = _mm_res([act], w_down_b, xf, l, tm=1024, tn=512, name=f"swiglu_down_{l}")
    out = _final_norm(xf, norm_final[None, :], tm=512)
    return out.reshape(b, s, d)
```

```python
import functools
import math

import jax
import jax.numpy as jnp
from jax import lax
from jax.experimental import pallas as pl
from jax.experimental.pallas import tpu as pltpu

D_MODEL = 2048
DEPTH = 4
CHUNK = 64
LEFT_CHUNKS = 8
A_HEAD_DIM = 128
A_HEADS = 8
A_WIDTH = A_HEADS * A_HEAD_DIM
REL_CLIP = 128
B_NOPE = 128
B_ROPE = 64
B_V = 128
B_HEADS = 8
B_WIDTH = B_HEADS * B_V
Q_LORA = 768
KV_LORA = 512
ROPE_THETA = 10000.0
N_MEM = 256
X_HEADS = 4
X_HEAD_DIM = 128
X_WIDTH = X_HEADS * X_HEAD_DIM
D_FF = 5632
EPS = 1e-6

V7X_LANES = 128
V7X_VMEM_BYTES = 64 * 1024 * 1024
V7X_VMEM_REQUEST_CAP = 56 * 1024 * 1024

BF16 = jnp.bfloat16
F32 = jnp.float32

IN_QLAT_OFF = 3 * A_WIDTH
IN_ROPE_OFF = IN_QLAT_OFF + Q_LORA
IN_KVLAT_OFF = IN_ROPE_OFF + 2 * V7X_LANES
IN_PAD_WIDTH = IN_KVLAT_OFF + KV_LORA

A_TQ = 4 * CHUNK
A_WIN = A_TQ + LEFT_CHUNKS * CHUNK
A_KBLKS = A_WIN // A_TQ
A_BIAS_VEC = 1024

B_TQ = 512
B_TK = 512
B_HEADS_PER_STEP = 4
B_LOOKAHEAD = 2
B_RED_SLABS = 8
B_QK = B_NOPE + 2 * B_ROPE

LOG2E = math.log2(math.e)


def _cparams(sem, vmem_bytes):
    limit = int(min(max(vmem_bytes, 32 * 1024 * 1024), V7X_VMEM_REQUEST_CAP))
    return pltpu.CompilerParams(dimension_semantics=sem, vmem_limit_bytes=limit)


def _rms_rows(x, g):
    ms = jnp.mean(x * x, axis=-1, keepdims=True)
    return (x * lax.rsqrt(ms + EPS)) * g


def _norm_into(h_ref, x_ref, g_ref, rows=128):
    n = x_ref.shape[0] // rows

    def body(c, carry):
        r0 = pl.multiple_of(c * rows, rows)
        x = x_ref[pl.ds(r0, rows), :].astype(F32)
        h_ref[pl.ds(r0, rows), :] = _rms_rows(x, g_ref[...]).astype(h_ref.dtype)
        return carry

    lax.fori_loop(0, n, body, 0)


def _norm_matmul_kernel(x_ref, g_ref, w_ref, o_ref, h_ref):
    @pl.when(pl.program_id(1) == 0)
    def _():
        _norm_into(h_ref, x_ref, g_ref)

    o_ref[...] = jnp.dot(h_ref[...], w_ref[...], preferred_element_type=F32).astype(o_ref.dtype)


def _norm_matmul(x, g, w, layer, *, tm, tn, name):
    m, k = x.shape
    n = w.shape[2]
    xb = x.dtype.itemsize
    est = 2 * tm * k * xb + tm * k * 2 + 2 * k * tn * 2 + 2 * tm * tn * 2 + tm * tn * 4 + (4 << 20)
    return pl.pallas_call(
        _norm_matmul_kernel,
        out_shape=jax.ShapeDtypeStruct((m, n), BF16),
        grid=(m // tm, n // tn),
        in_specs=[
            pl.BlockSpec((tm, k), lambda i, j: (i, 0)),
            pl.BlockSpec((None, 1, k), lambda i, j: (layer, 0, 0)),
            pl.BlockSpec((None, k, tn), lambda i, j: (layer, 0, j)),
        ],
        out_specs=pl.BlockSpec((tm, tn), lambda i, j: (i, j)),
        scratch_shapes=[pltpu.VMEM((tm, k), BF16)],
        compiler_params=_cparams(("arbitrary", "arbitrary"), est),
        name=name,
    )(x, g, w)


def _mem_kv_kernel(x_ref, g_ref, w_ref, o_ref, h_ref):
    @pl.when(pl.program_id(0) == 0)
    def _():
        _norm_into(h_ref, x_ref, g_ref)

    o_ref[...] = jnp.dot(h_ref[...], w_ref[...].astype(BF16),
                         preferred_element_type=F32).astype(o_ref.dtype)


def _mem_kv(mem2d, g, w_xkv):
    m, k = mem2d.shape
    layers, _, n = w_xkv.shape
    est = 2 * m * k * 4 + m * k * 2 + 2 * k * n * 4 + k * n * 2 + 2 * m * n * 2 + m * n * 4 + (4 << 20)
    return pl.pallas_call(
        _mem_kv_kernel,
        out_shape=jax.ShapeDtypeStruct((m, layers * n), BF16),
        grid=(layers,),
        in_specs=[
            pl.BlockSpec((m, k), lambda l: (0, 0)),
            pl.BlockSpec((1, k), lambda l: (0, 0)),
            pl.BlockSpec((None, k, n), lambda l: (l, 0, 0)),
        ],
        out_specs=pl.BlockSpec((m, n), lambda l: (0, l)),
        scratch_shapes=[pltpu.VMEM((m, k), BF16)],
        compiler_params=_cparams(("arbitrary",), est),
        name="mem_kv",
    )(mem2d, g, w_xkv)


def _mm_res_kernel(*refs, n_in):
    a_refs = refs[:n_in]
    w_refs = refs[n_in:2 * n_in]
    r_ref = refs[2 * n_in]
    o_ref = refs[2 * n_in + 1]
    acc = r_ref[...]
    for a_ref, w_ref in zip(a_refs, w_refs):
        acc = acc + jnp.dot(a_ref[...], w_ref[...], preferred_element_type=F32)
    o_ref[...] = acc


def _mm_res(a_list, w, res, layer, *, tm, tn, name):
    m, n = res.shape
    n_in = len(a_list)
    in_specs = []
    w_specs = []
    est = (4 + 2 * n_in) * tm * tn * 4 + (4 << 20)
    for idx, a in enumerate(a_list):
        k = a.shape[1]
        assert w.shape[1] == n_in * k
        in_specs.append(pl.BlockSpec((tm, k), lambda i, j: (i, 0)))
        w_specs.append(pl.BlockSpec((None, k, tn), lambda i, j, idx=idx: (layer, idx, j)))
        est += 2 * tm * k * 2 + 2 * k * tn * 2
    in_specs += w_specs
    in_specs.append(pl.BlockSpec((tm, tn), lambda i, j: (i, j)))
    return pl.pallas_call(
        functools.partial(_mm_res_kernel, n_in=n_in),
        out_shape=jax.ShapeDtypeStruct((m, n), F32),
        grid=(m // tm, n // tn),
        in_specs=in_specs,
        out_specs=pl.BlockSpec((tm, tn), lambda i, j: (i, j)),
        compiler_params=_cparams(("arbitrary", "arbitrary"), est),
        name=name,
    )(*a_list, *([w] * n_in), res)


def _swiglu_gu_kernel(x_ref, g_ref, wg_ref, wu_ref, o_ref, h_ref):
    @pl.when(pl.program_id(1) == 0)
    def _():
        _norm_into(h_ref, x_ref, g_ref)

    h = h_ref[...]
    gate = jnp.dot(h, wg_ref[...].astype(BF16), preferred_element_type=F32)
    up = jnp.dot(h, wu_ref[...].astype(BF16), preferred_element_type=F32)
    o_ref[...] = (gate * (1.0 / (1.0 + jnp.exp(-gate))) * up).astype(o_ref.dtype)


def _swiglu_gu(x, g, wg, wu, layer, *, tm, tn, name):
    m, k = x.shape
    n = wg.shape[2]
    wbytes = wg.dtype.itemsize
    est = (2 * tm * k * 4 + tm * k * 2 + 4 * k * tn * wbytes + 2 * k * tn * 2 + 2 * tm * tn * 2
           + 3 * tm * tn * 4 + (4 << 20))
    return pl.pallas_call(
        _swiglu_gu_kernel,
        out_shape=jax.ShapeDtypeStruct((m, n), BF16),
        grid=(m // tm, n // tn),
        in_specs=[
            pl.BlockSpec((tm, k), lambda i, j: (i, 0)),
            pl.BlockSpec((None, 1, k), lambda i, j: (layer, 0, 0)),
            pl.BlockSpec((None, k, tn), lambda i, j: (layer, 0, j)),
            pl.BlockSpec((None, k, tn), lambda i, j: (layer, 0, j)),
        ],
        out_specs=pl.BlockSpec((tm, tn), lambda i, j: (i, j)),
        scratch_shapes=[pltpu.VMEM((tm, k), BF16)],
        compiler_params=_cparams(("arbitrary", "arbitrary"), est),
        name=name,
    )(x, g, wg, wu)


def _rope_table_kernel(pos_ref, inv_ref, sign_ref, cs_ref):
    ang = pos_ref[...].astype(F32) * inv_ref[...]
    lane = lax.broadcasted_iota(jnp.int32, ang.shape, 1)
    cs_ref[...] = jnp.where(lane < B_ROPE, jnp.cos(ang), jnp.sin(ang) * sign_ref[...])


def _rope_table(pos_col, inv_row, sign_row, *, tm):
    m = pos_col.shape[0]
    return pl.pallas_call(
        _rope_table_kernel,
        out_shape=jax.ShapeDtypeStruct((m, V7X_LANES), F32),
        grid=(m // tm,),
        in_specs=[
            pl.BlockSpec((tm, 1), lambda i: (i, 0)),
            pl.BlockSpec((1, V7X_LANES), lambda i: (0, 0)),
            pl.BlockSpec((1, V7X_LANES), lambda i: (0, 0)),
        ],
        out_specs=pl.BlockSpec((tm, V7X_LANES), lambda i: (i, 0)),
        compiler_params=_cparams(("arbitrary",), 32 << 20),
        name="rope_table",
    )(pos_col, inv_row, sign_row)


def _rope_pair_sum(slot, cs):
    t = slot * cs
    return t + pltpu.roll(t, B_ROPE, axis=1)


def _mla_prep_kernel(ql_ref, rope_ref, kvl_ref, cs_ref, gq_ref, gkv_ref, wq_ref, wkv_ref,
                     qt_ref, kn_ref, vt_ref, kpe_ref, *, q_scale):
    cs = cs_ref[...]
    hq = _rms_rows(ql_ref[...].astype(F32), gq_ref[...]).astype(BF16)
    qf = jnp.dot(hq, wq_ref[...], preferred_element_type=F32)
    for h in range(B_HEADS):
        base = h * B_QK
        nope = qf[:, base:base + B_NOPE]
        pe = _rope_pair_sum(qf[:, base + B_NOPE:base + B_QK], cs)
        qt_ref[base:base + B_NOPE, :] = (nope * q_scale).T.astype(BF16)
        qt_ref[base + B_NOPE:base + B_QK, :] = (pe * q_scale).T.astype(BF16)

    hkv = _rms_rows(kvl_ref[...].astype(F32), gkv_ref[...]).astype(BF16)
    kvf = jnp.dot(hkv, wkv_ref[...], preferred_element_type=F32)
    kn_ref[...] = kvf[:, :B_HEADS * B_NOPE].astype(BF16)
    vt_ref[...] = kvf[:, B_HEADS * B_NOPE:].T.astype(BF16)

    kpe = _rope_pair_sum(rope_ref[...].astype(F32), cs)
    lane = lax.broadcasted_iota(jnp.int32, kpe.shape, 1)
    kpe_ref[...] = jnp.where(lane < B_ROPE, kpe, 0.0).astype(BF16)


def _mla_prep(proj, cs, gq, gkv, wq, wkv, layer, *, seq, tm, name):
    m = proj.shape[0]
    tiles_per_seq = seq // tm
    q_scale = (B_NOPE + B_ROPE) ** -0.5 * LOG2E
    est = (2 * tm * (Q_LORA + KV_LORA + 128) * 2 + 2 * tm * 128 * 4
           + 2 * (Q_LORA + KV_LORA) * 2048 * 2
           + 2 * tm * (2048 + 1024 + 1024 + 128) * 2 + 3 * tm * 2048 * 4 + (4 << 20))
    return pl.pallas_call(
        functools.partial(_mla_prep_kernel, q_scale=q_scale),
        out_shape=(
            jax.ShapeDtypeStruct((m // seq * B_HEADS * B_QK, seq), BF16),
            jax.ShapeDtypeStruct((m, B_HEADS * B_NOPE), BF16),
            jax.ShapeDtypeStruct((m // seq * B_WIDTH, seq), BF16),
            jax.ShapeDtypeStruct((m, V7X_LANES), BF16),
        ),
        grid=(m // tm,),
        in_specs=[
            pl.BlockSpec((tm, Q_LORA), lambda i: (i, IN_QLAT_OFF // Q_LORA)),
            pl.BlockSpec((tm, V7X_LANES), lambda i: (i, IN_ROPE_OFF // V7X_LANES)),
            pl.BlockSpec((tm, KV_LORA), lambda i: (i, IN_KVLAT_OFF // KV_LORA)),
            pl.BlockSpec((tm, V7X_LANES), lambda i: (i, 0)),
            pl.BlockSpec((None, 1, Q_LORA), lambda i: (layer, 0, 0)),
            pl.BlockSpec((None, 1, KV_LORA), lambda i: (layer, 0, 0)),
            pl.BlockSpec((None, Q_LORA, B_HEADS * B_QK), lambda i: (layer, 0, 0)),
            pl.BlockSpec((None, KV_LORA, 2 * B_WIDTH), lambda i: (layer, 0, 0)),
        ],
        out_specs=(
            pl.BlockSpec((B_HEADS * B_QK, tm), lambda i: (i // tiles_per_seq, i % tiles_per_seq)),
            pl.BlockSpec((tm, B_HEADS * B_NOPE), lambda i: (i, 0)),
            pl.BlockSpec((B_WIDTH, tm), lambda i: (i // tiles_per_seq, i % tiles_per_seq)),
            pl.BlockSpec((tm, V7X_LANES), lambda i: (i, 0)),
        ),
        compiler_params=_cparams(("arbitrary",), est),
        name=name,
    )(proj, proj, proj, cs, gq, gkv, wq, wkv)


def _attn_a_kernel(q_ref, k0_ref, k1_ref, k2_ref, v0_ref, v1_ref, v2_ref, bias_ref, o_ref):
    d, tq = A_HEAD_DIM, A_TQ
    scale = d ** -0.5 * LOG2E

    def scores(g):
        ps = slice(2 * g * d, (2 * g + 2) * d)
        k = jnp.concatenate([k0_ref[:, ps], k1_ref[:, ps], k2_ref[:, ps]], axis=0)
        qt = q_ref[:, ps].T
        zero = jnp.zeros((d, tq), qt.dtype)
        qd = jnp.concatenate([jnp.concatenate([qt[:d], zero], axis=1),
                              jnp.concatenate([zero, qt[d:]], axis=1)], axis=0)
        return jnp.dot(k, qd, preferred_element_type=F32)

    st_next = scores(0)
    for g in range(A_HEADS // 2):
        ps = slice(2 * g * d, (2 * g + 2) * d)
        st = st_next
        if g + 1 < A_HEADS // 2:
            st_next = scores(g + 1)
        v = jnp.concatenate([v0_ref[:, ps], v1_ref[:, ps], v2_ref[:, ps]], axis=0)
        st = st * scale + bias_ref[g]
        m = _col_reduce(st, jnp.max)
        p = jnp.exp2(st - m)
        l = _col_reduce(p, jnp.sum)
        pb = p.astype(BF16)
        for hh in range(2):
            ot = lax.dot_general(v[:, hh * d:(hh + 1) * d], pb[:, hh * tq:(hh + 1) * tq],
                                 (((0,), (0,)), ((), ())), preferred_element_type=F32)
            ot = ot / l[:, hh * tq:(hh + 1) * tq]
            o_ref[:, (2 * g + hh) * d:(2 * g + hh + 1) * d] = ot.T.astype(o_ref.dtype)


def _attn_a(proj, bias, layer, *, batch, seq, name):
    m = proj.shape[0]
    nq = seq // A_TQ
    qspec = pl.BlockSpec((A_TQ, A_WIDTH), lambda b, i: (b * nq + i, 0))

    def kv_spec(col_blk, c):
        back = A_KBLKS - 1 - c
        return pl.BlockSpec((A_TQ, A_WIDTH),
                            lambda b, i: (b * nq + jnp.maximum(i - back, 0), col_blk))

    est = 2 * 7 * A_TQ * A_WIDTH * 2 + 2 * A_HEADS * A_TQ * A_WIN * 4 + 2 * A_TQ * A_WIDTH * 2 + (8 << 20)
    return pl.pallas_call(
        _attn_a_kernel,
        out_shape=jax.ShapeDtypeStruct((m, A_WIDTH), BF16),
        grid=(batch, nq),
        in_specs=[qspec,
                  kv_spec(1, 0), kv_spec(1, 1), kv_spec(1, 2),
                  kv_spec(2, 0), kv_spec(2, 1), kv_spec(2, 2),
                  pl.BlockSpec((A_HEADS // 2, None, A_WIN, 2 * A_TQ),
                               lambda b, i: (layer, jnp.minimum(i, A_KBLKS - 1), 0, 0))],
        out_specs=pl.BlockSpec((A_TQ, A_WIDTH), lambda b, i: (b * nq + i, 0)),
        compiler_params=_cparams(("arbitrary", "arbitrary"), est),
        name=name,
    )(proj, proj, proj, proj, proj, proj, proj, bias)


def _col_reduce(x, op):
    rows, cols = x.shape
    slabs = op(x.reshape(B_RED_SLABS, rows // B_RED_SLABS, cols), axis=0)
    return op(slabs, axis=0, keepdims=True)


def _attn_b_kernel(q_ref, kn_ref, kpe_ref, vt_ref, o_ref, m_ref, l_ref, acc_ref, st_ref):
    i = pl.program_id(2)
    hp = B_HEADS_PER_STEP
    m_ref[...] = jnp.full(m_ref.shape, -jnp.inf, F32)
    l_ref[...] = jnp.zeros(l_ref.shape, F32)
    acc_ref[...] = jnp.zeros(acc_ref.shape, F32)

    def scores(j, h):
        r0 = pl.multiple_of(j * B_TK, B_TK)
        k = jnp.concatenate([kn_ref[pl.ds(r0, B_TK), h * B_NOPE:(h + 1) * B_NOPE],
                             kpe_ref[pl.ds(r0, B_TK), :]], axis=1)
        return jnp.dot(k, q_ref[h * B_QK:(h + 1) * B_QK, :], preferred_element_type=F32)

    def softmax_pv(j, h, st, allowed):
        r0 = pl.multiple_of(j * B_TK, B_TK)
        if allowed is not None:
            st = jnp.where(allowed, st, -jnp.inf)
        m_prev = m_ref[h]
        m_new = jnp.maximum(m_prev, _col_reduce(st, jnp.max))
        alpha = jnp.exp2(m_prev - m_new)
        p = jnp.exp2(st - m_new)
        l_ref[h] = alpha * l_ref[h] + _col_reduce(p, jnp.sum)
        pv = jnp.dot(vt_ref[h * B_V:(h + 1) * B_V, pl.ds(r0, B_TK)], p.astype(BF16),
                     preferred_element_type=F32)
        acc_ref[h] = alpha * acc_ref[h] + pv
        m_ref[h] = m_new

    def key_block(j, masked):
        allowed = None
        if masked:
            krow = lax.broadcasted_iota(jnp.int32, (B_TK, B_TQ), 0)
            qcol = lax.broadcasted_iota(jnp.int32, (B_TK, B_TQ), 1)
            allowed = (krow // CHUNK) <= (qcol // CHUNK)
        look = B_LOOKAHEAD
        tiles = [st_ref[h] for h in range(look)] + [None] * (hp - look)
        for h in range(hp):
            ahead = h + look
            if ahead < hp:
                tiles[ahead] = scores(j, ahead)
            elif not masked:
                st_ref[ahead - hp] = scores(j + 1, ahead - hp)
            softmax_pv(j, h, tiles[h], allowed)
            tiles[h] = None

    def body(j, carry):
        key_block(j, False)
        return carry

    for h in range(B_LOOKAHEAD):
        st_ref[h] = scores(0, h)
    lax.fori_loop(0, i, body, 0)
    key_block(i, True)

    for h in range(hp):
        o_ref[:, h * B_V:(h + 1) * B_V] = (acc_ref[h] / l_ref[h]).T.astype(o_ref.dtype)


def _attn_b(qt, kn, kpe, vt, *, batch, seq, name):
    m = kn.shape[0]
    nq = seq // B_TQ
    hp = B_HEADS_PER_STEP
    est = (2 * B_TQ * hp * B_QK * 2 + 2 * seq * hp * (B_NOPE + B_V) * 2 + 2 * seq * 128 * 2
           + 2 * B_TQ * hp * B_V * 2 + hp * (B_V + 16) * B_TQ * 4 + 8 * B_TQ * B_TK * 4 + (4 << 20))
    return pl.pallas_call(
        _attn_b_kernel,
        out_shape=jax.ShapeDtypeStruct((m, B_WIDTH), BF16),
        grid=(batch, B_HEADS // hp, nq),
        in_specs=[
            pl.BlockSpec((hp * B_QK, B_TQ), lambda b, g, i: (b * (B_HEADS // hp) + g, i)),
            pl.BlockSpec((seq, hp * B_NOPE), lambda b, g, i: (b, g)),
            pl.BlockSpec((seq, V7X_LANES), lambda b, g, i: (b, 0)),
            pl.BlockSpec((hp * B_V, seq), lambda b, g, i: (b * (B_HEADS // hp) + g, 0)),
        ],
        out_specs=pl.BlockSpec((B_TQ, hp * B_V), lambda b, g, i: (b * nq + i, g)),
        scratch_shapes=[
            pltpu.VMEM((hp, 1, B_TQ), F32),
            pltpu.VMEM((hp, 1, B_TQ), F32),
            pltpu.VMEM((hp, B_V, B_TQ), F32),
            pltpu.VMEM((B_LOOKAHEAD, B_TK, B_TQ), F32),
        ],
        compiler_params=_cparams(("arbitrary", "arbitrary", "arbitrary"), est),
        name=name,
    )(qt, kn, kpe, vt)


def _cross_kernel(x_ref, g_ref, wq_ref, k_ref, v_ref, wo_ref, o_ref, h_ref):
    _norm_into(h_ref, x_ref, g_ref)
    scale = X_HEAD_DIM ** -0.5
    q = (jnp.dot(h_ref[...], wq_ref[...], preferred_element_type=F32) * scale).astype(BF16)
    outs = []
    for h in range(X_HEADS):
        hs = slice(h * X_HEAD_DIM, (h + 1) * X_HEAD_DIM)
        s = lax.dot_general(q[:, hs], k_ref[:, hs], (((1,), (1,)), ((), ())),
                            preferred_element_type=F32)
        m = jnp.max(s, axis=1, keepdims=True)
        p = jnp.exp(s - m)
        l = jnp.sum(p, axis=1, keepdims=True)
        o = jnp.dot(p.astype(BF16), v_ref[:, hs], preferred_element_type=F32)
        outs.append((o / l).astype(BF16))
    o_all = jnp.concatenate(outs, axis=1)
    o_ref[...] = x_ref[...] + jnp.dot(o_all, wo_ref[...], preferred_element_type=F32)


def _cross(x, g, wq, kvm, wo, layer, *, seq, tm, name):
    m, d = x.shape
    blocks_per_batch = seq // tm
    est = (4 * tm * d * 4 + tm * d * 2 + 2 * d * X_WIDTH * 2 + 2 * X_WIDTH * d * 2
           + 4 * N_MEM * X_WIDTH * 2 + 4 * tm * N_MEM * 4 + 2 * tm * d * 4 + (4 << 20))
    return pl.pallas_call(
        _cross_kernel,
        out_shape=jax.ShapeDtypeStruct((m, d), F32),
        grid=(m // tm,),
        in_specs=[
            pl.BlockSpec((tm, d), lambda i: (i, 0)),
            pl.BlockSpec((None, 1, d), lambda i: (layer, 0, 0)),
            pl.BlockSpec((None, d, X_WIDTH), lambda i: (layer, 0, 0)),
            pl.BlockSpec((N_MEM, X_WIDTH), lambda i: (i // blocks_per_batch, 2 * layer)),
            pl.BlockSpec((N_MEM, X_WIDTH), lambda i: (i // blocks_per_batch, 2 * layer + 1)),
            pl.BlockSpec((None, X_WIDTH, d), lambda i: (layer, 0, 0)),
        ],
        out_specs=pl.BlockSpec((tm, d), lambda i: (i, 0)),
        scratch_shapes=[pltpu.VMEM((tm, d), BF16)],
        compiler_params=_cparams(("arbitrary",), est),
        name=name,
    )(x, g, wq, kvm, kvm, wo)


def _final_norm_kernel(x_ref, g_ref, o_ref):
    o_ref[...] = _rms_rows(x_ref[...], g_ref[...])


def _final_norm(x, g, *, tm):
    m, d = x.shape
    return pl.pallas_call(
        _final_norm_kernel,
        out_shape=jax.ShapeDtypeStruct((m, d), F32),
        grid=(m // tm,),
        in_specs=[pl.BlockSpec((tm, d), lambda i: (i, 0)), pl.BlockSpec((1, d), lambda i: (0, 0))],
        out_specs=pl.BlockSpec((tm, d), lambda i: (i, 0)),
        compiler_params=_cparams(("arbitrary",), 6 * tm * d * 4 + (4 << 20)),
        name="final_norm",
    )(x, g)


def _w_in_relayout_kernel(w_ref, o_ref):
    rows = w_ref.shape[0]
    lat_end = IN_ROPE_OFF + KV_LORA
    half = B_ROPE // 2
    o_ref[:, :IN_ROPE_OFF] = w_ref[:, :IN_ROPE_OFF].astype(BF16)
    kr = w_ref[:, lat_end:lat_end + B_ROPE]
    slot = jnp.concatenate([kr, kr[:, half:], kr[:, :half],
                            jnp.zeros((rows, IN_KVLAT_OFF - IN_ROPE_OFF - 2 * B_ROPE), F32)], axis=1)
    o_ref[:, IN_ROPE_OFF:IN_KVLAT_OFF] = slot.astype(BF16)
    o_ref[:, IN_KVLAT_OFF:] = w_ref[:, IN_ROPE_OFF:lat_end].astype(BF16)


def _w_in_relayout(w_in, *, tk):
    layers, k, n = w_in.shape
    return pl.pallas_call(
        _w_in_relayout_kernel,
        out_shape=jax.ShapeDtypeStruct((layers, k, IN_PAD_WIDTH), BF16),
        grid=(layers, k // tk),
        in_specs=[pl.BlockSpec((None, tk, n), lambda l, i: (l, i, 0))],
        out_specs=pl.BlockSpec((None, tk, IN_PAD_WIDTH), lambda l, i: (l, i, 0)),
        compiler_params=_cparams(("arbitrary", "arbitrary"), 4 * tk * n * 4 + 4 * tk * IN_PAD_WIDTH * 2),
        name="w_in_relayout",
    )(w_in)


def _prep_weights(w_in, w_uq, w_ukv):
    swap = (jnp.arange(B_ROPE) + B_ROPE // 2) % B_ROPE
    w_in_p = _w_in_relayout(w_in, tk=256)

    uq = w_uq.reshape(DEPTH, Q_LORA, B_HEADS, B_NOPE + B_ROPE)
    uq_pe = uq[..., B_NOPE:]
    wq_p = jnp.concatenate([uq[..., :B_NOPE], uq_pe, uq_pe[..., swap]], axis=-1)
    wq_p = wq_p.reshape(DEPTH, Q_LORA, B_HEADS * B_QK).astype(BF16)

    ukv = w_ukv.reshape(DEPTH, KV_LORA, B_HEADS, B_NOPE + B_V)
    wkv_p = jnp.concatenate([ukv[..., :B_NOPE].reshape(DEPTH, KV_LORA, B_HEADS * B_NOPE),
                             ukv[..., B_NOPE:].reshape(DEPTH, KV_LORA, B_WIDTH)],
                            axis=-1).astype(BF16)

    return w_in_p, wq_p, wkv_p


def _bias_table_kernel(w_ref, o_ref):
    v = pl.program_id(1)
    r = lax.broadcasted_iota(jnp.int32, (A_TQ, A_WIN), 0)
    t = lax.broadcasted_iota(jnp.int32, (A_TQ, A_WIN), 1)
    dchunk = t // CHUNK - r // CHUNK
    valid = (dchunk >= 0) & (dchunk <= LEFT_CHUNKS) & (v - (A_KBLKS - 1) + t // A_TQ >= 0)
    for hh in range(2):
        w = jnp.broadcast_to(w_ref[hh], (A_TQ, A_BIAS_VEC))
        tab = pltpu.roll(w, 0, axis=1, stride=1, stride_axis=0)[:, :A_WIN]
        o_ref[0, 0, :, hh * A_TQ:(hh + 1) * A_TQ] = jnp.where(valid, tab * LOG2E, -jnp.inf).T


def _relpos_bias_tables(rel_bias):
    u = jnp.arange(A_BIAS_VEC)
    delta = jnp.where(u <= A_WIN, u, u - A_BIAS_VEC)
    idx = jnp.clip(LEFT_CHUNKS * CHUNK - delta, -REL_CLIP, REL_CLIP) + REL_CLIP
    vec = rel_bias[:, :, idx].reshape(DEPTH * A_HEADS, 1, A_BIAS_VEC).astype(F32)
    return pl.pallas_call(
        _bias_table_kernel,
        out_shape=jax.ShapeDtypeStruct((DEPTH * A_HEADS // 2, A_KBLKS, A_WIN, 2 * A_TQ), F32),
        grid=(DEPTH * A_HEADS // 2, A_KBLKS),
        in_specs=[pl.BlockSpec((2, 1, A_BIAS_VEC), lambda i, v: (i, 0, 0))],
        out_specs=pl.BlockSpec((1, 1, A_WIN, 2 * A_TQ), lambda i, v: (i, v, 0, 0)),
        compiler_params=_cparams(("arbitrary", "arbitrary"), 32 << 20),
        name="bias_table",
    )(vec)


def kernel(x, mem, positions, norm_mix, w_in, rel_bias, q_norm, kv_norm, w_uq, w_ukv, w_out,
           norm_mem, mem_norm, w_xq, w_xkv, w_xo, norm_ffn, w_gate, w_up, w_down, norm_final):
    b, s, d = x.shape
    m = b * s
    assert (b, s, d) == (4, 4096, D_MODEL) and mem.shape == (b, N_MEM, d)

    w_in_p, wq_p, wkv_p = _prep_weights(w_in, w_uq, w_ukv)
    w_out_b = w_out.astype(BF16)
    w_xq_b = w_xq.astype(BF16)
    w_xo_b = w_xo.astype(BF16)
    w_down_b = w_down.astype(BF16)
    bias_tab = _relpos_bias_tables(rel_bias)

    half = B_ROPE // 2
    inv = ROPE_THETA ** (-jnp.arange(half, dtype=F32) / half)
    inv_row = jnp.tile(inv, 4)[None, :]
    sign_row = jnp.concatenate([jnp.ones((B_ROPE,), F32), -jnp.ones((half,), F32),
                                jnp.ones((half,), F32)])[None, :]
    cs = _rope_table(positions.reshape(m, 1), inv_row, sign_row, tm=2048)

    kvm = _mem_kv(mem.reshape(b * N_MEM, d), mem_norm[None, :], w_xkv)

    norm_mix, norm_mem, norm_ffn, q_norm, kv_norm = (
        g[:, None, :] for g in (norm_mix, norm_mem, norm_ffn, q_norm, kv_norm))
    xf = x.reshape(m, d)
    for l in range(DEPTH):
        proj = _norm_matmul(xf, norm_mix, w_in_p, l, tm=1024, tn=1536, name=f"in_proj_{l}")
        oa = _attn_a(proj, bias_tab, l, batch=b, seq=s, name=f"attn_a_{l}")
        qt, kn, vt, kpe = _mla_prep(proj, cs, q_norm, kv_norm, wq_p, wkv_p, l, seq=s, tm=512,
                                    name=f"mla_prep_{l}")
        ob = _attn_b(qt, kn, kpe, vt, batch=b, seq=s, name=f"attn_b_{l}")
        xf = _mm_res([oa, ob], w_out_b, xf, l, tm=1024, tn=1024, name=f"out_proj_{l}")
        xf = _cross(xf, norm_mem, w_xq_b, kvm, w_xo_b, l, seq=s, tm=512, name=f"cross_{l}")
        act = _swiglu_gu(xf, norm_ffn, w_gate, w_up, l, tm=1024, tn=512, name=f"swiglu_gu_{l}")
        xf = _mm_res([act], w_down_b, xf, l, tm=1024, tn=512, name=f"swiglu_down_{l}")
    out = _final_norm(xf, norm_final[None, :], tm=512)
    return out.reshape(b, s, d)
```

```python
import functools
import math

import jax
import jax.numpy as jnp
from jax import lax
from jax.experimental import pallas as pl
from jax.experimental.pallas import tpu as pltpu

D_MODEL = 2048
DEPTH = 4
CHUNK = 64
LEFT_CHUNKS = 8
A_HEAD_DIM = 128
A_HEADS = 8
A_WIDTH = A_HEADS * A_HEAD_DIM
REL_CLIP = 128
B_NOPE = 128
B_ROPE = 64
B_V = 128
B_HEADS = 8
B_WIDTH = B_HEADS * B_V
Q_LORA = 768
KV_LORA = 512
ROPE_THETA = 10000.0
N_MEM = 256
X_HEADS = 4
X_HEAD_DIM = 128
X_WIDTH = X_HEADS * X_HEAD_DIM
D_FF = 5632
EPS = 1e-6

V7X_LANES = 128
V7X_VMEM_BYTES = 64 * 1024 * 1024
V7X_VMEM_REQUEST_CAP = 56 * 1024 * 1024

BF16 = jnp.bfloat16
F32 = jnp.float32

IN_QLAT_OFF = 3 * A_WIDTH
IN_ROPE_OFF = IN_QLAT_OFF + Q_LORA
IN_KVLAT_OFF = IN_ROPE_OFF + 2 * V7X_LANES
IN_PAD_WIDTH = IN_KVLAT_OFF + KV_LORA

A_TQ = 4 * CHUNK
A_WIN = A_TQ + LEFT_CHUNKS * CHUNK
A_KBLKS = A_WIN // A_TQ
A_BIAS_VEC = 1024

B_TQ = 512
B_TK = 512
B_HEADS_PER_STEP = 4
B_LOOKAHEAD = 2
B_RED_SLABS = 8
B_QK = B_NOPE + 2 * B_ROPE

LOG2E = math.log2(math.e)
A_SCORE_SCALE = A_HEAD_DIM ** -0.5 * LOG2E


def _cparams(sem, vmem_bytes):
    limit = int(min(max(vmem_bytes, 32 * 1024 * 1024), V7X_VMEM_REQUEST_CAP))
    return pltpu.CompilerParams(dimension_semantics=sem, vmem_limit_bytes=limit)


def _rms_rows(x, g):
    ms = jnp.mean(x * x, axis=-1, keepdims=True)
    return (x * lax.rsqrt(ms + EPS)) * g


def _norm_into(h_ref, x_ref, g_ref, rows=128):
    n = x_ref.shape[0] // rows

    def body(c, carry):
        r0 = pl.multiple_of(c * rows, rows)
        x = x_ref[pl.ds(r0, rows), :].astype(F32)
        h_ref[pl.ds(r0, rows), :] = _rms_rows(x, g_ref[...]).astype(h_ref.dtype)
        return carry

    lax.fori_loop(0, n, body, 0)


def _norm_matmul_kernel(x_ref, g_ref, w_ref, o_ref, h_ref):
    @pl.when(pl.program_id(1) == 0)
    def _():
        _norm_into(h_ref, x_ref, g_ref)

    o_ref[...] = jnp.dot(h_ref[...], w_ref[...], preferred_element_type=F32).astype(o_ref.dtype)


def _norm_matmul(x, g, w, layer, *, tm, tn, name):
    m, k = x.shape
    n = w.shape[2]
    xb = x.dtype.itemsize
    est = 2 * tm * k * xb + tm * k * 2 + 2 * k * tn * 2 + 2 * tm * tn * 2 + tm * tn * 4 + (4 << 20)
    return pl.pallas_call(
        _norm_matmul_kernel,
        out_shape=jax.ShapeDtypeStruct((m, n), BF16),
        grid=(m // tm, n // tn),
        in_specs=[
            pl.BlockSpec((tm, k), lambda i, j: (i, 0)),
            pl.BlockSpec((None, 1, k), lambda i, j: (layer, 0, 0)),
            pl.BlockSpec((None, k, tn), lambda i, j: (layer, 0, j)),
        ],
        out_specs=pl.BlockSpec((tm, tn), lambda i, j: (i, j)),
        scratch_shapes=[pltpu.VMEM((tm, k), BF16)],
        compiler_params=_cparams(("arbitrary", "arbitrary"), est),
        name=name,
    )(x, g, w)


def _mem_kv_kernel(x_ref, g_ref, w_ref, o_ref, h_ref):
    @pl.when(pl.program_id(0) == 0)
    def _():
        _norm_into(h_ref, x_ref, g_ref)

    o_ref[...] = jnp.dot(h_ref[...], w_ref[...].astype(BF16),
                         preferred_element_type=F32).astype(o_ref.dtype)


def _mem_kv(mem2d, g, w_xkv):
    m, k = mem2d.shape
    layers, _, n = w_xkv.shape
    est = 2 * m * k * 4 + m * k * 2 + 2 * k * n * 4 + k * n * 2 + 2 * m * n * 2 + m * n * 4 + (4 << 20)
    return pl.pallas_call(
        _mem_kv_kernel,
        out_shape=jax.ShapeDtypeStruct((m, layers * n), BF16),
        grid=(layers,),
        in_specs=[
            pl.BlockSpec((m, k), lambda l: (0, 0)),
            pl.BlockSpec((1, k), lambda l: (0, 0)),
            pl.BlockSpec((None, k, n), lambda l: (l, 0, 0)),
        ],
        out_specs=pl.BlockSpec((m, n), lambda l: (0, l)),
        scratch_shapes=[pltpu.VMEM((m, k), BF16)],
        compiler_params=_cparams(("arbitrary",), est),
        name="mem_kv",
    )(mem2d, g, w_xkv)


def _mm_res_kernel(*refs, n_in):
    a_refs = refs[:n_in]
    w_refs = refs[n_in:2 * n_in]
    r_ref = refs[2 * n_in]
    o_ref = refs[2 * n_in + 1]
    acc = r_ref[...]
    for a_ref, w_ref in zip(a_refs, w_refs):
        acc = acc + jnp.dot(a_ref[...], w_ref[...], preferred_element_type=F32)
    o_ref[...] = acc


def _mm_res(a_list, w, res, layer, *, tm, tn, name):
    m, n = res.shape
    n_in = len(a_list)
    in_specs = []
    w_specs = []
    est = (4 + 2 * n_in) * tm * tn * 4 + (4 << 20)
    for idx, a in enumerate(a_list):
        k = a.shape[1]
        assert w.shape[1] == n_in * k
        in_specs.append(pl.BlockSpec((tm, k), lambda i, j: (i, 0)))
        w_specs.append(pl.BlockSpec((None, k, tn), lambda i, j, idx=idx: (layer, idx, j)))
        est += 2 * tm * k * 2 + 2 * k * tn * 2
    in_specs += w_specs
    in_specs.append(pl.BlockSpec((tm, tn), lambda i, j: (i, j)))
    return pl.pallas_call(
        functools.partial(_mm_res_kernel, n_in=n_in),
        out_shape=jax.ShapeDtypeStruct((m, n), F32),
        grid=(m // tm, n // tn),
        in_specs=in_specs,
        out_specs=pl.BlockSpec((tm, tn), lambda i, j: (i, j)),
        compiler_params=_cparams(("arbitrary", "arbitrary"), est),
        name=name,
    )(*a_list, *([w] * n_in), res)


def _swiglu_gu_kernel(x_ref, g_ref, wg_ref, wu_ref, wd_ref, o_ref, wdb_ref, h_ref):
    @pl.when(pl.program_id(1) == 0)
    def _():
        _norm_into(h_ref, x_ref, g_ref)

    h = h_ref[...]
    gate = jnp.dot(h, wg_ref[...].astype(BF16), preferred_element_type=F32)
    up = jnp.dot(h, wu_ref[...].astype(BF16), preferred_element_type=F32)
    o_ref[...] = (gate * (1.0 / (1.0 + jnp.exp(-gate))) * up).astype(o_ref.dtype)
    wdb_ref[...] = wd_ref[...].astype(BF16)


def _swiglu_gu(x, g, wg, wu, wd, layer, *, tm, tn, name):
    m, k = x.shape
    n = wg.shape[2]
    steps = (m // tm) * (n // tn)
    nj = n // tn
    wd_rows, wd_cols = wd.shape[1], wd.shape[2]
    slab = wd_rows // steps
    assert slab * steps == wd_rows and slab % 16 == 0
    wbytes = wg.dtype.itemsize
    est = (2 * tm * k * 4 + tm * k * 2 + 4 * k * tn * wbytes + 2 * k * tn * 2 + 2 * tm * tn * 2
           + 3 * tm * tn * 4 + 2 * slab * wd_cols * 6 + (4 << 20))
    return pl.pallas_call(
        _swiglu_gu_kernel,
        out_shape=(jax.ShapeDtypeStruct((m, n), BF16),
                   jax.ShapeDtypeStruct((1, wd_rows, wd_cols), BF16)),
        grid=(m // tm, nj),
        in_specs=[
            pl.BlockSpec((tm, k), lambda i, j: (i, 0)),
            pl.BlockSpec((None, 1, k), lambda i, j: (layer, 0, 0)),
            pl.BlockSpec((None, k, tn), lambda i, j: (layer, 0, j)),
            pl.BlockSpec((None, k, tn), lambda i, j: (layer, 0, j)),
            pl.BlockSpec((None, slab, wd_cols), lambda i, j: (layer, i * nj + j, 0)),
        ],
        out_specs=(pl.BlockSpec((tm, tn), lambda i, j: (i, j)),
                   pl.BlockSpec((None, slab, wd_cols), lambda i, j: (0, i * nj + j, 0))),
        scratch_shapes=[pltpu.VMEM((tm, k), BF16)],
        compiler_params=_cparams(("arbitrary", "arbitrary"), est),
        name=name,
    )(x, g, wg, wu, wd)


def _rope_table_kernel(pos_ref, inv_ref, sign_ref, cs_ref):
    ang = pos_ref[...].astype(F32) * inv_ref[...]
    lane = lax.broadcasted_iota(jnp.int32, ang.shape, 1)
    cs_ref[...] = jnp.where(lane < B_ROPE, jnp.cos(ang), jnp.sin(ang) * sign_ref[...])


def _rope_table(pos_col, inv_row, sign_row, *, tm):
    m = pos_col.shape[0]
    return pl.pallas_call(
        _rope_table_kernel,
        out_shape=jax.ShapeDtypeStruct((m, V7X_LANES), F32),
        grid=(m // tm,),
        in_specs=[
            pl.BlockSpec((tm, 1), lambda i: (i, 0)),
            pl.BlockSpec((1, V7X_LANES), lambda i: (0, 0)),
            pl.BlockSpec((1, V7X_LANES), lambda i: (0, 0)),
        ],
        out_specs=pl.BlockSpec((tm, V7X_LANES), lambda i: (i, 0)),
        compiler_params=_cparams(("arbitrary",), 32 << 20),
        name="rope_table",
    )(pos_col, inv_row, sign_row)


def _rope_pair_sum(slot, cs):
    t = slot * cs
    return t + pltpu.roll(t, B_ROPE, axis=1)


def _mla_prep_kernel(ql_ref, rope_ref, kvl_ref, cs_ref, gq_ref, gkv_ref, wq_ref, wkv_ref, wo_ref,
                     qt_ref, kn_ref, vt_ref, kpe_ref, wob_ref, *, q_scale):
    cs = cs_ref[...]
    hq = _rms_rows(ql_ref[...].astype(F32), gq_ref[...]).astype(BF16)
    qf = jnp.dot(hq, wq_ref[...], preferred_element_type=F32)
    for h in range(B_HEADS):
        base = h * B_QK
        nope = qf[:, base:base + B_NOPE]
        pe = _rope_pair_sum(qf[:, base + B_NOPE:base + B_QK], cs)
        qt_ref[base:base + B_NOPE, :] = (nope * q_scale).T.astype(BF16)
        qt_ref[base + B_NOPE:base + B_QK, :] = (pe * q_scale).T.astype(BF16)

    hkv = _rms_rows(kvl_ref[...].astype(F32), gkv_ref[...]).astype(BF16)
    kvf = jnp.dot(hkv, wkv_ref[...], preferred_element_type=F32)
    kn_ref[...] = kvf[:, :B_HEADS * B_NOPE].astype(BF16)
    vt_ref[...] = kvf[:, B_HEADS * B_NOPE:].T.astype(BF16)

    kpe = _rope_pair_sum(rope_ref[...].astype(F32), cs)
    lane = lax.broadcasted_iota(jnp.int32, kpe.shape, 1)
    kpe_ref[...] = jnp.where(lane < B_ROPE, kpe, 0.0).astype(BF16)
    wob_ref[...] = wo_ref[...].astype(BF16)


def _mla_prep(proj, cs, gq, gkv, wq, wkv, wo, layer, *, seq, tm, name):
    m = proj.shape[0]
    tiles_per_seq = seq // tm
    q_scale = (B_NOPE + B_ROPE) ** -0.5 * LOG2E
    wo_rows, wo_cols = wo.shape[1], wo.shape[2]
    slab = wo_rows // (m // tm)
    assert slab * (m // tm) == wo_rows and slab % 16 == 0
    est = (2 * tm * (Q_LORA + KV_LORA + 128) * 2 + 2 * tm * 128 * 4
           + 2 * (Q_LORA + KV_LORA) * 2048 * 2 + 2 * slab * wo_cols * 6
           + 2 * tm * (2048 + 1024 + 1024 + 128) * 2 + 3 * tm * 2048 * 4 + (4 << 20))
    return pl.pallas_call(
        functools.partial(_mla_prep_kernel, q_scale=q_scale),
        out_shape=(
            jax.ShapeDtypeStruct((m // seq * B_HEADS * B_QK, seq), BF16),
            jax.ShapeDtypeStruct((m, B_HEADS * B_NOPE), BF16),
            jax.ShapeDtypeStruct((m // seq * B_WIDTH, seq), BF16),
            jax.ShapeDtypeStruct((m, V7X_LANES), BF16),
            jax.ShapeDtypeStruct((1, wo_rows, wo_cols), BF16),
        ),
        grid=(m // tm,),
        in_specs=[
            pl.BlockSpec((tm, Q_LORA), lambda i: (i, IN_QLAT_OFF // Q_LORA)),
            pl.BlockSpec((tm, V7X_LANES), lambda i: (i, IN_ROPE_OFF // V7X_LANES)),
            pl.BlockSpec((tm, KV_LORA), lambda i: (i, IN_KVLAT_OFF // KV_LORA)),
            pl.BlockSpec((tm, V7X_LANES), lambda i: (i, 0)),
            pl.BlockSpec((None, 1, Q_LORA), lambda i: (layer, 0, 0)),
            pl.BlockSpec((None, 1, KV_LORA), lambda i: (layer, 0, 0)),
            pl.BlockSpec((None, Q_LORA, B_HEADS * B_QK), lambda i: (layer, 0, 0)),
            pl.BlockSpec((None, KV_LORA, 2 * B_WIDTH), lambda i: (layer, 0, 0)),
            pl.BlockSpec((None, slab, wo_cols), lambda i: (layer, i, 0)),
        ],
        out_specs=(
            pl.BlockSpec((B_HEADS * B_QK, tm), lambda i: (i // tiles_per_seq, i % tiles_per_seq)),
            pl.BlockSpec((tm, B_HEADS * B_NOPE), lambda i: (i, 0)),
            pl.BlockSpec((B_WIDTH, tm), lambda i: (i // tiles_per_seq, i % tiles_per_seq)),
            pl.BlockSpec((tm, V7X_LANES), lambda i: (i, 0)),
            pl.BlockSpec((None, slab, wo_cols), lambda i: (0, i, 0)),
        ),
        compiler_params=_cparams(("arbitrary",), est),
        name=name,
    )(proj, proj, proj, cs, gq, gkv, wq, wkv, wo)


def _attn_a_kernel(q_ref, k0_ref, k1_ref, k2_ref, v0_ref, v1_ref, v2_ref, bias_ref, o_ref):
    d, tq = A_HEAD_DIM, A_TQ
    look = 2

    def scores(h):
        hs = slice(h * d, (h + 1) * d)
        k = jnp.concatenate([k0_ref[:, hs], k1_ref[:, hs], k2_ref[:, hs]], axis=0)
        return jnp.dot(k, q_ref[:, hs].T, preferred_element_type=F32)

    tiles = [scores(h) for h in range(look)] + [None] * (A_HEADS - look)
    for h in range(A_HEADS):
        hs = slice(h * d, (h + 1) * d)
        if h + look < A_HEADS:
            tiles[h + look] = scores(h + look)
        v = jnp.concatenate([v0_ref[:, hs], v1_ref[:, hs], v2_ref[:, hs]], axis=0)
        st = bias_ref[h // 2][:, (h % 2) * tq:(h % 2 + 1) * tq] + tiles[h]
        tiles[h] = None
        m = _col_reduce(st, jnp.max)
        p = jnp.exp2(st - m)
        l = _col_reduce(p, jnp.sum)
        ot = lax.dot_general(v, p.astype(BF16), (((0,), (0,)), ((), ())),
                             preferred_element_type=F32)
        o_ref[:, hs] = (ot / l).T.astype(o_ref.dtype)


def _attn_a(proj, bias, layer, *, batch, seq, name):
    m = proj.shape[0]
    nq = seq // A_TQ
    qspec = pl.BlockSpec((A_TQ, A_WIDTH), lambda b, i: (b * nq + i, 0))

    def kv_spec(col_blk, c):
        back = A_KBLKS - 1 - c
        return pl.BlockSpec((A_TQ, A_WIDTH),
                            lambda b, i: (b * nq + jnp.maximum(i - back, 0), col_blk))

    est = 2 * 7 * A_TQ * A_WIDTH * 2 + 2 * A_HEADS * A_TQ * A_WIN * 4 + 2 * A_TQ * A_WIDTH * 2 + (8 << 20)
    return pl.pallas_call(
        _attn_a_kernel,
        out_shape=jax.ShapeDtypeStruct((m, A_WIDTH), BF16),
        grid=(batch, nq),
        in_specs=[qspec,
                  kv_spec(1, 0), kv_spec(1, 1), kv_spec(1, 2),
                  kv_spec(2, 0), kv_spec(2, 1), kv_spec(2, 2),
                  pl.BlockSpec((A_HEADS // 2, None, A_WIN, 2 * A_TQ),
                               lambda b, i: (layer, jnp.minimum(i, A_KBLKS - 1), 0, 0))],
        out_specs=pl.BlockSpec((A_TQ, A_WIDTH), lambda b, i: (b * nq + i, 0)),
        compiler_params=_cparams(("arbitrary", "arbitrary"), est),
        name=name,
    )(proj, proj, proj, proj, proj, proj, proj, bias)


def _col_reduce(x, op):
    rows, cols = x.shape
    slabs = op(x.reshape(B_RED_SLABS, rows // B_RED_SLABS, cols), axis=0)
    return op(slabs, axis=0, keepdims=True)


def _attn_b_kernel(q_ref, kn_ref, kpe_ref, vt_ref, o_ref, m_ref, l_ref, acc_ref, st_ref):
    i = pl.program_id(2)
    hp = B_HEADS_PER_STEP
    m_ref[...] = jnp.full(m_ref.shape, -jnp.inf, F32)
    l_ref[...] = jnp.zeros(l_ref.shape, F32)
    acc_ref[...] = jnp.zeros(acc_ref.shape, F32)

    def scores(j, h):
        r0 = pl.multiple_of(j * B_TK, B_TK)
        k = jnp.concatenate([kn_ref[pl.ds(r0, B_TK), h * B_NOPE:(h + 1) * B_NOPE],
                             kpe_ref[pl.ds(r0, B_TK), :]], axis=1)
        return jnp.dot(k, q_ref[h * B_QK:(h + 1) * B_QK, :], preferred_element_type=F32)

    def softmax_pv(j, h, st, allowed):
        r0 = pl.multiple_of(j * B_TK, B_TK)
        if allowed is not None:
            st = jnp.where(allowed, st, -jnp.inf)
        m_prev = m_ref[h]
        m_new = jnp.maximum(m_prev, _col_reduce(st, jnp.max))
        alpha = jnp.exp2(m_prev - m_new)
        p = jnp.exp2(st - m_new)
        l_ref[h] = alpha * l_ref[h] + _col_reduce(p, jnp.sum)
        pv = jnp.dot(vt_ref[h * B_V:(h + 1) * B_V, pl.ds(r0, B_TK)], p.astype(BF16),
                     preferred_element_type=F32)
        acc_ref[h] = alpha * acc_ref[h] + pv
        m_ref[h] = m_new

    def key_block(j, masked):
        allowed = None
        if masked:
            krow = lax.broadcasted_iota(jnp.int32, (B_TK, B_TQ), 0)
            qcol = lax.broadcasted_iota(jnp.int32, (B_TK, B_TQ), 1)
            allowed = (krow // CHUNK) <= (qcol // CHUNK)
        look = B_LOOKAHEAD
        tiles = [st_ref[h] for h in range(look)] + [None] * (hp - look)
        for h in range(hp):
            ahead = h + look
            if ahead < hp:
                tiles[ahead] = scores(j, ahead)
            elif not masked:
                st_ref[ahead - hp] = scores(j + 1, ahead - hp)
            softmax_pv(j, h, tiles[h], allowed)
            tiles[h] = None

    def body(j, carry):
        key_block(j, False)
        return carry

    for h in range(B_LOOKAHEAD):
        st_ref[h] = scores(0, h)
    lax.fori_loop(0, i, body, 0)
    key_block(i, True)

    for h in range(hp):
        o_ref[:, h * B_V:(h + 1) * B_V] = (acc_ref[h] / l_ref[h]).T.astype(o_ref.dtype)


def _attn_b(qt, kn, kpe, vt, *, batch, seq, name):
    m = kn.shape[0]
    nq = seq // B_TQ
    hp = B_HEADS_PER_STEP
    est = (2 * B_TQ * hp * B_QK * 2 + 2 * seq * hp * (B_NOPE + B_V) * 2 + 2 * seq * 128 * 2
           + 2 * B_TQ * hp * B_V * 2 + hp * (B_V + 16) * B_TQ * 4 + 8 * B_TQ * B_TK * 4 + (4 << 20))
    return pl.pallas_call(
        _attn_b_kernel,
        out_shape=jax.ShapeDtypeStruct((m, B_WIDTH), BF16),
        grid=(batch, B_HEADS // hp, nq),
        in_specs=[
            pl.BlockSpec((hp * B_QK, B_TQ), lambda b, g, i: (b * (B_HEADS // hp) + g, i)),
            pl.BlockSpec((seq, hp * B_NOPE), lambda b, g, i: (b, g)),
            pl.BlockSpec((seq, V7X_LANES), lambda b, g, i: (b, 0)),
            pl.BlockSpec((hp * B_V, seq), lambda b, g, i: (b * (B_HEADS // hp) + g, 0)),
        ],
        out_specs=pl.BlockSpec((B_TQ, hp * B_V), lambda b, g, i: (b * nq + i, g)),
        scratch_shapes=[
            pltpu.VMEM((hp, 1, B_TQ), F32),
            pltpu.VMEM((hp, 1, B_TQ), F32),
            pltpu.VMEM((hp, B_V, B_TQ), F32),
            pltpu.VMEM((B_LOOKAHEAD, B_TK, B_TQ), F32),
        ],
        compiler_params=_cparams(("arbitrary", "arbitrary", "arbitrary"), est),
        name=name,
    )(qt, kn, kpe, vt)


def _cross_kernel(x_ref, g_ref, wq_ref, k_ref, v_ref, wo_ref, o_ref, h_ref):
    _norm_into(h_ref, x_ref, g_ref)
    scale = X_HEAD_DIM ** -0.5
    q = (jnp.dot(h_ref[...], wq_ref[...], preferred_element_type=F32) * scale).astype(BF16)
    outs = []
    for h in range(X_HEADS):
        hs = slice(h * X_HEAD_DIM, (h + 1) * X_HEAD_DIM)
        s = lax.dot_general(q[:, hs], k_ref[:, hs], (((1,), (1,)), ((), ())),
                            preferred_element_type=F32)
        m = jnp.max(s, axis=1, keepdims=True)
        p = jnp.exp(s - m)
        l = jnp.sum(p, axis=1, keepdims=True)
        o = jnp.dot(p.astype(BF16), v_ref[:, hs], preferred_element_type=F32)
        outs.append((o / l).astype(BF16))
    o_all = jnp.concatenate(outs, axis=1)
    o_ref[...] = x_ref[...] + jnp.dot(o_all, wo_ref[...], preferred_element_type=F32)


def _cross(x, g, wq, kvm, wo, layer, *, seq, tm, name):
    m, d = x.shape
    blocks_per_batch = seq // tm
    est = (4 * tm * d * 4 + tm * d * 2 + 2 * d * X_WIDTH * 2 + 2 * X_WIDTH * d * 2
           + 4 * N_MEM * X_WIDTH * 2 + 4 * tm * N_MEM * 4 + 2 * tm * d * 4 + (4 << 20))
    return pl.pallas_call(
        _cross_kernel,
        out_shape=jax.ShapeDtypeStruct((m, d), F32),
        grid=(m // tm,),
        in_specs=[
            pl.BlockSpec((tm, d), lambda i: (i, 0)),
            pl.BlockSpec((None, 1, d), lambda i: (layer, 0, 0)),
            pl.BlockSpec((None, d, X_WIDTH), lambda i: (layer, 0, 0)),
            pl.BlockSpec((N_MEM, X_WIDTH), lambda i: (i // blocks_per_batch, 2 * layer)),
            pl.BlockSpec((N_MEM, X_WIDTH), lambda i: (i // blocks_per_batch, 2 * layer + 1)),
            pl.BlockSpec((None, X_WIDTH, d), lambda i: (layer, 0, 0)),
        ],
        out_specs=pl.BlockSpec((tm, d), lambda i: (i, 0)),
        scratch_shapes=[pltpu.VMEM((tm, d), BF16)],
        compiler_params=_cparams(("arbitrary",), est),
        name=name,
    )(x, g, wq, kvm, kvm, wo)


def _final_norm_kernel(x_ref, g_ref, o_ref):
    o_ref[...] = _rms_rows(x_ref[...], g_ref[...])


def _final_norm(x, g, *, tm):
    m, d = x.shape
    return pl.pallas_call(
        _final_norm_kernel,
        out_shape=jax.ShapeDtypeStruct((m, d), F32),
        grid=(m // tm,),
        in_specs=[pl.BlockSpec((tm, d), lambda i: (i, 0)), pl.BlockSpec((1, d), lambda i: (0, 0))],
        out_specs=pl.BlockSpec((tm, d), lambda i: (i, 0)),
        compiler_params=_cparams(("arbitrary",), 6 * tm * d * 4 + (4 << 20)),
        name="final_norm",
    )(x, g)


def _w_in_relayout_kernel(w_ref, o_ref):
    rows = w_ref.shape[0]
    lat_end = IN_ROPE_OFF + KV_LORA
    half = B_ROPE // 2
    o_ref[:, :A_WIDTH] = (w_ref[:, :A_WIDTH] * A_SCORE_SCALE).astype(BF16)
    o_ref[:, A_WIDTH:IN_ROPE_OFF] = w_ref[:, A_WIDTH:IN_ROPE_OFF].astype(BF16)
    kr = w_ref[:, lat_end:lat_end + B_ROPE]
    slot = jnp.concatenate([kr, kr[:, half:], kr[:, :half],
                            jnp.zeros((rows, IN_KVLAT_OFF - IN_ROPE_OFF - 2 * B_ROPE), F32)], axis=1)
    o_ref[:, IN_ROPE_OFF:IN_KVLAT_OFF] = slot.astype(BF16)
    o_ref[:, IN_KVLAT_OFF:] = w_ref[:, IN_ROPE_OFF:lat_end].astype(BF16)


def _w_in_relayout(w_in, *, tk):
    layers, k, n = w_in.shape
    return pl.pallas_call(
        _w_in_relayout_kernel,
        out_shape=jax.ShapeDtypeStruct((layers, k, IN_PAD_WIDTH), BF16),
        grid=(layers, k // tk),
        in_specs=[pl.BlockSpec((None, tk, n), lambda l, i: (l, i, 0))],
        out_specs=pl.BlockSpec((None, tk, IN_PAD_WIDTH), lambda l, i: (l, i, 0)),
        compiler_params=_cparams(("arbitrary", "arbitrary"), 4 * tk * n * 4 + 4 * tk * IN_PAD_WIDTH * 2),
        name="w_in_relayout",
    )(w_in)


def _prep_weights(w_in, w_uq, w_ukv):
    swap = (jnp.arange(B_ROPE) + B_ROPE // 2) % B_ROPE
    w_in_p = _w_in_relayout(w_in, tk=256)

    uq = w_uq.reshape(DEPTH, Q_LORA, B_HEADS, B_NOPE + B_ROPE)
    uq_pe = uq[..., B_NOPE:]
    wq_p = jnp.concatenate([uq[..., :B_NOPE], uq_pe, uq_pe[..., swap]], axis=-1)
    wq_p = wq_p.reshape(DEPTH, Q_LORA, B_HEADS * B_QK).astype(BF16)

    ukv = w_ukv.reshape(DEPTH, KV_LORA, B_HEADS, B_NOPE + B_V)
    wkv_p = jnp.concatenate([ukv[..., :B_NOPE].reshape(DEPTH, KV_LORA, B_HEADS * B_NOPE),
                             ukv[..., B_NOPE:].reshape(DEPTH, KV_LORA, B_WIDTH)],
                            axis=-1).astype(BF16)

    return w_in_p, wq_p, wkv_p


def _bias_table_kernel(w_ref, o_ref):
    v = pl.program_id(1)
    r = lax.broadcasted_iota(jnp.int32, (A_TQ, A_WIN), 0)
    t = lax.broadcasted_iota(jnp.int32, (A_TQ, A_WIN), 1)
    dchunk = t // CHUNK - r // CHUNK
    valid = (dchunk >= 0) & (dchunk <= LEFT_CHUNKS) & (v - (A_KBLKS - 1) + t // A_TQ >= 0)
    for hh in range(2):
        w = jnp.broadcast_to(w_ref[hh], (A_TQ, A_BIAS_VEC))
        tab = pltpu.roll(w, 0, axis=1, stride=1, stride_axis=0)[:, :A_WIN]
        o_ref[0, 0, :, hh * A_TQ:(hh + 1) * A_TQ] = jnp.where(valid, tab * LOG2E, -jnp.inf).T


def _relpos_bias_tables(rel_bias):
    u = jnp.arange(A_BIAS_VEC)
    delta = jnp.where(u <= A_WIN, u, u - A_BIAS_VEC)
    idx = jnp.clip(LEFT_CHUNKS * CHUNK - delta, -REL_CLIP, REL_CLIP) + REL_CLIP
    vec = rel_bias[:, :, idx].reshape(DEPTH * A_HEADS, 1, A_BIAS_VEC).astype(F32)
    return pl.pallas_call(
        _bias_table_kernel,
        out_shape=jax.ShapeDtypeStruct((DEPTH * A_HEADS // 2, A_KBLKS, A_WIN, 2 * A_TQ), F32),
        grid=(DEPTH * A_HEADS // 2, A_KBLKS),
        in_specs=[pl.BlockSpec((2, 1, A_BIAS_VEC), lambda i, v: (i, 0, 0))],
        out_specs=pl.BlockSpec((1, 1, A_WIN, 2 * A_TQ), lambda i, v: (i, v, 0, 0)),
        compiler_params=_cparams(("arbitrary", "arbitrary"), 32 << 20),
        name="bias_table",
    )(vec)


def kernel(x, mem, positions, norm_mix, w_in, rel_bias, q_norm, kv_norm, w_uq, w_ukv, w_out,
           norm_mem, mem_norm, w_xq, w_xkv, w_xo, norm_ffn, w_gate, w_up, w_down, norm_final):
    b, s, d = x.shape
    m = b * s
    assert (b, s, d) == (4, 4096, D_MODEL) and mem.shape == (b, N_MEM, d)

    w_in_p, wq_p, wkv_p = _prep_weights(w_in, w_uq, w_ukv)
    w_xq_b = w_xq.astype(BF16)
    w_xo_b = w_xo.astype(BF16)
    bias_tab = _relpos_bias_tables(rel_bias)

    half = B_ROPE // 2
    inv = ROPE_THETA ** (-jnp.arange(half, dtype=F32) / half)
    inv_row = jnp.tile(inv, 4)[None, :]
    sign_row = jnp.concatenate([jnp.ones((B_ROPE,), F32), -jnp.ones((half,), F32),
                                jnp.ones((half,), F32)])[None, :]
    cs = _rope_table(positions.reshape(m, 1), inv_row, sign_row, tm=2048)

    kvm = _mem_kv(mem.reshape(b * N_MEM, d), mem_norm[None, :], w_xkv)

    norm_mix, norm_mem, norm_ffn, q_norm, kv_norm = (
        g[:, None, :] for g in (norm_mix, norm_mem, norm_ffn, q_norm, kv_norm))
    xf = x.reshape(m, d)
    for l in range(DEPTH):
        proj = _norm_matmul(xf, norm_mix, w_in_p, l, tm=1024, tn=1536, name=f"in_proj_{l}")
        oa = _attn_a(proj, bias_tab, l, batch=b, seq=s, name=f"attn_a_{l}")
        qt, kn, vt, kpe, w_out_l = _mla_prep(proj, cs, q_norm, kv_norm, wq_p, wkv_p, w_out, l,
                                             seq=s, tm=512, name=f"mla_prep_{l}")
        ob = _attn_b(qt, kn, kpe, vt, batch=b, seq=s, name=f"attn_b_{l}")
        xf = _mm_res([oa, ob], w_out_l, xf, 0, tm=1024, tn=1024, name=f"out_proj_{l}")
        xf = _cross(xf, norm_mem, w_xq_b, kvm, w_xo_b, l, seq=s, tm=512, name=f"cross_{l}")
        act, w_down_l = _swiglu_gu(xf, norm_ffn, w_gate, w_up, w_down, l, tm=1024, tn=512,
                                   name=f"swiglu_gu_{l}")
        xf = _mm_res([act], w_down_l, xf, 0, tm=1024, tn=512, name=f"swiglu_down_{l}")
    out = _final_norm(xf, norm_final[None, :], tm=512)
    return out.reshape(b, s, d)
```

```python
import functools
import math

import jax
import jax.numpy as jnp
from jax import lax
from jax.experimental import pallas as pl
from jax.experimental.pallas import tpu as pltpu

D_MODEL = 2048
DEPTH = 4
CHUNK = 64
LEFT_CHUNKS = 8
A_HEAD_DIM = 128
A_HEADS = 8
A_WIDTH = A_HEADS * A_HEAD_DIM
REL_CLIP = 128
B_NOPE = 128
B_ROPE = 64
B_V = 128
B_HEADS = 8
B_WIDTH = B_HEADS * B_V
Q_LORA = 768
KV_LORA = 512
ROPE_THETA = 10000.0
N_MEM = 256
X_HEADS = 4
X_HEAD_DIM = 128
X_WIDTH = X_HEADS * X_HEAD_DIM
D_FF = 5632
EPS = 1e-6

V7X_LANES = 128
V7X_VMEM_BYTES = 64 * 1024 * 1024
V7X_VMEM_REQUEST_CAP = 56 * 1024 * 1024

BF16 = jnp.bfloat16
F32 = jnp.float32

IN_QLAT_OFF = 3 * A_WIDTH
IN_ROPE_OFF = IN_QLAT_OFF + Q_LORA
IN_KVLAT_OFF = IN_ROPE_OFF + 2 * V7X_LANES
IN_PAD_WIDTH = IN_KVLAT_OFF + KV_LORA

A_TQ = 4 * CHUNK
A_WIN = A_TQ + LEFT_CHUNKS * CHUNK
A_KBLKS = A_WIN // A_TQ
A_BIAS_VEC = 1024

B_TQ = 512
B_TK = 512
B_HEADS_PER_STEP = 4
B_LOOKAHEAD = 2
B_RED_SLABS = 8
B_QK = B_NOPE + 2 * B_ROPE

LOG2E = math.log2(math.e)
A_SCORE_SCALE = A_HEAD_DIM ** -0.5 * LOG2E


def _cparams(sem, vmem_bytes):
    limit = int(min(max(vmem_bytes, 32 * 1024 * 1024), V7X_VMEM_REQUEST_CAP))
    return pltpu.CompilerParams(dimension_semantics=sem, vmem_limit_bytes=limit)


def _rms_rows(x, g):
    ms = jnp.mean(x * x, axis=-1, keepdims=True)
    return (x * lax.rsqrt(ms + EPS)) * g


def _norm_into(h_ref, x_ref, g_ref, rows=128):
    n = x_ref.shape[0] // rows

    def body(c, carry):
        r0 = pl.multiple_of(c * rows, rows)
        x = x_ref[pl.ds(r0, rows), :].astype(F32)
        h_ref[pl.ds(r0, rows), :] = _rms_rows(x, g_ref[...]).astype(h_ref.dtype)
        return carry

    lax.fori_loop(0, n, body, 0)


def _norm_matmul_kernel(x_ref, g_ref, w_ref, o_ref, h_ref):
    @pl.when(pl.program_id(1) == 0)
    def _():
        _norm_into(h_ref, x_ref, g_ref)

    o_ref[...] = jnp.dot(h_ref[...], w_ref[...], preferred_element_type=F32).astype(o_ref.dtype)


def _norm_matmul(x, g, w, layer, *, tm, tn, name):
    m, k = x.shape
    n = w.shape[2]
    xb = x.dtype.itemsize
    est = 2 * tm * k * xb + tm * k * 2 + 2 * k * tn * 2 + 2 * tm * tn * 2 + tm * tn * 4 + (4 << 20)
    return pl.pallas_call(
        _norm_matmul_kernel,
        out_shape=jax.ShapeDtypeStruct((m, n), BF16),
        grid=(m // tm, n // tn),
        in_specs=[
            pl.BlockSpec((tm, k), lambda i, j: (i, 0)),
            pl.BlockSpec((None, 1, k), lambda i, j: (layer, 0, 0)),
            pl.BlockSpec((None, k, tn), lambda i, j: (layer, 0, j)),
        ],
        out_specs=pl.BlockSpec((tm, tn), lambda i, j: (i, j)),
        scratch_shapes=[pltpu.VMEM((tm, k), BF16)],
        compiler_params=_cparams(("arbitrary", "arbitrary"), est),
        name=name,
    )(x, g, w)


def _mem_kv_kernel(x_ref, g_ref, w_ref, o_ref, h_ref):
    @pl.when(pl.program_id(0) == 0)
    def _():
        _norm_into(h_ref, x_ref, g_ref)

    o_ref[...] = jnp.dot(h_ref[...], w_ref[...].astype(BF16),
                         preferred_element_type=F32).astype(o_ref.dtype)


def _mem_kv(mem2d, g, w_xkv):
    m, k = mem2d.shape
    layers, _, n = w_xkv.shape
    est = 2 * m * k * 4 + m * k * 2 + 2 * k * n * 4 + k * n * 2 + 2 * m * n * 2 + m * n * 4 + (4 << 20)
    return pl.pallas_call(
        _mem_kv_kernel,
        out_shape=jax.ShapeDtypeStruct((m, layers * n), BF16),
        grid=(layers,),
        in_specs=[
            pl.BlockSpec((m, k), lambda l: (0, 0)),
            pl.BlockSpec((1, k), lambda l: (0, 0)),
            pl.BlockSpec((None, k, n), lambda l: (l, 0, 0)),
        ],
        out_specs=pl.BlockSpec((m, n), lambda l: (0, l)),
        scratch_shapes=[pltpu.VMEM((m, k), BF16)],
        compiler_params=_cparams(("arbitrary",), est),
        name="mem_kv",
    )(mem2d, g, w_xkv)


def _mm_res_kernel(*refs, n_in):
    a_refs = refs[:n_in]
    w_refs = refs[n_in:2 * n_in]
    r_ref = refs[2 * n_in]
    o_ref = refs[2 * n_in + 1]
    acc = r_ref[...]
    for a_ref, w_ref in zip(a_refs, w_refs):
        acc = acc + jnp.dot(a_ref[...], w_ref[...], preferred_element_type=F32)
    o_ref[...] = acc


def _mm_res(a_list, w, res, layer, *, tm, tn, name):
    m, n = res.shape
    n_in = len(a_list)
    in_specs = []
    w_specs = []
    est = (4 + 2 * n_in) * tm * tn * 4 + (4 << 20)
    for idx, a in enumerate(a_list):
        k = a.shape[1]
        assert w.shape[1] == n_in * k
        in_specs.append(pl.BlockSpec((tm, k), lambda i, j: (i, 0)))
        w_specs.append(pl.BlockSpec((None, k, tn), lambda i, j, idx=idx: (layer, idx, j)))
        est += 2 * tm * k * 2 + 2 * k * tn * 2
    in_specs += w_specs
    in_specs.append(pl.BlockSpec((tm, tn), lambda i, j: (i, j)))
    return pl.pallas_call(
        functools.partial(_mm_res_kernel, n_in=n_in),
        out_shape=jax.ShapeDtypeStruct((m, n), F32),
        grid=(m // tm, n // tn),
        in_specs=in_specs,
        out_specs=pl.BlockSpec((tm, tn), lambda i, j: (i, j)),
        compiler_params=_cparams(("arbitrary", "arbitrary"), est),
        name=name,
    )(*a_list, *([w] * n_in), res)


def _swiglu_gu_kernel(x_ref, g_ref, wg_ref, wu_ref, wd_ref, o_ref, wdb_ref, h_ref):
    @pl.when(pl.program_id(1) == 0)
    def _():
        _norm_into(h_ref, x_ref, g_ref)

    h = h_ref[...]
    gate = jnp.dot(h, wg_ref[...].astype(BF16), preferred_element_type=F32)
    up = jnp.dot(h, wu_ref[...].astype(BF16), preferred_element_type=F32)
    o_ref[...] = (gate * (1.0 / (1.0 + jnp.exp(-gate))) * up).astype(o_ref.dtype)
    wdb_ref[...] = wd_ref[...].astype(BF16)


def _swiglu_gu(x, g, wg, wu, wd, layer, *, tm, tn, name):
    m, k = x.shape
    n = wg.shape[2]
    steps = (m // tm) * (n // tn)
    nj = n // tn
    wd_rows, wd_cols = wd.shape[1], wd.shape[2]
    slab = wd_rows // steps
    assert slab * steps == wd_rows and slab % 16 == 0
    wbytes = wg.dtype.itemsize
    est = (2 * tm * k * 4 + tm * k * 2 + 4 * k * tn * wbytes + 2 * k * tn * 2 + 2 * tm * tn * 2
           + 3 * tm * tn * 4 + 2 * slab * wd_cols * 6 + (4 << 20))
    return pl.pallas_call(
        _swiglu_gu_kernel,
        out_shape=(jax.ShapeDtypeStruct((m, n), BF16),
                   jax.ShapeDtypeStruct((1, wd_rows, wd_cols), BF16)),
        grid=(m // tm, nj),
        in_specs=[
            pl.BlockSpec((tm, k), lambda i, j: (i, 0)),
            pl.BlockSpec((None, 1, k), lambda i, j: (layer, 0, 0)),
            pl.BlockSpec((None, k, tn), lambda i, j: (layer, 0, j)),
            pl.BlockSpec((None, k, tn), lambda i, j: (layer, 0, j)),
            pl.BlockSpec((None, slab, wd_cols), lambda i, j: (layer, i * nj + j, 0)),
        ],
        out_specs=(pl.BlockSpec((tm, tn), lambda i, j: (i, j)),
                   pl.BlockSpec((None, slab, wd_cols), lambda i, j: (0, i * nj + j, 0))),
        scratch_shapes=[pltpu.VMEM((tm, k), BF16)],
        compiler_params=_cparams(("arbitrary", "arbitrary"), est),
        name=name,
    )(x, g, wg, wu, wd)


def _rope_table_kernel(pos_ref, inv_ref, sign_ref, cs_ref):
    ang = pos_ref[...].astype(F32) * inv_ref[...]
    lane = lax.broadcasted_iota(jnp.int32, ang.shape, 1)
    cs_ref[...] = jnp.where(lane < B_ROPE, jnp.cos(ang), jnp.sin(ang) * sign_ref[...])


def _rope_table(pos_col, inv_row, sign_row, *, tm):
    m = pos_col.shape[0]
    return pl.pallas_call(
        _rope_table_kernel,
        out_shape=jax.ShapeDtypeStruct((m, V7X_LANES), F32),
        grid=(m // tm,),
        in_specs=[
            pl.BlockSpec((tm, 1), lambda i: (i, 0)),
            pl.BlockSpec((1, V7X_LANES), lambda i: (0, 0)),
            pl.BlockSpec((1, V7X_LANES), lambda i: (0, 0)),
        ],
        out_specs=pl.BlockSpec((tm, V7X_LANES), lambda i: (i, 0)),
        compiler_params=_cparams(("arbitrary",), 32 << 20),
        name="rope_table",
    )(pos_col, inv_row, sign_row)


def _rope_pair_sum(slot, cs):
    t = slot * cs
    return t + pltpu.roll(t, B_ROPE, axis=1)


def _mla_prep_kernel(ql_ref, rope_ref, kvl_ref, cs_ref, gq_ref, gkv_ref, wq_ref, wkv_ref, wo_ref,
                     qt_ref, kn_ref, vt_ref, kpe_ref, wob_ref, *, q_scale):
    cs = cs_ref[...]
    hq = _rms_rows(ql_ref[...].astype(F32), gq_ref[...]).astype(BF16)
    qf = jnp.dot(hq, wq_ref[...], preferred_element_type=F32)
    for h in range(B_HEADS):
        base = h * B_QK
        nope = qf[:, base:base + B_NOPE]
        pe = _rope_pair_sum(qf[:, base + B_NOPE:base + B_QK], cs)
        qt_ref[base:base + B_NOPE, :] = (nope * q_scale).T.astype(BF16)
        qt_ref[base + B_NOPE:base + B_QK, :] = (pe * q_scale).T.astype(BF16)

    hkv = _rms_rows(kvl_ref[...].astype(F32), gkv_ref[...]).astype(BF16)
    kvf = jnp.dot(hkv, wkv_ref[...], preferred_element_type=F32)
    kn_ref[...] = kvf[:, :B_HEADS * B_NOPE].astype(BF16)
    vt_ref[...] = kvf[:, B_HEADS * B_NOPE:].T.astype(BF16)

    kpe = _rope_pair_sum(rope_ref[...].astype(F32), cs)
    lane = lax.broadcasted_iota(jnp.int32, kpe.shape, 1)
    kpe_ref[...] = jnp.where(lane < B_ROPE, kpe, 0.0).astype(BF16)
    wob_ref[...] = wo_ref[...].astype(BF16)


def _mla_prep(proj, cs, gq, gkv, wq, wkv, wo, layer, *, seq, tm, name):
    m = proj.shape[0]
    tiles_per_seq = seq // tm
    q_scale = (B_NOPE + B_ROPE) ** -0.5 * LOG2E
    wo_rows, wo_cols = wo.shape[1], wo.shape[2]
    slab = wo_rows // (m // tm)
    assert slab * (m // tm) == wo_rows and slab % 16 == 0
    est = (2 * tm * (Q_LORA + KV_LORA + 128) * 2 + 2 * tm * 128 * 4
           + 2 * (Q_LORA + KV_LORA) * 2048 * 2 + 2 * slab * wo_cols * 6
           + 2 * tm * (2048 + 1024 + 1024 + 128) * 2 + 3 * tm * 2048 * 4 + (4 << 20))
    return pl.pallas_call(
        functools.partial(_mla_prep_kernel, q_scale=q_scale),
        out_shape=(
            jax.ShapeDtypeStruct((m // seq * B_HEADS * B_QK, seq), BF16),
            jax.ShapeDtypeStruct((m, B_HEADS * B_NOPE), BF16),
            jax.ShapeDtypeStruct((m // seq * B_WIDTH, seq), BF16),
            jax.ShapeDtypeStruct((m, V7X_LANES), BF16),
            jax.ShapeDtypeStruct((1, wo_rows, wo_cols), BF16),
        ),
        grid=(m // tm,),
        in_specs=[
            pl.BlockSpec((tm, Q_LORA), lambda i: (i, IN_QLAT_OFF // Q_LORA)),
            pl.BlockSpec((tm, V7X_LANES), lambda i: (i, IN_ROPE_OFF // V7X_LANES)),
            pl.BlockSpec((tm, KV_LORA), lambda i: (i, IN_KVLAT_OFF // KV_LORA)),
            pl.BlockSpec((tm, V7X_LANES), lambda i: (i, 0)),
            pl.BlockSpec((None, 1, Q_LORA), lambda i: (layer, 0, 0)),
            pl.BlockSpec((None, 1, KV_LORA), lambda i: (layer, 0, 0)),
            pl.BlockSpec((None, Q_LORA, B_HEADS * B_QK), lambda i: (layer, 0, 0)),
            pl.BlockSpec((None, KV_LORA, 2 * B_WIDTH), lambda i: (layer, 0, 0)),
            pl.BlockSpec((None, slab, wo_cols), lambda i: (layer, i, 0)),
        ],
        out_specs=(
            pl.BlockSpec((B_HEADS * B_QK, tm), lambda i: (i // tiles_per_seq, i % tiles_per_seq)),
            pl.BlockSpec((tm, B_HEADS * B_NOPE), lambda i: (i, 0)),
            pl.BlockSpec((B_WIDTH, tm), lambda i: (i // tiles_per_seq, i % tiles_per_seq)),
            pl.BlockSpec((tm, V7X_LANES), lambda i: (i, 0)),
            pl.BlockSpec((None, slab, wo_cols), lambda i: (0, i, 0)),
        ),
        compiler_params=_cparams(("arbitrary",), est),
        name=name,
    )(proj, proj, proj, cs, gq, gkv, wq, wkv, wo)


def _attn_a_kernel(q_ref, k0_ref, k1_ref, k2_ref, v0_ref, v1_ref, v2_ref, bias_ref, o_ref):
    d, tq = A_HEAD_DIM, A_TQ
    look = 2

    def scores(h):
        hs = slice(h * d, (h + 1) * d)
        k = jnp.concatenate([k0_ref[:, hs], k1_ref[:, hs], k2_ref[:, hs]], axis=0)
        return jnp.dot(k, q_ref[:, hs].T, preferred_element_type=F32)

    tiles = [scores(h) for h in range(look)] + [None] * (A_HEADS - look)
    for h in range(A_HEADS):
        hs = slice(h * d, (h + 1) * d)
        if h + look < A_HEADS:
            tiles[h + look] = scores(h + look)
        v = jnp.concatenate([v0_ref[:, hs], v1_ref[:, hs], v2_ref[:, hs]], axis=0)
        st = bias_ref[h // 2][:, (h % 2) * tq:(h % 2 + 1) * tq] + tiles[h]
        tiles[h] = None
        m = _col_reduce(st, jnp.max)
        p = jnp.exp2(st - m)
        l = _col_reduce(p, jnp.sum)
        ot = lax.dot_general(v, p.astype(BF16), (((0,), (0,)), ((), ())),
                             preferred_element_type=F32)
        o_ref[:, hs] = (ot / l).T.astype(o_ref.dtype)


def _attn_a(proj, bias, layer, *, batch, seq, name):
    m = proj.shape[0]
    nq = seq // A_TQ
    qspec = pl.BlockSpec((A_TQ, A_WIDTH), lambda b, i: (b * nq + i, 0))

    def kv_spec(col_blk, c):
        back = A_KBLKS - 1 - c
        return pl.BlockSpec((A_TQ, A_WIDTH),
                            lambda b, i: (b * nq + jnp.maximum(i - back, 0), col_blk))

    est = 2 * 7 * A_TQ * A_WIDTH * 2 + 2 * A_HEADS * A_TQ * A_WIN * 4 + 2 * A_TQ * A_WIDTH * 2 + (8 << 20)
    return pl.pallas_call(
        _attn_a_kernel,
        out_shape=jax.ShapeDtypeStruct((m, A_WIDTH), BF16),
        grid=(batch, nq),
        in_specs=[qspec,
                  kv_spec(1, 0), kv_spec(1, 1), kv_spec(1, 2),
                  kv_spec(2, 0), kv_spec(2, 1), kv_spec(2, 2),
                  pl.BlockSpec((A_HEADS // 2, None, A_WIN, 2 * A_TQ),
                               lambda b, i: (layer, jnp.minimum(i, A_KBLKS - 1), 0, 0))],
        out_specs=pl.BlockSpec((A_TQ, A_WIDTH), lambda b, i: (b * nq + i, 0)),
        compiler_params=_cparams(("arbitrary", "arbitrary"), est),
        name=name,
    )(proj, proj, proj, proj, proj, proj, proj, bias)


def _col_reduce(x, op):
    rows, cols = x.shape
    slabs = op(x.reshape(B_RED_SLABS, rows // B_RED_SLABS, cols), axis=0)
    return op(slabs, axis=0, keepdims=True)


def _attn_b_kernel(q_ref, kn_ref, kpe_ref, vt_ref, o_ref, m_ref, l_ref, acc_ref, st_ref):
    i = pl.program_id(2)
    hp = B_HEADS_PER_STEP
    m_ref[...] = jnp.full(m_ref.shape, -jnp.inf, F32)
    l_ref[...] = jnp.zeros(l_ref.shape, F32)
    acc_ref[...] = jnp.zeros(acc_ref.shape, F32)

    def scores(j, h):
        r0 = pl.multiple_of(j * B_TK, B_TK)
        k = jnp.concatenate([kn_ref[pl.ds(r0, B_TK), h * B_NOPE:(h + 1) * B_NOPE],
                             kpe_ref[pl.ds(r0, B_TK), :]], axis=1)
        return jnp.dot(k, q_ref[h * B_QK:(h + 1) * B_QK, :], preferred_element_type=F32)

    def softmax_pv(j, h, st, allowed):
        r0 = pl.multiple_of(j * B_TK, B_TK)
        if allowed is not None:
            st = jnp.where(allowed, st, -jnp.inf)
        m_prev = m_ref[h]
        m_new = jnp.maximum(m_prev, _col_reduce(st, jnp.max))
        alpha = jnp.exp2(m_prev - m_new)
        p = jnp.exp2(st - m_new)
        l_ref[h] = alpha * l_ref[h] + _col_reduce(p, jnp.sum)
        pv = jnp.dot(vt_ref[h * B_V:(h + 1) * B_V, pl.ds(r0, B_TK)], p.astype(BF16),
                     preferred_element_type=F32)
        acc_ref[h] = alpha * acc_ref[h] + pv
        m_ref[h] = m_new

    def key_block(j, masked):
        allowed = None
        if masked:
            krow = lax.broadcasted_iota(jnp.int32, (B_TK, B_TQ), 0)
            qcol = lax.broadcasted_iota(jnp.int32, (B_TK, B_TQ), 1)
            allowed = (krow // CHUNK) <= (qcol // CHUNK)
        look = B_LOOKAHEAD
        tiles = [st_ref[h] for h in range(look)] + [None] * (hp - look)
        for h in range(hp):
            ahead = h + look
            if ahead < hp:
                tiles[ahead] = scores(j, ahead)
            elif not masked:
                st_ref[ahead - hp] = scores(j + 1, ahead - hp)
            softmax_pv(j, h, tiles[h], allowed)
            tiles[h] = None

    def body(j, carry):
        key_block(j, False)
        return carry

    for h in range(B_LOOKAHEAD):
        st_ref[h] = scores(0, h)
    lax.fori_loop(0, i, body, 0)
    key_block(i, True)

    for h in range(hp):
        o_ref[:, h * B_V:(h + 1) * B_V] = (acc_ref[h] / l_ref[h]).T.astype(o_ref.dtype)


def _attn_b(qt, kn, kpe, vt, *, batch, seq, name):
    m = kn.shape[0]
    nq = seq // B_TQ
    hp = B_HEADS_PER_STEP
    est = (2 * B_TQ * hp * B_QK * 2 + 2 * seq * hp * (B_NOPE + B_V) * 2 + 2 * seq * 128 * 2
           + 2 * B_TQ * hp * B_V * 2 + hp * (B_V + 16) * B_TQ * 4 + 8 * B_TQ * B_TK * 4 + (4 << 20))
    return pl.pallas_call(
        _attn_b_kernel,
        out_shape=jax.ShapeDtypeStruct((m, B_WIDTH), BF16),
        grid=(batch, B_HEADS // hp, nq),
        in_specs=[
            pl.BlockSpec((hp * B_QK, B_TQ), lambda b, g, i: (b * (B_HEADS // hp) + g, i)),
            pl.BlockSpec((seq, hp * B_NOPE), lambda b, g, i: (b, g)),
            pl.BlockSpec((seq, V7X_LANES), lambda b, g, i: (b, 0)),
            pl.BlockSpec((hp * B_V, seq), lambda b, g, i: (b * (B_HEADS // hp) + g, 0)),
        ],
        out_specs=pl.BlockSpec((B_TQ, hp * B_V), lambda b, g, i: (b * nq + i, g)),
        scratch_shapes=[
            pltpu.VMEM((hp, 1, B_TQ), F32),
            pltpu.VMEM((hp, 1, B_TQ), F32),
            pltpu.VMEM((hp, B_V, B_TQ), F32),
            pltpu.VMEM((B_LOOKAHEAD, B_TK, B_TQ), F32),
        ],
        compiler_params=_cparams(("arbitrary", "arbitrary", "arbitrary"), est),
        name=name,
    )(qt, kn, kpe, vt)


def _cross_kernel(x_ref, g_ref, wq_ref, k_ref, v_ref, wo_ref, o_ref, h_ref):
    _norm_into(h_ref, x_ref, g_ref)
    scale = X_HEAD_DIM ** -0.5
    q = (jnp.dot(h_ref[...], wq_ref[...], preferred_element_type=F32) * scale).astype(BF16)
    outs = []
    for h in range(X_HEADS):
        hs = slice(h * X_HEAD_DIM, (h + 1) * X_HEAD_DIM)
        s = lax.dot_general(q[:, hs], k_ref[:, hs], (((1,), (1,)), ((), ())),
                            preferred_element_type=F32)
        m = jnp.max(s, axis=1, keepdims=True)
        p = jnp.exp(s - m)
        l = jnp.sum(p, axis=1, keepdims=True)
        o = jnp.dot(p.astype(BF16), v_ref[:, hs], preferred_element_type=F32)
        outs.append((o / l).astype(BF16))
    o_all = jnp.concatenate(outs, axis=1)
    o_ref[...] = x_ref[...] + jnp.dot(o_all, wo_ref[...], preferred_element_type=F32)


def _cross(x, g, wq, kvm, wo, layer, *, seq, tm, name):
    m, d = x.shape
    blocks_per_batch = seq // tm
    est = (4 * tm * d * 4 + tm * d * 2 + 2 * d * X_WIDTH * 2 + 2 * X_WIDTH * d * 2
           + 4 * N_MEM * X_WIDTH * 2 + 4 * tm * N_MEM * 4 + 2 * tm * d * 4 + (4 << 20))
    return pl.pallas_call(
        _cross_kernel,
        out_shape=jax.ShapeDtypeStruct((m, d), F32),
        grid=(m // tm,),
        in_specs=[
            pl.BlockSpec((tm, d), lambda i: (i, 0)),
            pl.BlockSpec((None, 1, d), lambda i: (layer, 0, 0)),
            pl.BlockSpec((None, d, X_WIDTH), lambda i: (layer, 0, 0)),
            pl.BlockSpec((N_MEM, X_WIDTH), lambda i: (i // blocks_per_batch, 2 * layer)),
            pl.BlockSpec((N_MEM, X_WIDTH), lambda i: (i // blocks_per_batch, 2 * layer + 1)),
            pl.BlockSpec((None, X_WIDTH, d), lambda i: (layer, 0, 0)),
        ],
        out_specs=pl.BlockSpec((tm, d), lambda i: (i, 0)),
        scratch_shapes=[pltpu.VMEM((tm, d), BF16)],
        compiler_params=_cparams(("arbitrary",), est),
        name=name,
    )(x, g, wq, kvm, kvm, wo)


def _final_norm_kernel(x_ref, g_ref, o_ref):
    o_ref[...] = _rms_rows(x_ref[...], g_ref[...])


def _final_norm(x, g, *, tm):
    m, d = x.shape
    return pl.pallas_call(
        _final_norm_kernel,
        out_shape=jax.ShapeDtypeStruct((m, d), F32),
        grid=(m // tm,),
        in_specs=[pl.BlockSpec((tm, d), lambda i: (i, 0)), pl.BlockSpec((1, d), lambda i: (0, 0))],
        out_specs=pl.BlockSpec((tm, d), lambda i: (i, 0)),
        compiler_params=_cparams(("arbitrary",), 6 * tm * d * 4 + (4 << 20)),
        name="final_norm",
    )(x, g)


RELAYOUT_COLS = 256
RELAYOUT_ROPE_BLK = IN_ROPE_OFF // RELAYOUT_COLS


def _w_in_relayout_kernel(wt_ref, o_ref):
    j = pl.program_id(1)

    @pl.when(j != RELAYOUT_ROPE_BLK)
    def _():
        factor = jnp.where(j < A_WIDTH // RELAYOUT_COLS, A_SCORE_SCALE, 1.0)
        o_ref[...] = (wt_ref[...] * factor).T.astype(BF16)

    @pl.when(j == RELAYOUT_ROPE_BLK)
    def _():
        half = B_ROPE // 2
        kr = wt_ref[:B_ROPE, :]
        slot = jnp.concatenate([kr, kr[half:], kr[:half],
                                jnp.zeros((RELAYOUT_COLS - 2 * B_ROPE, kr.shape[1]), F32)], axis=0)
        o_ref[...] = slot.T.astype(BF16)


def _w_in_relayout(w_in_t):
    layers, n, k = w_in_t.shape
    lat_blk = IN_ROPE_OFF // RELAYOUT_COLS
    rope_src = (IN_ROPE_OFF + KV_LORA) // RELAYOUT_COLS
    assert IN_KVLAT_OFF == (RELAYOUT_ROPE_BLK + 1) * RELAYOUT_COLS

    def src_block(l, j):
        return (l, jnp.where(j < RELAYOUT_ROPE_BLK, j,
                             jnp.where(j == RELAYOUT_ROPE_BLK, rope_src, j - 1 + lat_blk - RELAYOUT_ROPE_BLK)), 0)

    return pl.pallas_call(
        _w_in_relayout_kernel,
        out_shape=jax.ShapeDtypeStruct((layers, k, IN_PAD_WIDTH), BF16),
        grid=(layers, IN_PAD_WIDTH // RELAYOUT_COLS),
        in_specs=[pl.BlockSpec((None, RELAYOUT_COLS, k), src_block)],
        out_specs=pl.BlockSpec((None, k, RELAYOUT_COLS), lambda l, j: (l, 0, j)),
        compiler_params=_cparams(("arbitrary", "arbitrary"), 32 << 20),
        name="w_in_relayout",
    )(w_in_t)


def _prep_weights(w_in, w_uq, w_ukv):
    swap = (jnp.arange(B_ROPE) + B_ROPE // 2) % B_ROPE
    w_in_p = _w_in_relayout(jnp.swapaxes(w_in, 1, 2))

    uq = w_uq.reshape(DEPTH, Q_LORA, B_HEADS, B_NOPE + B_ROPE)
    uq_pe = uq[..., B_NOPE:]
    wq_p = jnp.concatenate([uq[..., :B_NOPE], uq_pe, uq_pe[..., swap]], axis=-1)
    wq_p = wq_p.reshape(DEPTH, Q_LORA, B_HEADS * B_QK).astype(BF16)

    ukv = w_ukv.reshape(DEPTH, KV_LORA, B_HEADS, B_NOPE + B_V)
    wkv_p = jnp.concatenate([ukv[..., :B_NOPE].reshape(DEPTH, KV_LORA, B_HEADS * B_NOPE),
                             ukv[..., B_NOPE:].reshape(DEPTH, KV_LORA, B_WIDTH)],
                            axis=-1).astype(BF16)

    return w_in_p, wq_p, wkv_p


def _bias_table_kernel(w_ref, o_ref):
    v = pl.program_id(1)
    r = lax.broadcasted_iota(jnp.int32, (A_TQ, A_WIN), 0)
    t = lax.broadcasted_iota(jnp.int32, (A_TQ, A_WIN), 1)
    dchunk = t // CHUNK - r // CHUNK
    valid = (dchunk >= 0) & (dchunk <= LEFT_CHUNKS) & (v - (A_KBLKS - 1) + t // A_TQ >= 0)
    for hh in range(2):
        w = jnp.broadcast_to(w_ref[hh], (A_TQ, A_BIAS_VEC))
        tab = pltpu.roll(w, 0, axis=1, stride=1, stride_axis=0)[:, :A_WIN]
        o_ref[0, 0, :, hh * A_TQ:(hh + 1) * A_TQ] = jnp.where(valid, tab * LOG2E, -jnp.inf).T


def _relpos_bias_tables(rel_bias):
    u = jnp.arange(A_BIAS_VEC)
    delta = jnp.where(u <= A_WIN, u, u - A_BIAS_VEC)
    idx = jnp.clip(LEFT_CHUNKS * CHUNK - delta, -REL_CLIP, REL_CLIP) + REL_CLIP
    vec = rel_bias[:, :, idx].reshape(DEPTH * A_HEADS, 1, A_BIAS_VEC).astype(F32)
    return pl.pallas_call(
        _bias_table_kernel,
        out_shape=jax.ShapeDtypeStruct((DEPTH * A_HEADS // 2, A_KBLKS, A_WIN, 2 * A_TQ), F32),
        grid=(DEPTH * A_HEADS // 2, A_KBLKS),
        in_specs=[pl.BlockSpec((2, 1, A_BIAS_VEC), lambda i, v: (i, 0, 0))],
        out_specs=pl.BlockSpec((1, 1, A_WIN, 2 * A_TQ), lambda i, v: (i, v, 0, 0)),
        compiler_params=_cparams(("arbitrary", "arbitrary"), 32 << 20),
        name="bias_table",
    )(vec)


def kernel(x, mem, positions, norm_mix, w_in, rel_bias, q_norm, kv_norm, w_uq, w_ukv, w_out,
           norm_mem, mem_norm, w_xq, w_xkv, w_xo, norm_ffn, w_gate, w_up, w_down, norm_final):
    b, s, d = x.shape
    m = b * s
    assert (b, s, d) == (4, 4096, D_MODEL) and mem.shape == (b, N_MEM, d)

    w_in_p, wq_p, wkv_p = _prep_weights(w_in, w_uq, w_ukv)
    w_xq_b = w_xq.astype(BF16)
    w_xo_b = w_xo.astype(BF16)
    bias_tab = _relpos_bias_tables(rel_bias)

    half = B_ROPE // 2
    inv = ROPE_THETA ** (-jnp.arange(half, dtype=F32) / half)
    inv_row = jnp.tile(inv, 4)[None, :]
    sign_row = jnp.concatenate([jnp.ones((B_ROPE,), F32), -jnp.ones((half,), F32),
                                jnp.ones((half,), F32)])[None, :]
    cs = _rope_table(positions.reshape(m, 1), inv_row, sign_row, tm=2048)

    kvm = _mem_kv(mem.reshape(b * N_MEM, d), mem_norm[None, :], w_xkv)

    norm_mix, norm_mem, norm_ffn, q_norm, kv_norm = (
        g[:, None, :] for g in (norm_mix, norm_mem, norm_ffn, q_norm, kv_norm))
    xf = x.reshape(m, d)
    for l in range(DEPTH):
        proj = _norm_matmul(xf, norm_mix, w_in_p, l, tm=1024, tn=1536, name=f"in_proj_{l}")
        oa = _attn_a(proj, bias_tab, l, batch=b, seq=s, name=f"attn_a_{l}")
        qt, kn, vt, kpe, w_out_l = _mla_prep(proj, cs, q_norm, kv_norm, wq_p, wkv_p, w_out, l,
                                             seq=s, tm=512, name=f"mla_prep_{l}")
        ob = _attn_b(qt, kn, kpe, vt, batch=b, seq=s, name=f"attn_b_{l}")
        xf = _mm_res([oa, ob], w_out_l, xf, 0, tm=1024, tn=1024, name=f"out_proj_{l}")
        xf = _cross(xf, norm_mem, w_xq_b, kvm, w_xo_b, l, seq=s, tm=512, name=f"cross_{l}")
        act, w_down_l = _swiglu_gu(xf, norm_ffn, w_gate, w_up, w_down, l, tm=1024, tn=512,
                                   name=f"swiglu_gu_{l}")
        xf = _mm_res([act], w_down_l, xf, 0, tm=1024, tn=512, name=f"swiglu_down_{l}")
    out = _final_norm(xf, norm_final[None, :], tm=512)
    return out.reshape(b, s, d)
```

```python
import functools
import math

import jax
import jax.numpy as jnp
from jax import lax
from jax.experimental import pallas as pl
from jax.experimental.pallas import tpu as pltpu

D_MODEL = 2048
DEPTH = 4
CHUNK = 64
LEFT_CHUNKS = 8
A_HEAD_DIM = 128
A_HEADS = 8
A_WIDTH = A_HEADS * A_HEAD_DIM
REL_CLIP = 128
B_NOPE = 128
B_ROPE = 64
B_V = 128
B_HEADS = 8
B_WIDTH = B_HEADS * B_V
Q_LORA = 768
KV_LORA = 512
ROPE_THETA = 10000.0
N_MEM = 256
X_HEADS = 4
X_HEAD_DIM = 128
X_WIDTH = X_HEADS * X_HEAD_DIM
X_ROW_PARTS = 2
D_FF = 5632
EPS = 1e-6

V7X_LANES = 128
V7X_VMEM_BYTES = 64 * 1024 * 1024
V7X_VMEM_REQUEST_CAP = 56 * 1024 * 1024

BF16 = jnp.bfloat16
F32 = jnp.float32

IN_QLAT_OFF = 3 * A_WIDTH
IN_ROPE_OFF = IN_QLAT_OFF + Q_LORA
IN_KVLAT_OFF = IN_ROPE_OFF + 2 * V7X_LANES
IN_PAD_WIDTH = IN_KVLAT_OFF + KV_LORA

A_TQ = 4 * CHUNK
A_WIN = A_TQ + LEFT_CHUNKS * CHUNK
A_KBLKS = A_WIN // A_TQ
A_BIAS_VEC = 1024

B_TQ = 512
B_TK = 512
B_HEADS_PER_STEP = 4
B_LOOKAHEAD = 2
B_RED_SLABS = 8
B_QK = B_NOPE + 2 * B_ROPE

LOG2E = math.log2(math.e)
A_SCORE_SCALE = A_HEAD_DIM ** -0.5 * LOG2E


def _cparams(sem, vmem_bytes):
    limit = int(min(max(vmem_bytes, 32 * 1024 * 1024), V7X_VMEM_REQUEST_CAP))
    return pltpu.CompilerParams(dimension_semantics=sem, vmem_limit_bytes=limit)


def _rms_rows(x, g):
    ms = jnp.mean(x * x, axis=-1, keepdims=True)
    return (x * lax.rsqrt(ms + EPS)) * g


def _norm_into(h_ref, x_ref, g_ref, rows=128):
    n = x_ref.shape[0] // rows

    def body(c, carry):
        r0 = pl.multiple_of(c * rows, rows)
        x = x_ref[pl.ds(r0, rows), :].astype(F32)
        h_ref[pl.ds(r0, rows), :] = _rms_rows(x, g_ref[...]).astype(h_ref.dtype)
        return carry

    lax.fori_loop(0, n, body, 0)


def _norm_matmul_kernel(x_ref, g_ref, w_ref, o_ref, h_ref):
    @pl.when(pl.program_id(1) == 0)
    def _():
        _norm_into(h_ref, x_ref, g_ref)

    o_ref[...] = jnp.dot(h_ref[...], w_ref[...], preferred_element_type=F32).astype(o_ref.dtype)


def _norm_matmul(x, g, w, layer, *, tm, tn, name):
    m, k = x.shape
    n = w.shape[2]
    xb = x.dtype.itemsize
    est = 2 * tm * k * xb + tm * k * 2 + 2 * k * tn * 2 + 2 * tm * tn * 2 + tm * tn * 4 + (4 << 20)
    return pl.pallas_call(
        _norm_matmul_kernel,
        out_shape=jax.ShapeDtypeStruct((m, n), BF16),
        grid=(m // tm, n // tn),
        in_specs=[
            pl.BlockSpec((tm, k), lambda i, j: (i, 0)),
            pl.BlockSpec((None, 1, k), lambda i, j: (layer, 0, 0)),
            pl.BlockSpec((None, k, tn), lambda i, j: (layer, 0, j)),
        ],
        out_specs=pl.BlockSpec((tm, tn), lambda i, j: (i, j)),
        scratch_shapes=[pltpu.VMEM((tm, k), BF16)],
        compiler_params=_cparams(("arbitrary", "arbitrary"), est),
        name=name,
    )(x, g, w)


def _mem_kv_kernel(x_ref, g_ref, w_ref, o_ref, h_ref):
    @pl.when(pl.program_id(0) == 0)
    def _():
        _norm_into(h_ref, x_ref, g_ref)

    o_ref[...] = jnp.dot(h_ref[...], w_ref[...].astype(BF16),
                         preferred_element_type=F32).astype(o_ref.dtype)


def _mem_kv(mem2d, g, w_xkv):
    m, k = mem2d.shape
    layers, _, n = w_xkv.shape
    est = 2 * m * k * 4 + m * k * 2 + 2 * k * n * 4 + k * n * 2 + 2 * m * n * 2 + m * n * 4 + (4 << 20)
    return pl.pallas_call(
        _mem_kv_kernel,
        out_shape=jax.ShapeDtypeStruct((m, layers * n), BF16),
        grid=(layers,),
        in_specs=[
            pl.BlockSpec((m, k), lambda l: (0, 0)),
            pl.BlockSpec((1, k), lambda l: (0, 0)),
            pl.BlockSpec((None, k, n), lambda l: (l, 0, 0)),
        ],
        out_specs=pl.BlockSpec((m, n), lambda l: (0, l)),
        scratch_shapes=[pltpu.VMEM((m, k), BF16)],
        compiler_params=_cparams(("arbitrary",), est),
        name="mem_kv",
    )(mem2d, g, w_xkv)


def _mm_res_kernel(*refs, n_in, n_cast):
    a_refs = refs[:n_in]
    w_refs = refs[n_in:2 * n_in]
    r_ref = refs[2 * n_in]
    c_refs = refs[2 * n_in + 1:2 * n_in + 1 + n_cast]
    o_ref = refs[2 * n_in + 1 + n_cast]
    cb_refs = refs[2 * n_in + 2 + n_cast:]
    acc = r_ref[...]
    for a_ref, w_ref in zip(a_refs, w_refs):
        acc = acc + jnp.dot(a_ref[...], w_ref[...], preferred_element_type=F32)
    o_ref[...] = acc
    for c_ref, cb_ref in zip(c_refs, cb_refs):
        cb_ref[...] = c_ref[...].astype(BF16)


def _mm_res(a_list, w, res, layer, *, tm, tn, name, cast=(), cast_layer=0):
    m, n = res.shape
    n_in = len(a_list)
    nj = n // tn
    steps = (m // tm) * nj
    in_specs = []
    w_specs = []
    est = (4 + 2 * n_in) * tm * tn * 4 + (4 << 20)
    for idx, a in enumerate(a_list):
        k = a.shape[1]
        assert w.shape[1] == n_in * k
        in_specs.append(pl.BlockSpec((tm, k), lambda i, j: (i, 0)))
        w_specs.append(pl.BlockSpec((None, k, tn), lambda i, j, idx=idx: (layer, idx, j)))
        est += 2 * tm * k * 2 + 2 * k * tn * 2
    in_specs += w_specs
    in_specs.append(pl.BlockSpec((tm, tn), lambda i, j: (i, j)))
    out_shapes = [jax.ShapeDtypeStruct((m, n), F32)]
    out_specs = [pl.BlockSpec((tm, tn), lambda i, j: (i, j))]
    for c in cast:
        rows, cols = c.shape[1], c.shape[2]
        slab = rows // steps
        assert slab * steps == rows and slab % 16 == 0
        in_specs.append(pl.BlockSpec((None, slab, cols), lambda i, j: (cast_layer, i * nj + j, 0)))
        out_shapes.append(jax.ShapeDtypeStruct((1, rows, cols), BF16))
        out_specs.append(pl.BlockSpec((None, slab, cols), lambda i, j: (0, i * nj + j, 0)))
        est += 2 * slab * cols * 6
    outs = pl.pallas_call(
        functools.partial(_mm_res_kernel, n_in=n_in, n_cast=len(cast)),
        out_shape=tuple(out_shapes),
        grid=(m // tm, nj),
        in_specs=in_specs,
        out_specs=tuple(out_specs),
        compiler_params=_cparams(("arbitrary", "arbitrary"), est),
        name=name,
    )(*a_list, *([w] * n_in), res, *cast)
    return outs if cast else outs[0]


def _swiglu_gu_kernel(x_ref, g_ref, wg_ref, wu_ref, wd_ref, o_ref, wdb_ref, h_ref):
    @pl.when(pl.program_id(1) == 0)
    def _():
        _norm_into(h_ref, x_ref, g_ref)

    h = h_ref[...]
    gate = jnp.dot(h, wg_ref[...], preferred_element_type=F32)
    up = jnp.dot(h, wu_ref[...], preferred_element_type=F32)
    o_ref[...] = (gate * (1.0 / (1.0 + jnp.exp(-gate))) * up).astype(o_ref.dtype)
    wdb_ref[...] = wd_ref[...].astype(BF16)


def _swiglu_gu(x, g, wg, wu, wd, layer, gu_layer, *, tm, tn, name):
    m, k = x.shape
    n = wg.shape[2]
    steps = (m // tm) * (n // tn)
    nj = n // tn
    wd_rows, wd_cols = wd.shape[1], wd.shape[2]
    slab = wd_rows // steps
    assert slab * steps == wd_rows and slab % 16 == 0
    wbytes = wg.dtype.itemsize
    est = (2 * tm * k * 4 + tm * k * 2 + 4 * k * tn * wbytes + 2 * k * tn * 2 + 2 * tm * tn * 2
           + 3 * tm * tn * 4 + 2 * slab * wd_cols * 6 + (4 << 20))
    return pl.pallas_call(
        _swiglu_gu_kernel,
        out_shape=(jax.ShapeDtypeStruct((m, n), BF16),
                   jax.ShapeDtypeStruct((1, wd_rows, wd_cols), BF16)),
        grid=(m // tm, nj),
        in_specs=[
            pl.BlockSpec((tm, k), lambda i, j: (i, 0)),
            pl.BlockSpec((None, 1, k), lambda i, j: (layer, 0, 0)),
            pl.BlockSpec((None, k, tn), lambda i, j: (gu_layer, 0, j)),
            pl.BlockSpec((None, k, tn), lambda i, j: (gu_layer, 0, j)),
            pl.BlockSpec((None, slab, wd_cols), lambda i, j: (layer, i * nj + j, 0)),
        ],
        out_specs=(pl.BlockSpec((tm, tn), lambda i, j: (i, j)),
                   pl.BlockSpec((None, slab, wd_cols), lambda i, j: (0, i * nj + j, 0))),
        scratch_shapes=[pltpu.VMEM((tm, k), BF16)],
        compiler_params=_cparams(("arbitrary", "arbitrary"), est),
        name=name,
    )(x, g, wg, wu, wd)


def _rope_table_kernel(pos_ref, inv_ref, sign_ref, cs_ref):
    ang = pos_ref[...].astype(F32) * inv_ref[...]
    lane = lax.broadcasted_iota(jnp.int32, ang.shape, 1)
    cs_ref[...] = jnp.where(lane < B_ROPE, jnp.cos(ang), jnp.sin(ang) * sign_ref[...])


def _rope_table(pos_col, inv_row, sign_row, *, tm):
    m = pos_col.shape[0]
    return pl.pallas_call(
        _rope_table_kernel,
        out_shape=jax.ShapeDtypeStruct((m, V7X_LANES), F32),
        grid=(m // tm,),
        in_specs=[
            pl.BlockSpec((tm, 1), lambda i: (i, 0)),
            pl.BlockSpec((1, V7X_LANES), lambda i: (0, 0)),
            pl.BlockSpec((1, V7X_LANES), lambda i: (0, 0)),
        ],
        out_specs=pl.BlockSpec((tm, V7X_LANES), lambda i: (i, 0)),
        compiler_params=_cparams(("arbitrary",), 32 << 20),
        name="rope_table",
    )(pos_col, inv_row, sign_row)


def _rope_pair_sum(slot, cs):
    t = slot * cs
    return t + pltpu.roll(t, B_ROPE, axis=1)


def _mla_prep_kernel(ql_ref, rope_ref, kvl_ref, cs_ref, gq_ref, gkv_ref, wq_ref, wkv_ref, wo_ref,
                     qt_ref, kn_ref, vt_ref, kpe_ref, wob_ref, *, q_scale):
    cs = cs_ref[...]
    hq = _rms_rows(ql_ref[...].astype(F32), gq_ref[...]).astype(BF16)
    qf = jnp.dot(hq, wq_ref[...], preferred_element_type=F32)
    for h in range(B_HEADS):
        base = h * B_QK
        nope = qf[:, base:base + B_NOPE]
        pe = _rope_pair_sum(qf[:, base + B_NOPE:base + B_QK], cs)
        qt_ref[base:base + B_NOPE, :] = (nope * q_scale).T.astype(BF16)
        qt_ref[base + B_NOPE:base + B_QK, :] = (pe * q_scale).T.astype(BF16)

    hkv = _rms_rows(kvl_ref[...].astype(F32), gkv_ref[...]).astype(BF16)
    kvf = jnp.dot(hkv, wkv_ref[...], preferred_element_type=F32)
    kn_ref[...] = kvf[:, :B_HEADS * B_NOPE].astype(BF16)
    vt_ref[...] = kvf[:, B_HEADS * B_NOPE:].T.astype(BF16)

    kpe = _rope_pair_sum(rope_ref[...].astype(F32), cs)
    lane = lax.broadcasted_iota(jnp.int32, kpe.shape, 1)
    kpe_ref[...] = jnp.where(lane < B_ROPE, kpe, 0.0).astype(BF16)
    wob_ref[...] = wo_ref[...].astype(BF16)


def _mla_prep(proj, cs, gq, gkv, wq, wkv, wo, layer, *, seq, tm, name):
    m = proj.shape[0]
    tiles_per_seq = seq // tm
    q_scale = (B_NOPE + B_ROPE) ** -0.5 * LOG2E
    wo_rows, wo_cols = wo.shape[1], wo.shape[2]
    slab = wo_rows // (m // tm)
    assert slab * (m // tm) == wo_rows and slab % 16 == 0
    est = (2 * tm * (Q_LORA + KV_LORA + 128) * 2 + 2 * tm * 128 * 4
           + 2 * (Q_LORA + KV_LORA) * 2048 * 2 + 2 * slab * wo_cols * 6
           + 2 * tm * (2048 + 1024 + 1024 + 128) * 2 + 3 * tm * 2048 * 4 + (4 << 20))
    return pl.pallas_call(
        functools.partial(_mla_prep_kernel, q_scale=q_scale),
        out_shape=(
            jax.ShapeDtypeStruct((m // seq * B_HEADS * B_QK, seq), BF16),
            jax.ShapeDtypeStruct((m, B_HEADS * B_NOPE), BF16),
            jax.ShapeDtypeStruct((m // seq * B_WIDTH, seq), BF16),
            jax.ShapeDtypeStruct((m, V7X_LANES), BF16),
            jax.ShapeDtypeStruct((1, wo_rows, wo_cols), BF16),
        ),
        grid=(m // tm,),
        in_specs=[
            pl.BlockSpec((tm, Q_LORA), lambda i: (i, IN_QLAT_OFF // Q_LORA)),
            pl.BlockSpec((tm, V7X_LANES), lambda i: (i, IN_ROPE_OFF // V7X_LANES)),
            pl.BlockSpec((tm, KV_LORA), lambda i: (i, IN_KVLAT_OFF // KV_LORA)),
            pl.BlockSpec((tm, V7X_LANES), lambda i: (i, 0)),
            pl.BlockSpec((None, 1, Q_LORA), lambda i: (layer, 0, 0)),
            pl.BlockSpec((None, 1, KV_LORA), lambda i: (layer, 0, 0)),
            pl.BlockSpec((None, Q_LORA, B_HEADS * B_QK), lambda i: (layer, 0, 0)),
            pl.BlockSpec((None, KV_LORA, 2 * B_WIDTH), lambda i: (layer, 0, 0)),
            pl.BlockSpec((None, slab, wo_cols), lambda i: (layer, i, 0)),
        ],
        out_specs=(
            pl.BlockSpec((B_HEADS * B_QK, tm), lambda i: (i // tiles_per_seq, i % tiles_per_seq)),
            pl.BlockSpec((tm, B_HEADS * B_NOPE), lambda i: (i, 0)),
            pl.BlockSpec((B_WIDTH, tm), lambda i: (i // tiles_per_seq, i % tiles_per_seq)),
            pl.BlockSpec((tm, V7X_LANES), lambda i: (i, 0)),
            pl.BlockSpec((None, slab, wo_cols), lambda i: (0, i, 0)),
        ),
        compiler_params=_cparams(("arbitrary",), est),
        name=name,
    )(proj, proj, proj, cs, gq, gkv, wq, wkv, wo)


def _attn_a_kernel(q_ref, k0_ref, k1_ref, k2_ref, v0_ref, v1_ref, v2_ref, bias_ref, o_ref):
    d, tq = A_HEAD_DIM, A_TQ
    look = 2

    def scores(h):
        hs = slice(h * d, (h + 1) * d)
        k = jnp.concatenate([k0_ref[:, hs], k1_ref[:, hs], k2_ref[:, hs]], axis=0)
        return jnp.dot(k, q_ref[:, hs].T, preferred_element_type=F32)

    tiles = [scores(h) for h in range(look)] + [None] * (A_HEADS - look)
    for h in range(A_HEADS):
        hs = slice(h * d, (h + 1) * d)
        if h + look < A_HEADS:
            tiles[h + look] = scores(h + look)
        v = jnp.concatenate([v0_ref[:, hs], v1_ref[:, hs], v2_ref[:, hs]], axis=0)
        st = bias_ref[h // 2][:, (h % 2) * tq:(h % 2 + 1) * tq] + tiles[h]
        tiles[h] = None
        m = _col_reduce(st, jnp.max)
        p = jnp.exp2(st - m)
        l = _col_reduce(p, jnp.sum)
        ot = lax.dot_general(v, p.astype(BF16), (((0,), (0,)), ((), ())),
                             preferred_element_type=F32)
        o_ref[:, hs] = (ot / l).T.astype(o_ref.dtype)


def _attn_a(proj, bias, layer, *, batch, seq, name):
    m = proj.shape[0]
    nq = seq // A_TQ
    qspec = pl.BlockSpec((A_TQ, A_WIDTH), lambda b, i: (b * nq + i, 0))

    def kv_spec(col_blk, c):
        back = A_KBLKS - 1 - c
        return pl.BlockSpec((A_TQ, A_WIDTH),
                            lambda b, i: (b * nq + jnp.maximum(i - back, 0), col_blk))

    est = 2 * 7 * A_TQ * A_WIDTH * 2 + 2 * A_HEADS * A_TQ * A_WIN * 4 + 2 * A_TQ * A_WIDTH * 2 + (8 << 20)
    return pl.pallas_call(
        _attn_a_kernel,
        out_shape=jax.ShapeDtypeStruct((m, A_WIDTH), BF16),
        grid=(batch, nq),
        in_specs=[qspec,
                  kv_spec(1, 0), kv_spec(1, 1), kv_spec(1, 2),
                  kv_spec(2, 0), kv_spec(2, 1), kv_spec(2, 2),
                  pl.BlockSpec((A_HEADS // 2, None, A_WIN, 2 * A_TQ),
                               lambda b, i: (layer, jnp.minimum(i, A_KBLKS - 1), 0, 0))],
        out_specs=pl.BlockSpec((A_TQ, A_WIDTH), lambda b, i: (b * nq + i, 0)),
        compiler_params=_cparams(("arbitrary", "arbitrary"), est),
        name=name,
    )(proj, proj, proj, proj, proj, proj, proj, bias)


def _col_reduce(x, op):
    rows, cols = x.shape
    slabs = op(x.reshape(B_RED_SLABS, rows // B_RED_SLABS, cols), axis=0)
    return op(slabs, axis=0, keepdims=True)


def _attn_b_kernel(q_ref, kn_ref, kpe_ref, vt_ref, o_ref, m_ref, l_ref, acc_ref, st_ref):
    i = pl.program_id(2)
    hp = B_HEADS_PER_STEP
    m_ref[...] = jnp.full(m_ref.shape, -jnp.inf, F32)
    l_ref[...] = jnp.zeros(l_ref.shape, F32)
    acc_ref[...] = jnp.zeros(acc_ref.shape, F32)

    def scores(j, h):
        r0 = pl.multiple_of(j * B_TK, B_TK)
        k = jnp.concatenate([kn_ref[pl.ds(r0, B_TK), h * B_NOPE:(h + 1) * B_NOPE],
                             kpe_ref[pl.ds(r0, B_TK), :]], axis=1)
        return jnp.dot(k, q_ref[h * B_QK:(h + 1) * B_QK, :], preferred_element_type=F32)

    def softmax_pv(j, h, st, allowed):
        r0 = pl.multiple_of(j * B_TK, B_TK)
        if allowed is not None:
            st = jnp.where(allowed, st, -jnp.inf)
        m_prev = m_ref[h]
        m_new = jnp.maximum(m_prev, _col_reduce(st, jnp.max))
        alpha = jnp.exp2(m_prev - m_new)
        p = jnp.exp2(st - m_new)
        l_ref[h] = alpha * l_ref[h] + _col_reduce(p, jnp.sum)
        pv = jnp.dot(vt_ref[h * B_V:(h + 1) * B_V, pl.ds(r0, B_TK)], p.astype(BF16),
                     preferred_element_type=F32)
        acc_ref[h] = alpha * acc_ref[h] + pv
        m_ref[h] = m_new

    def key_block(j, masked):
        allowed = None
        if masked:
            krow = lax.broadcasted_iota(jnp.int32, (B_TK, B_TQ), 0)
            qcol = lax.broadcasted_iota(jnp.int32, (B_TK, B_TQ), 1)
            allowed = (krow // CHUNK) <= (qcol // CHUNK)
        look = B_LOOKAHEAD
        tiles = [st_ref[h] for h in range(look)] + [None] * (hp - look)
        for h in range(hp):
            ahead = h + look
            if ahead < hp:
                tiles[ahead] = scores(j, ahead)
            elif not masked:
                st_ref[ahead - hp] = scores(j + 1, ahead - hp)
            softmax_pv(j, h, tiles[h], allowed)
            tiles[h] = None

    def body(j, carry):
        key_block(j, False)
        return carry

    for h in range(B_LOOKAHEAD):
        st_ref[h] = scores(0, h)
    lax.fori_loop(0, i, body, 0)
    key_block(i, True)

    for h in range(hp):
        o_ref[:, h * B_V:(h + 1) * B_V] = (acc_ref[h] / l_ref[h]).T.astype(o_ref.dtype)


def _attn_b(qt, kn, kpe, vt, *, batch, seq, name):
    m = kn.shape[0]
    nq = seq // B_TQ
    hp = B_HEADS_PER_STEP
    est = (2 * B_TQ * hp * B_QK * 2 + 2 * seq * hp * (B_NOPE + B_V) * 2 + 2 * seq * 128 * 2
           + 2 * B_TQ * hp * B_V * 2 + hp * (B_V + 16) * B_TQ * 4 + 8 * B_TQ * B_TK * 4 + (4 << 20))
    return pl.pallas_call(
        _attn_b_kernel,
        out_shape=jax.ShapeDtypeStruct((m, B_WIDTH), BF16),
        grid=(batch, B_HEADS // hp, nq),
        in_specs=[
            pl.BlockSpec((hp * B_QK, B_TQ), lambda b, g, i: (b * (B_HEADS // hp) + g, i)),
            pl.BlockSpec((seq, hp * B_NOPE), lambda b, g, i: (b, g)),
            pl.BlockSpec((seq, V7X_LANES), lambda b, g, i: (b, 0)),
            pl.BlockSpec((hp * B_V, seq), lambda b, g, i: (b * (B_HEADS // hp) + g, 0)),
        ],
        out_specs=pl.BlockSpec((B_TQ, hp * B_V), lambda b, g, i: (b * nq + i, g)),
        scratch_shapes=[
            pltpu.VMEM((hp, 1, B_TQ), F32),
            pltpu.VMEM((hp, 1, B_TQ), F32),
            pltpu.VMEM((hp, B_V, B_TQ), F32),
            pltpu.VMEM((B_LOOKAHEAD, B_TK, B_TQ), F32),
        ],
        compiler_params=_cparams(("arbitrary", "arbitrary", "arbitrary"), est),
        name=name,
    )(qt, kn, kpe, vt)


def _cross_kernel(x_ref, g_ref, wq_ref, k_ref, v_ref, wo_ref, o_ref):
    scale = X_HEAD_DIM ** -0.5
    rows = x_ref.shape[0] // X_ROW_PARTS
    for r in range(X_ROW_PARTS):
        rs = slice(r * rows, (r + 1) * rows)
        x = x_ref[rs, :]
        hn = _rms_rows(x, g_ref[...]).astype(BF16)
        q = (jnp.dot(hn, wq_ref[...], preferred_element_type=F32) * scale).astype(BF16)
        outs = []
        for h in range(X_HEADS):
            hs = slice(h * X_HEAD_DIM, (h + 1) * X_HEAD_DIM)
            s = lax.dot_general(q[:, hs], k_ref[:, hs], (((1,), (1,)), ((), ())),
                                preferred_element_type=F32)
            m = jnp.max(s, axis=1, keepdims=True)
            p = jnp.exp(s - m)
            l = jnp.sum(p, axis=1, keepdims=True)
            o = jnp.dot(p.astype(BF16), v_ref[:, hs], preferred_element_type=F32)
            outs.append((o / l).astype(BF16))
        o_all = jnp.concatenate(outs, axis=1)
        o_ref[rs, :] = x + jnp.dot(o_all, wo_ref[...], preferred_element_type=F32)


def _cross(x, g, wq, kvm, wo, layer, *, seq, tm, name):
    m, d = x.shape
    blocks_per_batch = seq // tm
    est = (4 * tm * d * 4 + tm * d * 2 + 2 * d * X_WIDTH * 2 + 2 * X_WIDTH * d * 2
           + 4 * N_MEM * X_WIDTH * 2 + 4 * tm * N_MEM * 4 + 2 * tm * d * 4 + (4 << 20))
    return pl.pallas_call(
        _cross_kernel,
        out_shape=jax.ShapeDtypeStruct((m, d), F32),
        grid=(m // tm,),
        in_specs=[
            pl.BlockSpec((tm, d), lambda i: (i, 0)),
            pl.BlockSpec((None, 1, d), lambda i: (layer, 0, 0)),
            pl.BlockSpec((None, d, X_WIDTH), lambda i: (layer, 0, 0)),
            pl.BlockSpec((N_MEM, X_WIDTH), lambda i: (i // blocks_per_batch, 2 * layer)),
            pl.BlockSpec((N_MEM, X_WIDTH), lambda i: (i // blocks_per_batch, 2 * layer + 1)),
            pl.BlockSpec((None, X_WIDTH, d), lambda i: (layer, 0, 0)),
        ],
        out_specs=pl.BlockSpec((tm, d), lambda i: (i, 0)),
        compiler_params=_cparams(("arbitrary",), est),
        name=name,
    )(x, g, wq, kvm, kvm, wo)


def _final_norm_kernel(x_ref, g_ref, o_ref):
    o_ref[...] = _rms_rows(x_ref[...], g_ref[...])


def _final_norm(x, g, *, tm):
    m, d = x.shape
    return pl.pallas_call(
        _final_norm_kernel,
        out_shape=jax.ShapeDtypeStruct((m, d), F32),
        grid=(m // tm,),
        in_specs=[pl.BlockSpec((tm, d), lambda i: (i, 0)), pl.BlockSpec((1, d), lambda i: (0, 0))],
        out_specs=pl.BlockSpec((tm, d), lambda i: (i, 0)),
        compiler_params=_cparams(("arbitrary",), 6 * tm * d * 4 + (4 << 20)),
        name="final_norm",
    )(x, g)


RELAYOUT_COLS = 256
RELAYOUT_ROPE_BLK = IN_ROPE_OFF // RELAYOUT_COLS


def _w_in_relayout_kernel(wt_ref, o_ref):
    j = pl.program_id(1)

    @pl.when(j != RELAYOUT_ROPE_BLK)
    def _():
        factor = jnp.where(j < A_WIDTH // RELAYOUT_COLS, A_SCORE_SCALE, 1.0)
        o_ref[...] = (wt_ref[...] * factor).T.astype(BF16)

    @pl.when(j == RELAYOUT_ROPE_BLK)
    def _():
        half = B_ROPE // 2
        kr = wt_ref[:B_ROPE, :]
        slot = jnp.concatenate([kr, kr[half:], kr[:half],
                                jnp.zeros((RELAYOUT_COLS - 2 * B_ROPE, kr.shape[1]), F32)], axis=0)
        o_ref[...] = slot.T.astype(BF16)


def _w_in_relayout(w_in_t):
    layers, n, k = w_in_t.shape
    lat_blk = IN_ROPE_OFF // RELAYOUT_COLS
    rope_src = (IN_ROPE_OFF + KV_LORA) // RELAYOUT_COLS
    assert IN_KVLAT_OFF == (RELAYOUT_ROPE_BLK + 1) * RELAYOUT_COLS

    def src_block(l, j):
        return (l, jnp.where(j < RELAYOUT_ROPE_BLK, j,
                             jnp.where(j == RELAYOUT_ROPE_BLK, rope_src, j - 1 + lat_blk - RELAYOUT_ROPE_BLK)), 0)

    return pl.pallas_call(
        _w_in_relayout_kernel,
        out_shape=jax.ShapeDtypeStruct((layers, k, IN_PAD_WIDTH), BF16),
        grid=(layers, IN_PAD_WIDTH // RELAYOUT_COLS),
        in_specs=[pl.BlockSpec((None, RELAYOUT_COLS, k), src_block)],
        out_specs=pl.BlockSpec((None, k, RELAYOUT_COLS), lambda l, j: (l, 0, j)),
        compiler_params=_cparams(("arbitrary", "arbitrary"), 32 << 20),
        name="w_in_relayout",
    )(w_in_t)


def _prep_weights(w_in, w_uq, w_ukv):
    swap = (jnp.arange(B_ROPE) + B_ROPE // 2) % B_ROPE
    w_in_p = _w_in_relayout(jnp.swapaxes(w_in, 1, 2))

    uq = w_uq.reshape(DEPTH, Q_LORA, B_HEADS, B_NOPE + B_ROPE)
    uq_pe = uq[..., B_NOPE:]
    wq_p = jnp.concatenate([uq[..., :B_NOPE], uq_pe, uq_pe[..., swap]], axis=-1)
    wq_p = wq_p.reshape(DEPTH, Q_LORA, B_HEADS * B_QK).astype(BF16)

    ukv = w_ukv.reshape(DEPTH, KV_LORA, B_HEADS, B_NOPE + B_V)
    wkv_p = jnp.concatenate([ukv[..., :B_NOPE].reshape(DEPTH, KV_LORA, B_HEADS * B_NOPE),
                             ukv[..., B_NOPE:].reshape(DEPTH, KV_LORA, B_WIDTH)],
                            axis=-1).astype(BF16)

    return w_in_p, wq_p, wkv_p


def _bias_table_kernel(w_ref, o_ref):
    v = pl.program_id(1)
    r = lax.broadcasted_iota(jnp.int32, (A_TQ, A_WIN), 0)
    t = lax.broadcasted_iota(jnp.int32, (A_TQ, A_WIN), 1)
    dchunk = t // CHUNK - r // CHUNK
    valid = (dchunk >= 0) & (dchunk <= LEFT_CHUNKS) & (v - (A_KBLKS - 1) + t // A_TQ >= 0)
    for hh in range(2):
        w = jnp.broadcast_to(w_ref[hh], (A_TQ, A_BIAS_VEC))
        tab = pltpu.roll(w, 0, axis=1, stride=1, stride_axis=0)[:, :A_WIN]
        o_ref[0, 0, :, hh * A_TQ:(hh + 1) * A_TQ] = jnp.where(valid, tab * LOG2E, -jnp.inf).T


def _relpos_bias_tables(rel_bias):
    u = jnp.arange(A_BIAS_VEC)
    delta = jnp.where(u <= A_WIN, u, u - A_BIAS_VEC)
    idx = jnp.clip(LEFT_CHUNKS * CHUNK - delta, -REL_CLIP, REL_CLIP) + REL_CLIP
    vec = rel_bias[:, :, idx].reshape(DEPTH * A_HEADS, 1, A_BIAS_VEC).astype(F32)
    return pl.pallas_call(
        _bias_table_kernel,
        out_shape=jax.ShapeDtypeStruct((DEPTH * A_HEADS // 2, A_KBLKS, A_WIN, 2 * A_TQ), F32),
        grid=(DEPTH * A_HEADS // 2, A_KBLKS),
        in_specs=[pl.BlockSpec((2, 1, A_BIAS_VEC), lambda i, v: (i, 0, 0))],
        out_specs=pl.BlockSpec((1, 1, A_WIN, 2 * A_TQ), lambda i, v: (i, v, 0, 0)),
        compiler_params=_cparams(("arbitrary", "arbitrary"), 32 << 20),
        name="bias_table",
    )(vec)


def kernel(x, mem, positions, norm_mix, w_in, rel_bias, q_norm, kv_norm, w_uq, w_ukv, w_out,
           norm_mem, mem_norm, w_xq, w_xkv, w_xo, norm_ffn, w_gate, w_up, w_down, norm_final):
    b, s, d = x.shape
    m = b * s
    assert (b, s, d) == (4, 4096, D_MODEL) and mem.shape == (b, N_MEM, d)

    w_in_p, wq_p, wkv_p = _prep_weights(w_in, w_uq, w_ukv)
    w_xq_b = w_xq.astype(BF16)
    w_xo_b = w_xo.astype(BF16)
    bias_tab = _relpos_bias_tables(rel_bias)

    half = B_ROPE // 2
    inv = ROPE_THETA ** (-jnp.arange(half, dtype=F32) / half)
    inv_row = jnp.tile(inv, 4)[None, :]
    sign_row = jnp.concatenate([jnp.ones((B_ROPE,), F32), -jnp.ones((half,), F32),
                                jnp.ones((half,), F32)])[None, :]
    cs = _rope_table(positions.reshape(m, 1), inv_row, sign_row, tm=2048)

    kvm = _mem_kv(mem.reshape(b * N_MEM, d), mem_norm[None, :], w_xkv)

    norm_mix, norm_mem, norm_ffn, q_norm, kv_norm = (
        g[:, None, :] for g in (norm_mix, norm_mem, norm_ffn, q_norm, kv_norm))
    xf = x.reshape(m, d)
    for l in range(DEPTH):
        proj = _norm_matmul(xf, norm_mix, w_in_p, l, tm=1024, tn=1536, name=f"in_proj_{l}")
        oa = _attn_a(proj, bias_tab, l, batch=b, seq=s, name=f"attn_a_{l}")
        qt, kn, vt, kpe, w_out_l = _mla_prep(proj, cs, q_norm, kv_norm, wq_p, wkv_p, w_out, l,
                                             seq=s, tm=512, name=f"mla_prep_{l}")
        ob = _attn_b(qt, kn, kpe, vt, batch=b, seq=s, name=f"attn_b_{l}")
        xf, w_gate_l, w_up_l = _mm_res([oa, ob], w_out_l, xf, 0, tm=1024, tn=1024,
                                       name=f"out_proj_{l}", cast=(w_gate, w_up), cast_layer=l)
        xf = _cross(xf, norm_mem, w_xq_b, kvm, w_xo_b, l, seq=s, tm=512, name=f"cross_{l}")
        act, w_down_l = _swiglu_gu(xf, norm_ffn, w_gate_l, w_up_l, w_down, l, 0, tm=1024, tn=512,
                                   name=f"swiglu_gu_{l}")
        xf = _mm_res([act], w_down_l, xf, 0, tm=1024, tn=512, name=f"swiglu_down_{l}")
    out = _final_norm(xf, norm_final[None, :], tm=512)
    return out.reshape(b, s, d)
```

```python
import functools
import math

import jax
import jax.numpy as jnp
from jax import lax
from jax.experimental import pallas as pl
from jax.experimental.pallas import tpu as pltpu

D_MODEL = 2048
DEPTH = 4
CHUNK = 64
LEFT_CHUNKS = 8
A_HEAD_DIM = 128
A_HEADS = 8
A_WIDTH = A_HEADS * A_HEAD_DIM
REL_CLIP = 128
B_NOPE = 128
B_ROPE = 64
B_V = 128
B_HEADS = 8
B_WIDTH = B_HEADS * B_V
Q_LORA = 768
KV_LORA = 512
ROPE_THETA = 10000.0
N_MEM = 256
X_HEADS = 4
X_HEAD_DIM = 128
X_WIDTH = X_HEADS * X_HEAD_DIM
X_ROW_PARTS = 2
D_FF = 5632
EPS = 1e-6

V7X_LANES = 128
V7X_VMEM_BYTES = 64 * 1024 * 1024
V7X_VMEM_REQUEST_CAP = 56 * 1024 * 1024

BF16 = jnp.bfloat16
F32 = jnp.float32

IN_QLAT_OFF = 3 * A_WIDTH
IN_ROPE_OFF = IN_QLAT_OFF + Q_LORA
IN_KVLAT_OFF = IN_ROPE_OFF + 2 * V7X_LANES
IN_PAD_WIDTH = IN_KVLAT_OFF + KV_LORA

A_TQ = 4 * CHUNK
A_WIN = A_TQ + LEFT_CHUNKS * CHUNK
A_KBLKS = A_WIN // A_TQ
A_BIAS_VEC = 1024

B_TQ = 512
B_TK = 512
B_HEADS_PER_STEP = 4
B_LOOKAHEAD = 2
B_RED_SLABS = 8
B_QK = B_NOPE + 2 * B_ROPE

LOG2E = math.log2(math.e)
A_SCORE_SCALE = A_HEAD_DIM ** -0.5 * LOG2E


def _cparams(sem, vmem_bytes):
    limit = int(min(max(vmem_bytes, 32 * 1024 * 1024), V7X_VMEM_REQUEST_CAP))
    return pltpu.CompilerParams(dimension_semantics=sem, vmem_limit_bytes=limit)


def _rms_rows(x, g):
    ms = jnp.mean(x * x, axis=-1, keepdims=True)
    return (x * lax.rsqrt(ms + EPS)) * g


def _norm_into(h_ref, x_ref, g_ref, rows=128):
    n = x_ref.shape[0] // rows

    def body(c, carry):
        r0 = pl.multiple_of(c * rows, rows)
        x = x_ref[pl.ds(r0, rows), :].astype(F32)
        h_ref[pl.ds(r0, rows), :] = _rms_rows(x, g_ref[...]).astype(h_ref.dtype)
        return carry

    lax.fori_loop(0, n, body, 0)


def _norm_matmul_kernel(x_ref, g_ref, w_ref, o_ref, h_ref):
    @pl.when(pl.program_id(1) == 0)
    def _():
        _norm_into(h_ref, x_ref, g_ref)

    o_ref[...] = jnp.dot(h_ref[...], w_ref[...], preferred_element_type=F32).astype(o_ref.dtype)


def _norm_matmul(x, g, w, layer, *, tm, tn, name):
    m, k = x.shape
    n = w.shape[2]
    xb = x.dtype.itemsize
    est = 2 * tm * k * xb + tm * k * 2 + 2 * k * tn * 2 + 2 * tm * tn * 2 + tm * tn * 4 + (4 << 20)
    return pl.pallas_call(
        _norm_matmul_kernel,
        out_shape=jax.ShapeDtypeStruct((m, n), BF16),
        grid=(m // tm, n // tn),
        in_specs=[
            pl.BlockSpec((tm, k), lambda i, j: (i, 0)),
            pl.BlockSpec((None, 1, k), lambda i, j: (layer, 0, 0)),
            pl.BlockSpec((None, k, tn), lambda i, j: (layer, 0, j)),
        ],
        out_specs=pl.BlockSpec((tm, tn), lambda i, j: (i, j)),
        scratch_shapes=[pltpu.VMEM((tm, k), BF16)],
        compiler_params=_cparams(("arbitrary", "arbitrary"), est),
        name=name,
    )(x, g, w)


def _mem_kv_kernel(x_ref, g_ref, w_ref, o_ref, h_ref):
    @pl.when(pl.program_id(0) == 0)
    def _():
        _norm_into(h_ref, x_ref, g_ref)

    o_ref[...] = jnp.dot(h_ref[...], w_ref[...].astype(BF16),
                         preferred_element_type=F32).astype(o_ref.dtype)


def _mem_kv(mem2d, g, w_xkv):
    m, k = mem2d.shape
    layers, _, n = w_xkv.shape
    est = 2 * m * k * 4 + m * k * 2 + 2 * k * n * 4 + k * n * 2 + 2 * m * n * 2 + m * n * 4 + (4 << 20)
    return pl.pallas_call(
        _mem_kv_kernel,
        out_shape=jax.ShapeDtypeStruct((m, layers * n), BF16),
        grid=(layers,),
        in_specs=[
            pl.BlockSpec((m, k), lambda l: (0, 0)),
            pl.BlockSpec((1, k), lambda l: (0, 0)),
            pl.BlockSpec((None, k, n), lambda l: (l, 0, 0)),
        ],
        out_specs=pl.BlockSpec((m, n), lambda l: (0, l)),
        scratch_shapes=[pltpu.VMEM((m, k), BF16)],
        compiler_params=_cparams(("arbitrary",), est),
        name="mem_kv",
    )(mem2d, g, w_xkv)


def _mm_res_kernel(*refs, n_in):
    a_refs = refs[:n_in]
    w_refs = refs[n_in:2 * n_in]
    r_ref = refs[2 * n_in]
    o_ref = refs[2 * n_in + 1]
    acc = r_ref[...]
    for a_ref, w_ref in zip(a_refs, w_refs):
        acc = acc + jnp.dot(a_ref[...], w_ref[...], preferred_element_type=F32)
    o_ref[...] = acc


def _mm_res(a_list, w, res, layer, *, tm, tn, name):
    m, n = res.shape
    n_in = len(a_list)
    in_specs = []
    w_specs = []
    est = (4 + 2 * n_in) * tm * tn * 4 + (4 << 20)
    for idx, a in enumerate(a_list):
        k = a.shape[1]
        assert w.shape[1] == n_in * k
        in_specs.append(pl.BlockSpec((tm, k), lambda i, j: (i, 0)))
        w_specs.append(pl.BlockSpec((None, k, tn), lambda i, j, idx=idx: (layer, idx, j)))
        est += 2 * tm * k * 2 + 2 * k * tn * 2
    in_specs += w_specs
    in_specs.append(pl.BlockSpec((tm, tn), lambda i, j: (i, j)))
    return pl.pallas_call(
        functools.partial(_mm_res_kernel, n_in=n_in),
        out_shape=jax.ShapeDtypeStruct((m, n), F32),
        grid=(m // tm, n // tn),
        in_specs=in_specs,
        out_specs=pl.BlockSpec((tm, tn), lambda i, j: (i, j)),
        compiler_params=_cparams(("arbitrary", "arbitrary"), est),
        name=name,
    )(*a_list, *([w] * n_in), res)


def _swiglu_gu_kernel(x_ref, g_ref, wg_ref, wu_ref, wd_ref, o_ref, wdb_ref, h_ref):
    @pl.when(pl.program_id(1) == 0)
    def _():
        _norm_into(h_ref, x_ref, g_ref)

    h = h_ref[...]
    gate = jnp.dot(h, wg_ref[...], preferred_element_type=F32)
    up = jnp.dot(h, wu_ref[...], preferred_element_type=F32)
    o_ref[...] = (gate * (1.0 / (1.0 + jnp.exp(-gate))) * up).astype(o_ref.dtype)
    wdb_ref[...] = wd_ref[...].astype(BF16)


def _swiglu_gu(x, g, wg, wu, wd, layer, gu_layer, *, tm, tn, name):
    m, k = x.shape
    n = wg.shape[2]
    steps = (m // tm) * (n // tn)
    nj = n // tn
    wd_rows, wd_cols = wd.shape[1], wd.shape[2]
    slab = wd_rows // steps
    assert slab * steps == wd_rows and slab % 16 == 0
    wbytes = wg.dtype.itemsize
    est = (2 * tm * k * 4 + tm * k * 2 + 4 * k * tn * wbytes + 2 * k * tn * 2 + 2 * tm * tn * 2
           + 3 * tm * tn * 4 + 2 * slab * wd_cols * 6 + (4 << 20))
    return pl.pallas_call(
        _swiglu_gu_kernel,
        out_shape=(jax.ShapeDtypeStruct((m, n), BF16),
                   jax.ShapeDtypeStruct((1, wd_rows, wd_cols), BF16)),
        grid=(m // tm, nj),
        in_specs=[
            pl.BlockSpec((tm, k), lambda i, j: (i, 0)),
            pl.BlockSpec((None, 1, k), lambda i, j: (layer, 0, 0)),
            pl.BlockSpec((None, k, tn), lambda i, j: (gu_layer, 0, j)),
            pl.BlockSpec((None, k, tn), lambda i, j: (gu_layer, 0, j)),
            pl.BlockSpec((None, slab, wd_cols), lambda i, j: (layer, i * nj + j, 0)),
        ],
        out_specs=(pl.BlockSpec((tm, tn), lambda i, j: (i, j)),
                   pl.BlockSpec((None, slab, wd_cols), lambda i, j: (0, i * nj + j, 0))),
        scratch_shapes=[pltpu.VMEM((tm, k), BF16)],
        compiler_params=_cparams(("arbitrary", "arbitrary"), est),
        name=name,
    )(x, g, wg, wu, wd)


def _rope_table_kernel(pos_ref, inv_ref, sign_ref, cs_ref):
    ang = pos_ref[...].astype(F32) * inv_ref[...]
    lane = lax.broadcasted_iota(jnp.int32, ang.shape, 1)
    cs_ref[...] = jnp.where(lane < B_ROPE, jnp.cos(ang), jnp.sin(ang) * sign_ref[...])


def _rope_table(pos_col, inv_row, sign_row, *, tm):
    m = pos_col.shape[0]
    return pl.pallas_call(
        _rope_table_kernel,
        out_shape=jax.ShapeDtypeStruct((m, V7X_LANES), F32),
        grid=(m // tm,),
        in_specs=[
            pl.BlockSpec((tm, 1), lambda i: (i, 0)),
            pl.BlockSpec((1, V7X_LANES), lambda i: (0, 0)),
            pl.BlockSpec((1, V7X_LANES), lambda i: (0, 0)),
        ],
        out_specs=pl.BlockSpec((tm, V7X_LANES), lambda i: (i, 0)),
        compiler_params=_cparams(("arbitrary",), 32 << 20),
        name="rope_table",
    )(pos_col, inv_row, sign_row)


def _rope_pair_sum(slot, cs):
    t = slot * cs
    return t + pltpu.roll(t, B_ROPE, axis=1)


def _mla_prep_kernel(ql_ref, rope_ref, kvl_ref, cs_ref, gq_ref, gkv_ref, wq_ref, wkv_ref, wo_ref,
                     qt_ref, kn_ref, vt_ref, kpe_ref, wob_ref, *, q_scale):
    cs = cs_ref[...]
    hq = _rms_rows(ql_ref[...].astype(F32), gq_ref[...]).astype(BF16)
    qf = jnp.dot(hq, wq_ref[...], preferred_element_type=F32)
    for h in range(B_HEADS):
        base = h * B_QK
        nope = qf[:, base:base + B_NOPE]
        pe = _rope_pair_sum(qf[:, base + B_NOPE:base + B_QK], cs)
        qt_ref[base:base + B_NOPE, :] = (nope * q_scale).T.astype(BF16)
        qt_ref[base + B_NOPE:base + B_QK, :] = (pe * q_scale).T.astype(BF16)

    hkv = _rms_rows(kvl_ref[...].astype(F32), gkv_ref[...]).astype(BF16)
    kvf = jnp.dot(hkv, wkv_ref[...], preferred_element_type=F32)
    kn_ref[...] = kvf[:, :B_HEADS * B_NOPE].astype(BF16)
    vt_ref[...] = kvf[:, B_HEADS * B_NOPE:].T.astype(BF16)

    kpe = _rope_pair_sum(rope_ref[...].astype(F32), cs)
    lane = lax.broadcasted_iota(jnp.int32, kpe.shape, 1)
    kpe_ref[...] = jnp.where(lane < B_ROPE, kpe, 0.0).astype(BF16)
    wob_ref[...] = wo_ref[...].astype(BF16)


def _mla_prep(proj, cs, gq, gkv, wq, wkv, wo, layer, *, seq, tm, name):
    m = proj.shape[0]
    tiles_per_seq = seq // tm
    q_scale = (B_NOPE + B_ROPE) ** -0.5 * LOG2E
    wo_rows, wo_cols = wo.shape[1], wo.shape[2]
    slab = wo_rows // (m // tm)
    assert slab * (m // tm) == wo_rows and slab % 16 == 0
    est = (2 * tm * (Q_LORA + KV_LORA + 128) * 2 + 2 * tm * 128 * 4
           + 2 * (Q_LORA + KV_LORA) * 2048 * 2 + 2 * slab * wo_cols * 6
           + 2 * tm * (2048 + 1024 + 1024 + 128) * 2 + 3 * tm * 2048 * 4 + (4 << 20))
    return pl.pallas_call(
        functools.partial(_mla_prep_kernel, q_scale=q_scale),
        out_shape=(
            jax.ShapeDtypeStruct((m // seq * B_HEADS * B_QK, seq), BF16),
            jax.ShapeDtypeStruct((m, B_HEADS * B_NOPE), BF16),
            jax.ShapeDtypeStruct((m // seq * B_WIDTH, seq), BF16),
            jax.ShapeDtypeStruct((m, V7X_LANES), BF16),
            jax.ShapeDtypeStruct((1, wo_rows, wo_cols), BF16),
        ),
        grid=(m // tm,),
        in_specs=[
            pl.BlockSpec((tm, Q_LORA), lambda i: (i, IN_QLAT_OFF // Q_LORA)),
            pl.BlockSpec((tm, V7X_LANES), lambda i: (i, IN_ROPE_OFF // V7X_LANES)),
            pl.BlockSpec((tm, KV_LORA), lambda i: (i, IN_KVLAT_OFF // KV_LORA)),
            pl.BlockSpec((tm, V7X_LANES), lambda i: (i, 0)),
            pl.BlockSpec((None, 1, Q_LORA), lambda i: (layer, 0, 0)),
            pl.BlockSpec((None, 1, KV_LORA), lambda i: (layer, 0, 0)),
            pl.BlockSpec((None, Q_LORA, B_HEADS * B_QK), lambda i: (layer, 0, 0)),
            pl.BlockSpec((None, KV_LORA, 2 * B_WIDTH), lambda i: (layer, 0, 0)),
            pl.BlockSpec((None, slab, wo_cols), lambda i: (layer, i, 0)),
        ],
        out_specs=(
            pl.BlockSpec((B_HEADS * B_QK, tm), lambda i: (i // tiles_per_seq, i % tiles_per_seq)),
            pl.BlockSpec((tm, B_HEADS * B_NOPE), lambda i: (i, 0)),
            pl.BlockSpec((B_WIDTH, tm), lambda i: (i // tiles_per_seq, i % tiles_per_seq)),
            pl.BlockSpec((tm, V7X_LANES), lambda i: (i, 0)),
            pl.BlockSpec((None, slab, wo_cols), lambda i: (0, i, 0)),
        ),
        compiler_params=_cparams(("arbitrary",), est),
        name=name,
    )(proj, proj, proj, cs, gq, gkv, wq, wkv, wo)


def _attn_a_kernel(q_ref, k0_ref, k1_ref, k2_ref, v0_ref, v1_ref, v2_ref, bias_ref, o_ref):
    d, tq = A_HEAD_DIM, A_TQ
    look = 2

    def scores(h):
        hs = slice(h * d, (h + 1) * d)
        k = jnp.concatenate([k0_ref[:, hs], k1_ref[:, hs], k2_ref[:, hs]], axis=0)
        return jnp.dot(k, q_ref[:, hs].T, preferred_element_type=F32)

    tiles = [scores(h) for h in range(look)] + [None] * (A_HEADS - look)
    for h in range(A_HEADS):
        hs = slice(h * d, (h + 1) * d)
        if h + look < A_HEADS:
            tiles[h + look] = scores(h + look)
        v = jnp.concatenate([v0_ref[:, hs], v1_ref[:, hs], v2_ref[:, hs]], axis=0)
        st = bias_ref[h // 2][:, (h % 2) * tq:(h % 2 + 1) * tq] + tiles[h]
        tiles[h] = None
        m = _col_reduce(st, jnp.max)
        p = jnp.exp2(st - m)
        l = _col_reduce(p, jnp.sum)
        ot = lax.dot_general(v, p.astype(BF16), (((0,), (0,)), ((), ())),
                             preferred_element_type=F32)
        o_ref[:, hs] = (ot / l).T.astype(o_ref.dtype)


def _attn_a(proj, bias, layer, *, batch, seq, name):
    m = proj.shape[0]
    nq = seq // A_TQ
    qspec = pl.BlockSpec((A_TQ, A_WIDTH), lambda b, i: (b * nq + i, 0))

    def kv_spec(col_blk, c):
        back = A_KBLKS - 1 - c
        return pl.BlockSpec((A_TQ, A_WIDTH),
                            lambda b, i: (b * nq + jnp.maximum(i - back, 0), col_blk))

    est = 2 * 7 * A_TQ * A_WIDTH * 2 + 2 * A_HEADS * A_TQ * A_WIN * 4 + 2 * A_TQ * A_WIDTH * 2 + (8 << 20)
    return pl.pallas_call(
        _attn_a_kernel,
        out_shape=jax.ShapeDtypeStruct((m, A_WIDTH), BF16),
        grid=(batch, nq),
        in_specs=[qspec,
                  kv_spec(1, 0), kv_spec(1, 1), kv_spec(1, 2),
                  kv_spec(2, 0), kv_spec(2, 1), kv_spec(2, 2),
                  pl.BlockSpec((A_HEADS // 2, None, A_WIN, 2 * A_TQ),
                               lambda b, i: (layer, jnp.minimum(i, A_KBLKS - 1), 0, 0))],
        out_specs=pl.BlockSpec((A_TQ, A_WIDTH), lambda b, i: (b * nq + i, 0)),
        compiler_params=_cparams(("arbitrary", "arbitrary"), est),
        name=name,
    )(proj, proj, proj, proj, proj, proj, proj, bias)


def _col_reduce(x, op):
    rows, cols = x.shape
    slabs = op(x.reshape(B_RED_SLABS, rows // B_RED_SLABS, cols), axis=0)
    return op(slabs, axis=0, keepdims=True)


def _attn_b_kernel(q_ref, kn_ref, kpe_ref, vt_ref, wg_ref, wu_ref, o_ref, wgb_ref, wub_ref,
                   m_ref, l_ref, acc_ref, st_ref):
    i = pl.program_id(2)
    hp = B_HEADS_PER_STEP
    m_ref[...] = jnp.full(m_ref.shape, -jnp.inf, F32)
    l_ref[...] = jnp.zeros(l_ref.shape, F32)
    acc_ref[...] = jnp.zeros(acc_ref.shape, F32)

    def scores(j, h):
        r0 = pl.multiple_of(j * B_TK, B_TK)
        k = jnp.concatenate([kn_ref[pl.ds(r0, B_TK), h * B_NOPE:(h + 1) * B_NOPE],
                             kpe_ref[pl.ds(r0, B_TK), :]], axis=1)
        return jnp.dot(k, q_ref[h * B_QK:(h + 1) * B_QK, :], preferred_element_type=F32)

    def softmax_pv(j, h, st, allowed):
        r0 = pl.multiple_of(j * B_TK, B_TK)
        if allowed is not None:
            st = jnp.where(allowed, st, -jnp.inf)
        m_prev = m_ref[h]
        m_new = jnp.maximum(m_prev, _col_reduce(st, jnp.max))
        alpha = jnp.exp2(m_prev - m_new)
        p = jnp.exp2(st - m_new)
        l_ref[h] = alpha * l_ref[h] + _col_reduce(p, jnp.sum)
        pv = jnp.dot(vt_ref[h * B_V:(h + 1) * B_V, pl.ds(r0, B_TK)], p.astype(BF16),
                     preferred_element_type=F32)
        acc_ref[h] = alpha * acc_ref[h] + pv
        m_ref[h] = m_new

    def key_block(j, masked):
        allowed = None
        if masked:
            krow = lax.broadcasted_iota(jnp.int32, (B_TK, B_TQ), 0)
            qcol = lax.broadcasted_iota(jnp.int32, (B_TK, B_TQ), 1)
            allowed = (krow // CHUNK) <= (qcol // CHUNK)
        look = B_LOOKAHEAD
        tiles = [st_ref[h] for h in range(look)] + [None] * (hp - look)
        for h in range(hp):
            ahead = h + look
            if ahead < hp:
                tiles[ahead] = scores(j, ahead)
            elif not masked:
                st_ref[ahead - hp] = scores(j + 1, ahead - hp)
            softmax_pv(j, h, tiles[h], allowed)
            tiles[h] = None

    def body(j, carry):
        key_block(j, False)
        return carry

    for h in range(B_LOOKAHEAD):
        st_ref[h] = scores(0, h)
    lax.fori_loop(0, i, body, 0)
    key_block(i, True)

    for h in range(hp):
        o_ref[:, h * B_V:(h + 1) * B_V] = (acc_ref[h] / l_ref[h]).T.astype(o_ref.dtype)

    wgb_ref[...] = wg_ref[...].astype(BF16)
    wub_ref[...] = wu_ref[...].astype(BF16)


def _attn_b(qt, kn, kpe, vt, wg, wu, layer, *, batch, seq, name):
    m = kn.shape[0]
    nq = seq // B_TQ
    hp = B_HEADS_PER_STEP
    ng = B_HEADS // hp
    steps = batch * ng * nq
    w_rows, w_cols = wg.shape[1], wg.shape[2]
    slab = w_rows // steps
    assert slab * steps == w_rows and slab % 16 == 0 and wu.shape == wg.shape
    est = (2 * B_TQ * hp * B_QK * 2 + 2 * seq * hp * (B_NOPE + B_V) * 2 + 2 * seq * 128 * 2
           + 2 * B_TQ * hp * B_V * 2 + hp * (B_V + 16) * B_TQ * 4 + 8 * B_TQ * B_TK * 4
           + 4 * slab * w_cols * 6 + (4 << 20))

    def slab_in(b, g, i):
        return (layer, (b * ng + g) * nq + i, 0)

    def slab_out(b, g, i):
        return (0, (b * ng + g) * nq + i, 0)

    return pl.pallas_call(
        _attn_b_kernel,
        out_shape=(jax.ShapeDtypeStruct((m, B_WIDTH), BF16),
                   jax.ShapeDtypeStruct((1, w_rows, w_cols), BF16),
                   jax.ShapeDtypeStruct((1, w_rows, w_cols), BF16)),
        grid=(batch, ng, nq),
        in_specs=[
            pl.BlockSpec((hp * B_QK, B_TQ), lambda b, g, i: (b * ng + g, i)),
            pl.BlockSpec((seq, hp * B_NOPE), lambda b, g, i: (b, g)),
            pl.BlockSpec((seq, V7X_LANES), lambda b, g, i: (b, 0)),
            pl.BlockSpec((hp * B_V, seq), lambda b, g, i: (b * ng + g, 0)),
            pl.BlockSpec((None, slab, w_cols), slab_in),
            pl.BlockSpec((None, slab, w_cols), slab_in),
        ],
        out_specs=(pl.BlockSpec((B_TQ, hp * B_V), lambda b, g, i: (b * nq + i, g)),
                   pl.BlockSpec((None, slab, w_cols), slab_out),
                   pl.BlockSpec((None, slab, w_cols), slab_out)),
        scratch_shapes=[
            pltpu.VMEM((hp, 1, B_TQ), F32),
            pltpu.VMEM((hp, 1, B_TQ), F32),
            pltpu.VMEM((hp, B_V, B_TQ), F32),
            pltpu.VMEM((B_LOOKAHEAD, B_TK, B_TQ), F32),
        ],
        compiler_params=_cparams(("arbitrary", "arbitrary", "arbitrary"), est),
        name=name,
    )(qt, kn, kpe, vt, wg, wu)


def _cross_kernel(x_ref, g_ref, wq_ref, k_ref, v_ref, wo_ref, o_ref):
    scale = X_HEAD_DIM ** -0.5
    rows = x_ref.shape[0] // X_ROW_PARTS
    for r in range(X_ROW_PARTS):
        rs = slice(r * rows, (r + 1) * rows)
        x = x_ref[rs, :]
        hn = _rms_rows(x, g_ref[...]).astype(BF16)
        q = (jnp.dot(hn, wq_ref[...], preferred_element_type=F32) * scale).astype(BF16)
        outs = []
        for h in range(X_HEADS):
            hs = slice(h * X_HEAD_DIM, (h + 1) * X_HEAD_DIM)
            s = lax.dot_general(q[:, hs], k_ref[:, hs], (((1,), (1,)), ((), ())),
                                preferred_element_type=F32)
            m = jnp.max(s, axis=1, keepdims=True)
            p = jnp.exp(s - m)
            l = jnp.sum(p, axis=1, keepdims=True)
            o = jnp.dot(p.astype(BF16), v_ref[:, hs], preferred_element_type=F32)
            outs.append((o / l).astype(BF16))
        o_all = jnp.concatenate(outs, axis=1)
        o_ref[rs, :] = x + jnp.dot(o_all, wo_ref[...], preferred_element_type=F32)


def _cross(x, g, wq, kvm, wo, layer, *, seq, tm, name):
    m, d = x.shape
    blocks_per_batch = seq // tm
    est = (4 * tm * d * 4 + tm * d * 2 + 2 * d * X_WIDTH * 2 + 2 * X_WIDTH * d * 2
           + 4 * N_MEM * X_WIDTH * 2 + 4 * tm * N_MEM * 4 + 2 * tm * d * 4 + (4 << 20))
    return pl.pallas_call(
        _cross_kernel,
        out_shape=jax.ShapeDtypeStruct((m, d), F32),
        grid=(m // tm,),
        in_specs=[
            pl.BlockSpec((tm, d), lambda i: (i, 0)),
            pl.BlockSpec((None, 1, d), lambda i: (layer, 0, 0)),
            pl.BlockSpec((None, d, X_WIDTH), lambda i: (layer, 0, 0)),
            pl.BlockSpec((N_MEM, X_WIDTH), lambda i: (i // blocks_per_batch, 2 * layer)),
            pl.BlockSpec((N_MEM, X_WIDTH), lambda i: (i // blocks_per_batch, 2 * layer + 1)),
            pl.BlockSpec((None, X_WIDTH, d), lambda i: (layer, 0, 0)),
        ],
        out_specs=pl.BlockSpec((tm, d), lambda i: (i, 0)),
        compiler_params=_cparams(("arbitrary",), est),
        name=name,
    )(x, g, wq, kvm, kvm, wo)


def _final_norm_kernel(x_ref, g_ref, o_ref):
    o_ref[...] = _rms_rows(x_ref[...], g_ref[...])


def _final_norm(x, g, *, tm):
    m, d = x.shape
    return pl.pallas_call(
        _final_norm_kernel,
        out_shape=jax.ShapeDtypeStruct((m, d), F32),
        grid=(m // tm,),
        in_specs=[pl.BlockSpec((tm, d), lambda i: (i, 0)), pl.BlockSpec((1, d), lambda i: (0, 0))],
        out_specs=pl.BlockSpec((tm, d), lambda i: (i, 0)),
        compiler_params=_cparams(("arbitrary",), 6 * tm * d * 4 + (4 << 20)),
        name="final_norm",
    )(x, g)


RELAYOUT_COLS = 256
RELAYOUT_ROPE_BLK = IN_ROPE_OFF // RELAYOUT_COLS


def _w_in_relayout_kernel(wt_ref, o_ref):
    j = pl.program_id(1)

    @pl.when(j != RELAYOUT_ROPE_BLK)
    def _():
        factor = jnp.where(j < A_WIDTH // RELAYOUT_COLS, A_SCORE_SCALE, 1.0)
        o_ref[...] = (wt_ref[...] * factor).T.astype(BF16)

    @pl.when(j == RELAYOUT_ROPE_BLK)
    def _():
        half = B_ROPE // 2
        kr = wt_ref[:B_ROPE, :]
        slot = jnp.concatenate([kr, kr[half:], kr[:half],
                                jnp.zeros((RELAYOUT_COLS - 2 * B_ROPE, kr.shape[1]), F32)], axis=0)
        o_ref[...] = slot.T.astype(BF16)


def _w_in_relayout(w_in_t):
    layers, n, k = w_in_t.shape
    lat_blk = IN_ROPE_OFF // RELAYOUT_COLS
    rope_src = (IN_ROPE_OFF + KV_LORA) // RELAYOUT_COLS
    assert IN_KVLAT_OFF == (RELAYOUT_ROPE_BLK + 1) * RELAYOUT_COLS

    def src_block(l, j):
        return (l, jnp.where(j < RELAYOUT_ROPE_BLK, j,
                             jnp.where(j == RELAYOUT_ROPE_BLK, rope_src, j - 1 + lat_blk - RELAYOUT_ROPE_BLK)), 0)

    return pl.pallas_call(
        _w_in_relayout_kernel,
        out_shape=jax.ShapeDtypeStruct((layers, k, IN_PAD_WIDTH), BF16),
        grid=(layers, IN_PAD_WIDTH // RELAYOUT_COLS),
        in_specs=[pl.BlockSpec((None, RELAYOUT_COLS, k), src_block)],
        out_specs=pl.BlockSpec((None, k, RELAYOUT_COLS), lambda l, j: (l, 0, j)),
        compiler_params=_cparams(("arbitrary", "arbitrary"), 32 << 20),
        name="w_in_relayout",
    )(w_in_t)


def _prep_weights(w_in, w_uq, w_ukv):
    swap = (jnp.arange(B_ROPE) + B_ROPE // 2) % B_ROPE
    w_in_p = _w_in_relayout(jnp.swapaxes(w_in, 1, 2))

    uq = w_uq.reshape(DEPTH, Q_LORA, B_HEADS, B_NOPE + B_ROPE)
    uq_pe = uq[..., B_NOPE:]
    wq_p = jnp.concatenate([uq[..., :B_NOPE], uq_pe, uq_pe[..., swap]], axis=-1)
    wq_p = wq_p.reshape(DEPTH, Q_LORA, B_HEADS * B_QK).astype(BF16)

    ukv = w_ukv.reshape(DEPTH, KV_LORA, B_HEADS, B_NOPE + B_V)
    wkv_p = jnp.concatenate([ukv[..., :B_NOPE].reshape(DEPTH, KV_LORA, B_HEADS * B_NOPE),
                             ukv[..., B_NOPE:].reshape(DEPTH, KV_LORA, B_WIDTH)],
                            axis=-1).astype(BF16)

    return w_in_p, wq_p, wkv_p


def _bias_table_kernel(w_ref, o_ref):
    v = pl.program_id(1)
    r = lax.broadcasted_iota(jnp.int32, (A_TQ, A_WIN), 0)
    t = lax.broadcasted_iota(jnp.int32, (A_TQ, A_WIN), 1)
    dchunk = t // CHUNK - r // CHUNK
    valid = (dchunk >= 0) & (dchunk <= LEFT_CHUNKS) & (v - (A_KBLKS - 1) + t // A_TQ >= 0)
    for hh in range(2):
        w = jnp.broadcast_to(w_ref[hh], (A_TQ, A_BIAS_VEC))
        tab = pltpu.roll(w, 0, axis=1, stride=1, stride_axis=0)[:, :A_WIN]
        o_ref[0, 0, :, hh * A_TQ:(hh + 1) * A_TQ] = jnp.where(valid, tab * LOG2E, -jnp.inf).T


def _relpos_bias_tables(rel_bias):
    u = jnp.arange(A_BIAS_VEC)
    delta = jnp.where(u <= A_WIN, u, u - A_BIAS_VEC)
    idx = jnp.clip(LEFT_CHUNKS * CHUNK - delta, -REL_CLIP, REL_CLIP) + REL_CLIP
    vec = rel_bias[:, :, idx].reshape(DEPTH * A_HEADS, 1, A_BIAS_VEC).astype(F32)
    return pl.pallas_call(
        _bias_table_kernel,
        out_shape=jax.ShapeDtypeStruct((DEPTH * A_HEADS // 2, A_KBLKS, A_WIN, 2 * A_TQ), F32),
        grid=(DEPTH * A_HEADS // 2, A_KBLKS),
        in_specs=[pl.BlockSpec((2, 1, A_BIAS_VEC), lambda i, v: (i, 0, 0))],
        out_specs=pl.BlockSpec((1, 1, A_WIN, 2 * A_TQ), lambda i, v: (i, v, 0, 0)),
        compiler_params=_cparams(("arbitrary", "arbitrary"), 32 << 20),
        name="bias_table",
    )(vec)


def kernel(x, mem, positions, norm_mix, w_in, rel_bias, q_norm, kv_norm, w_uq, w_ukv, w_out,
           norm_mem, mem_norm, w_xq, w_xkv, w_xo, norm_ffn, w_gate, w_up, w_down, norm_final):
    b, s, d = x.shape
    m = b * s
    assert (b, s, d) == (4, 4096, D_MODEL) and mem.shape == (b, N_MEM, d)

    w_in_p, wq_p, wkv_p = _prep_weights(w_in, w_uq, w_ukv)
    w_xq_b = w_xq.astype(BF16)
    w_xo_b = w_xo.astype(BF16)
    bias_tab = _relpos_bias_tables(rel_bias)

    half = B_ROPE // 2
    inv = ROPE_THETA ** (-jnp.arange(half, dtype=F32) / half)
    inv_row = jnp.tile(inv, 4)[None, :]
    sign_row = jnp.concatenate([jnp.ones((B_ROPE,), F32), -jnp.ones((half,), F32),
                                jnp.ones((half,), F32)])[None, :]
    cs = _rope_table(positions.reshape(m, 1), inv_row, sign_row, tm=2048)

    kvm = _mem_kv(mem.reshape(b * N_MEM, d), mem_norm[None, :], w_xkv)

    norm_mix, norm_mem, norm_ffn, q_norm, kv_norm = (
        g[:, None, :] for g in (norm_mix, norm_mem, norm_ffn, q_norm, kv_norm))
    xf = x.reshape(m, d)
    for l in range(DEPTH):
        proj = _norm_matmul(xf, norm_mix, w_in_p, l, tm=1024, tn=1536, name=f"in_proj_{l}")
        oa = _attn_a(proj, bias_tab, l, batch=b, seq=s, name=f"attn_a_{l}")
        qt, kn, vt, kpe, w_out_l = _mla_prep(proj, cs, q_norm, kv_norm, wq_p, wkv_p, w_out, l,
                                             seq=s, tm=512, name=f"mla_prep_{l}")
        ob, w_gate_l, w_up_l = _attn_b(qt, kn, kpe, vt, w_gate, w_up, l, batch=b, seq=s,
                                       name=f"attn_b_{l}")
        xf = _mm_res([oa, ob], w_out_l, xf, 0, tm=1024, tn=1024, name=f"out_proj_{l}")
        xf = _cross(xf, norm_mem, w_xq_b, kvm, w_xo_b, l, seq=s, tm=512, name=f"cross_{l}")
        act, w_down_l = _swiglu_gu(xf, norm_ffn, w_gate_l, w_up_l, w_down, l, 0, tm=1024, tn=512,
                                   name=f"swiglu_gu_{l}")
        xf = _mm_res([act], w_down_l, xf, 0, tm=1024, tn=512, name=f"swiglu_down_{l}")
    out = _final_norm(xf, norm_final[None, :], tm=512)
    return out.reshape(b, s, d)
```

```python
import functools
import math

import jax
import jax.numpy as jnp
from jax import lax
from jax.experimental import pallas as pl
from jax.experimental.pallas import tpu as pltpu

D_MODEL = 2048
DEPTH = 4
CHUNK = 64
LEFT_CHUNKS = 8
A_HEAD_DIM = 128
A_HEADS = 8
A_WIDTH = A_HEADS * A_HEAD_DIM
REL_CLIP = 128
B_NOPE = 128
B_ROPE = 64
B_V = 128
B_HEADS = 8
B_WIDTH = B_HEADS * B_V
Q_LORA = 768
KV_LORA = 512
ROPE_THETA = 10000.0
N_MEM = 256
X_HEADS = 4
X_HEAD_DIM = 128
X_WIDTH = X_HEADS * X_HEAD_DIM
X_ROW_PARTS = 2
D_FF = 5632
EPS = 1e-6

V7X_LANES = 128
V7X_VMEM_BYTES = 64 * 1024 * 1024
V7X_VMEM_REQUEST_CAP = 56 * 1024 * 1024

BF16 = jnp.bfloat16
F32 = jnp.float32

IN_QLAT_OFF = 3 * A_WIDTH
IN_ROPE_OFF = IN_QLAT_OFF + Q_LORA
IN_KVLAT_OFF = IN_ROPE_OFF + 2 * V7X_LANES
IN_PAD_WIDTH = IN_KVLAT_OFF + KV_LORA

A_TQ = 4 * CHUNK
A_WIN = A_TQ + LEFT_CHUNKS * CHUNK
A_KBLKS = A_WIN // A_TQ
A_BIAS_VEC = 1024

B_TQ = 512
B_TK = 512
B_HEADS_PER_STEP = 4
B_LOOKAHEAD = 2
B_RED_SLABS = 8
B_QK = B_NOPE + 2 * B_ROPE

LOG2E = math.log2(math.e)
A_SCORE_SCALE = A_HEAD_DIM ** -0.5 * LOG2E


def _cparams(sem, vmem_bytes):
    limit = int(min(max(vmem_bytes, 32 * 1024 * 1024), V7X_VMEM_REQUEST_CAP))
    return pltpu.CompilerParams(dimension_semantics=sem, vmem_limit_bytes=limit)


def _rms_rows(x, g):
    ms = jnp.mean(x * x, axis=-1, keepdims=True)
    return (x * lax.rsqrt(ms + EPS)) * g


def _norm_into(h_ref, x_ref, g_ref, rows=128):
    n = x_ref.shape[0] // rows

    def body(c, carry):
        r0 = pl.multiple_of(c * rows, rows)
        x = x_ref[pl.ds(r0, rows), :].astype(F32)
        h_ref[pl.ds(r0, rows), :] = _rms_rows(x, g_ref[...]).astype(h_ref.dtype)
        return carry

    lax.fori_loop(0, n, body, 0)


def _norm_matmul_kernel(x_ref, g_ref, w_ref, o_ref, h_ref):
    @pl.when(pl.program_id(1) == 0)
    def _():
        _norm_into(h_ref, x_ref, g_ref)

    o_ref[...] = jnp.dot(h_ref[...], w_ref[...], preferred_element_type=F32).astype(o_ref.dtype)


def _norm_matmul(x, g, w, layer, *, tm, tn, name):
    m, k = x.shape
    n = w.shape[2]
    xb = x.dtype.itemsize
    est = 2 * tm * k * xb + tm * k * 2 + 2 * k * tn * 2 + 2 * tm * tn * 2 + tm * tn * 4 + (4 << 20)
    return pl.pallas_call(
        _norm_matmul_kernel,
        out_shape=jax.ShapeDtypeStruct((m, n), BF16),
        grid=(m // tm, n // tn),
        in_specs=[
            pl.BlockSpec((tm, k), lambda i, j: (i, 0)),
            pl.BlockSpec((None, 1, k), lambda i, j: (layer, 0, 0)),
            pl.BlockSpec((None, k, tn), lambda i, j: (layer, 0, j)),
        ],
        out_specs=pl.BlockSpec((tm, tn), lambda i, j: (i, j)),
        scratch_shapes=[pltpu.VMEM((tm, k), BF16)],
        compiler_params=_cparams(("arbitrary", "arbitrary"), est),
        name=name,
    )(x, g, w)


def _mem_kv_kernel(x_ref, g_ref, w_ref, o_ref, h_ref):
    @pl.when(pl.program_id(0) == 0)
    def _():
        _norm_into(h_ref, x_ref, g_ref)

    o_ref[...] = jnp.dot(h_ref[...], w_ref[...].astype(BF16),
                         preferred_element_type=F32).astype(o_ref.dtype)


def _mem_kv(mem2d, g, w_xkv):
    m, k = mem2d.shape
    layers, _, n = w_xkv.shape
    est = 2 * m * k * 4 + m * k * 2 + 2 * k * n * 4 + k * n * 2 + 2 * m * n * 2 + m * n * 4 + (4 << 20)
    return pl.pallas_call(
        _mem_kv_kernel,
        out_shape=jax.ShapeDtypeStruct((m, layers * n), BF16),
        grid=(layers,),
        in_specs=[
            pl.BlockSpec((m, k), lambda l: (0, 0)),
            pl.BlockSpec((1, k), lambda l: (0, 0)),
            pl.BlockSpec((None, k, n), lambda l: (l, 0, 0)),
        ],
        out_specs=pl.BlockSpec((m, n), lambda l: (0, l)),
        scratch_shapes=[pltpu.VMEM((m, k), BF16)],
        compiler_params=_cparams(("arbitrary",), est),
        name="mem_kv",
    )(mem2d, g, w_xkv)


def _mm_res_kernel(*refs, n_in):
    a_refs = refs[:n_in]
    w_refs = refs[n_in:2 * n_in]
    r_ref = refs[2 * n_in]
    o_ref = refs[2 * n_in + 1]
    acc = r_ref[...]
    for a_ref, w_ref in zip(a_refs, w_refs):
        acc = acc + jnp.dot(a_ref[...], w_ref[...], preferred_element_type=F32)
    o_ref[...] = acc


def _mm_res(a_list, w, res, layer, *, tm, tn, name):
    m, n = res.shape
    n_in = len(a_list)
    in_specs = []
    w_specs = []
    est = (4 + 2 * n_in) * tm * tn * 4 + (4 << 20)
    for idx, a in enumerate(a_list):
        k = a.shape[1]
        assert w.shape[1] == n_in * k
        in_specs.append(pl.BlockSpec((tm, k), lambda i, j: (i, 0)))
        w_specs.append(pl.BlockSpec((None, k, tn), lambda i, j, idx=idx: (layer, idx, j)))
        est += 2 * tm * k * 2 + 2 * k * tn * 2
    in_specs += w_specs
    in_specs.append(pl.BlockSpec((tm, tn), lambda i, j: (i, j)))
    return pl.pallas_call(
        functools.partial(_mm_res_kernel, n_in=n_in),
        out_shape=jax.ShapeDtypeStruct((m, n), F32),
        grid=(m // tm, n // tn),
        in_specs=in_specs,
        out_specs=pl.BlockSpec((tm, tn), lambda i, j: (i, j)),
        compiler_params=_cparams(("arbitrary", "arbitrary"), est),
        name=name,
    )(*a_list, *([w] * n_in), res)


def _swiglu_gu_kernel(x_ref, g_ref, wg_ref, wu_ref, wd_ref, o_ref, wdb_ref, h_ref):
    @pl.when(pl.program_id(1) == 0)
    def _():
        _norm_into(h_ref, x_ref, g_ref)

    h = h_ref[...]
    gate = jnp.dot(h, wg_ref[...], preferred_element_type=F32)
    up = jnp.dot(h, wu_ref[...], preferred_element_type=F32)
    o_ref[...] = (gate * (1.0 / (1.0 + jnp.exp(-gate))) * up).astype(o_ref.dtype)
    wdb_ref[...] = wd_ref[...].astype(BF16)


def _swiglu_gu(x, g, wg, wu, wd, layer, gu_layer, *, tm, tn, name):
    m, k = x.shape
    n = wg.shape[2]
    steps = (m // tm) * (n // tn)
    nj = n // tn
    wd_rows, wd_cols = wd.shape[1], wd.shape[2]
    slab = wd_rows // steps
    assert slab * steps == wd_rows and slab % 16 == 0
    wbytes = wg.dtype.itemsize
    est = (2 * tm * k * 4 + tm * k * 2 + 4 * k * tn * wbytes + 2 * k * tn * 2 + 2 * tm * tn * 2
           + 3 * tm * tn * 4 + 2 * slab * wd_cols * 6 + (4 << 20))
    return pl.pallas_call(
        _swiglu_gu_kernel,
        out_shape=(jax.ShapeDtypeStruct((m, n), BF16),
                   jax.ShapeDtypeStruct((1, wd_rows, wd_cols), BF16)),
        grid=(m // tm, nj),
        in_specs=[
            pl.BlockSpec((tm, k), lambda i, j: (i, 0)),
            pl.BlockSpec((None, 1, k), lambda i, j: (layer, 0, 0)),
            pl.BlockSpec((None, k, tn), lambda i, j: (gu_layer, 0, j)),
            pl.BlockSpec((None, k, tn), lambda i, j: (gu_layer, 0, j)),
            pl.BlockSpec((None, slab, wd_cols), lambda i, j: (layer, i * nj + j, 0)),
        ],
        out_specs=(pl.BlockSpec((tm, tn), lambda i, j: (i, j)),
                   pl.BlockSpec((None, slab, wd_cols), lambda i, j: (0, i * nj + j, 0))),
        scratch_shapes=[pltpu.VMEM((tm, k), BF16)],
        compiler_params=_cparams(("arbitrary", "arbitrary"), est),
        name=name,
    )(x, g, wg, wu, wd)


def _rope_table_kernel(pos_ref, inv_ref, sign_ref, cs_ref):
    ang = pos_ref[...].astype(F32) * inv_ref[...]
    lane = lax.broadcasted_iota(jnp.int32, ang.shape, 1)
    cs_ref[...] = jnp.where(lane < B_ROPE, jnp.cos(ang), jnp.sin(ang) * sign_ref[...])


def _rope_table(pos_col, inv_row, sign_row, *, tm):
    m = pos_col.shape[0]
    return pl.pallas_call(
        _rope_table_kernel,
        out_shape=jax.ShapeDtypeStruct((m, V7X_LANES), F32),
        grid=(m // tm,),
        in_specs=[
            pl.BlockSpec((tm, 1), lambda i: (i, 0)),
            pl.BlockSpec((1, V7X_LANES), lambda i: (0, 0)),
            pl.BlockSpec((1, V7X_LANES), lambda i: (0, 0)),
        ],
        out_specs=pl.BlockSpec((tm, V7X_LANES), lambda i: (i, 0)),
        compiler_params=_cparams(("arbitrary",), 32 << 20),
        name="rope_table",
    )(pos_col, inv_row, sign_row)


def _rope_pair_sum(slot, cs):
    t = slot * cs
    return t + pltpu.roll(t, B_ROPE, axis=1)


def _mla_prep_kernel(ql_ref, rope_ref, kvl_ref, cs_ref, gq_ref, gkv_ref, wq_ref, wkv_ref, wo_ref,
                     qt_ref, kn_ref, vt_ref, kpe_ref, wob_ref, *, q_scale):
    cs = cs_ref[...]
    hq = _rms_rows(ql_ref[...].astype(F32), gq_ref[...]).astype(BF16)
    qf = jnp.dot(hq, wq_ref[...], preferred_element_type=F32)
    for h in range(B_HEADS):
        base = h * B_QK
        nope = qf[:, base:base + B_NOPE]
        pe = _rope_pair_sum(qf[:, base + B_NOPE:base + B_QK], cs)
        qt_ref[base:base + B_NOPE, :] = (nope * q_scale).T.astype(BF16)
        qt_ref[base + B_NOPE:base + B_QK, :] = (pe * q_scale).T.astype(BF16)

    hkv = _rms_rows(kvl_ref[...].astype(F32), gkv_ref[...]).astype(BF16)
    kvf = jnp.dot(hkv, wkv_ref[...], preferred_element_type=F32)
    kn_ref[...] = kvf[:, :B_HEADS * B_NOPE].astype(BF16)
    vt_ref[...] = kvf[:, B_HEADS * B_NOPE:].T.astype(BF16)

    kpe = _rope_pair_sum(rope_ref[...].astype(F32), cs)
    lane = lax.broadcasted_iota(jnp.int32, kpe.shape, 1)
    kpe_ref[...] = jnp.where(lane < B_ROPE, kpe, 0.0).astype(BF16)
    wob_ref[...] = wo_ref[...].astype(BF16)


def _mla_prep(proj, cs, gq, gkv, wq, wkv, wo, layer, *, seq, tm, name):
    m = proj.shape[0]
    tiles_per_seq = seq // tm
    q_scale = (B_NOPE + B_ROPE) ** -0.5 * LOG2E
    wo_rows, wo_cols = wo.shape[1], wo.shape[2]
    slab = wo_rows // (m // tm)
    assert slab * (m // tm) == wo_rows and slab % 16 == 0
    est = (2 * tm * (Q_LORA + KV_LORA + 128) * 2 + 2 * tm * 128 * 4
           + 2 * (Q_LORA + KV_LORA) * 2048 * 2 + 2 * slab * wo_cols * 6
           + 2 * tm * (2048 + 1024 + 1024 + 128) * 2 + 3 * tm * 2048 * 4 + (4 << 20))
    return pl.pallas_call(
        functools.partial(_mla_prep_kernel, q_scale=q_scale),
        out_shape=(
            jax.ShapeDtypeStruct((m // seq * B_HEADS * B_QK, seq), BF16),
            jax.ShapeDtypeStruct((m, B_HEADS * B_NOPE), BF16),
            jax.ShapeDtypeStruct((m // seq * B_WIDTH, seq), BF16),
            jax.ShapeDtypeStruct((m, V7X_LANES), BF16),
            jax.ShapeDtypeStruct((1, wo_rows, wo_cols), BF16),
        ),
        grid=(m // tm,),
        in_specs=[
            pl.BlockSpec((tm, Q_LORA), lambda i: (i, IN_QLAT_OFF // Q_LORA)),
            pl.BlockSpec((tm, V7X_LANES), lambda i: (i, IN_ROPE_OFF // V7X_LANES)),
            pl.BlockSpec((tm, KV_LORA), lambda i: (i, IN_KVLAT_OFF // KV_LORA)),
            pl.BlockSpec((tm, V7X_LANES), lambda i: (i, 0)),
            pl.BlockSpec((None, 1, Q_LORA), lambda i: (layer, 0, 0)),
            pl.BlockSpec((None, 1, KV_LORA), lambda i: (layer, 0, 0)),
            pl.BlockSpec((None, Q_LORA, B_HEADS * B_QK), lambda i: (layer, 0, 0)),
            pl.BlockSpec((None, KV_LORA, 2 * B_WIDTH), lambda i: (layer, 0, 0)),
            pl.BlockSpec((None, slab, wo_cols), lambda i: (layer, i, 0)),
        ],
        out_specs=(
            pl.BlockSpec((B_HEADS * B_QK, tm), lambda i: (i // tiles_per_seq, i % tiles_per_seq)),
            pl.BlockSpec((tm, B_HEADS * B_NOPE), lambda i: (i, 0)),
            pl.BlockSpec((B_WIDTH, tm), lambda i: (i // tiles_per_seq, i % tiles_per_seq)),
            pl.BlockSpec((tm, V7X_LANES), lambda i: (i, 0)),
            pl.BlockSpec((None, slab, wo_cols), lambda i: (0, i, 0)),
        ),
        compiler_params=_cparams(("arbitrary",), est),
        name=name,
    )(proj, proj, proj, cs, gq, gkv, wq, wkv, wo)


def _attn_a_kernel(q_ref, k0_ref, k1_ref, k2_ref, v0_ref, v1_ref, v2_ref, bias_ref, o_ref):
    d, tq = A_HEAD_DIM, A_TQ
    look = 2

    def scores(h):
        hs = slice(h * d, (h + 1) * d)
        k = jnp.concatenate([k0_ref[:, hs], k1_ref[:, hs], k2_ref[:, hs]], axis=0)
        return jnp.dot(k, q_ref[:, hs].T, preferred_element_type=F32)

    tiles = [scores(h) for h in range(look)] + [None] * (A_HEADS - look)
    for h in range(A_HEADS):
        hs = slice(h * d, (h + 1) * d)
        if h + look < A_HEADS:
            tiles[h + look] = scores(h + look)
        v = jnp.concatenate([v0_ref[:, hs], v1_ref[:, hs], v2_ref[:, hs]], axis=0)
        st = bias_ref[h // 2][:, (h % 2) * tq:(h % 2 + 1) * tq] + tiles[h]
        tiles[h] = None
        m = _col_reduce(st, jnp.max)
        p = jnp.exp2(st - m)
        l = _col_reduce(p, jnp.sum)
        ot = lax.dot_general(v, p.astype(BF16), (((0,), (0,)), ((), ())),
                             preferred_element_type=F32)
        o_ref[:, hs] = (ot / l).T.astype(o_ref.dtype)


def _attn_a(proj, bias, layer, *, batch, seq, name):
    m = proj.shape[0]
    nq = seq // A_TQ
    qspec = pl.BlockSpec((A_TQ, A_WIDTH), lambda b, i: (b * nq + i, 0))

    def kv_spec(col_blk, c):
        back = A_KBLKS - 1 - c
        return pl.BlockSpec((A_TQ, A_WIDTH),
                            lambda b, i: (b * nq + jnp.maximum(i - back, 0), col_blk))

    est = 2 * 7 * A_TQ * A_WIDTH * 2 + 2 * A_HEADS * A_TQ * A_WIN * 4 + 2 * A_TQ * A_WIDTH * 2 + (8 << 20)
    return pl.pallas_call(
        _attn_a_kernel,
        out_shape=jax.ShapeDtypeStruct((m, A_WIDTH), BF16),
        grid=(batch, nq),
        in_specs=[qspec,
                  kv_spec(1, 0), kv_spec(1, 1), kv_spec(1, 2),
                  kv_spec(2, 0), kv_spec(2, 1), kv_spec(2, 2),
                  pl.BlockSpec((A_HEADS // 2, None, A_WIN, 2 * A_TQ),
                               lambda b, i: (layer, jnp.minimum(i, A_KBLKS - 1), 0, 0))],
        out_specs=pl.BlockSpec((A_TQ, A_WIDTH), lambda b, i: (b * nq + i, 0)),
        compiler_params=_cparams(("arbitrary", "arbitrary"), est),
        name=name,
    )(proj, proj, proj, proj, proj, proj, proj, bias)


def _col_reduce(x, op):
    rows, cols = x.shape
    slabs = op(x.reshape(B_RED_SLABS, rows // B_RED_SLABS, cols), axis=0)
    return op(slabs, axis=0, keepdims=True)


def _attn_b_kernel(q_ref, kn_ref, kpe_ref, vt_ref, wg_ref, wu_ref, o_ref, wgb_ref, wub_ref,
                   m_ref, l_ref, acc_ref, st_ref):
    i = pl.program_id(2)
    hp = B_HEADS_PER_STEP
    m_ref[...] = jnp.full(m_ref.shape, -jnp.inf, F32)
    l_ref[...] = jnp.zeros(l_ref.shape, F32)
    acc_ref[...] = jnp.zeros(acc_ref.shape, F32)

    def scores(j, h):
        r0 = pl.multiple_of(j * B_TK, B_TK)
        k = jnp.concatenate([kn_ref[pl.ds(r0, B_TK), h * B_NOPE:(h + 1) * B_NOPE],
                             kpe_ref[pl.ds(r0, B_TK), :]], axis=1)
        return jnp.dot(k, q_ref[h * B_QK:(h + 1) * B_QK, :], preferred_element_type=F32)

    def softmax_pv(j, h, st, allowed):
        r0 = pl.multiple_of(j * B_TK, B_TK)
        if allowed is not None:
            st = jnp.where(allowed, st, -jnp.inf)
        m_prev = m_ref[h]
        m_new = jnp.maximum(m_prev, _col_reduce(st, jnp.max))
        alpha = jnp.exp2(m_prev - m_new)
        p = jnp.exp2(st - m_new)
        l_ref[h] = alpha * l_ref[h] + _col_reduce(p, jnp.sum)
        pv = jnp.dot(vt_ref[h * B_V:(h + 1) * B_V, pl.ds(r0, B_TK)], p.astype(BF16),
                     preferred_element_type=F32)
        acc_ref[h] = alpha * acc_ref[h] + pv
        m_ref[h] = m_new

    def key_block(j, masked):
        allowed = None
        if masked:
            krow = lax.broadcasted_iota(jnp.int32, (B_TK, B_TQ), 0)
            qcol = lax.broadcasted_iota(jnp.int32, (B_TK, B_TQ), 1)
            allowed = (krow // CHUNK) <= (qcol // CHUNK)
        look = B_LOOKAHEAD
        tiles = [st_ref[h] for h in range(look)] + [None] * (hp - look)
        for h in range(hp):
            ahead = h + look
            if ahead < hp:
                tiles[ahead] = scores(j, ahead)
            elif not masked:
                st_ref[ahead - hp] = scores(j + 1, ahead - hp)
            softmax_pv(j, h, tiles[h], allowed)
            tiles[h] = None

    def body(j, carry):
        key_block(j, False)
        return carry

    for h in range(B_LOOKAHEAD):
        st_ref[h] = scores(0, h)
    lax.fori_loop(0, i, body, 0)
    key_block(i, True)

    for h in range(hp):
        o_ref[:, h * B_V:(h + 1) * B_V] = (acc_ref[h] / l_ref[h]).T.astype(o_ref.dtype)

    wgb_ref[...] = wg_ref[...].astype(BF16)
    wub_ref[...] = wu_ref[...].astype(BF16)


def _attn_b(qt, kn, kpe, vt, wg, wu, layer, *, batch, seq, name):
    m = kn.shape[0]
    nq = seq // B_TQ
    hp = B_HEADS_PER_STEP
    ng = B_HEADS // hp
    steps = batch * ng * nq
    w_rows, w_cols = wg.shape[1], wg.shape[2]
    slab = w_rows // steps
    assert slab * steps == w_rows and slab % 16 == 0 and wu.shape == wg.shape
    est = (2 * B_TQ * hp * B_QK * 2 + 2 * seq * hp * (B_NOPE + B_V) * 2 + 2 * seq * 128 * 2
           + 2 * B_TQ * hp * B_V * 2 + hp * (B_V + 16) * B_TQ * 4 + 8 * B_TQ * B_TK * 4
           + 4 * slab * w_cols * 6 + (4 << 20))

    def slab_in(b, g, i):
        return (layer, (b * ng + g) * nq + i, 0)

    def slab_out(b, g, i):
        return (0, (b * ng + g) * nq + i, 0)

    return pl.pallas_call(
        _attn_b_kernel,
        out_shape=(jax.ShapeDtypeStruct((m, B_WIDTH), BF16),
                   jax.ShapeDtypeStruct((1, w_rows, w_cols), BF16),
                   jax.ShapeDtypeStruct((1, w_rows, w_cols), BF16)),
        grid=(batch, ng, nq),
        in_specs=[
            pl.BlockSpec((hp * B_QK, B_TQ), lambda b, g, i: (b * ng + g, i)),
            pl.BlockSpec((seq, hp * B_NOPE), lambda b, g, i: (b, g)),
            pl.BlockSpec((seq, V7X_LANES), lambda b, g, i: (b, 0)),
            pl.BlockSpec((hp * B_V, seq), lambda b, g, i: (b * ng + g, 0)),
            pl.BlockSpec((None, slab, w_cols), slab_in),
            pl.BlockSpec((None, slab, w_cols), slab_in),
        ],
        out_specs=(pl.BlockSpec((B_TQ, hp * B_V), lambda b, g, i: (b * nq + i, g)),
                   pl.BlockSpec((None, slab, w_cols), slab_out),
                   pl.BlockSpec((None, slab, w_cols), slab_out)),
        scratch_shapes=[
            pltpu.VMEM((hp, 1, B_TQ), F32),
            pltpu.VMEM((hp, 1, B_TQ), F32),
            pltpu.VMEM((hp, B_V, B_TQ), F32),
            pltpu.VMEM((B_LOOKAHEAD, B_TK, B_TQ), F32),
        ],
        compiler_params=_cparams(("arbitrary", "arbitrary", "arbitrary"), est),
        name=name,
    )(qt, kn, kpe, vt, wg, wu)


def _cross_kernel(x_ref, oa_ref, ob_ref, woa_ref, wob_ref, g_ref, wq_ref, k_ref, v_ref, wo_ref,
                  o_ref):
    scale = X_HEAD_DIM ** -0.5
    rows = x_ref.shape[0] // X_ROW_PARTS
    for r in range(X_ROW_PARTS):
        rs = slice(r * rows, (r + 1) * rows)
        x = (x_ref[rs, :] + jnp.dot(oa_ref[rs, :], woa_ref[...], preferred_element_type=F32)
             + jnp.dot(ob_ref[rs, :], wob_ref[...], preferred_element_type=F32))
        hn = _rms_rows(x, g_ref[...]).astype(BF16)
        q = (jnp.dot(hn, wq_ref[...], preferred_element_type=F32) * scale).astype(BF16)
        outs = []
        for h in range(X_HEADS):
            hs = slice(h * X_HEAD_DIM, (h + 1) * X_HEAD_DIM)
            s = lax.dot_general(q[:, hs], k_ref[:, hs], (((1,), (1,)), ((), ())),
                                preferred_element_type=F32)
            m = jnp.max(s, axis=1, keepdims=True)
            p = jnp.exp(s - m)
            l = jnp.sum(p, axis=1, keepdims=True)
            o = jnp.dot(p.astype(BF16), v_ref[:, hs], preferred_element_type=F32)
            outs.append((o / l).astype(BF16))
        o_all = jnp.concatenate(outs, axis=1)
        o_ref[rs, :] = x + jnp.dot(o_all, wo_ref[...], preferred_element_type=F32)


def _cross(x, oa, ob, w_out_l, g, wq, kvm, wo, layer, *, seq, tm, name):
    m, d = x.shape
    blocks_per_batch = seq // tm
    est = (4 * tm * d * 4 + tm * d * 2 + 2 * d * X_WIDTH * 2 + 2 * X_WIDTH * d * 2
           + 4 * N_MEM * X_WIDTH * 2 + 4 * tm * N_MEM * 4 + 3 * tm * d * 4
           + 2 * tm * (A_WIDTH + B_WIDTH) * 2 + 2 * (A_WIDTH + B_WIDTH) * d * 2 + (4 << 20))
    return pl.pallas_call(
        _cross_kernel,
        out_shape=jax.ShapeDtypeStruct((m, d), F32),
        grid=(m // tm,),
        in_specs=[
            pl.BlockSpec((tm, d), lambda i: (i, 0)),
            pl.BlockSpec((tm, A_WIDTH), lambda i: (i, 0)),
            pl.BlockSpec((tm, B_WIDTH), lambda i: (i, 0)),
            pl.BlockSpec((None, A_WIDTH, d), lambda i: (0, 0, 0)),
            pl.BlockSpec((None, B_WIDTH, d), lambda i: (0, A_WIDTH // B_WIDTH, 0)),
            pl.BlockSpec((None, 1, d), lambda i: (layer, 0, 0)),
            pl.BlockSpec((None, d, X_WIDTH), lambda i: (layer, 0, 0)),
            pl.BlockSpec((N_MEM, X_WIDTH), lambda i: (i // blocks_per_batch, 2 * layer)),
            pl.BlockSpec((N_MEM, X_WIDTH), lambda i: (i // blocks_per_batch, 2 * layer + 1)),
            pl.BlockSpec((None, X_WIDTH, d), lambda i: (layer, 0, 0)),
        ],
        out_specs=pl.BlockSpec((tm, d), lambda i: (i, 0)),
        compiler_params=_cparams(("arbitrary",), est),
        name=name,
    )(x, oa, ob, w_out_l, w_out_l, g, wq, kvm, kvm, wo)


def _final_norm_kernel(x_ref, g_ref, o_ref):
    o_ref[...] = _rms_rows(x_ref[...], g_ref[...])


def _final_norm(x, g, *, tm):
    m, d = x.shape
    return pl.pallas_call(
        _final_norm_kernel,
        out_shape=jax.ShapeDtypeStruct((m, d), F32),
        grid=(m // tm,),
        in_specs=[pl.BlockSpec((tm, d), lambda i: (i, 0)), pl.BlockSpec((1, d), lambda i: (0, 0))],
        out_specs=pl.BlockSpec((tm, d), lambda i: (i, 0)),
        compiler_params=_cparams(("arbitrary",), 6 * tm * d * 4 + (4 << 20)),
        name="final_norm",
    )(x, g)


RELAYOUT_COLS = 256
RELAYOUT_ROPE_BLK = IN_ROPE_OFF // RELAYOUT_COLS


def _w_in_relayout_kernel(wt_ref, o_ref):
    j = pl.program_id(1)

    @pl.when(j != RELAYOUT_ROPE_BLK)
    def _():
        factor = jnp.where(j < A_WIDTH // RELAYOUT_COLS, A_SCORE_SCALE, 1.0)
        o_ref[...] = (wt_ref[...] * factor).T.astype(BF16)

    @pl.when(j == RELAYOUT_ROPE_BLK)
    def _():
        half = B_ROPE // 2
        kr = wt_ref[:B_ROPE, :]
        slot = jnp.concatenate([kr, kr[half:], kr[:half],
                                jnp.zeros((RELAYOUT_COLS - 2 * B_ROPE, kr.shape[1]), F32)], axis=0)
        o_ref[...] = slot.T.astype(BF16)


def _w_in_relayout(w_in_t):
    layers, n, k = w_in_t.shape
    lat_blk = IN_ROPE_OFF // RELAYOUT_COLS
    rope_src = (IN_ROPE_OFF + KV_LORA) // RELAYOUT_COLS
    assert IN_KVLAT_OFF == (RELAYOUT_ROPE_BLK + 1) * RELAYOUT_COLS

    def src_block(l, j):
        return (l, jnp.where(j < RELAYOUT_ROPE_BLK, j,
                             jnp.where(j == RELAYOUT_ROPE_BLK, rope_src, j - 1 + lat_blk - RELAYOUT_ROPE_BLK)), 0)

    return pl.pallas_call(
        _w_in_relayout_kernel,
        out_shape=jax.ShapeDtypeStruct((layers, k, IN_PAD_WIDTH), BF16),
        grid=(layers, IN_PAD_WIDTH // RELAYOUT_COLS),
        in_specs=[pl.BlockSpec((None, RELAYOUT_COLS, k), src_block)],
        out_specs=pl.BlockSpec((None, k, RELAYOUT_COLS), lambda l, j: (l, 0, j)),
        compiler_params=_cparams(("arbitrary", "arbitrary"), 32 << 20),
        name="w_in_relayout",
    )(w_in_t)


def _prep_weights(w_in, w_uq, w_ukv):
    swap = (jnp.arange(B_ROPE) + B_ROPE // 2) % B_ROPE
    w_in_p = _w_in_relayout(jnp.swapaxes(w_in, 1, 2))

    uq = w_uq.reshape(DEPTH, Q_LORA, B_HEADS, B_NOPE + B_ROPE)
    uq_pe = uq[..., B_NOPE:]
    wq_p = jnp.concatenate([uq[..., :B_NOPE], uq_pe, uq_pe[..., swap]], axis=-1)
    wq_p = wq_p.reshape(DEPTH, Q_LORA, B_HEADS * B_QK).astype(BF16)

    ukv = w_ukv.reshape(DEPTH, KV_LORA, B_HEADS, B_NOPE + B_V)
    wkv_p = jnp.concatenate([ukv[..., :B_NOPE].reshape(DEPTH, KV_LORA, B_HEADS * B_NOPE),
                             ukv[..., B_NOPE:].reshape(DEPTH, KV_LORA, B_WIDTH)],
                            axis=-1).astype(BF16)

    return w_in_p, wq_p, wkv_p


def _bias_table_kernel(w_ref, o_ref):
    v = pl.program_id(1)
    r = lax.broadcasted_iota(jnp.int32, (A_TQ, A_WIN), 0)
    t = lax.broadcasted_iota(jnp.int32, (A_TQ, A_WIN), 1)
    dchunk = t // CHUNK - r // CHUNK
    valid = (dchunk >= 0) & (dchunk <= LEFT_CHUNKS) & (v - (A_KBLKS - 1) + t // A_TQ >= 0)
    for hh in range(2):
        w = jnp.broadcast_to(w_ref[hh], (A_TQ, A_BIAS_VEC))
        tab = pltpu.roll(w, 0, axis=1, stride=1, stride_axis=0)[:, :A_WIN]
        o_ref[0, 0, :, hh * A_TQ:(hh + 1) * A_TQ] = jnp.where(valid, tab * LOG2E, -jnp.inf).T


def _relpos_bias_tables(rel_bias):
    u = jnp.arange(A_BIAS_VEC)
    delta = jnp.where(u <= A_WIN, u, u - A_BIAS_VEC)
    idx = jnp.clip(LEFT_CHUNKS * CHUNK - delta, -REL_CLIP, REL_CLIP) + REL_CLIP
    vec = rel_bias[:, :, idx].reshape(DEPTH * A_HEADS, 1, A_BIAS_VEC).astype(F32)
    return pl.pallas_call(
        _bias_table_kernel,
        out_shape=jax.ShapeDtypeStruct((DEPTH * A_HEADS // 2, A_KBLKS, A_WIN, 2 * A_TQ), F32),
        grid=(DEPTH * A_HEADS // 2, A_KBLKS),
        in_specs=[pl.BlockSpec((2, 1, A_BIAS_VEC), lambda i, v: (i, 0, 0))],
        out_specs=pl.BlockSpec((1, 1, A_WIN, 2 * A_TQ), lambda i, v: (i, v, 0, 0)),
        compiler_params=_cparams(("arbitrary", "arbitrary"), 32 << 20),
        name="bias_table",
    )(vec)


def kernel(x, mem, positions, norm_mix, w_in, rel_bias, q_norm, kv_norm, w_uq, w_ukv, w_out,
           norm_mem, mem_norm, w_xq, w_xkv, w_xo, norm_ffn, w_gate, w_up, w_down, norm_final):
    b, s, d = x.shape
    m = b * s
    assert (b, s, d) == (4, 4096, D_MODEL) and mem.shape == (b, N_MEM, d)

    w_in_p, wq_p, wkv_p = _prep_weights(w_in, w_uq, w_ukv)
    w_xq_b = w_xq.astype(BF16)
    w_xo_b = w_xo.astype(BF16)
    bias_tab = _relpos_bias_tables(rel_bias)

    half = B_ROPE // 2
    inv = ROPE_THETA ** (-jnp.arange(half, dtype=F32) / half)
    inv_row = jnp.tile(inv, 4)[None, :]
    sign_row = jnp.concatenate([jnp.ones((B_ROPE,), F32), -jnp.ones((half,), F32),
                                jnp.ones((half,), F32)])[None, :]
    cs = _rope_table(positions.reshape(m, 1), inv_row, sign_row, tm=2048)

    kvm = _mem_kv(mem.reshape(b * N_MEM, d), mem_norm[None, :], w_xkv)

    norm_mix, norm_mem, norm_ffn, q_norm, kv_norm = (
        g[:, None, :] for g in (norm_mix, norm_mem, norm_ffn, q_norm, kv_norm))
    xf = x.reshape(m, d)
    for l in range(DEPTH):
        proj = _norm_matmul(xf, norm_mix, w_in_p, l, tm=1024, tn=1536, name=f"in_proj_{l}")
        oa = _attn_a(proj, bias_tab, l, batch=b, seq=s, name=f"attn_a_{l}")
        qt, kn, vt, kpe, w_out_l = _mla_prep(proj, cs, q_norm, kv_norm, wq_p, wkv_p, w_out, l,
                                             seq=s, tm=512, name=f"mla_prep_{l}")
        ob, w_gate_l, w_up_l = _attn_b(qt, kn, kpe, vt, w_gate, w_up, l, batch=b, seq=s,
                                       name=f"attn_b_{l}")
        xf = _cross(xf, oa, ob, w_out_l, norm_mem, w_xq_b, kvm, w_xo_b, l, seq=s, tm=512,
                    name=f"out_cross_{l}")
        act, w_down_l = _swiglu_gu(xf, norm_ffn, w_gate_l, w_up_l, w_down, l, 0, tm=1024, tn=512,
                                   name=f"swiglu_gu_{l}")
        xf = _mm_res([act], w_down_l, xf, 0, tm=1024, tn=512, name=f"swiglu_down_{l}")
    out = _final_norm(xf, norm_final[None, :], tm=512)
    return out.reshape(b, s, d)
```

```python
import functools
import math

import jax
import jax.numpy as jnp
from jax import lax
from jax.experimental import pallas as pl
from jax.experimental.pallas import tpu as pltpu

D_MODEL = 2048
DEPTH = 4
CHUNK = 64
LEFT_CHUNKS = 8
A_HEAD_DIM = 128
A_HEADS = 8
A_WIDTH = A_HEADS * A_HEAD_DIM
REL_CLIP = 128
B_NOPE = 128
B_ROPE = 64
B_V = 128
B_HEADS = 8
B_WIDTH = B_HEADS * B_V
Q_LORA = 768
KV_LORA = 512
ROPE_THETA = 10000.0
N_MEM = 256
X_HEADS = 4
X_HEAD_DIM = 128
X_WIDTH = X_HEADS * X_HEAD_DIM
X_ROW_PARTS = 2
D_FF = 5632
EPS = 1e-6

V7X_LANES = 128
V7X_VMEM_BYTES = 64 * 1024 * 1024
V7X_VMEM_REQUEST_CAP = 56 * 1024 * 1024

BF16 = jnp.bfloat16
F32 = jnp.float32

IN_QLAT_OFF = 3 * A_WIDTH
IN_ROPE_OFF = IN_QLAT_OFF + Q_LORA
IN_KVLAT_OFF = IN_ROPE_OFF + 2 * V7X_LANES
IN_PAD_WIDTH = IN_KVLAT_OFF + KV_LORA

A_TQ = 4 * CHUNK
A_WIN = A_TQ + LEFT_CHUNKS * CHUNK
A_KBLKS = A_WIN // A_TQ
A_BIAS_VEC = 1024
A_LOOKAHEAD = 2

B_TQ = 512
B_TK = 512
B_HEADS_PER_STEP = 4
B_LOOKAHEAD = 2
B_RED_SLABS = 8
B_QK = B_NOPE + 2 * B_ROPE

LOG2E = math.log2(math.e)
A_SCORE_SCALE = A_HEAD_DIM ** -0.5 * LOG2E


def _cparams(sem, vmem_bytes):
    limit = int(min(max(vmem_bytes, 32 * 1024 * 1024), V7X_VMEM_REQUEST_CAP))
    return pltpu.CompilerParams(dimension_semantics=sem, vmem_limit_bytes=limit)


def _rms_rows(x, g):
    ms = jnp.mean(x * x, axis=-1, keepdims=True)
    return (x * lax.rsqrt(ms + EPS)) * g


def _norm_into(h_ref, x_ref, g_ref, rows=128):
    n = x_ref.shape[0] // rows

    def body(c, carry):
        r0 = pl.multiple_of(c * rows, rows)
        x = x_ref[pl.ds(r0, rows), :].astype(F32)
        h_ref[pl.ds(r0, rows), :] = _rms_rows(x, g_ref[...]).astype(h_ref.dtype)
        return carry

    lax.fori_loop(0, n, body, 0)


def _norm_matmul_kernel(x_ref, g_ref, w_ref, o_ref, h_ref):
    @pl.when(pl.program_id(1) == 0)
    def _():
        _norm_into(h_ref, x_ref, g_ref)

    o_ref[...] = jnp.dot(h_ref[...], w_ref[...], preferred_element_type=F32).astype(o_ref.dtype)


def _norm_matmul(x, g, w, layer, *, tm, tn, name):
    m, k = x.shape
    n = w.shape[2]
    xb = x.dtype.itemsize
    est = 2 * tm * k * xb + tm * k * 2 + 2 * k * tn * 2 + 2 * tm * tn * 2 + tm * tn * 4 + (4 << 20)
    return pl.pallas_call(
        _norm_matmul_kernel,
        out_shape=jax.ShapeDtypeStruct((m, n), BF16),
        grid=(m // tm, n // tn),
        in_specs=[
            pl.BlockSpec((tm, k), lambda i, j: (i, 0)),
            pl.BlockSpec((None, 1, k), lambda i, j: (layer, 0, 0)),
            pl.BlockSpec((None, k, tn), lambda i, j: (layer, 0, j)),
        ],
        out_specs=pl.BlockSpec((tm, tn), lambda i, j: (i, j)),
        scratch_shapes=[pltpu.VMEM((tm, k), BF16)],
        compiler_params=_cparams(("arbitrary", "arbitrary"), est),
        name=name,
    )(x, g, w)


def _mem_kv_kernel(x_ref, g_ref, w_ref, o_ref, h_ref):
    @pl.when(pl.program_id(0) == 0)
    def _():
        _norm_into(h_ref, x_ref, g_ref)

    o_ref[...] = jnp.dot(h_ref[...], w_ref[...].astype(BF16),
                         preferred_element_type=F32).astype(o_ref.dtype)


def _mem_kv(mem2d, g, w_xkv):
    m, k = mem2d.shape
    layers, _, n = w_xkv.shape
    est = 2 * m * k * 4 + m * k * 2 + 2 * k * n * 4 + k * n * 2 + 2 * m * n * 2 + m * n * 4 + (4 << 20)
    return pl.pallas_call(
        _mem_kv_kernel,
        out_shape=jax.ShapeDtypeStruct((m, layers * n), BF16),
        grid=(layers,),
        in_specs=[
            pl.BlockSpec((m, k), lambda l: (0, 0)),
            pl.BlockSpec((1, k), lambda l: (0, 0)),
            pl.BlockSpec((None, k, n), lambda l: (l, 0, 0)),
        ],
        out_specs=pl.BlockSpec((m, n), lambda l: (0, l)),
        scratch_shapes=[pltpu.VMEM((m, k), BF16)],
        compiler_params=_cparams(("arbitrary",), est),
        name="mem_kv",
    )(mem2d, g, w_xkv)


def _mm_res_kernel(*refs, n_in):
    a_refs = refs[:n_in]
    w_refs = refs[n_in:2 * n_in]
    r_ref = refs[2 * n_in]
    o_ref = refs[2 * n_in + 1]
    acc = r_ref[...]
    for a_ref, w_ref in zip(a_refs, w_refs):
        acc = acc + jnp.dot(a_ref[...], w_ref[...], preferred_element_type=F32)
    o_ref[...] = acc


def _mm_res(a_list, w, res, layer, *, tm, tn, name):
    m, n = res.shape
    n_in = len(a_list)
    in_specs = []
    w_specs = []
    est = (4 + 2 * n_in) * tm * tn * 4 + (4 << 20)
    for idx, a in enumerate(a_list):
        k = a.shape[1]
        assert w.shape[1] == n_in * k
        in_specs.append(pl.BlockSpec((tm, k), lambda i, j: (i, 0)))
        w_specs.append(pl.BlockSpec((None, k, tn), lambda i, j, idx=idx: (layer, idx, j)))
        est += 2 * tm * k * 2 + 2 * k * tn * 2
    in_specs += w_specs
    in_specs.append(pl.BlockSpec((tm, tn), lambda i, j: (i, j)))
    return pl.pallas_call(
        functools.partial(_mm_res_kernel, n_in=n_in),
        out_shape=jax.ShapeDtypeStruct((m, n), F32),
        grid=(m // tm, n // tn),
        in_specs=in_specs,
        out_specs=pl.BlockSpec((tm, tn), lambda i, j: (i, j)),
        compiler_params=_cparams(("arbitrary", "arbitrary"), est),
        name=name,
    )(*a_list, *([w] * n_in), res)


def _swiglu_gu_kernel(x_ref, g_ref, wg_ref, wu_ref, wd_ref, o_ref, wdb_ref, h_ref):
    @pl.when(pl.program_id(1) == 0)
    def _():
        _norm_into(h_ref, x_ref, g_ref)

    h = h_ref[...]
    gate = jnp.dot(h, wg_ref[...], preferred_element_type=F32)
    up = jnp.dot(h, wu_ref[...], preferred_element_type=F32)
    o_ref[...] = (gate * (1.0 / (1.0 + jnp.exp(-gate))) * up).astype(o_ref.dtype)
    wdb_ref[...] = wd_ref[...].astype(BF16)


def _swiglu_gu(x, g, wg, wu, wd, layer, gu_layer, *, tm, tn, name):
    m, k = x.shape
    n = wg.shape[2]
    steps = (m // tm) * (n // tn)
    nj = n // tn
    wd_rows, wd_cols = wd.shape[1], wd.shape[2]
    slab = wd_rows // steps
    assert slab * steps == wd_rows and slab % 16 == 0
    wbytes = wg.dtype.itemsize
    est = (2 * tm * k * 4 + tm * k * 2 + 4 * k * tn * wbytes + 2 * k * tn * 2 + 2 * tm * tn * 2
           + 3 * tm * tn * 4 + 2 * slab * wd_cols * 6 + (4 << 20))
    return pl.pallas_call(
        _swiglu_gu_kernel,
        out_shape=(jax.ShapeDtypeStruct((m, n), BF16),
                   jax.ShapeDtypeStruct((1, wd_rows, wd_cols), BF16)),
        grid=(m // tm, nj),
        in_specs=[
            pl.BlockSpec((tm, k), lambda i, j: (i, 0)),
            pl.BlockSpec((None, 1, k), lambda i, j: (layer, 0, 0)),
            pl.BlockSpec((None, k, tn), lambda i, j: (gu_layer, 0, j)),
            pl.BlockSpec((None, k, tn), lambda i, j: (gu_layer, 0, j)),
            pl.BlockSpec((None, slab, wd_cols), lambda i, j: (layer, i * nj + j, 0)),
        ],
        out_specs=(pl.BlockSpec((tm, tn), lambda i, j: (i, j)),
                   pl.BlockSpec((None, slab, wd_cols), lambda i, j: (0, i * nj + j, 0))),
        scratch_shapes=[pltpu.VMEM((tm, k), BF16)],
        compiler_params=_cparams(("arbitrary", "arbitrary"), est),
        name=name,
    )(x, g, wg, wu, wd)


def _rope_table_kernel(pos_ref, inv_ref, sign_ref, cs_ref):
    ang = pos_ref[...].astype(F32) * inv_ref[...]
    lane = lax.broadcasted_iota(jnp.int32, ang.shape, 1)
    cs_ref[...] = jnp.where(lane < B_ROPE, jnp.cos(ang), jnp.sin(ang) * sign_ref[...])


def _rope_table(pos_col, inv_row, sign_row, *, tm):
    m = pos_col.shape[0]
    return pl.pallas_call(
        _rope_table_kernel,
        out_shape=jax.ShapeDtypeStruct((m, V7X_LANES), F32),
        grid=(m // tm,),
        in_specs=[
            pl.BlockSpec((tm, 1), lambda i: (i, 0)),
            pl.BlockSpec((1, V7X_LANES), lambda i: (0, 0)),
            pl.BlockSpec((1, V7X_LANES), lambda i: (0, 0)),
        ],
        out_specs=pl.BlockSpec((tm, V7X_LANES), lambda i: (i, 0)),
        compiler_params=_cparams(("arbitrary",), 32 << 20),
        name="rope_table",
    )(pos_col, inv_row, sign_row)


def _rope_pair_sum(slot, cs):
    t = slot * cs
    return t + pltpu.roll(t, B_ROPE, axis=1)


def _mla_prep_kernel(ql_ref, rope_ref, kvl_ref, cs_ref, gq_ref, gkv_ref, wq_ref, wkv_ref, wo_ref,
                     qt_ref, kn_ref, vt_ref, kpe_ref, wob_ref, *, q_scale):
    cs = cs_ref[...]
    hq = _rms_rows(ql_ref[...].astype(F32), gq_ref[...]).astype(BF16)
    qf = jnp.dot(hq, wq_ref[...], preferred_element_type=F32)
    for h in range(B_HEADS):
        base = h * B_QK
        nope = qf[:, base:base + B_NOPE]
        pe = _rope_pair_sum(qf[:, base + B_NOPE:base + B_QK], cs)
        qt_ref[base:base + B_NOPE, :] = (nope * q_scale).T.astype(BF16)
        qt_ref[base + B_NOPE:base + B_QK, :] = (pe * q_scale).T.astype(BF16)

    hkv = _rms_rows(kvl_ref[...].astype(F32), gkv_ref[...]).astype(BF16)
    kvf = jnp.dot(hkv, wkv_ref[...], preferred_element_type=F32)
    kn_ref[...] = kvf[:, :B_HEADS * B_NOPE].astype(BF16)
    vt_ref[...] = kvf[:, B_HEADS * B_NOPE:].T.astype(BF16)

    kpe = _rope_pair_sum(rope_ref[...].astype(F32), cs)
    lane = lax.broadcasted_iota(jnp.int32, kpe.shape, 1)
    kpe_ref[...] = jnp.where(lane < B_ROPE, kpe, 0.0).astype(BF16)
    wob_ref[...] = wo_ref[...].astype(BF16)


def _mla_prep(proj, cs, gq, gkv, wq, wkv, wo, layer, *, seq, tm, name):
    m = proj.shape[0]
    tiles_per_seq = seq // tm
    q_scale = (B_NOPE + B_ROPE) ** -0.5 * LOG2E
    wo_rows, wo_cols = wo.shape[1], wo.shape[2]
    slab = wo_rows // (m // tm)
    assert slab * (m // tm) == wo_rows and slab % 16 == 0
    est = (2 * tm * (Q_LORA + KV_LORA + 128) * 2 + 2 * tm * 128 * 4
           + 2 * (Q_LORA + KV_LORA) * 2048 * 2 + 2 * slab * wo_cols * 6
           + 2 * tm * (2048 + 1024 + 1024 + 128) * 2 + 3 * tm * 2048 * 4 + (4 << 20))
    return pl.pallas_call(
        functools.partial(_mla_prep_kernel, q_scale=q_scale),
        out_shape=(
            jax.ShapeDtypeStruct((m // seq * B_HEADS * B_QK, seq), BF16),
            jax.ShapeDtypeStruct((m, B_HEADS * B_NOPE), BF16),
            jax.ShapeDtypeStruct((m // seq * B_WIDTH, seq), BF16),
            jax.ShapeDtypeStruct((m, V7X_LANES), BF16),
            jax.ShapeDtypeStruct((1, wo_rows, wo_cols), BF16),
        ),
        grid=(m // tm,),
        in_specs=[
            pl.BlockSpec((tm, Q_LORA), lambda i: (i, IN_QLAT_OFF // Q_LORA)),
            pl.BlockSpec((tm, V7X_LANES), lambda i: (i, IN_ROPE_OFF // V7X_LANES)),
            pl.BlockSpec((tm, KV_LORA), lambda i: (i, IN_KVLAT_OFF // KV_LORA)),
            pl.BlockSpec((tm, V7X_LANES), lambda i: (i, 0)),
            pl.BlockSpec((None, 1, Q_LORA), lambda i: (layer, 0, 0)),
            pl.BlockSpec((None, 1, KV_LORA), lambda i: (layer, 0, 0)),
            pl.BlockSpec((None, Q_LORA, B_HEADS * B_QK), lambda i: (layer, 0, 0)),
            pl.BlockSpec((None, KV_LORA, 2 * B_WIDTH), lambda i: (layer, 0, 0)),
            pl.BlockSpec((None, slab, wo_cols), lambda i: (layer, i, 0)),
        ],
        out_specs=(
            pl.BlockSpec((B_HEADS * B_QK, tm), lambda i: (i // tiles_per_seq, i % tiles_per_seq)),
            pl.BlockSpec((tm, B_HEADS * B_NOPE), lambda i: (i, 0)),
            pl.BlockSpec((B_WIDTH, tm), lambda i: (i // tiles_per_seq, i % tiles_per_seq)),
            pl.BlockSpec((tm, V7X_LANES), lambda i: (i, 0)),
            pl.BlockSpec((None, slab, wo_cols), lambda i: (0, i, 0)),
        ),
        compiler_params=_cparams(("arbitrary",), est),
        name=name,
    )(proj, proj, proj, cs, gq, gkv, wq, wkv, wo)


class _MixerAHeads:
    def __init__(self, q_ref, k_refs, v_refs, bias_ref, o_ref):
        self.q_ref, self.k_refs, self.v_refs = q_ref, k_refs, v_refs
        self.bias_ref, self.o_ref = bias_ref, o_ref
        self.tiles = [self._scores(h) for h in range(A_LOOKAHEAD)] + [None] * (A_HEADS - A_LOOKAHEAD)

    def _scores(self, h):
        hs = slice(h * A_HEAD_DIM, (h + 1) * A_HEAD_DIM)
        k = jnp.concatenate([r[:, hs] for r in self.k_refs], axis=0)
        return jnp.dot(k, self.q_ref[:, hs].T, preferred_element_type=F32)

    def head(self, h):
        d, tq = A_HEAD_DIM, A_TQ
        hs = slice(h * d, (h + 1) * d)
        if h + A_LOOKAHEAD < A_HEADS:
            self.tiles[h + A_LOOKAHEAD] = self._scores(h + A_LOOKAHEAD)
        v = jnp.concatenate([r[:, hs] for r in self.v_refs], axis=0)
        st = self.bias_ref[h // 2][:, (h % 2) * tq:(h % 2 + 1) * tq] + self.tiles[h]
        self.tiles[h] = None
        m = _col_reduce(st, jnp.max)
        p = jnp.exp2(st - m)
        l = _col_reduce(p, jnp.sum)
        ot = lax.dot_general(v, p.astype(BF16), (((0,), (0,)), ((), ())),
                             preferred_element_type=F32)
        self.o_ref[:, hs] = (ot / l).T.astype(self.o_ref.dtype)


def _col_reduce(x, op):
    rows, cols = x.shape
    slabs = op(x.reshape(B_RED_SLABS, rows // B_RED_SLABS, cols), axis=0)
    return op(slabs, axis=0, keepdims=True)


def _mixers_kernel(q_ref, kn_ref, kpe_ref, vt_ref, wg_ref, wu_ref,
                   qa_ref, ka0_ref, ka1_ref, ka2_ref, va0_ref, va1_ref, va2_ref, bias_ref,
                   o_ref, wgb_ref, wub_ref, oa_ref,
                   m_ref, l_ref, acc_ref, st_ref):
    i = pl.program_id(2)
    hp = B_HEADS_PER_STEP
    m_ref[...] = jnp.full(m_ref.shape, -jnp.inf, F32)
    l_ref[...] = jnp.zeros(l_ref.shape, F32)
    acc_ref[...] = jnp.zeros(acc_ref.shape, F32)

    def scores(j, h):
        r0 = pl.multiple_of(j * B_TK, B_TK)
        k = jnp.concatenate([kn_ref[pl.ds(r0, B_TK), h * B_NOPE:(h + 1) * B_NOPE],
                             kpe_ref[pl.ds(r0, B_TK), :]], axis=1)
        return jnp.dot(k, q_ref[h * B_QK:(h + 1) * B_QK, :], preferred_element_type=F32)

    def softmax_pv(j, h, st, allowed):
        r0 = pl.multiple_of(j * B_TK, B_TK)
        if allowed is not None:
            st = jnp.where(allowed, st, -jnp.inf)
        m_prev = m_ref[h]
        m_new = jnp.maximum(m_prev, _col_reduce(st, jnp.max))
        alpha = jnp.exp2(m_prev - m_new)
        p = jnp.exp2(st - m_new)
        l_ref[h] = alpha * l_ref[h] + _col_reduce(p, jnp.sum)
        pv = jnp.dot(vt_ref[h * B_V:(h + 1) * B_V, pl.ds(r0, B_TK)], p.astype(BF16),
                     preferred_element_type=F32)
        acc_ref[h] = alpha * acc_ref[h] + pv
        m_ref[h] = m_new

    def key_block(j, masked, after_head=None):
        allowed = None
        if masked:
            krow = lax.broadcasted_iota(jnp.int32, (B_TK, B_TQ), 0)
            qcol = lax.broadcasted_iota(jnp.int32, (B_TK, B_TQ), 1)
            allowed = (krow // CHUNK) <= (qcol // CHUNK)
        look = B_LOOKAHEAD
        tiles = [st_ref[h] for h in range(look)] + [None] * (hp - look)
        for h in range(hp):
            ahead = h + look
            if ahead < hp:
                tiles[ahead] = scores(j, ahead)
            elif not masked:
                st_ref[ahead - hp] = scores(j + 1, ahead - hp)
            softmax_pv(j, h, tiles[h], allowed)
            tiles[h] = None
            if after_head is not None:
                after_head(h)

    def body(j, carry):
        key_block(j, False)
        return carry

    for h in range(B_LOOKAHEAD):
        st_ref[h] = scores(0, h)
    lax.fori_loop(0, i, body, 0)

    mixer_a = _MixerAHeads(qa_ref, (ka0_ref, ka1_ref, ka2_ref), (va0_ref, va1_ref, va2_ref),
                           bias_ref, oa_ref)
    a_per_b = A_HEADS // hp

    def mixer_a_heads(h):
        for ha in range(h * a_per_b, (h + 1) * a_per_b):
            mixer_a.head(ha)

    key_block(i, True, after_head=mixer_a_heads)

    for h in range(hp):
        o_ref[:, h * B_V:(h + 1) * B_V] = (acc_ref[h] / l_ref[h]).T.astype(o_ref.dtype)

    wgb_ref[...] = wg_ref[...].astype(BF16)
    wub_ref[...] = wu_ref[...].astype(BF16)


def _mixers(proj, bias, qt, kn, kpe, vt, wg, wu, layer, *, batch, seq, name):
    m = kn.shape[0]
    nq = seq // B_TQ
    hp = B_HEADS_PER_STEP
    ng = B_HEADS // hp
    steps = batch * ng * nq
    nqa = seq // A_TQ
    assert steps == batch * nqa
    w_rows, w_cols = wg.shape[1], wg.shape[2]
    slab = w_rows // steps
    assert slab * steps == w_rows and slab % 16 == 0 and wu.shape == wg.shape
    est = (2 * B_TQ * hp * B_QK * 2 + 2 * seq * hp * (B_NOPE + B_V) * 2 + 2 * seq * 128 * 2
           + 2 * B_TQ * hp * B_V * 2 + hp * (B_V + 16) * B_TQ * 4 + 8 * B_TQ * B_TK * 4
           + 4 * slab * w_cols * 6
           + 2 * 8 * A_TQ * A_WIDTH * 2 + 2 * A_HEADS * A_TQ * A_WIN * 4 + 6 * A_WIN * A_TQ * 4
           + (4 << 20))

    def slab_in(b, g, i):
        return (layer, (b * ng + g) * nq + i, 0)

    def slab_out(b, g, i):
        return (0, (b * ng + g) * nq + i, 0)

    def a_blk(b, g, i):
        return g * nq + i

    def a_kv_spec(col_blk, c):
        back = A_KBLKS - 1 - c
        return pl.BlockSpec((A_TQ, A_WIDTH),
                            lambda b, g, i: (b * nqa + jnp.maximum(a_blk(b, g, i) - back, 0), col_blk))

    return pl.pallas_call(
        _mixers_kernel,
        out_shape=(jax.ShapeDtypeStruct((m, B_WIDTH), BF16),
                   jax.ShapeDtypeStruct((1, w_rows, w_cols), BF16),
                   jax.ShapeDtypeStruct((1, w_rows, w_cols), BF16),
                   jax.ShapeDtypeStruct((m, A_WIDTH), BF16)),
        grid=(batch, ng, nq),
        in_specs=[
            pl.BlockSpec((hp * B_QK, B_TQ), lambda b, g, i: (b * ng + g, i)),
            pl.BlockSpec((seq, hp * B_NOPE), lambda b, g, i: (b, g)),
            pl.BlockSpec((seq, V7X_LANES), lambda b, g, i: (b, 0)),
            pl.BlockSpec((hp * B_V, seq), lambda b, g, i: (b * ng + g, 0)),
            pl.BlockSpec((None, slab, w_cols), slab_in),
            pl.BlockSpec((None, slab, w_cols), slab_in),
            pl.BlockSpec((A_TQ, A_WIDTH), lambda b, g, i: (b * nqa + a_blk(b, g, i), 0)),
            a_kv_spec(1, 0), a_kv_spec(1, 1), a_kv_spec(1, 2),
            a_kv_spec(2, 0), a_kv_spec(2, 1), a_kv_spec(2, 2),
            pl.BlockSpec((A_HEADS // 2, None, A_WIN, 2 * A_TQ),
                         lambda b, g, i: (layer, jnp.minimum(a_blk(b, g, i), A_KBLKS - 1), 0, 0)),
        ],
        out_specs=(pl.BlockSpec((B_TQ, hp * B_V), lambda b, g, i: (b * nq + i, g)),
                   pl.BlockSpec((None, slab, w_cols), slab_out),
                   pl.BlockSpec((None, slab, w_cols), slab_out),
                   pl.BlockSpec((A_TQ, A_WIDTH), lambda b, g, i: (b * nqa + a_blk(b, g, i), 0))),
        scratch_shapes=[
            pltpu.VMEM((hp, 1, B_TQ), F32),
            pltpu.VMEM((hp, 1, B_TQ), F32),
            pltpu.VMEM((hp, B_V, B_TQ), F32),
            pltpu.VMEM((B_LOOKAHEAD, B_TK, B_TQ), F32),
        ],
        compiler_params=_cparams(("arbitrary", "arbitrary", "arbitrary"), est),
        name=name,
    )(qt, kn, kpe, vt, wg, wu, proj, proj, proj, proj, proj, proj, proj, bias)


def _cross_kernel(x_ref, oa_ref, ob_ref, woa_ref, wob_ref, g_ref, wq_ref, k_ref, v_ref, wo_ref,
                  o_ref):
    scale = X_HEAD_DIM ** -0.5
    rows = x_ref.shape[0] // X_ROW_PARTS
    for r in range(X_ROW_PARTS):
        rs = slice(r * rows, (r + 1) * rows)
        x = (x_ref[rs, :] + jnp.dot(oa_ref[rs, :], woa_ref[...], preferred_element_type=F32)
             + jnp.dot(ob_ref[rs, :], wob_ref[...], preferred_element_type=F32))
        hn = _rms_rows(x, g_ref[...]).astype(BF16)
        q = (jnp.dot(hn, wq_ref[...], preferred_element_type=F32) * scale).astype(BF16)
        outs = []
        for h in range(X_HEADS):
            hs = slice(h * X_HEAD_DIM, (h + 1) * X_HEAD_DIM)
            s = lax.dot_general(q[:, hs], k_ref[:, hs], (((1,), (1,)), ((), ())),
                                preferred_element_type=F32)
            m = jnp.max(s, axis=1, keepdims=True)
            p = jnp.exp(s - m)
            l = jnp.sum(p, axis=1, keepdims=True)
            o = jnp.dot(p.astype(BF16), v_ref[:, hs], preferred_element_type=F32)
            outs.append((o / l).astype(BF16))
        o_all = jnp.concatenate(outs, axis=1)
        o_ref[rs, :] = x + jnp.dot(o_all, wo_ref[...], preferred_element_type=F32)


def _cross(x, oa, ob, w_out_l, g, wq, kvm, wo, layer, *, seq, tm, name):
    m, d = x.shape
    blocks_per_batch = seq // tm
    est = (4 * tm * d * 4 + tm * d * 2 + 2 * d * X_WIDTH * 2 + 2 * X_WIDTH * d * 2
           + 4 * N_MEM * X_WIDTH * 2 + 4 * tm * N_MEM * 4 + 3 * tm * d * 4
           + 2 * tm * (A_WIDTH + B_WIDTH) * 2 + 2 * (A_WIDTH + B_WIDTH) * d * 2 + (4 << 20))
    return pl.pallas_call(
        _cross_kernel,
        out_shape=jax.ShapeDtypeStruct((m, d), F32),
        grid=(m // tm,),
        in_specs=[
            pl.BlockSpec((tm, d), lambda i: (i, 0)),
            pl.BlockSpec((tm, A_WIDTH), lambda i: (i, 0)),
            pl.BlockSpec((tm, B_WIDTH), lambda i: (i, 0)),
            pl.BlockSpec((None, A_WIDTH, d), lambda i: (0, 0, 0)),
            pl.BlockSpec((None, B_WIDTH, d), lambda i: (0, A_WIDTH // B_WIDTH, 0)),
            pl.BlockSpec((None, 1, d), lambda i: (layer, 0, 0)),
            pl.BlockSpec((None, d, X_WIDTH), lambda i: (layer, 0, 0)),
            pl.BlockSpec((N_MEM, X_WIDTH), lambda i: (i // blocks_per_batch, 2 * layer)),
            pl.BlockSpec((N_MEM, X_WIDTH), lambda i: (i // blocks_per_batch, 2 * layer + 1)),
            pl.BlockSpec((None, X_WIDTH, d), lambda i: (layer, 0, 0)),
        ],
        out_specs=pl.BlockSpec((tm, d), lambda i: (i, 0)),
        compiler_params=_cparams(("arbitrary",), est),
        name=name,
    )(x, oa, ob, w_out_l, w_out_l, g, wq, kvm, kvm, wo)


def _final_norm_kernel(x_ref, g_ref, o_ref):
    o_ref[...] = _rms_rows(x_ref[...], g_ref[...])


def _final_norm(x, g, *, tm):
    m, d = x.shape
    return pl.pallas_call(
        _final_norm_kernel,
        out_shape=jax.ShapeDtypeStruct((m, d), F32),
        grid=(m // tm,),
        in_specs=[pl.BlockSpec((tm, d), lambda i: (i, 0)), pl.BlockSpec((1, d), lambda i: (0, 0))],
        out_specs=pl.BlockSpec((tm, d), lambda i: (i, 0)),
        compiler_params=_cparams(("arbitrary",), 6 * tm * d * 4 + (4 << 20)),
        name="final_norm",
    )(x, g)


RELAYOUT_COLS = 256
RELAYOUT_ROPE_BLK = IN_ROPE_OFF // RELAYOUT_COLS


def _w_in_relayout_kernel(wt_ref, o_ref):
    j = pl.program_id(1)

    @pl.when(j != RELAYOUT_ROPE_BLK)
    def _():
        factor = jnp.where(j < A_WIDTH // RELAYOUT_COLS, A_SCORE_SCALE, 1.0)
        o_ref[...] = (wt_ref[...] * factor).T.astype(BF16)

    @pl.when(j == RELAYOUT_ROPE_BLK)
    def _():
        half = B_ROPE // 2
        kr = wt_ref[:B_ROPE, :]
        slot = jnp.concatenate([kr, kr[half:], kr[:half],
                                jnp.zeros((RELAYOUT_COLS - 2 * B_ROPE, kr.shape[1]), F32)], axis=0)
        o_ref[...] = slot.T.astype(BF16)


def _w_in_relayout(w_in_t):
    layers, n, k = w_in_t.shape
    lat_blk = IN_ROPE_OFF // RELAYOUT_COLS
    rope_src = (IN_ROPE_OFF + KV_LORA) // RELAYOUT_COLS
    assert IN_KVLAT_OFF == (RELAYOUT_ROPE_BLK + 1) * RELAYOUT_COLS

    def src_block(l, j):
        return (l, jnp.where(j < RELAYOUT_ROPE_BLK, j,
                             jnp.where(j == RELAYOUT_ROPE_BLK, rope_src, j - 1 + lat_blk - RELAYOUT_ROPE_BLK)), 0)

    return pl.pallas_call(
        _w_in_relayout_kernel,
        out_shape=jax.ShapeDtypeStruct((layers, k, IN_PAD_WIDTH), BF16),
        grid=(layers, IN_PAD_WIDTH // RELAYOUT_COLS),
        in_specs=[pl.BlockSpec((None, RELAYOUT_COLS, k), src_block)],
        out_specs=pl.BlockSpec((None, k, RELAYOUT_COLS), lambda l, j: (l, 0, j)),
        compiler_params=_cparams(("arbitrary", "arbitrary"), 32 << 20),
        name="w_in_relayout",
    )(w_in_t)


def _prep_weights(w_in, w_uq, w_ukv):
    swap = (jnp.arange(B_ROPE) + B_ROPE // 2) % B_ROPE
    w_in_p = _w_in_relayout(jnp.swapaxes(w_in, 1, 2))

    uq = w_uq.reshape(DEPTH, Q_LORA, B_HEADS, B_NOPE + B_ROPE)
    uq_pe = uq[..., B_NOPE:]
    wq_p = jnp.concatenate([uq[..., :B_NOPE], uq_pe, uq_pe[..., swap]], axis=-1)
    wq_p = wq_p.reshape(DEPTH, Q_LORA, B_HEADS * B_QK).astype(BF16)

    ukv = w_ukv.reshape(DEPTH, KV_LORA, B_HEADS, B_NOPE + B_V)
    wkv_p = jnp.concatenate([ukv[..., :B_NOPE].reshape(DEPTH, KV_LORA, B_HEADS * B_NOPE),
                             ukv[..., B_NOPE:].reshape(DEPTH, KV_LORA, B_WIDTH)],
                            axis=-1).astype(BF16)

    return w_in_p, wq_p, wkv_p


def _bias_table_kernel(w_ref, o_ref):
    v = pl.program_id(1)
    r = lax.broadcasted_iota(jnp.int32, (A_TQ, A_WIN), 0)
    t = lax.broadcasted_iota(jnp.int32, (A_TQ, A_WIN), 1)
    dchunk = t // CHUNK - r // CHUNK
    valid = (dchunk >= 0) & (dchunk <= LEFT_CHUNKS) & (v - (A_KBLKS - 1) + t // A_TQ >= 0)
    for hh in range(2):
        w = jnp.broadcast_to(w_ref[hh], (A_TQ, A_BIAS_VEC))
        tab = pltpu.roll(w, 0, axis=1, stride=1, stride_axis=0)[:, :A_WIN]
        o_ref[0, 0, :, hh * A_TQ:(hh + 1) * A_TQ] = jnp.where(valid, tab * LOG2E, -jnp.inf).T


def _relpos_bias_tables(rel_bias):
    u = jnp.arange(A_BIAS_VEC)
    delta = jnp.where(u <= A_WIN, u, u - A_BIAS_VEC)
    idx = jnp.clip(LEFT_CHUNKS * CHUNK - delta, -REL_CLIP, REL_CLIP) + REL_CLIP
    vec = rel_bias[:, :, idx].reshape(DEPTH * A_HEADS, 1, A_BIAS_VEC).astype(F32)
    return pl.pallas_call(
        _bias_table_kernel,
        out_shape=jax.ShapeDtypeStruct((DEPTH * A_HEADS // 2, A_KBLKS, A_WIN, 2 * A_TQ), F32),
        grid=(DEPTH * A_HEADS // 2, A_KBLKS),
        in_specs=[pl.BlockSpec((2, 1, A_BIAS_VEC), lambda i, v: (i, 0, 0))],
        out_specs=pl.BlockSpec((1, 1, A_WIN, 2 * A_TQ), lambda i, v: (i, v, 0, 0)),
        compiler_params=_cparams(("arbitrary", "arbitrary"), 32 << 20),
        name="bias_table",
    )(vec)


def kernel(x, mem, positions, norm_mix, w_in, rel_bias, q_norm, kv_norm, w_uq, w_ukv, w_out,
           norm_mem, mem_norm, w_xq, w_xkv, w_xo, norm_ffn, w_gate, w_up, w_down, norm_final):
    b, s, d = x.shape
    m = b * s
    assert (b, s, d) == (4, 4096, D_MODEL) and mem.shape == (b, N_MEM, d)

    w_in_p, wq_p, wkv_p = _prep_weights(w_in, w_uq, w_ukv)
    w_xq_b = w_xq.astype(BF16)
    w_xo_b = w_xo.astype(BF16)
    bias_tab = _relpos_bias_tables(rel_bias)

    half = B_ROPE // 2
    inv = ROPE_THETA ** (-jnp.arange(half, dtype=F32) / half)
    inv_row = jnp.tile(inv, 4)[None, :]
    sign_row = jnp.concatenate([jnp.ones((B_ROPE,), F32), -jnp.ones((half,), F32),
                                jnp.ones((half,), F32)])[None, :]
    cs = _rope_table(positions.reshape(m, 1), inv_row, sign_row, tm=2048)

    kvm = _mem_kv(mem.reshape(b * N_MEM, d), mem_norm[None, :], w_xkv)

    norm_mix, norm_mem, norm_ffn, q_norm, kv_norm = (
        g[:, None, :] for g in (norm_mix, norm_mem, norm_ffn, q_norm, kv_norm))
    xf = x.reshape(m, d)
    for l in range(DEPTH):
        proj = _norm_matmul(xf, norm_mix, w_in_p, l, tm=1024, tn=1536, name=f"in_proj_{l}")
        qt, kn, vt, kpe, w_out_l = _mla_prep(proj, cs, q_norm, kv_norm, wq_p, wkv_p, w_out, l,
                                             seq=s, tm=512, name=f"mla_prep_{l}")
        ob, w_gate_l, w_up_l, oa = _mixers(proj, bias_tab, qt, kn, kpe, vt, w_gate, w_up, l,
                                           batch=b, seq=s, name=f"mixers_{l}")
        xf = _cross(xf, oa, ob, w_out_l, norm_mem, w_xq_b, kvm, w_xo_b, l, seq=s, tm=512,
                    name=f"out_cross_{l}")
        act, w_down_l = _swiglu_gu(xf, norm_ffn, w_gate_l, w_up_l, w_down, l, 0, tm=1024, tn=512,
                                   name=f"swiglu_gu_{l}")
        xf = _mm_res([act], w_down_l, xf, 0, tm=1024, tn=512, name=f"swiglu_down_{l}")
    out = _final_norm(xf, norm_final[None, :], tm=512)
    return out.reshape(b, s, d)
```

```python
import functools
import math

import jax
import jax.numpy as jnp
from jax import lax
from jax.experimental import pallas as pl
from jax.experimental.pallas import tpu as pltpu

D_MODEL = 2048
DEPTH = 4
CHUNK = 64
LEFT_CHUNKS = 8
A_HEAD_DIM = 128
A_HEADS = 8
A_WIDTH = A_HEADS * A_HEAD_DIM
REL_CLIP = 128
B_NOPE = 128
B_ROPE = 64
B_V = 128
B_HEADS = 8
B_WIDTH = B_HEADS * B_V
Q_LORA = 768
KV_LORA = 512
ROPE_THETA = 10000.0
N_MEM = 256
X_HEADS = 4
X_HEAD_DIM = 128
X_WIDTH = X_HEADS * X_HEAD_DIM
X_ROW_PARTS = 2
D_FF = 5632
EPS = 1e-6

V7X_LANES = 128
V7X_VMEM_BYTES = 64 * 1024 * 1024
V7X_VMEM_REQUEST_CAP = 56 * 1024 * 1024

BF16 = jnp.bfloat16
F32 = jnp.float32

IN_QLAT_OFF = 3 * A_WIDTH
IN_ROPE_OFF = IN_QLAT_OFF + Q_LORA
IN_KVLAT_OFF = IN_ROPE_OFF + 2 * V7X_LANES
IN_PAD_WIDTH = IN_KVLAT_OFF + KV_LORA

A_TQ = 4 * CHUNK
A_WIN = A_TQ + LEFT_CHUNKS * CHUNK
A_KBLKS = A_WIN // A_TQ
A_BIAS_VEC = 1024
A_LOOKAHEAD = 2

B_TQ = 512
B_TK = 512
B_HEADS_PER_STEP = 4
B_LOOKAHEAD = 2
B_RED_SLABS = 8
B_QK = B_NOPE + 2 * B_ROPE

LOG2E = math.log2(math.e)
A_SCORE_SCALE = A_HEAD_DIM ** -0.5 * LOG2E


def _cparams(sem, vmem_bytes):
    limit = int(min(max(vmem_bytes, 32 * 1024 * 1024), V7X_VMEM_REQUEST_CAP))
    return pltpu.CompilerParams(dimension_semantics=sem, vmem_limit_bytes=limit)


def _rms_rows(x, g):
    ms = jnp.mean(x * x, axis=-1, keepdims=True)
    return (x * lax.rsqrt(ms + EPS)) * g


def _norm_into(h_ref, x_ref, g_ref, rows=128):
    n = x_ref.shape[0] // rows

    def body(c, carry):
        r0 = pl.multiple_of(c * rows, rows)
        x = x_ref[pl.ds(r0, rows), :].astype(F32)
        h_ref[pl.ds(r0, rows), :] = _rms_rows(x, g_ref[...]).astype(h_ref.dtype)
        return carry

    lax.fori_loop(0, n, body, 0)


def _norm_matmul_kernel(x_ref, g_ref, w_ref, o_ref, h_ref):
    @pl.when(pl.program_id(1) == 0)
    def _():
        _norm_into(h_ref, x_ref, g_ref)

    o_ref[...] = jnp.dot(h_ref[...], w_ref[...], preferred_element_type=F32).astype(o_ref.dtype)


def _norm_matmul(x, g, w, layer, *, tm, tn, name):
    m, k = x.shape
    n = w.shape[2]
    xb = x.dtype.itemsize
    est = 2 * tm * k * xb + tm * k * 2 + 2 * k * tn * 2 + 2 * tm * tn * 2 + tm * tn * 4 + (4 << 20)
    return pl.pallas_call(
        _norm_matmul_kernel,
        out_shape=jax.ShapeDtypeStruct((m, n), BF16),
        grid=(m // tm, n // tn),
        in_specs=[
            pl.BlockSpec((tm, k), lambda i, j: (i, 0)),
            pl.BlockSpec((None, 1, k), lambda i, j: (layer, 0, 0)),
            pl.BlockSpec((None, k, tn), lambda i, j: (layer, 0, j)),
        ],
        out_specs=pl.BlockSpec((tm, tn), lambda i, j: (i, j)),
        scratch_shapes=[pltpu.VMEM((tm, k), BF16)],
        compiler_params=_cparams(("arbitrary", "arbitrary"), est),
        name=name,
    )(x, g, w)


def _mem_kv_kernel(x_ref, g_ref, w_ref, o_ref, h_ref):
    @pl.when(pl.program_id(0) == 0)
    def _():
        _norm_into(h_ref, x_ref, g_ref)

    o_ref[...] = jnp.dot(h_ref[...], w_ref[...].astype(BF16),
                         preferred_element_type=F32).astype(o_ref.dtype)


def _mem_kv(mem2d, g, w_xkv):
    m, k = mem2d.shape
    layers, _, n = w_xkv.shape
    est = 2 * m * k * 4 + m * k * 2 + 2 * k * n * 4 + k * n * 2 + 2 * m * n * 2 + m * n * 4 + (4 << 20)
    return pl.pallas_call(
        _mem_kv_kernel,
        out_shape=jax.ShapeDtypeStruct((m, layers * n), BF16),
        grid=(layers,),
        in_specs=[
            pl.BlockSpec((m, k), lambda l: (0, 0)),
            pl.BlockSpec((1, k), lambda l: (0, 0)),
            pl.BlockSpec((None, k, n), lambda l: (l, 0, 0)),
        ],
        out_specs=pl.BlockSpec((m, n), lambda l: (0, l)),
        scratch_shapes=[pltpu.VMEM((m, k), BF16)],
        compiler_params=_cparams(("arbitrary",), est),
        name="mem_kv",
    )(mem2d, g, w_xkv)


def _mm_res_kernel(*refs, n_in):
    a_refs = refs[:n_in]
    w_refs = refs[n_in:2 * n_in]
    r_ref = refs[2 * n_in]
    o_ref = refs[2 * n_in + 1]
    acc = r_ref[...]
    for a_ref, w_ref in zip(a_refs, w_refs):
        acc = acc + jnp.dot(a_ref[...], w_ref[...], preferred_element_type=F32)
    o_ref[...] = acc


def _mm_res(a_list, w, res, layer, *, tm, tn, name):
    m, n = res.shape
    n_in = len(a_list)
    in_specs = []
    w_specs = []
    est = (4 + 2 * n_in) * tm * tn * 4 + (4 << 20)
    for idx, a in enumerate(a_list):
        k = a.shape[1]
        assert w.shape[1] == n_in * k
        in_specs.append(pl.BlockSpec((tm, k), lambda i, j: (i, 0)))
        w_specs.append(pl.BlockSpec((None, k, tn), lambda i, j, idx=idx: (layer, idx, j)))
        est += 2 * tm * k * 2 + 2 * k * tn * 2
    in_specs += w_specs
    in_specs.append(pl.BlockSpec((tm, tn), lambda i, j: (i, j)))
    return pl.pallas_call(
        functools.partial(_mm_res_kernel, n_in=n_in),
        out_shape=jax.ShapeDtypeStruct((m, n), F32),
        grid=(m // tm, n // tn),
        in_specs=in_specs,
        out_specs=pl.BlockSpec((tm, tn), lambda i, j: (i, j)),
        compiler_params=_cparams(("arbitrary", "arbitrary"), est),
        name=name,
    )(*a_list, *([w] * n_in), res)


def _swiglu_gu_kernel(h_ref, wg_ref, wu_ref, wd_ref, o_ref, wdb_ref):
    h = h_ref[...]
    gate = jnp.dot(h, wg_ref[...], preferred_element_type=F32)
    up = jnp.dot(h, wu_ref[...], preferred_element_type=F32)
    o_ref[...] = (gate * (1.0 / (1.0 + jnp.exp(-gate))) * up).astype(o_ref.dtype)
    wdb_ref[...] = wd_ref[...].astype(BF16)


def _swiglu_gu(h, wg, wu, wd, layer, *, tm, tn, name):
    m, k = h.shape
    n = wg.shape[2]
    steps = (m // tm) * (n // tn)
    nj = n // tn
    wd_rows, wd_cols = wd.shape[1], wd.shape[2]
    slab = wd_rows // steps
    assert slab * steps == wd_rows and slab % 16 == 0
    est = (2 * tm * k * 2 + 4 * k * tn * 2 + 2 * tm * tn * 2 + 3 * tm * tn * 4
           + 2 * slab * wd_cols * 6 + (4 << 20))
    return pl.pallas_call(
        _swiglu_gu_kernel,
        out_shape=(jax.ShapeDtypeStruct((m, n), BF16),
                   jax.ShapeDtypeStruct((1, wd_rows, wd_cols), BF16)),
        grid=(m // tm, nj),
        in_specs=[
            pl.BlockSpec((tm, k), lambda i, j: (i, 0)),
            pl.BlockSpec((None, k, tn), lambda i, j: (0, 0, j)),
            pl.BlockSpec((None, k, tn), lambda i, j: (0, 0, j)),
            pl.BlockSpec((None, slab, wd_cols), lambda i, j: (layer, i * nj + j, 0)),
        ],
        out_specs=(pl.BlockSpec((tm, tn), lambda i, j: (i, j)),
                   pl.BlockSpec((None, slab, wd_cols), lambda i, j: (0, i * nj + j, 0))),
        compiler_params=_cparams(("arbitrary", "arbitrary"), est),
        name=name,
    )(h, wg, wu, wd)


def _rope_table_kernel(pos_ref, inv_ref, sign_ref, cs_ref):
    ang = pos_ref[...].astype(F32) * inv_ref[...]
    lane = lax.broadcasted_iota(jnp.int32, ang.shape, 1)
    cs_ref[...] = jnp.where(lane < B_ROPE, jnp.cos(ang), jnp.sin(ang) * sign_ref[...])


def _rope_table(pos_col, inv_row, sign_row, *, tm):
    m = pos_col.shape[0]
    return pl.pallas_call(
        _rope_table_kernel,
        out_shape=jax.ShapeDtypeStruct((m, V7X_LANES), F32),
        grid=(m // tm,),
        in_specs=[
            pl.BlockSpec((tm, 1), lambda i: (i, 0)),
            pl.BlockSpec((1, V7X_LANES), lambda i: (0, 0)),
            pl.BlockSpec((1, V7X_LANES), lambda i: (0, 0)),
        ],
        out_specs=pl.BlockSpec((tm, V7X_LANES), lambda i: (i, 0)),
        compiler_params=_cparams(("arbitrary",), 32 << 20),
        name="rope_table",
    )(pos_col, inv_row, sign_row)


def _rope_pair_sum(slot, cs):
    t = slot * cs
    return t + pltpu.roll(t, B_ROPE, axis=1)


def _mla_prep_kernel(ql_ref, rope_ref, kvl_ref, cs_ref, gq_ref, gkv_ref, wq_ref, wkv_ref, wo_ref,
                     qt_ref, kn_ref, vt_ref, kpe_ref, wob_ref, *, q_scale):
    cs = cs_ref[...]
    hq = _rms_rows(ql_ref[...].astype(F32), gq_ref[...]).astype(BF16)
    qf = jnp.dot(hq, wq_ref[...], preferred_element_type=F32)
    for h in range(B_HEADS):
        base = h * B_QK
        nope = qf[:, base:base + B_NOPE]
        pe = _rope_pair_sum(qf[:, base + B_NOPE:base + B_QK], cs)
        qt_ref[base:base + B_NOPE, :] = (nope * q_scale).T.astype(BF16)
        qt_ref[base + B_NOPE:base + B_QK, :] = (pe * q_scale).T.astype(BF16)

    hkv = _rms_rows(kvl_ref[...].astype(F32), gkv_ref[...]).astype(BF16)
    kvf = jnp.dot(hkv, wkv_ref[...], preferred_element_type=F32)
    kn_ref[...] = kvf[:, :B_HEADS * B_NOPE].astype(BF16)
    vt_ref[...] = kvf[:, B_HEADS * B_NOPE:].T.astype(BF16)

    kpe = _rope_pair_sum(rope_ref[...].astype(F32), cs)
    lane = lax.broadcasted_iota(jnp.int32, kpe.shape, 1)
    kpe_ref[...] = jnp.where(lane < B_ROPE, kpe, 0.0).astype(BF16)
    wob_ref[...] = wo_ref[...].astype(BF16)


def _mla_prep(proj, cs, gq, gkv, wq, wkv, wo, layer, *, seq, tm, name):
    m = proj.shape[0]
    tiles_per_seq = seq // tm
    q_scale = (B_NOPE + B_ROPE) ** -0.5 * LOG2E
    wo_rows, wo_cols = wo.shape[1], wo.shape[2]
    slab = wo_rows // (m // tm)
    assert slab * (m // tm) == wo_rows and slab % 16 == 0
    est = (2 * tm * (Q_LORA + KV_LORA + 128) * 2 + 2 * tm * 128 * 4
           + 2 * (Q_LORA + KV_LORA) * 2048 * 2 + 2 * slab * wo_cols * 6
           + 2 * tm * (2048 + 1024 + 1024 + 128) * 2 + 3 * tm * 2048 * 4 + (4 << 20))
    return pl.pallas_call(
        functools.partial(_mla_prep_kernel, q_scale=q_scale),
        out_shape=(
            jax.ShapeDtypeStruct((m // seq * B_HEADS * B_QK, seq), BF16),
            jax.ShapeDtypeStruct((m, B_HEADS * B_NOPE), BF16),
            jax.ShapeDtypeStruct((m // seq * B_WIDTH, seq), BF16),
            jax.ShapeDtypeStruct((m, V7X_LANES), BF16),
            jax.ShapeDtypeStruct((1, wo_rows, wo_cols), BF16),
        ),
        grid=(m // tm,),
        in_specs=[
            pl.BlockSpec((tm, Q_LORA), lambda i: (i, IN_QLAT_OFF // Q_LORA)),
            pl.BlockSpec((tm, V7X_LANES), lambda i: (i, IN_ROPE_OFF // V7X_LANES)),
            pl.BlockSpec((tm, KV_LORA), lambda i: (i, IN_KVLAT_OFF // KV_LORA)),
            pl.BlockSpec((tm, V7X_LANES), lambda i: (i, 0)),
            pl.BlockSpec((None, 1, Q_LORA), lambda i: (layer, 0, 0)),
            pl.BlockSpec((None, 1, KV_LORA), lambda i: (layer, 0, 0)),
            pl.BlockSpec((None, Q_LORA, B_HEADS * B_QK), lambda i: (layer, 0, 0)),
            pl.BlockSpec((None, KV_LORA, 2 * B_WIDTH), lambda i: (layer, 0, 0)),
            pl.BlockSpec((None, slab, wo_cols), lambda i: (layer, i, 0)),
        ],
        out_specs=(
            pl.BlockSpec((B_HEADS * B_QK, tm), lambda i: (i // tiles_per_seq, i % tiles_per_seq)),
            pl.BlockSpec((tm, B_HEADS * B_NOPE), lambda i: (i, 0)),
            pl.BlockSpec((B_WIDTH, tm), lambda i: (i // tiles_per_seq, i % tiles_per_seq)),
            pl.BlockSpec((tm, V7X_LANES), lambda i: (i, 0)),
            pl.BlockSpec((None, slab, wo_cols), lambda i: (0, i, 0)),
        ),
        compiler_params=_cparams(("arbitrary",), est),
        name=name,
    )(proj, proj, proj, cs, gq, gkv, wq, wkv, wo)


class _MixerAHeads:
    def __init__(self, q_ref, k_refs, v_refs, bias_ref, o_ref):
        self.q_ref, self.k_refs, self.v_refs = q_ref, k_refs, v_refs
        self.bias_ref, self.o_ref = bias_ref, o_ref
        self.tiles = [None] * A_HEADS
        self.end = 0

    def _scores(self, h):
        hs = slice(h * A_HEAD_DIM, (h + 1) * A_HEAD_DIM)
        k = jnp.concatenate([r[:, hs] for r in self.k_refs], axis=0)
        return jnp.dot(k, self.q_ref[:, hs].T, preferred_element_type=F32)

    def begin(self, first, end):
        self.end = end
        for h in range(first, min(first + A_LOOKAHEAD, end)):
            self.tiles[h] = self._scores(h)

    def head(self, h):
        d, tq = A_HEAD_DIM, A_TQ
        hs = slice(h * d, (h + 1) * d)
        if h + A_LOOKAHEAD < self.end:
            self.tiles[h + A_LOOKAHEAD] = self._scores(h + A_LOOKAHEAD)
        v = jnp.concatenate([r[:, hs] for r in self.v_refs], axis=0)
        st = self.bias_ref[h // 2][:, (h % 2) * tq:(h % 2 + 1) * tq] + self.tiles[h]
        self.tiles[h] = None
        m = _col_reduce(st, jnp.max)
        p = jnp.exp2(st - m)
        l = _col_reduce(p, jnp.sum)
        ot = lax.dot_general(v, p.astype(BF16), (((0,), (0,)), ((), ())),
                             preferred_element_type=F32)
        self.o_ref[:, hs] = (ot / l).T.astype(self.o_ref.dtype)


def _col_reduce(x, op):
    rows, cols = x.shape
    slabs = op(x.reshape(B_RED_SLABS, rows // B_RED_SLABS, cols), axis=0)
    return op(slabs, axis=0, keepdims=True)


def _mixers_kernel(q_ref, kn_ref, kpe_ref, vt_ref, wg_ref, wu_ref,
                   qa_ref, ka0_ref, ka1_ref, ka2_ref, va0_ref, va1_ref, va2_ref, bias_ref,
                   o_ref, wgb_ref, wub_ref, oa_ref,
                   m_ref, l_ref, acc_ref, st_ref):
    i = pl.program_id(2)
    hp = B_HEADS_PER_STEP
    m_ref[...] = jnp.full(m_ref.shape, -jnp.inf, F32)
    l_ref[...] = jnp.zeros(l_ref.shape, F32)
    acc_ref[...] = jnp.zeros(acc_ref.shape, F32)

    def scores(j, h):
        r0 = pl.multiple_of(j * B_TK, B_TK)
        k = jnp.concatenate([kn_ref[pl.ds(r0, B_TK), h * B_NOPE:(h + 1) * B_NOPE],
                             kpe_ref[pl.ds(r0, B_TK), :]], axis=1)
        return jnp.dot(k, q_ref[h * B_QK:(h + 1) * B_QK, :], preferred_element_type=F32)

    def softmax_pv(j, h, st, allowed):
        r0 = pl.multiple_of(j * B_TK, B_TK)
        if allowed is not None:
            st = jnp.where(allowed, st, -jnp.inf)
        m_prev = m_ref[h]
        m_new = jnp.maximum(m_prev, _col_reduce(st, jnp.max))
        alpha = jnp.exp2(m_prev - m_new)
        p = jnp.exp2(st - m_new)
        l_ref[h] = alpha * l_ref[h] + _col_reduce(p, jnp.sum)
        pv = jnp.dot(vt_ref[h * B_V:(h + 1) * B_V, pl.ds(r0, B_TK)], p.astype(BF16),
                     preferred_element_type=F32)
        acc_ref[h] = alpha * acc_ref[h] + pv
        m_ref[h] = m_new

    def key_block(j, masked, after_head=None):
        allowed = None
        if masked:
            krow = lax.broadcasted_iota(jnp.int32, (B_TK, B_TQ), 0)
            qcol = lax.broadcasted_iota(jnp.int32, (B_TK, B_TQ), 1)
            allowed = (krow // CHUNK) <= (qcol // CHUNK)
        look = B_LOOKAHEAD
        tiles = [st_ref[h] for h in range(look)] + [None] * (hp - look)
        for h in range(hp):
            ahead = h + look
            if ahead < hp:
                tiles[ahead] = scores(j, ahead)
            elif not masked:
                st_ref[ahead - hp] = scores(j + 1, ahead - hp)
            softmax_pv(j, h, tiles[h], allowed)
            tiles[h] = None
            if after_head is not None:
                after_head(h)

    def body(j, carry):
        key_block(j, False)
        return carry

    mixer_a = _MixerAHeads(qa_ref, (ka0_ref, ka1_ref, ka2_ref), (va0_ref, va1_ref, va2_ref),
                           bias_ref, oa_ref)

    for h in range(B_LOOKAHEAD):
        st_ref[h] = scores(0, h)
    lax.fori_loop(0, i, body, 0)

    mixer_a.begin(0, A_HEADS)

    def mixer_a_heads(h):
        for ha in range(h * A_HEADS // hp, (h + 1) * A_HEADS // hp):
            mixer_a.head(ha)

    key_block(i, True, after_head=mixer_a_heads)

    for h in range(hp):
        o_ref[:, h * B_V:(h + 1) * B_V] = (acc_ref[h] / l_ref[h]).T.astype(o_ref.dtype)

    wgb_ref[...] = wg_ref[...].astype(BF16)
    wub_ref[...] = wu_ref[...].astype(BF16)


def _mixers(proj, bias, qt, kn, kpe, vt, wg, wu, layer, *, batch, seq, name):
    m = kn.shape[0]
    nq = seq // B_TQ
    hp = B_HEADS_PER_STEP
    ng = B_HEADS // hp
    steps = batch * ng * nq
    nqa = seq // A_TQ
    assert steps == batch * nqa
    w_rows, w_cols = wg.shape[1], wg.shape[2]
    slab = w_rows // steps
    assert slab * steps == w_rows and slab % 16 == 0 and wu.shape == wg.shape
    est = (2 * B_TQ * hp * B_QK * 2 + 2 * seq * hp * (B_NOPE + B_V) * 2 + 2 * seq * 128 * 2
           + 2 * B_TQ * hp * B_V * 2 + hp * (B_V + 16) * B_TQ * 4 + 8 * B_TQ * B_TK * 4
           + 4 * slab * w_cols * 6
           + 2 * 8 * A_TQ * A_WIDTH * 2 + 2 * A_HEADS * A_TQ * A_WIN * 4 + 6 * A_WIN * A_TQ * 4
           + (4 << 20))

    def slab_in(b, g, i):
        return (layer, (b * ng + g) * nq + i, 0)

    def slab_out(b, g, i):
        return (0, (b * ng + g) * nq + i, 0)

    def a_blk(b, g, i):
        return g * nq + i

    def a_kv_spec(col_blk, c):
        back = A_KBLKS - 1 - c
        return pl.BlockSpec((A_TQ, A_WIDTH),
                            lambda b, g, i: (b * nqa + jnp.maximum(a_blk(b, g, i) - back, 0), col_blk))

    return pl.pallas_call(
        _mixers_kernel,
        out_shape=(jax.ShapeDtypeStruct((m, B_WIDTH), BF16),
                   jax.ShapeDtypeStruct((1, w_rows, w_cols), BF16),
                   jax.ShapeDtypeStruct((1, w_rows, w_cols), BF16),
                   jax.ShapeDtypeStruct((m, A_WIDTH), BF16)),
        grid=(batch, ng, nq),
        in_specs=[
            pl.BlockSpec((hp * B_QK, B_TQ), lambda b, g, i: (b * ng + g, i)),
            pl.BlockSpec((seq, hp * B_NOPE), lambda b, g, i: (b, g)),
            pl.BlockSpec((seq, V7X_LANES), lambda b, g, i: (b, 0)),
            pl.BlockSpec((hp * B_V, seq), lambda b, g, i: (b * ng + g, 0)),
            pl.BlockSpec((None, slab, w_cols), slab_in),
            pl.BlockSpec((None, slab, w_cols), slab_in),
            pl.BlockSpec((A_TQ, A_WIDTH), lambda b, g, i: (b * nqa + a_blk(b, g, i), 0)),
            a_kv_spec(1, 0), a_kv_spec(1, 1), a_kv_spec(1, 2),
            a_kv_spec(2, 0), a_kv_spec(2, 1), a_kv_spec(2, 2),
            pl.BlockSpec((A_HEADS // 2, None, A_WIN, 2 * A_TQ),
                         lambda b, g, i: (layer, jnp.minimum(a_blk(b, g, i), A_KBLKS - 1), 0, 0)),
        ],
        out_specs=(pl.BlockSpec((B_TQ, hp * B_V), lambda b, g, i: (b * nq + i, g)),
                   pl.BlockSpec((None, slab, w_cols), slab_out),
                   pl.BlockSpec((None, slab, w_cols), slab_out),
                   pl.BlockSpec((A_TQ, A_WIDTH), lambda b, g, i: (b * nqa + a_blk(b, g, i), 0))),
        scratch_shapes=[
            pltpu.VMEM((hp, 1, B_TQ), F32),
            pltpu.VMEM((hp, 1, B_TQ), F32),
            pltpu.VMEM((hp, B_V, B_TQ), F32),
            pltpu.VMEM((B_LOOKAHEAD, B_TK, B_TQ), F32),
        ],
        compiler_params=_cparams(("arbitrary", "arbitrary", "arbitrary"), est),
        name=name,
    )(qt, kn, kpe, vt, wg, wu, proj, proj, proj, proj, proj, proj, proj, bias)


def _cross_kernel(x_ref, oa_ref, ob_ref, woa_ref, wob_ref, g_ref, wq_ref, k_ref, v_ref, wo_ref,
                  gf_ref, o_ref, hf_ref):
    scale = X_HEAD_DIM ** -0.5
    rows = x_ref.shape[0] // X_ROW_PARTS
    for r in range(X_ROW_PARTS):
        rs = slice(r * rows, (r + 1) * rows)
        x = (x_ref[rs, :] + jnp.dot(oa_ref[rs, :], woa_ref[...], preferred_element_type=F32)
             + jnp.dot(ob_ref[rs, :], wob_ref[...], preferred_element_type=F32))
        hn = _rms_rows(x, g_ref[...]).astype(BF16)
        q = (jnp.dot(hn, wq_ref[...], preferred_element_type=F32) * scale).astype(BF16)
        outs = []
        for h in range(X_HEADS):
            hs = slice(h * X_HEAD_DIM, (h + 1) * X_HEAD_DIM)
            s = lax.dot_general(q[:, hs], k_ref[:, hs], (((1,), (1,)), ((), ())),
                                preferred_element_type=F32)
            m = jnp.max(s, axis=1, keepdims=True)
            p = jnp.exp(s - m)
            l = jnp.sum(p, axis=1, keepdims=True)
            o = jnp.dot(p.astype(BF16), v_ref[:, hs], preferred_element_type=F32)
            outs.append((o / l).astype(BF16))
        o_all = jnp.concatenate(outs, axis=1)
        x2 = x + jnp.dot(o_all, wo_ref[...], preferred_element_type=F32)
        o_ref[rs, :] = x2
        hf_ref[rs, :] = _rms_rows(x2, gf_ref[...]).astype(BF16)


def _cross(x, oa, ob, w_out_l, g, wq, kvm, wo, g_ffn, layer, *, seq, tm, name):
    m, d = x.shape
    blocks_per_batch = seq // tm
    est = (4 * tm * d * 4 + tm * d * 2 + 2 * d * X_WIDTH * 2 + 2 * X_WIDTH * d * 2
           + 4 * N_MEM * X_WIDTH * 2 + 4 * tm * N_MEM * 4 + 3 * tm * d * 4
           + 2 * tm * (A_WIDTH + B_WIDTH) * 2 + 2 * (A_WIDTH + B_WIDTH) * d * 2
           + 2 * tm * d * 2 + (4 << 20))
    return pl.pallas_call(
        _cross_kernel,
        out_shape=(jax.ShapeDtypeStruct((m, d), F32), jax.ShapeDtypeStruct((m, d), BF16)),
        grid=(m // tm,),
        in_specs=[
            pl.BlockSpec((tm, d), lambda i: (i, 0)),
            pl.BlockSpec((tm, A_WIDTH), lambda i: (i, 0)),
            pl.BlockSpec((tm, B_WIDTH), lambda i: (i, 0)),
            pl.BlockSpec((None, A_WIDTH, d), lambda i: (0, 0, 0)),
            pl.BlockSpec((None, B_WIDTH, d), lambda i: (0, A_WIDTH // B_WIDTH, 0)),
            pl.BlockSpec((None, 1, d), lambda i: (layer, 0, 0)),
            pl.BlockSpec((None, d, X_WIDTH), lambda i: (layer, 0, 0)),
            pl.BlockSpec((N_MEM, X_WIDTH), lambda i: (i // blocks_per_batch, 2 * layer)),
            pl.BlockSpec((N_MEM, X_WIDTH), lambda i: (i // blocks_per_batch, 2 * layer + 1)),
            pl.BlockSpec((None, X_WIDTH, d), lambda i: (layer, 0, 0)),
            pl.BlockSpec((None, 1, d), lambda i: (layer, 0, 0)),
        ],
        out_specs=(pl.BlockSpec((tm, d), lambda i: (i, 0)),
                   pl.BlockSpec((tm, d), lambda i: (i, 0))),
        compiler_params=_cparams(("arbitrary",), est),
        name=name,
    )(x, oa, ob, w_out_l, w_out_l, g, wq, kvm, kvm, wo, g_ffn)


def _final_norm_kernel(x_ref, g_ref, o_ref):
    o_ref[...] = _rms_rows(x_ref[...], g_ref[...])


def _final_norm(x, g, *, tm):
    m, d = x.shape
    return pl.pallas_call(
        _final_norm_kernel,
        out_shape=jax.ShapeDtypeStruct((m, d), F32),
        grid=(m // tm,),
        in_specs=[pl.BlockSpec((tm, d), lambda i: (i, 0)), pl.BlockSpec((1, d), lambda i: (0, 0))],
        out_specs=pl.BlockSpec((tm, d), lambda i: (i, 0)),
        compiler_params=_cparams(("arbitrary",), 6 * tm * d * 4 + (4 << 20)),
        name="final_norm",
    )(x, g)


RELAYOUT_COLS = 256
RELAYOUT_ROPE_BLK = IN_ROPE_OFF // RELAYOUT_COLS


def _w_in_relayout_kernel(wt_ref, o_ref):
    j = pl.program_id(1)

    @pl.when(j != RELAYOUT_ROPE_BLK)
    def _():
        factor = jnp.where(j < A_WIDTH // RELAYOUT_COLS, A_SCORE_SCALE, 1.0)
        o_ref[...] = (wt_ref[...] * factor).T.astype(BF16)

    @pl.when(j == RELAYOUT_ROPE_BLK)
    def _():
        half = B_ROPE // 2
        kr = wt_ref[:B_ROPE, :]
        slot = jnp.concatenate([kr, kr[half:], kr[:half],
                                jnp.zeros((RELAYOUT_COLS - 2 * B_ROPE, kr.shape[1]), F32)], axis=0)
        o_ref[...] = slot.T.astype(BF16)


def _w_in_relayout(w_in_t):
    layers, n, k = w_in_t.shape
    lat_blk = IN_ROPE_OFF // RELAYOUT_COLS
    rope_src = (IN_ROPE_OFF + KV_LORA) // RELAYOUT_COLS
    assert IN_KVLAT_OFF == (RELAYOUT_ROPE_BLK + 1) * RELAYOUT_COLS

    def src_block(l, j):
        return (l, jnp.where(j < RELAYOUT_ROPE_BLK, j,
                             jnp.where(j == RELAYOUT_ROPE_BLK, rope_src, j - 1 + lat_blk - RELAYOUT_ROPE_BLK)), 0)

    return pl.pallas_call(
        _w_in_relayout_kernel,
        out_shape=jax.ShapeDtypeStruct((layers, k, IN_PAD_WIDTH), BF16),
        grid=(layers, IN_PAD_WIDTH // RELAYOUT_COLS),
        in_specs=[pl.BlockSpec((None, RELAYOUT_COLS, k), src_block)],
        out_specs=pl.BlockSpec((None, k, RELAYOUT_COLS), lambda l, j: (l, 0, j)),
        compiler_params=_cparams(("arbitrary", "arbitrary"), 32 << 20),
        name="w_in_relayout",
    )(w_in_t)


def _prep_weights(w_in, w_uq, w_ukv):
    swap = (jnp.arange(B_ROPE) + B_ROPE // 2) % B_ROPE
    w_in_p = _w_in_relayout(jnp.swapaxes(w_in, 1, 2))

    uq = w_uq.reshape(DEPTH, Q_LORA, B_HEADS, B_NOPE + B_ROPE)
    uq_pe = uq[..., B_NOPE:]
    wq_p = jnp.concatenate([uq[..., :B_NOPE], uq_pe, uq_pe[..., swap]], axis=-1)
    wq_p = wq_p.reshape(DEPTH, Q_LORA, B_HEADS * B_QK).astype(BF16)

    ukv = w_ukv.reshape(DEPTH, KV_LORA, B_HEADS, B_NOPE + B_V)
    wkv_p = jnp.concatenate([ukv[..., :B_NOPE].reshape(DEPTH, KV_LORA, B_HEADS * B_NOPE),
                             ukv[..., B_NOPE:].reshape(DEPTH, KV_LORA, B_WIDTH)],
                            axis=-1).astype(BF16)

    return w_in_p, wq_p, wkv_p


def _bias_table_kernel(w_ref, o_ref):
    v = pl.program_id(1)
    r = lax.broadcasted_iota(jnp.int32, (A_TQ, A_WIN), 0)
    t = lax.broadcasted_iota(jnp.int32, (A_TQ, A_WIN), 1)
    dchunk = t // CHUNK - r // CHUNK
    valid = (dchunk >= 0) & (dchunk <= LEFT_CHUNKS) & (v - (A_KBLKS - 1) + t // A_TQ >= 0)
    for hh in range(2):
        w = jnp.broadcast_to(w_ref[hh], (A_TQ, A_BIAS_VEC))
        tab = pltpu.roll(w, 0, axis=1, stride=1, stride_axis=0)[:, :A_WIN]
        o_ref[0, 0, :, hh * A_TQ:(hh + 1) * A_TQ] = jnp.where(valid, tab * LOG2E, -jnp.inf).T


def _relpos_bias_tables(rel_bias):
    u = jnp.arange(A_BIAS_VEC)
    delta = jnp.where(u <= A_WIN, u, u - A_BIAS_VEC)
    idx = jnp.clip(LEFT_CHUNKS * CHUNK - delta, -REL_CLIP, REL_CLIP) + REL_CLIP
    vec = rel_bias[:, :, idx].reshape(DEPTH * A_HEADS, 1, A_BIAS_VEC).astype(F32)
    return pl.pallas_call(
        _bias_table_kernel,
        out_shape=jax.ShapeDtypeStruct((DEPTH * A_HEADS // 2, A_KBLKS, A_WIN, 2 * A_TQ), F32),
        grid=(DEPTH * A_HEADS // 2, A_KBLKS),
        in_specs=[pl.BlockSpec((2, 1, A_BIAS_VEC), lambda i, v: (i, 0, 0))],
        out_specs=pl.BlockSpec((1, 1, A_WIN, 2 * A_TQ), lambda i, v: (i, v, 0, 0)),
        compiler_params=_cparams(("arbitrary", "arbitrary"), 32 << 20),
        name="bias_table",
    )(vec)


def kernel(x, mem, positions, norm_mix, w_in, rel_bias, q_norm, kv_norm, w_uq, w_ukv, w_out,
           norm_mem, mem_norm, w_xq, w_xkv, w_xo, norm_ffn, w_gate, w_up, w_down, norm_final):
    b, s, d = x.shape
    m = b * s
    assert (b, s, d) == (4, 4096, D_MODEL) and mem.shape == (b, N_MEM, d)

    w_in_p, wq_p, wkv_p = _prep_weights(w_in, w_uq, w_ukv)
    w_xq_b = w_xq.astype(BF16)
    w_xo_b = w_xo.astype(BF16)
    bias_tab = _relpos_bias_tables(rel_bias)

    half = B_ROPE // 2
    inv = ROPE_THETA ** (-jnp.arange(half, dtype=F32) / half)
    inv_row = jnp.tile(inv, 4)[None, :]
    sign_row = jnp.concatenate([jnp.ones((B_ROPE,), F32), -jnp.ones((half,), F32),
                                jnp.ones((half,), F32)])[None, :]
    cs = _rope_table(positions.reshape(m, 1), inv_row, sign_row, tm=2048)

    kvm = _mem_kv(mem.reshape(b * N_MEM, d), mem_norm[None, :], w_xkv)

    norm_mix, norm_mem, norm_ffn, q_norm, kv_norm = (
        g[:, None, :] for g in (norm_mix, norm_mem, norm_ffn, q_norm, kv_norm))
    xf = x.reshape(m, d)
    for l in range(DEPTH):
        proj = _norm_matmul(xf, norm_mix, w_in_p, l, tm=1024, tn=1536, name=f"in_proj_{l}")
        qt, kn, vt, kpe, w_out_l = _mla_prep(proj, cs, q_norm, kv_norm, wq_p, wkv_p, w_out, l,
                                             seq=s, tm=512, name=f"mla_prep_{l}")
        ob, w_gate_l, w_up_l, oa = _mixers(proj, bias_tab, qt, kn, kpe, vt, w_gate, w_up, l,
                                           batch=b, seq=s, name=f"mixers_{l}")
        xf, h_ffn = _cross(xf, oa, ob, w_out_l, norm_mem, w_xq_b, kvm, w_xo_b, norm_ffn, l,
                           seq=s, tm=512, name=f"out_cross_{l}")
        act, w_down_l = _swiglu_gu(h_ffn, w_gate_l, w_up_l, w_down, l, tm=1024, tn=512,
                                   name=f"swiglu_gu_{l}")
        xf = _mm_res([act], w_down_l, xf, 0, tm=1024, tn=512, name=f"swiglu_down_{l}")
    out = _final_norm(xf, norm_final[None, :], tm=512)
    return out.reshape(b, s, d)
```

```python
import functools
import math

import jax
import jax.numpy as jnp
from jax import lax
from jax.experimental import pallas as pl
from jax.experimental.pallas import tpu as pltpu

D_MODEL = 2048
DEPTH = 4
CHUNK = 64
LEFT_CHUNKS = 8
A_HEAD_DIM = 128
A_HEADS = 8
A_WIDTH = A_HEADS * A_HEAD_DIM
REL_CLIP = 128
B_NOPE = 128
B_ROPE = 64
B_V = 128
B_HEADS = 8
B_WIDTH = B_HEADS * B_V
Q_LORA = 768
KV_LORA = 512
ROPE_THETA = 10000.0
N_MEM = 256
X_HEADS = 4
X_HEAD_DIM = 128
X_WIDTH = X_HEADS * X_HEAD_DIM
X_ROW_PARTS = 2
D_FF = 5632
EPS = 1e-6

V7X_LANES = 128
V7X_VMEM_BYTES = 64 * 1024 * 1024
V7X_VMEM_REQUEST_CAP = 56 * 1024 * 1024

BF16 = jnp.bfloat16
F32 = jnp.float32

IN_QLAT_OFF = 3 * A_WIDTH
IN_ROPE_OFF = IN_QLAT_OFF + Q_LORA
IN_KVLAT_OFF = IN_ROPE_OFF + 2 * V7X_LANES
IN_PAD_WIDTH = IN_KVLAT_OFF + KV_LORA

A_TQ = 4 * CHUNK
A_WIN = A_TQ + LEFT_CHUNKS * CHUNK
A_KBLKS = A_WIN // A_TQ
A_BIAS_VEC = 1024
A_LOOKAHEAD = 2

B_TQ = 512
B_TK = 512
B_HEADS_PER_STEP = 4
B_LOOKAHEAD = 2
B_RED_SLABS = 8
B_QK = B_NOPE + 2 * B_ROPE

LOG2E = math.log2(math.e)
A_SCORE_SCALE = A_HEAD_DIM ** -0.5 * LOG2E


def _cparams(sem, vmem_bytes):
    limit = int(min(max(vmem_bytes, 32 * 1024 * 1024), V7X_VMEM_REQUEST_CAP))
    return pltpu.CompilerParams(dimension_semantics=sem, vmem_limit_bytes=limit)


def _rms_rows(x, g):
    ms = jnp.mean(x * x, axis=-1, keepdims=True)
    return (x * lax.rsqrt(ms + EPS)) * g


def _norm_into(h_ref, x_ref, g_ref, rows=128):
    n = x_ref.shape[0] // rows

    def body(c, carry):
        r0 = pl.multiple_of(c * rows, rows)
        x = x_ref[pl.ds(r0, rows), :].astype(F32)
        h_ref[pl.ds(r0, rows), :] = _rms_rows(x, g_ref[...]).astype(h_ref.dtype)
        return carry

    lax.fori_loop(0, n, body, 0)


def _norm_matmul_kernel(x_ref, g_ref, w_ref, o_ref, h_ref):
    @pl.when(pl.program_id(1) == 0)
    def _():
        _norm_into(h_ref, x_ref, g_ref)

    o_ref[...] = jnp.dot(h_ref[...], w_ref[...], preferred_element_type=F32).astype(o_ref.dtype)


def _norm_matmul(x, g, w, layer, *, tm, tn, name):
    m, k = x.shape
    n = w.shape[2]
    xb = x.dtype.itemsize
    est = 2 * tm * k * xb + tm * k * 2 + 2 * k * tn * 2 + 2 * tm * tn * 2 + tm * tn * 4 + (4 << 20)
    return pl.pallas_call(
        _norm_matmul_kernel,
        out_shape=jax.ShapeDtypeStruct((m, n), BF16),
        grid=(m // tm, n // tn),
        in_specs=[
            pl.BlockSpec((tm, k), lambda i, j: (i, 0)),
            pl.BlockSpec((None, 1, k), lambda i, j: (layer, 0, 0)),
            pl.BlockSpec((None, k, tn), lambda i, j: (layer, 0, j)),
        ],
        out_specs=pl.BlockSpec((tm, tn), lambda i, j: (i, j)),
        scratch_shapes=[pltpu.VMEM((tm, k), BF16)],
        compiler_params=_cparams(("arbitrary", "arbitrary"), est),
        name=name,
    )(x, g, w)


def _mem_kv_kernel(x_ref, g_ref, w_ref, o_ref, h_ref):
    @pl.when(pl.program_id(0) == 0)
    def _():
        _norm_into(h_ref, x_ref, g_ref)

    o_ref[...] = jnp.dot(h_ref[...], w_ref[...].astype(BF16),
                         preferred_element_type=F32).astype(o_ref.dtype)


def _mem_kv(mem2d, g, w_xkv):
    m, k = mem2d.shape
    layers, _, n = w_xkv.shape
    est = 2 * m * k * 4 + m * k * 2 + 2 * k * n * 4 + k * n * 2 + 2 * m * n * 2 + m * n * 4 + (4 << 20)
    return pl.pallas_call(
        _mem_kv_kernel,
        out_shape=jax.ShapeDtypeStruct((m, layers * n), BF16),
        grid=(layers,),
        in_specs=[
            pl.BlockSpec((m, k), lambda l: (0, 0)),
            pl.BlockSpec((1, k), lambda l: (0, 0)),
            pl.BlockSpec((None, k, n), lambda l: (l, 0, 0)),
        ],
        out_specs=pl.BlockSpec((m, n), lambda l: (0, l)),
        scratch_shapes=[pltpu.VMEM((m, k), BF16)],
        compiler_params=_cparams(("arbitrary",), est),
        name="mem_kv",
    )(mem2d, g, w_xkv)


def _mm_res_kernel(*refs, n_in):
    a_refs = refs[:n_in]
    w_refs = refs[n_in:2 * n_in]
    r_ref = refs[2 * n_in]
    o_ref = refs[2 * n_in + 1]
    acc = r_ref[...]
    for a_ref, w_ref in zip(a_refs, w_refs):
        acc = acc + jnp.dot(a_ref[...], w_ref[...], preferred_element_type=F32)
    o_ref[...] = acc


def _mm_res(a_list, w, res, layer, *, tm, tn, name):
    m, n = res.shape
    n_in = len(a_list)
    in_specs = []
    w_specs = []
    est = (4 + 2 * n_in) * tm * tn * 4 + (4 << 20)
    for idx, a in enumerate(a_list):
        k = a.shape[1]
        assert w.shape[1] == n_in * k
        in_specs.append(pl.BlockSpec((tm, k), lambda i, j: (i, 0)))
        w_specs.append(pl.BlockSpec((None, k, tn), lambda i, j, idx=idx: (layer, idx, j)))
        est += 2 * tm * k * 2 + 2 * k * tn * 2
    in_specs += w_specs
    in_specs.append(pl.BlockSpec((tm, tn), lambda i, j: (i, j)))
    return pl.pallas_call(
        functools.partial(_mm_res_kernel, n_in=n_in),
        out_shape=jax.ShapeDtypeStruct((m, n), F32),
        grid=(m // tm, n // tn),
        in_specs=in_specs,
        out_specs=pl.BlockSpec((tm, tn), lambda i, j: (i, j)),
        compiler_params=_cparams(("arbitrary", "arbitrary"), est),
        name=name,
    )(*a_list, *([w] * n_in), res)


def _swiglu_gu_kernel(h_ref, wg_ref, wu_ref, wd_ref, o_ref, wdb_ref):
    h = h_ref[...]
    gate = jnp.dot(h, wg_ref[...], preferred_element_type=F32)
    up = jnp.dot(h, wu_ref[...], preferred_element_type=F32)
    o_ref[...] = (gate * (1.0 / (1.0 + jnp.exp(-gate))) * up).astype(o_ref.dtype)
    wdb_ref[...] = wd_ref[...].astype(BF16)


def _swiglu_gu(h, wg, wu, wd, layer, *, tm, tn, name):
    m, k = h.shape
    n = wg.shape[2]
    steps = (m // tm) * (n // tn)
    nj = n // tn
    wd_rows, wd_cols = wd.shape[1], wd.shape[2]
    slab = wd_rows // steps
    assert slab * steps == wd_rows and slab % 16 == 0
    est = (2 * tm * k * 2 + 4 * k * tn * 2 + 2 * tm * tn * 2 + 3 * tm * tn * 4
           + 2 * slab * wd_cols * 6 + (4 << 20))
    return pl.pallas_call(
        _swiglu_gu_kernel,
        out_shape=(jax.ShapeDtypeStruct((m, n), BF16),
                   jax.ShapeDtypeStruct((1, wd_rows, wd_cols), BF16)),
        grid=(m // tm, nj),
        in_specs=[
            pl.BlockSpec((tm, k), lambda i, j: (i, 0)),
            pl.BlockSpec((None, k, tn), lambda i, j: (0, 0, j)),
            pl.BlockSpec((None, k, tn), lambda i, j: (0, 0, j)),
            pl.BlockSpec((None, slab, wd_cols), lambda i, j: (layer, i * nj + j, 0)),
        ],
        out_specs=(pl.BlockSpec((tm, tn), lambda i, j: (i, j)),
                   pl.BlockSpec((None, slab, wd_cols), lambda i, j: (0, i * nj + j, 0))),
        compiler_params=_cparams(("arbitrary", "arbitrary"), est),
        name=name,
    )(h, wg, wu, wd)


def _rope_table_kernel(pos_ref, inv_ref, sign_ref, cs_ref):
    ang = pos_ref[...].astype(F32) * inv_ref[...]
    lane = lax.broadcasted_iota(jnp.int32, ang.shape, 1)
    cs_ref[...] = jnp.where(lane < B_ROPE, jnp.cos(ang), jnp.sin(ang) * sign_ref[...])


def _rope_table(pos_col, inv_row, sign_row, *, tm):
    m = pos_col.shape[0]
    return pl.pallas_call(
        _rope_table_kernel,
        out_shape=jax.ShapeDtypeStruct((m, V7X_LANES), F32),
        grid=(m // tm,),
        in_specs=[
            pl.BlockSpec((tm, 1), lambda i: (i, 0)),
            pl.BlockSpec((1, V7X_LANES), lambda i: (0, 0)),
            pl.BlockSpec((1, V7X_LANES), lambda i: (0, 0)),
        ],
        out_specs=pl.BlockSpec((tm, V7X_LANES), lambda i: (i, 0)),
        compiler_params=_cparams(("arbitrary",), 32 << 20),
        name="rope_table",
    )(pos_col, inv_row, sign_row)


def _rope_pair_sum(slot, cs):
    t = slot * cs
    return t + pltpu.roll(t, B_ROPE, axis=1)


def _mla_prep_kernel(ql_ref, rope_ref, kvl_ref, cs_ref, gq_ref, gkv_ref, wq_ref, wkv_ref, wo_ref,
                     qt_ref, kn_ref, vt_ref, kpe_ref, wob_ref, *, q_scale):
    cs = cs_ref[...]
    hq = _rms_rows(ql_ref[...].astype(F32), gq_ref[...]).astype(BF16)
    qf = jnp.dot(hq, wq_ref[...], preferred_element_type=F32)
    for h in range(B_HEADS):
        base = h * B_QK
        nope = qf[:, base:base + B_NOPE]
        pe = _rope_pair_sum(qf[:, base + B_NOPE:base + B_QK], cs)
        qt_ref[base:base + B_NOPE, :] = (nope * q_scale).T.astype(BF16)
        qt_ref[base + B_NOPE:base + B_QK, :] = (pe * q_scale).T.astype(BF16)

    hkv = _rms_rows(kvl_ref[...].astype(F32), gkv_ref[...]).astype(BF16)
    kvf = jnp.dot(hkv, wkv_ref[...], preferred_element_type=F32)
    kn_ref[...] = kvf[:, :B_HEADS * B_NOPE].astype(BF16)
    vt_ref[...] = kvf[:, B_HEADS * B_NOPE:].T.astype(BF16)

    kpe = _rope_pair_sum(rope_ref[...].astype(F32), cs)
    lane = lax.broadcasted_iota(jnp.int32, kpe.shape, 1)
    kpe_ref[...] = jnp.where(lane < B_ROPE, kpe, 0.0).astype(BF16)
    wob_ref[...] = wo_ref[...].astype(BF16)


def _mla_prep(proj, cs, gq, gkv, wq, wkv, wo, layer, *, seq, tm, name):
    m = proj.shape[0]
    tiles_per_seq = seq // tm
    q_scale = (B_NOPE + B_ROPE) ** -0.5 * LOG2E
    wo_rows, wo_cols = wo.shape[1], wo.shape[2]
    slab = wo_rows // (m // tm)
    assert slab * (m // tm) == wo_rows and slab % 16 == 0
    est = (2 * tm * (Q_LORA + KV_LORA + 128) * 2 + 2 * tm * 128 * 4
           + 2 * (Q_LORA + KV_LORA) * 2048 * 2 + 2 * slab * wo_cols * 6
           + 2 * tm * (2048 + 1024 + 1024 + 128) * 2 + 3 * tm * 2048 * 4 + (4 << 20))
    return pl.pallas_call(
        functools.partial(_mla_prep_kernel, q_scale=q_scale),
        out_shape=(
            jax.ShapeDtypeStruct((m // seq * B_HEADS * B_QK, seq), BF16),
            jax.ShapeDtypeStruct((m, B_HEADS * B_NOPE), BF16),
            jax.ShapeDtypeStruct((m // seq * B_WIDTH, seq), BF16),
            jax.ShapeDtypeStruct((m, V7X_LANES), BF16),
            jax.ShapeDtypeStruct((1, wo_rows, wo_cols), BF16),
        ),
        grid=(m // tm,),
        in_specs=[
            pl.BlockSpec((tm, Q_LORA), lambda i: (i, IN_QLAT_OFF // Q_LORA)),
            pl.BlockSpec((tm, V7X_LANES), lambda i: (i, IN_ROPE_OFF // V7X_LANES)),
            pl.BlockSpec((tm, KV_LORA), lambda i: (i, IN_KVLAT_OFF // KV_LORA)),
            pl.BlockSpec((tm, V7X_LANES), lambda i: (i, 0)),
            pl.BlockSpec((None, 1, Q_LORA), lambda i: (layer, 0, 0)),
            pl.BlockSpec((None, 1, KV_LORA), lambda i: (layer, 0, 0)),
            pl.BlockSpec((None, Q_LORA, B_HEADS * B_QK), lambda i: (layer, 0, 0)),
            pl.BlockSpec((None, KV_LORA, 2 * B_WIDTH), lambda i: (layer, 0, 0)),
            pl.BlockSpec((None, slab, wo_cols), lambda i: (layer, i, 0)),
        ],
        out_specs=(
            pl.BlockSpec((B_HEADS * B_QK, tm), lambda i: (i // tiles_per_seq, i % tiles_per_seq)),
            pl.BlockSpec((tm, B_HEADS * B_NOPE), lambda i: (i, 0)),
            pl.BlockSpec((B_WIDTH, tm), lambda i: (i // tiles_per_seq, i % tiles_per_seq)),
            pl.BlockSpec((tm, V7X_LANES), lambda i: (i, 0)),
            pl.BlockSpec((None, slab, wo_cols), lambda i: (0, i, 0)),
        ),
        compiler_params=_cparams(("arbitrary",), est),
        name=name,
    )(proj, proj, proj, cs, gq, gkv, wq, wkv, wo)


class _MixerAHeads:
    def __init__(self, q_ref, k_refs, v_refs, bias_ref, o_ref):
        self.q_ref, self.k_refs, self.v_refs = q_ref, k_refs, v_refs
        self.bias_ref, self.o_ref = bias_ref, o_ref
        self.tiles = [None] * A_HEADS
        self.end = 0

    def _scores(self, h):
        hs = slice(h * A_HEAD_DIM, (h + 1) * A_HEAD_DIM)
        k = jnp.concatenate([r[:, hs] for r in self.k_refs], axis=0)
        return jnp.dot(k, self.q_ref[:, hs].T, preferred_element_type=F32)

    def begin(self, first, end):
        self.end = end
        for h in range(first, min(first + A_LOOKAHEAD, end)):
            self.tiles[h] = self._scores(h)

    def head(self, h):
        d, tq = A_HEAD_DIM, A_TQ
        hs = slice(h * d, (h + 1) * d)
        if h + A_LOOKAHEAD < self.end:
            self.tiles[h + A_LOOKAHEAD] = self._scores(h + A_LOOKAHEAD)
        v = jnp.concatenate([r[:, hs] for r in self.v_refs], axis=0)
        st = self.bias_ref[h // 2][:, (h % 2) * tq:(h % 2 + 1) * tq] + self.tiles[h]
        self.tiles[h] = None
        m = _col_reduce(st, jnp.max)
        p = jnp.exp2(st - m)
        l = _col_reduce(p, jnp.sum)
        ot = lax.dot_general(v, p.astype(BF16), (((0,), (0,)), ((), ())),
                             preferred_element_type=F32)
        self.o_ref[:, hs] = (ot / l).T.astype(self.o_ref.dtype)


def _col_reduce(x, op):
    rows, cols = x.shape
    slabs = op(x.reshape(B_RED_SLABS, rows // B_RED_SLABS, cols), axis=0)
    return op(slabs, axis=0, keepdims=True)


def _mixers_kernel(q_ref, kn_ref, kpe_ref, vt_ref, wg_ref, wu_ref,
                   qa_ref, ka0_ref, ka1_ref, ka2_ref, va0_ref, va1_ref, va2_ref, bias_ref,
                   o_ref, wgb_ref, wub_ref, oa_ref,
                   m_ref, l_ref, acc_ref, st_ref):
    i = pl.program_id(2)
    hp = B_HEADS_PER_STEP
    m_ref[...] = jnp.full(m_ref.shape, -jnp.inf, F32)
    l_ref[...] = jnp.zeros(l_ref.shape, F32)
    acc_ref[...] = jnp.zeros(acc_ref.shape, F32)

    def scores(j, h):
        r0 = pl.multiple_of(j * B_TK, B_TK)
        k = jnp.concatenate([kn_ref[pl.ds(r0, B_TK), h * B_NOPE:(h + 1) * B_NOPE],
                             kpe_ref[pl.ds(r0, B_TK), :]], axis=1)
        return jnp.dot(k, q_ref[h * B_QK:(h + 1) * B_QK, :], preferred_element_type=F32)

    def softmax_pv(j, h, st, allowed):
        r0 = pl.multiple_of(j * B_TK, B_TK)
        if allowed is not None:
            st = jnp.where(allowed, st, -jnp.inf)
        m_prev = m_ref[h]
        m_new = jnp.maximum(m_prev, _col_reduce(st, jnp.max))
        alpha = jnp.exp2(m_prev - m_new)
        p = jnp.exp2(st - m_new)
        l_ref[h] = alpha * l_ref[h] + _col_reduce(p, jnp.sum)
        pv = jnp.dot(vt_ref[h * B_V:(h + 1) * B_V, pl.ds(r0, B_TK)], p.astype(BF16),
                     preferred_element_type=F32)
        acc_ref[h] = alpha * acc_ref[h] + pv
        m_ref[h] = m_new

    def key_block(j, masked, after_head=None):
        allowed = None
        if masked:
            krow = lax.broadcasted_iota(jnp.int32, (B_TK, B_TQ), 0)
            qcol = lax.broadcasted_iota(jnp.int32, (B_TK, B_TQ), 1)
            allowed = (krow // CHUNK) <= (qcol // CHUNK)
        look = B_LOOKAHEAD
        tiles = [st_ref[h] for h in range(look)] + [None] * (hp - look)
        for h in range(hp):
            ahead = h + look
            if ahead < hp:
                tiles[ahead] = scores(j, ahead)
            elif not masked:
                st_ref[ahead - hp] = scores(j + 1, ahead - hp)
            softmax_pv(j, h, tiles[h], allowed)
            tiles[h] = None
            if after_head is not None:
                after_head(h)

    def body_pair(u, carry):
        key_block(2 * u, False)
        key_block(2 * u + 1, False)
        return carry

    mixer_a = _MixerAHeads(qa_ref, (ka0_ref, ka1_ref, ka2_ref), (va0_ref, va1_ref, va2_ref),
                           bias_ref, oa_ref)

    for h in range(B_LOOKAHEAD):
        st_ref[h] = scores(0, h)
    lax.fori_loop(0, lax.shift_right_logical(i, 1), body_pair, 0)

    @pl.when(lax.bitwise_and(i, 1) == 1)
    def _():
        key_block(i - 1, False)

    mixer_a.begin(0, A_HEADS)

    def mixer_a_heads(h):
        for ha in range(h * A_HEADS // hp, (h + 1) * A_HEADS // hp):
            mixer_a.head(ha)

    key_block(i, True, after_head=mixer_a_heads)

    for h in range(hp):
        o_ref[:, h * B_V:(h + 1) * B_V] = (acc_ref[h] / l_ref[h]).T.astype(o_ref.dtype)

    wgb_ref[...] = wg_ref[...].astype(BF16)
    wub_ref[...] = wu_ref[...].astype(BF16)


def _mixers(proj, bias, qt, kn, kpe, vt, wg, wu, layer, *, batch, seq, name):
    m = kn.shape[0]
    nq = seq // B_TQ
    hp = B_HEADS_PER_STEP
    ng = B_HEADS // hp
    steps = batch * ng * nq
    nqa = seq // A_TQ
    assert steps == batch * nqa
    w_rows, w_cols = wg.shape[1], wg.shape[2]
    slab = w_rows // steps
    assert slab * steps == w_rows and slab % 16 == 0 and wu.shape == wg.shape
    est = (2 * B_TQ * hp * B_QK * 2 + 2 * seq * hp * (B_NOPE + B_V) * 2 + 2 * seq * 128 * 2
           + 2 * B_TQ * hp * B_V * 2 + hp * (B_V + 16) * B_TQ * 4 + 8 * B_TQ * B_TK * 4
           + 4 * slab * w_cols * 6
           + 2 * 8 * A_TQ * A_WIDTH * 2 + 2 * A_HEADS * A_TQ * A_WIN * 4 + 6 * A_WIN * A_TQ * 4
           + (4 << 20))

    def slab_in(b, g, i):
        return (layer, (b * ng + g) * nq + i, 0)

    def slab_out(b, g, i):
        return (0, (b * ng + g) * nq + i, 0)

    def a_blk(b, g, i):
        return g * nq + i

    def a_kv_spec(col_blk, c):
        back = A_KBLKS - 1 - c
        return pl.BlockSpec((A_TQ, A_WIDTH),
                            lambda b, g, i: (b * nqa + jnp.maximum(a_blk(b, g, i) - back, 0), col_blk))

    return pl.pallas_call(
        _mixers_kernel,
        out_shape=(jax.ShapeDtypeStruct((m, B_WIDTH), BF16),
                   jax.ShapeDtypeStruct((1, w_rows, w_cols), BF16),
                   jax.ShapeDtypeStruct((1, w_rows, w_cols), BF16),
                   jax.ShapeDtypeStruct((m, A_WIDTH), BF16)),
        grid=(batch, ng, nq),
        in_specs=[
            pl.BlockSpec((hp * B_QK, B_TQ), lambda b, g, i: (b * ng + g, i)),
            pl.BlockSpec((seq, hp * B_NOPE), lambda b, g, i: (b, g)),
            pl.BlockSpec((seq, V7X_LANES), lambda b, g, i: (b, 0)),
            pl.BlockSpec((hp * B_V, seq), lambda b, g, i: (b * ng + g, 0)),
            pl.BlockSpec((None, slab, w_cols), slab_in),
            pl.BlockSpec((None, slab, w_cols), slab_in),
            pl.BlockSpec((A_TQ, A_WIDTH), lambda b, g, i: (b * nqa + a_blk(b, g, i), 0)),
            a_kv_spec(1, 0), a_kv_spec(1, 1), a_kv_spec(1, 2),
            a_kv_spec(2, 0), a_kv_spec(2, 1), a_kv_spec(2, 2),
            pl.BlockSpec((A_HEADS // 2, None, A_WIN, 2 * A_TQ),
                         lambda b, g, i: (layer, jnp.minimum(a_blk(b, g, i), A_KBLKS - 1), 0, 0)),
        ],
        out_specs=(pl.BlockSpec((B_TQ, hp * B_V), lambda b, g, i: (b * nq + i, g)),
                   pl.BlockSpec((None, slab, w_cols), slab_out),
                   pl.BlockSpec((None, slab, w_cols), slab_out),
                   pl.BlockSpec((A_TQ, A_WIDTH), lambda b, g, i: (b * nqa + a_blk(b, g, i), 0))),
        scratch_shapes=[
            pltpu.VMEM((hp, 1, B_TQ), F32),
            pltpu.VMEM((hp, 1, B_TQ), F32),
            pltpu.VMEM((hp, B_V, B_TQ), F32),
            pltpu.VMEM((B_LOOKAHEAD, B_TK, B_TQ), F32),
        ],
        compiler_params=_cparams(("arbitrary", "arbitrary", "arbitrary"), est),
        name=name,
    )(qt, kn, kpe, vt, wg, wu, proj, proj, proj, proj, proj, proj, proj, bias)


def _cross_kernel(x_ref, oa_ref, ob_ref, woa_ref, wob_ref, g_ref, wq_ref, k_ref, v_ref, wo_ref,
                  gf_ref, o_ref, hf_ref):
    scale = X_HEAD_DIM ** -0.5
    rows = x_ref.shape[0] // X_ROW_PARTS
    for r in range(X_ROW_PARTS):
        rs = slice(r * rows, (r + 1) * rows)
        x = (x_ref[rs, :] + jnp.dot(oa_ref[rs, :], woa_ref[...], preferred_element_type=F32)
             + jnp.dot(ob_ref[rs, :], wob_ref[...], preferred_element_type=F32))
        hn = _rms_rows(x, g_ref[...]).astype(BF16)
        q = (jnp.dot(hn, wq_ref[...], preferred_element_type=F32) * scale).astype(BF16)
        outs = []
        for h in range(X_HEADS):
            hs = slice(h * X_HEAD_DIM, (h + 1) * X_HEAD_DIM)
            s = lax.dot_general(q[:, hs], k_ref[:, hs], (((1,), (1,)), ((), ())),
                                preferred_element_type=F32)
            m = jnp.max(s, axis=1, keepdims=True)
            p = jnp.exp(s - m)
            l = jnp.sum(p, axis=1, keepdims=True)
            o = jnp.dot(p.astype(BF16), v_ref[:, hs], preferred_element_type=F32)
            outs.append((o / l).astype(BF16))
        o_all = jnp.concatenate(outs, axis=1)
        x2 = x + jnp.dot(o_all, wo_ref[...], preferred_element_type=F32)
        o_ref[rs, :] = x2
        hf_ref[rs, :] = _rms_rows(x2, gf_ref[...]).astype(BF16)


def _cross(x, oa, ob, w_out_l, g, wq, kvm, wo, g_ffn, layer, *, seq, tm, name):
    m, d = x.shape
    blocks_per_batch = seq // tm
    est = (4 * tm * d * 4 + tm * d * 2 + 2 * d * X_WIDTH * 2 + 2 * X_WIDTH * d * 2
           + 4 * N_MEM * X_WIDTH * 2 + 4 * tm * N_MEM * 4 + 3 * tm * d * 4
           + 2 * tm * (A_WIDTH + B_WIDTH) * 2 + 2 * (A_WIDTH + B_WIDTH) * d * 2
           + 2 * tm * d * 2 + (4 << 20))
    return pl.pallas_call(
        _cross_kernel,
        out_shape=(jax.ShapeDtypeStruct((m, d), F32), jax.ShapeDtypeStruct((m, d), BF16)),
        grid=(m // tm,),
        in_specs=[
            pl.BlockSpec((tm, d), lambda i: (i, 0)),
            pl.BlockSpec((tm, A_WIDTH), lambda i: (i, 0)),
            pl.BlockSpec((tm, B_WIDTH), lambda i: (i, 0)),
            pl.BlockSpec((None, A_WIDTH, d), lambda i: (0, 0, 0)),
            pl.BlockSpec((None, B_WIDTH, d), lambda i: (0, A_WIDTH // B_WIDTH, 0)),
            pl.BlockSpec((None, 1, d), lambda i: (layer, 0, 0)),
            pl.BlockSpec((None, d, X_WIDTH), lambda i: (layer, 0, 0)),
            pl.BlockSpec((N_MEM, X_WIDTH), lambda i: (i // blocks_per_batch, 2 * layer)),
            pl.BlockSpec((N_MEM, X_WIDTH), lambda i: (i // blocks_per_batch, 2 * layer + 1)),
            pl.BlockSpec((None, X_WIDTH, d), lambda i: (layer, 0, 0)),
            pl.BlockSpec((None, 1, d), lambda i: (layer, 0, 0)),
        ],
        out_specs=(pl.BlockSpec((tm, d), lambda i: (i, 0)),
                   pl.BlockSpec((tm, d), lambda i: (i, 0))),
        compiler_params=_cparams(("arbitrary",), est),
        name=name,
    )(x, oa, ob, w_out_l, w_out_l, g, wq, kvm, kvm, wo, g_ffn)


def _final_norm_kernel(x_ref, g_ref, o_ref):
    o_ref[...] = _rms_rows(x_ref[...], g_ref[...])


def _final_norm(x, g, *, tm):
    m, d = x.shape
    return pl.pallas_call(
        _final_norm_kernel,
        out_shape=jax.ShapeDtypeStruct((m, d), F32),
        grid=(m // tm,),
        in_specs=[pl.BlockSpec((tm, d), lambda i: (i, 0)), pl.BlockSpec((1, d), lambda i: (0, 0))],
        out_specs=pl.BlockSpec((tm, d), lambda i: (i, 0)),
        compiler_params=_cparams(("arbitrary",), 6 * tm * d * 4 + (4 << 20)),
        name="final_norm",
    )(x, g)


RELAYOUT_COLS = 256
RELAYOUT_ROPE_BLK = IN_ROPE_OFF // RELAYOUT_COLS


def _w_in_relayout_kernel(wt_ref, o_ref):
    j = pl.program_id(1)

    @pl.when(j != RELAYOUT_ROPE_BLK)
    def _():
        factor = jnp.where(j < A_WIDTH // RELAYOUT_COLS, A_SCORE_SCALE, 1.0)
        o_ref[...] = (wt_ref[...] * factor).T.astype(BF16)

    @pl.when(j == RELAYOUT_ROPE_BLK)
    def _():
        half = B_ROPE // 2
        kr = wt_ref[:B_ROPE, :]
        slot = jnp.concatenate([kr, kr[half:], kr[:half],
                                jnp.zeros((RELAYOUT_COLS - 2 * B_ROPE, kr.shape[1]), F32)], axis=0)
        o_ref[...] = slot.T.astype(BF16)


def _w_in_relayout(w_in_t):
    layers, n, k = w_in_t.shape
    lat_blk = IN_ROPE_OFF // RELAYOUT_COLS
    rope_src = (IN_ROPE_OFF + KV_LORA) // RELAYOUT_COLS
    assert IN_KVLAT_OFF == (RELAYOUT_ROPE_BLK + 1) * RELAYOUT_COLS

    def src_block(l, j):
        return (l, jnp.where(j < RELAYOUT_ROPE_BLK, j,
                             jnp.where(j == RELAYOUT_ROPE_BLK, rope_src, j - 1 + lat_blk - RELAYOUT_ROPE_BLK)), 0)

    return pl.pallas_call(
        _w_in_relayout_kernel,
        out_shape=jax.ShapeDtypeStruct((layers, k, IN_PAD_WIDTH), BF16),
        grid=(layers, IN_PAD_WIDTH // RELAYOUT_COLS),
        in_specs=[pl.BlockSpec((None, RELAYOUT_COLS, k), src_block)],
        out_specs=pl.BlockSpec((None, k, RELAYOUT_COLS), lambda l, j: (l, 0, j)),
        compiler_params=_cparams(("arbitrary", "arbitrary"), 32 << 20),
        name="w_in_relayout",
    )(w_in_t)


def _prep_weights(w_in, w_uq, w_ukv):
    swap = (jnp.arange(B_ROPE) + B_ROPE // 2) % B_ROPE
    w_in_p = _w_in_relayout(jnp.swapaxes(w_in, 1, 2))

    uq = w_uq.reshape(DEPTH, Q_LORA, B_HEADS, B_NOPE + B_ROPE)
    uq_pe = uq[..., B_NOPE:]
    wq_p = jnp.concatenate([uq[..., :B_NOPE], uq_pe, uq_pe[..., swap]], axis=-1)
    wq_p = wq_p.reshape(DEPTH, Q_LORA, B_HEADS * B_QK).astype(BF16)

    ukv = w_ukv.reshape(DEPTH, KV_LORA, B_HEADS, B_NOPE + B_V)
    wkv_p = jnp.concatenate([ukv[..., :B_NOPE].reshape(DEPTH, KV_LORA, B_HEADS * B_NOPE),
                             ukv[..., B_NOPE:].reshape(DEPTH, KV_LORA, B_WIDTH)],
                            axis=-1).astype(BF16)

    return w_in_p, wq_p, wkv_p


def _bias_table_kernel(w_ref, o_ref):
    v = pl.program_id(1)
    r = lax.broadcasted_iota(jnp.int32, (A_TQ, A_WIN), 0)
    t = lax.broadcasted_iota(jnp.int32, (A_TQ, A_WIN), 1)
    dchunk = t // CHUNK - r // CHUNK
    valid = (dchunk >= 0) & (dchunk <= LEFT_CHUNKS) & (v - (A_KBLKS - 1) + t // A_TQ >= 0)
    for hh in range(2):
        w = jnp.broadcast_to(w_ref[hh], (A_TQ, A_BIAS_VEC))
        tab = pltpu.roll(w, 0, axis=1, stride=1, stride_axis=0)[:, :A_WIN]
        o_ref[0, 0, :, hh * A_TQ:(hh + 1) * A_TQ] = jnp.where(valid, tab * LOG2E, -jnp.inf).T


def _relpos_bias_tables(rel_bias):
    u = jnp.arange(A_BIAS_VEC)
    delta = jnp.where(u <= A_WIN, u, u - A_BIAS_VEC)
    idx = jnp.clip(LEFT_CHUNKS * CHUNK - delta, -REL_CLIP, REL_CLIP) + REL_CLIP
    vec = rel_bias[:, :, idx].reshape(DEPTH * A_HEADS, 1, A_BIAS_VEC).astype(F32)
    return pl.pallas_call(
        _bias_table_kernel,
        out_shape=jax.ShapeDtypeStruct((DEPTH * A_HEADS // 2, A_KBLKS, A_WIN, 2 * A_TQ), F32),
        grid=(DEPTH * A_HEADS // 2, A_KBLKS),
        in_specs=[pl.BlockSpec((2, 1, A_BIAS_VEC), lambda i, v: (i, 0, 0))],
        out_specs=pl.BlockSpec((1, 1, A_WIN, 2 * A_TQ), lambda i, v: (i, v, 0, 0)),
        compiler_params=_cparams(("arbitrary", "arbitrary"), 32 << 20),
        name="bias_table",
    )(vec)


def kernel(x, mem, positions, norm_mix, w_in, rel_bias, q_norm, kv_norm, w_uq, w_ukv, w_out,
           norm_mem, mem_norm, w_xq, w_xkv, w_xo, norm_ffn, w_gate, w_up, w_down, norm_final):
    b, s, d = x.shape
    m = b * s
    assert (b, s, d) == (4, 4096, D_MODEL) and mem.shape == (b, N_MEM, d)

    w_in_p, wq_p, wkv_p = _prep_weights(w_in, w_uq, w_ukv)
    w_xq_b = w_xq.astype(BF16)
    w_xo_b = w_xo.astype(BF16)
    bias_tab = _relpos_bias_tables(rel_bias)

    half = B_ROPE // 2
    inv = ROPE_THETA ** (-jnp.arange(half, dtype=F32) / half)
    inv_row = jnp.tile(inv, 4)[None, :]
    sign_row = jnp.concatenate([jnp.ones((B_ROPE,), F32), -jnp.ones((half,), F32),
                                jnp.ones((half,), F32)])[None, :]
    cs = _rope_table(positions.reshape(m, 1), inv_row, sign_row, tm=2048)

    kvm = _mem_kv(mem.reshape(b * N_MEM, d), mem_norm[None, :], w_xkv)

    norm_mix, norm_mem, norm_ffn, q_norm, kv_norm = (
        g[:, None, :] for g in (norm_mix, norm_mem, norm_ffn, q_norm, kv_norm))
    xf = x.reshape(m, d)
    for l in range(DEPTH):
        proj = _norm_matmul(xf, norm_mix, w_in_p, l, tm=1024, tn=1536, name=f"in_proj_{l}")
        qt, kn, vt, kpe, w_out_l = _mla_prep(proj, cs, q_norm, kv_norm, wq_p, wkv_p, w_out, l,
                                             seq=s, tm=512, name=f"mla_prep_{l}")
        ob, w_gate_l, w_up_l, oa = _mixers(proj, bias_tab, qt, kn, kpe, vt, w_gate, w_up, l,
                                           batch=b, seq=s, name=f"mixers_{l}")
        xf, h_ffn = _cross(xf, oa, ob, w_out_l, norm_mem, w_xq_b, kvm, w_xo_b, norm_ffn, l,
                           seq=s, tm=512, name=f"out_cross_{l}")
        act, w_down_l = _swiglu_gu(h_ffn, w_gate_l, w_up_l, w_down, l, tm=1024, tn=512,
                                   name=f"swiglu_gu_{l}")
        xf = _mm_res([act], w_down_l, xf, 0, tm=1024, tn=512, name=f"swiglu_down_{l}")
    out = _final_norm(xf, norm_final[None, :], tm=512)
    return out.reshape(b, s, d)
```

```python
import functools
import math

import jax
import jax.numpy as jnp
from jax import lax
from jax.experimental import pallas as pl
from jax.experimental.pallas import tpu as pltpu

D_MODEL = 2048
DEPTH = 4
CHUNK = 64
LEFT_CHUNKS = 8
A_HEAD_DIM = 128
A_HEADS = 8
A_WIDTH = A_HEADS * A_HEAD_DIM
REL_CLIP = 128
B_NOPE = 128
B_ROPE = 64
B_V = 128
B_HEADS = 8
B_WIDTH = B_HEADS * B_V
Q_LORA = 768
KV_LORA = 512
ROPE_THETA = 10000.0
N_MEM = 256
X_HEADS = 4
X_HEAD_DIM = 128
X_WIDTH = X_HEADS * X_HEAD_DIM
X_ROW_PARTS = 2
D_FF = 5632
EPS = 1e-6

V7X_LANES = 128
V7X_VMEM_BYTES = 64 * 1024 * 1024
V7X_VMEM_REQUEST_CAP = 56 * 1024 * 1024

BF16 = jnp.bfloat16
F32 = jnp.float32

IN_QLAT_OFF = 3 * A_WIDTH
IN_ROPE_OFF = IN_QLAT_OFF + Q_LORA
IN_KVLAT_OFF = IN_ROPE_OFF + 2 * V7X_LANES
IN_PAD_WIDTH = IN_KVLAT_OFF + KV_LORA

A_TQ = 4 * CHUNK
A_WIN = A_TQ + LEFT_CHUNKS * CHUNK
A_KBLKS = A_WIN // A_TQ
A_BIAS_VEC = 1024
A_LOOKAHEAD = 2

B_TQ = 512
B_TK = 512
B_HEADS_PER_STEP = 4
B_LOOKAHEAD = 2
B_RED_SLABS = 8
B_QK = B_NOPE + 2 * B_ROPE

LOG2E = math.log2(math.e)
A_SCORE_SCALE = A_HEAD_DIM ** -0.5 * LOG2E


def _cparams(sem, vmem_bytes):
    limit = int(min(max(vmem_bytes, 32 * 1024 * 1024), V7X_VMEM_REQUEST_CAP))
    return pltpu.CompilerParams(dimension_semantics=sem, vmem_limit_bytes=limit)


def _rms_rows(x, g):
    ms = jnp.mean(x * x, axis=-1, keepdims=True)
    return (x * lax.rsqrt(ms + EPS)) * g


def _norm_into(h_ref, x_ref, g_ref, rows=128):
    for r0 in range(0, x_ref.shape[0], rows):
        x = x_ref[r0:r0 + rows, :].astype(F32)
        h_ref[r0:r0 + rows, :] = _rms_rows(x, g_ref[...]).astype(h_ref.dtype)


def _norm_matmul_kernel(x_ref, g_ref, w_ref, o_ref, h_ref):
    @pl.when(pl.program_id(1) == 0)
    def _():
        _norm_into(h_ref, x_ref, g_ref)

    o_ref[...] = jnp.dot(h_ref[...], w_ref[...], preferred_element_type=F32).astype(o_ref.dtype)


def _norm_matmul(x, g, w, layer, *, tm, tn, name):
    m, k = x.shape
    n = w.shape[2]
    xb = x.dtype.itemsize
    est = 2 * tm * k * xb + tm * k * 2 + 2 * k * tn * 2 + 2 * tm * tn * 2 + tm * tn * 4 + (4 << 20)
    return pl.pallas_call(
        _norm_matmul_kernel,
        out_shape=jax.ShapeDtypeStruct((m, n), BF16),
        grid=(m // tm, n // tn),
        in_specs=[
            pl.BlockSpec((tm, k), lambda i, j: (i, 0)),
            pl.BlockSpec((None, 1, k), lambda i, j: (layer, 0, 0)),
            pl.BlockSpec((None, k, tn), lambda i, j: (layer, 0, j)),
        ],
        out_specs=pl.BlockSpec((tm, tn), lambda i, j: (i, j)),
        scratch_shapes=[pltpu.VMEM((tm, k), BF16)],
        compiler_params=_cparams(("arbitrary", "arbitrary"), est),
        name=name,
    )(x, g, w)


def _mem_kv_kernel(x_ref, g_ref, w_ref, o_ref, h_ref):
    @pl.when(pl.program_id(0) == 0)
    def _():
        _norm_into(h_ref, x_ref, g_ref)

    o_ref[...] = jnp.dot(h_ref[...], w_ref[...].astype(BF16),
                         preferred_element_type=F32).astype(o_ref.dtype)


def _mem_kv(mem2d, g, w_xkv):
    m, k = mem2d.shape
    layers, _, n = w_xkv.shape
    est = 2 * m * k * 4 + m * k * 2 + 2 * k * n * 4 + k * n * 2 + 2 * m * n * 2 + m * n * 4 + (4 << 20)
    return pl.pallas_call(
        _mem_kv_kernel,
        out_shape=jax.ShapeDtypeStruct((m, layers * n), BF16),
        grid=(layers,),
        in_specs=[
            pl.BlockSpec((m, k), lambda l: (0, 0)),
            pl.BlockSpec((1, k), lambda l: (0, 0)),
            pl.BlockSpec((None, k, n), lambda l: (l, 0, 0)),
        ],
        out_specs=pl.BlockSpec((m, n), lambda l: (0, l)),
        scratch_shapes=[pltpu.VMEM((m, k), BF16)],
        compiler_params=_cparams(("arbitrary",), est),
        name="mem_kv",
    )(mem2d, g, w_xkv)


def _mm_res_kernel(*refs, n_in):
    a_refs = refs[:n_in]
    w_refs = refs[n_in:2 * n_in]
    r_ref = refs[2 * n_in]
    o_ref = refs[2 * n_in + 1]
    acc = r_ref[...]
    for a_ref, w_ref in zip(a_refs, w_refs):
        acc = acc + jnp.dot(a_ref[...], w_ref[...], preferred_element_type=F32)
    o_ref[...] = acc


def _mm_res(a_list, w, res, layer, *, tm, tn, name):
    m, n = res.shape
    n_in = len(a_list)
    in_specs = []
    w_specs = []
    est = (4 + 2 * n_in) * tm * tn * 4 + (4 << 20)
    for idx, a in enumerate(a_list):
        k = a.shape[1]
        assert w.shape[1] == n_in * k
        in_specs.append(pl.BlockSpec((tm, k), lambda i, j: (i, 0)))
        w_specs.append(pl.BlockSpec((None, k, tn), lambda i, j, idx=idx: (layer, idx, j)))
        est += 2 * tm * k * 2 + 2 * k * tn * 2
    in_specs += w_specs
    in_specs.append(pl.BlockSpec((tm, tn), lambda i, j: (i, j)))
    return pl.pallas_call(
        functools.partial(_mm_res_kernel, n_in=n_in),
        out_shape=jax.ShapeDtypeStruct((m, n), F32),
        grid=(m // tm, n // tn),
        in_specs=in_specs,
        out_specs=pl.BlockSpec((tm, tn), lambda i, j: (i, j)),
        compiler_params=_cparams(("arbitrary", "arbitrary"), est),
        name=name,
    )(*a_list, *([w] * n_in), res)


def _swiglu_gu_kernel(h_ref, wg_ref, wu_ref, wd_ref, o_ref, wdb_ref):
    h = h_ref[...]
    gate = jnp.dot(h, wg_ref[...], preferred_element_type=F32)
    up = jnp.dot(h, wu_ref[...], preferred_element_type=F32)
    o_ref[...] = (gate * (1.0 / (1.0 + jnp.exp(-gate))) * up).astype(o_ref.dtype)
    wdb_ref[...] = wd_ref[...].astype(BF16)


def _swiglu_gu(h, wg, wu, wd, layer, *, tm, tn, name):
    m, k = h.shape
    n = wg.shape[2]
    steps = (m // tm) * (n // tn)
    nj = n // tn
    wd_rows, wd_cols = wd.shape[1], wd.shape[2]
    slab = wd_rows // steps
    assert slab * steps == wd_rows and slab % 16 == 0
    est = (2 * tm * k * 2 + 4 * k * tn * 2 + 2 * tm * tn * 2 + 3 * tm * tn * 4
           + 2 * slab * wd_cols * 6 + (4 << 20))
    return pl.pallas_call(
        _swiglu_gu_kernel,
        out_shape=(jax.ShapeDtypeStruct((m, n), BF16),
                   jax.ShapeDtypeStruct((1, wd_rows, wd_cols), BF16)),
        grid=(m // tm, nj),
        in_specs=[
            pl.BlockSpec((tm, k), lambda i, j: (i, 0)),
            pl.BlockSpec((None, k, tn), lambda i, j: (0, 0, j)),
            pl.BlockSpec((None, k, tn), lambda i, j: (0, 0, j)),
            pl.BlockSpec((None, slab, wd_cols), lambda i, j: (layer, i * nj + j, 0)),
        ],
        out_specs=(pl.BlockSpec((tm, tn), lambda i, j: (i, j)),
                   pl.BlockSpec((None, slab, wd_cols), lambda i, j: (0, i * nj + j, 0))),
        compiler_params=_cparams(("arbitrary", "arbitrary"), est),
        name=name,
    )(h, wg, wu, wd)


def _rope_table_kernel(pos_ref, inv_ref, sign_ref, cs_ref):
    ang = pos_ref[...].astype(F32) * inv_ref[...]
    lane = lax.broadcasted_iota(jnp.int32, ang.shape, 1)
    cs_ref[...] = jnp.where(lane < B_ROPE, jnp.cos(ang), jnp.sin(ang) * sign_ref[...])


def _rope_table(pos_col, inv_row, sign_row, *, tm):
    m = pos_col.shape[0]
    return pl.pallas_call(
        _rope_table_kernel,
        out_shape=jax.ShapeDtypeStruct((m, V7X_LANES), F32),
        grid=(m // tm,),
        in_specs=[
            pl.BlockSpec((tm, 1), lambda i: (i, 0)),
            pl.BlockSpec((1, V7X_LANES), lambda i: (0, 0)),
            pl.BlockSpec((1, V7X_LANES), lambda i: (0, 0)),
        ],
        out_specs=pl.BlockSpec((tm, V7X_LANES), lambda i: (i, 0)),
        compiler_params=_cparams(("arbitrary",), 32 << 20),
        name="rope_table",
    )(pos_col, inv_row, sign_row)


def _rope_pair_sum(slot, cs):
    t = slot * cs
    return t + pltpu.roll(t, B_ROPE, axis=1)


def _mla_prep_kernel(ql_ref, rope_ref, kvl_ref, cs_ref, gq_ref, gkv_ref, wq_ref, wkv_ref, wo_ref,
                     qt_ref, kn_ref, vt_ref, kpe_ref, wob_ref, *, q_scale):
    cs = cs_ref[...]
    hq = _rms_rows(ql_ref[...].astype(F32), gq_ref[...]).astype(BF16)
    qf = jnp.dot(hq, wq_ref[...], preferred_element_type=F32)
    for h in range(B_HEADS):
        base = h * B_QK
        nope = qf[:, base:base + B_NOPE]
        pe = _rope_pair_sum(qf[:, base + B_NOPE:base + B_QK], cs)
        qt_ref[base:base + B_NOPE, :] = (nope * q_scale).T.astype(BF16)
        qt_ref[base + B_NOPE:base + B_QK, :] = (pe * q_scale).T.astype(BF16)

    hkv = _rms_rows(kvl_ref[...].astype(F32), gkv_ref[...]).astype(BF16)
    kvf = jnp.dot(hkv, wkv_ref[...], preferred_element_type=F32)
    kn_ref[...] = kvf[:, :B_HEADS * B_NOPE].astype(BF16)
    vt_ref[...] = kvf[:, B_HEADS * B_NOPE:].T.astype(BF16)

    kpe = _rope_pair_sum(rope_ref[...].astype(F32), cs)
    lane = lax.broadcasted_iota(jnp.int32, kpe.shape, 1)
    kpe_ref[...] = jnp.where(lane < B_ROPE, kpe, 0.0).astype(BF16)
    wob_ref[...] = wo_ref[...].astype(BF16)


def _mla_prep(proj, cs, gq, gkv, wq, wkv, wo, layer, *, seq, tm, name):
    m = proj.shape[0]
    tiles_per_seq = seq // tm
    q_scale = (B_NOPE + B_ROPE) ** -0.5 * LOG2E
    wo_rows, wo_cols = wo.shape[1], wo.shape[2]
    slab = wo_rows // (m // tm)
    assert slab * (m // tm) == wo_rows and slab % 16 == 0
    est = (2 * tm * (Q_LORA + KV_LORA + 128) * 2 + 2 * tm * 128 * 4
           + 2 * (Q_LORA + KV_LORA) * 2048 * 2 + 2 * slab * wo_cols * 6
           + 2 * tm * (2048 + 1024 + 1024 + 128) * 2 + 3 * tm * 2048 * 4 + (4 << 20))
    return pl.pallas_call(
        functools.partial(_mla_prep_kernel, q_scale=q_scale),
        out_shape=(
            jax.ShapeDtypeStruct((m // seq * B_HEADS * B_QK, seq), BF16),
            jax.ShapeDtypeStruct((m, B_HEADS * B_NOPE), BF16),
            jax.ShapeDtypeStruct((m // seq * B_WIDTH, seq), BF16),
            jax.ShapeDtypeStruct((m, V7X_LANES), BF16),
            jax.ShapeDtypeStruct((1, wo_rows, wo_cols), BF16),
        ),
        grid=(m // tm,),
        in_specs=[
            pl.BlockSpec((tm, Q_LORA), lambda i: (i, IN_QLAT_OFF // Q_LORA)),
            pl.BlockSpec((tm, V7X_LANES), lambda i: (i, IN_ROPE_OFF // V7X_LANES)),
            pl.BlockSpec((tm, KV_LORA), lambda i: (i, IN_KVLAT_OFF // KV_LORA)),
            pl.BlockSpec((tm, V7X_LANES), lambda i: (i, 0)),
            pl.BlockSpec((None, 1, Q_LORA), lambda i: (layer, 0, 0)),
            pl.BlockSpec((None, 1, KV_LORA), lambda i: (layer, 0, 0)),
            pl.BlockSpec((None, Q_LORA, B_HEADS * B_QK), lambda i: (layer, 0, 0)),
            pl.BlockSpec((None, KV_LORA, 2 * B_WIDTH), lambda i: (layer, 0, 0)),
            pl.BlockSpec((None, slab, wo_cols), lambda i: (layer, i, 0)),
        ],
        out_specs=(
            pl.BlockSpec((B_HEADS * B_QK, tm), lambda i: (i // tiles_per_seq, i % tiles_per_seq)),
            pl.BlockSpec((tm, B_HEADS * B_NOPE), lambda i: (i, 0)),
            pl.BlockSpec((B_WIDTH, tm), lambda i: (i // tiles_per_seq, i % tiles_per_seq)),
            pl.BlockSpec((tm, V7X_LANES), lambda i: (i, 0)),
            pl.BlockSpec((None, slab, wo_cols), lambda i: (0, i, 0)),
        ),
        compiler_params=_cparams(("arbitrary",), est),
        name=name,
    )(proj, proj, proj, cs, gq, gkv, wq, wkv, wo)


class _MixerAHeads:
    def __init__(self, q_ref, k_refs, v_refs, bias_ref, o_ref):
        self.q_ref, self.k_refs, self.v_refs = q_ref, k_refs, v_refs
        self.bias_ref, self.o_ref = bias_ref, o_ref
        self.tiles = [None] * A_HEADS
        self.end = 0

    def _scores(self, h):
        hs = slice(h * A_HEAD_DIM, (h + 1) * A_HEAD_DIM)
        k = jnp.concatenate([r[:, hs] for r in self.k_refs], axis=0)
        return jnp.dot(k, self.q_ref[:, hs].T, preferred_element_type=F32)

    def begin(self, first, end):
        self.end = end
        for h in range(first, min(first + A_LOOKAHEAD, end)):
            self.tiles[h] = self._scores(h)

    def head(self, h):
        d, tq = A_HEAD_DIM, A_TQ
        hs = slice(h * d, (h + 1) * d)
        if h + A_LOOKAHEAD < self.end:
            self.tiles[h + A_LOOKAHEAD] = self._scores(h + A_LOOKAHEAD)
        v = jnp.concatenate([r[:, hs] for r in self.v_refs], axis=0)
        st = self.bias_ref[h // 2][:, (h % 2) * tq:(h % 2 + 1) * tq] + self.tiles[h]
        self.tiles[h] = None
        m = _col_reduce(st, jnp.max)
        p = jnp.exp2(st - m)
        l = _col_reduce(p, jnp.sum)
        ot = lax.dot_general(v, p.astype(BF16), (((0,), (0,)), ((), ())),
                             preferred_element_type=F32)
        self.o_ref[:, hs] = (ot / l).T.astype(self.o_ref.dtype)


def _col_reduce(x, op):
    rows, cols = x.shape
    slabs = op(x.reshape(B_RED_SLABS, rows // B_RED_SLABS, cols), axis=0)
    return op(slabs, axis=0, keepdims=True)


def _mixers_kernel(q_ref, kn_ref, kpe_ref, vt_ref, wg_ref, wu_ref,
                   qa_ref, ka0_ref, ka1_ref, ka2_ref, va0_ref, va1_ref, va2_ref, bias_ref,
                   o_ref, wgb_ref, wub_ref, oa_ref,
                   m_ref, l_ref, acc_ref, st_ref):
    i = pl.program_id(2)
    hp = B_HEADS_PER_STEP
    m_ref[...] = jnp.full(m_ref.shape, -jnp.inf, F32)
    l_ref[...] = jnp.zeros(l_ref.shape, F32)
    acc_ref[...] = jnp.zeros(acc_ref.shape, F32)

    def scores(j, h):
        r0 = pl.multiple_of(j * B_TK, B_TK)
        k = jnp.concatenate([kn_ref[pl.ds(r0, B_TK), h * B_NOPE:(h + 1) * B_NOPE],
                             kpe_ref[pl.ds(r0, B_TK), :]], axis=1)
        return jnp.dot(k, q_ref[h * B_QK:(h + 1) * B_QK, :], preferred_element_type=F32)

    def softmax_pv(j, h, st, allowed):
        r0 = pl.multiple_of(j * B_TK, B_TK)
        if allowed is not None:
            st = jnp.where(allowed, st, -jnp.inf)
        m_prev = m_ref[h]
        m_new = jnp.maximum(m_prev, _col_reduce(st, jnp.max))
        alpha = jnp.exp2(m_prev - m_new)
        p = jnp.exp2(st - m_new)
        l_ref[h] = alpha * l_ref[h] + _col_reduce(p, jnp.sum)
        pv = jnp.dot(vt_ref[h * B_V:(h + 1) * B_V, pl.ds(r0, B_TK)], p.astype(BF16),
                     preferred_element_type=F32)
        acc_ref[h] = alpha * acc_ref[h] + pv
        m_ref[h] = m_new

    def key_block(j, masked, after_head=None):
        allowed = None
        if masked:
            krow = lax.broadcasted_iota(jnp.int32, (B_TK, B_TQ), 0)
            qcol = lax.broadcasted_iota(jnp.int32, (B_TK, B_TQ), 1)
            allowed = (krow // CHUNK) <= (qcol // CHUNK)
        look = B_LOOKAHEAD
        tiles = [st_ref[h] for h in range(look)] + [None] * (hp - look)
        for h in range(hp):
            ahead = h + look
            if ahead < hp:
                tiles[ahead] = scores(j, ahead)
            elif not masked:
                st_ref[ahead - hp] = scores(j + 1, ahead - hp)
            softmax_pv(j, h, tiles[h], allowed)
            tiles[h] = None
            if after_head is not None:
                after_head(h)

    def body_pair(u, carry):
        key_block(2 * u, False)
        key_block(2 * u + 1, False)
        return carry

    mixer_a = _MixerAHeads(qa_ref, (ka0_ref, ka1_ref, ka2_ref), (va0_ref, va1_ref, va2_ref),
                           bias_ref, oa_ref)

    for h in range(B_LOOKAHEAD):
        st_ref[h] = scores(0, h)
    lax.fori_loop(0, lax.shift_right_logical(i, 1), body_pair, 0)

    @pl.when(lax.bitwise_and(i, 1) == 1)
    def _():
        key_block(i - 1, False)

    mixer_a.begin(0, A_HEADS)

    def mixer_a_heads(h):
        for ha in range(h * A_HEADS // hp, (h + 1) * A_HEADS // hp):
            mixer_a.head(ha)

    key_block(i, True, after_head=mixer_a_heads)

    for h in range(hp):
        o_ref[:, h * B_V:(h + 1) * B_V] = (acc_ref[h] / l_ref[h]).T.astype(o_ref.dtype)

    wgb_ref[...] = wg_ref[...].astype(BF16)
    wub_ref[...] = wu_ref[...].astype(BF16)


def _mixers(proj, bias, qt, kn, kpe, vt, wg, wu, layer, *, batch, seq, name):
    m = kn.shape[0]
    nq = seq // B_TQ
    hp = B_HEADS_PER_STEP
    ng = B_HEADS // hp
    steps = batch * ng * nq
    nqa = seq // A_TQ
    assert steps == batch * nqa
    w_rows, w_cols = wg.shape[1], wg.shape[2]
    slab = w_rows // steps
    assert slab * steps == w_rows and slab % 16 == 0 and wu.shape == wg.shape
    est = (2 * B_TQ * hp * B_QK * 2 + 2 * seq * hp * (B_NOPE + B_V) * 2 + 2 * seq * 128 * 2
           + 2 * B_TQ * hp * B_V * 2 + hp * (B_V + 16) * B_TQ * 4 + 8 * B_TQ * B_TK * 4
           + 4 * slab * w_cols * 6
           + 2 * 8 * A_TQ * A_WIDTH * 2 + 2 * A_HEADS * A_TQ * A_WIN * 4 + 6 * A_WIN * A_TQ * 4
           + (4 << 20))

    def slab_in(b, g, i):
        return (layer, (b * ng + g) * nq + i, 0)

    def slab_out(b, g, i):
        return (0, (b * ng + g) * nq + i, 0)

    def a_blk(b, g, i):
        return g * nq + i

    def a_kv_spec(col_blk, c):
        back = A_KBLKS - 1 - c
        return pl.BlockSpec((A_TQ, A_WIDTH),
                            lambda b, g, i: (b * nqa + jnp.maximum(a_blk(b, g, i) - back, 0), col_blk))

    return pl.pallas_call(
        _mixers_kernel,
        out_shape=(jax.ShapeDtypeStruct((m, B_WIDTH), BF16),
                   jax.ShapeDtypeStruct((1, w_rows, w_cols), BF16),
                   jax.ShapeDtypeStruct((1, w_rows, w_cols), BF16),
                   jax.ShapeDtypeStruct((m, A_WIDTH), BF16)),
        grid=(batch, ng, nq),
        in_specs=[
            pl.BlockSpec((hp * B_QK, B_TQ), lambda b, g, i: (b * ng + g, i)),
            pl.BlockSpec((seq, hp * B_NOPE), lambda b, g, i: (b, g)),
            pl.BlockSpec((seq, V7X_LANES), lambda b, g, i: (b, 0)),
            pl.BlockSpec((hp * B_V, seq), lambda b, g, i: (b * ng + g, 0)),
            pl.BlockSpec((None, slab, w_cols), slab_in),
            pl.BlockSpec((None, slab, w_cols), slab_in),
            pl.BlockSpec((A_TQ, A_WIDTH), lambda b, g, i: (b * nqa + a_blk(b, g, i), 0)),
            a_kv_spec(1, 0), a_kv_spec(1, 1), a_kv_spec(1, 2),
            a_kv_spec(2, 0), a_kv_spec(2, 1), a_kv_spec(2, 2),
            pl.BlockSpec((A_HEADS // 2, None, A_WIN, 2 * A_TQ),
                         lambda b, g, i: (layer, jnp.minimum(a_blk(b, g, i), A_KBLKS - 1), 0, 0)),
        ],
        out_specs=(pl.BlockSpec((B_TQ, hp * B_V), lambda b, g, i: (b * nq + i, g)),
                   pl.BlockSpec((None, slab, w_cols), slab_out),
                   pl.BlockSpec((None, slab, w_cols), slab_out),
                   pl.BlockSpec((A_TQ, A_WIDTH), lambda b, g, i: (b * nqa + a_blk(b, g, i), 0))),
        scratch_shapes=[
            pltpu.VMEM((hp, 1, B_TQ), F32),
            pltpu.VMEM((hp, 1, B_TQ), F32),
            pltpu.VMEM((hp, B_V, B_TQ), F32),
            pltpu.VMEM((B_LOOKAHEAD, B_TK, B_TQ), F32),
        ],
        compiler_params=_cparams(("arbitrary", "arbitrary", "arbitrary"), est),
        name=name,
    )(qt, kn, kpe, vt, wg, wu, proj, proj, proj, proj, proj, proj, proj, bias)


def _cross_kernel(x_ref, oa_ref, ob_ref, woa_ref, wob_ref, g_ref, wq_ref, k_ref, v_ref, wo_ref,
                  gf_ref, o_ref, hf_ref):
    scale = X_HEAD_DIM ** -0.5
    rows = x_ref.shape[0] // X_ROW_PARTS
    for r in range(X_ROW_PARTS):
        rs = slice(r * rows, (r + 1) * rows)
        x = (x_ref[rs, :] + jnp.dot(oa_ref[rs, :], woa_ref[...], preferred_element_type=F32)
             + jnp.dot(ob_ref[rs, :], wob_ref[...], preferred_element_type=F32))
        hn = _rms_rows(x, g_ref[...]).astype(BF16)
        q = (jnp.dot(hn, wq_ref[...], preferred_element_type=F32) * scale).astype(BF16)
        outs = []
        for h in range(X_HEADS):
            hs = slice(h * X_HEAD_DIM, (h + 1) * X_HEAD_DIM)
            s = lax.dot_general(q[:, hs], k_ref[:, hs], (((1,), (1,)), ((), ())),
                                preferred_element_type=F32)
            m = jnp.max(s, axis=1, keepdims=True)
            p = jnp.exp(s - m)
            l = jnp.sum(p, axis=1, keepdims=True)
            o = jnp.dot(p.astype(BF16), v_ref[:, hs], preferred_element_type=F32)
            outs.append((o / l).astype(BF16))
        o_all = jnp.concatenate(outs, axis=1)
        x2 = x + jnp.dot(o_all, wo_ref[...], preferred_element_type=F32)
        o_ref[rs, :] = x2
        hf_ref[rs, :] = _rms_rows(x2, gf_ref[...]).astype(BF16)


def _cross(x, oa, ob, w_out_l, g, wq, kvm, wo, g_ffn, layer, *, seq, tm, name):
    m, d = x.shape
    blocks_per_batch = seq // tm
    est = (4 * tm * d * 4 + tm * d * 2 + 2 * d * X_WIDTH * 2 + 2 * X_WIDTH * d * 2
           + 4 * N_MEM * X_WIDTH * 2 + 4 * tm * N_MEM * 4 + 3 * tm * d * 4
           + 2 * tm * (A_WIDTH + B_WIDTH) * 2 + 2 * (A_WIDTH + B_WIDTH) * d * 2
           + 2 * tm * d * 2 + (4 << 20))
    return pl.pallas_call(
        _cross_kernel,
        out_shape=(jax.ShapeDtypeStruct((m, d), F32), jax.ShapeDtypeStruct((m, d), BF16)),
        grid=(m // tm,),
        in_specs=[
            pl.BlockSpec((tm, d), lambda i: (i, 0)),
            pl.BlockSpec((tm, A_WIDTH), lambda i: (i, 0)),
            pl.BlockSpec((tm, B_WIDTH), lambda i: (i, 0)),
            pl.BlockSpec((None, A_WIDTH, d), lambda i: (0, 0, 0)),
            pl.BlockSpec((None, B_WIDTH, d), lambda i: (0, A_WIDTH // B_WIDTH, 0)),
            pl.BlockSpec((None, 1, d), lambda i: (layer, 0, 0)),
            pl.BlockSpec((None, d, X_WIDTH), lambda i: (layer, 0, 0)),
            pl.BlockSpec((N_MEM, X_WIDTH), lambda i: (i // blocks_per_batch, 2 * layer)),
            pl.BlockSpec((N_MEM, X_WIDTH), lambda i: (i // blocks_per_batch, 2 * layer + 1)),
            pl.BlockSpec((None, X_WIDTH, d), lambda i: (layer, 0, 0)),
            pl.BlockSpec((None, 1, d), lambda i: (layer, 0, 0)),
        ],
        out_specs=(pl.BlockSpec((tm, d), lambda i: (i, 0)),
                   pl.BlockSpec((tm, d), lambda i: (i, 0))),
        compiler_params=_cparams(("arbitrary",), est),
        name=name,
    )(x, oa, ob, w_out_l, w_out_l, g, wq, kvm, kvm, wo, g_ffn)


def _final_norm_kernel(x_ref, g_ref, o_ref):
    o_ref[...] = _rms_rows(x_ref[...], g_ref[...])


def _final_norm(x, g, *, tm):
    m, d = x.shape
    return pl.pallas_call(
        _final_norm_kernel,
        out_shape=jax.ShapeDtypeStruct((m, d), F32),
        grid=(m // tm,),
        in_specs=[pl.BlockSpec((tm, d), lambda i: (i, 0)), pl.BlockSpec((1, d), lambda i: (0, 0))],
        out_specs=pl.BlockSpec((tm, d), lambda i: (i, 0)),
        compiler_params=_cparams(("arbitrary",), 6 * tm * d * 4 + (4 << 20)),
        name="final_norm",
    )(x, g)


RELAYOUT_COLS = 256
RELAYOUT_ROPE_BLK = IN_ROPE_OFF // RELAYOUT_COLS


def _w_in_relayout_kernel(wt_ref, o_ref):
    j = pl.program_id(1)

    @pl.when(j != RELAYOUT_ROPE_BLK)
    def _():
        factor = jnp.where(j < A_WIDTH // RELAYOUT_COLS, A_SCORE_SCALE, 1.0)
        o_ref[...] = (wt_ref[...] * factor).T.astype(BF16)

    @pl.when(j == RELAYOUT_ROPE_BLK)
    def _():
        half = B_ROPE // 2
        kr = wt_ref[:B_ROPE, :]
        slot = jnp.concatenate([kr, kr[half:], kr[:half],
                                jnp.zeros((RELAYOUT_COLS - 2 * B_ROPE, kr.shape[1]), F32)], axis=0)
        o_ref[...] = slot.T.astype(BF16)


def _w_in_relayout(w_in_t):
    layers, n, k = w_in_t.shape
    lat_blk = IN_ROPE_OFF // RELAYOUT_COLS
    rope_src = (IN_ROPE_OFF + KV_LORA) // RELAYOUT_COLS
    assert IN_KVLAT_OFF == (RELAYOUT_ROPE_BLK + 1) * RELAYOUT_COLS

    def src_block(l, j):
        return (l, jnp.where(j < RELAYOUT_ROPE_BLK, j,
                             jnp.where(j == RELAYOUT_ROPE_BLK, rope_src, j - 1 + lat_blk - RELAYOUT_ROPE_BLK)), 0)

    return pl.pallas_call(
        _w_in_relayout_kernel,
        out_shape=jax.ShapeDtypeStruct((layers, k, IN_PAD_WIDTH), BF16),
        grid=(layers, IN_PAD_WIDTH // RELAYOUT_COLS),
        in_specs=[pl.BlockSpec((None, RELAYOUT_COLS, k), src_block)],
        out_specs=pl.BlockSpec((None, k, RELAYOUT_COLS), lambda l, j: (l, 0, j)),
        compiler_params=_cparams(("arbitrary", "arbitrary"), 32 << 20),
        name="w_in_relayout",
    )(w_in_t)


def _prep_weights(w_in, w_uq, w_ukv):
    swap = (jnp.arange(B_ROPE) + B_ROPE // 2) % B_ROPE
    w_in_p = _w_in_relayout(jnp.swapaxes(w_in, 1, 2))

    uq = w_uq.reshape(DEPTH, Q_LORA, B_HEADS, B_NOPE + B_ROPE)
    uq_pe = uq[..., B_NOPE:]
    wq_p = jnp.concatenate([uq[..., :B_NOPE], uq_pe, uq_pe[..., swap]], axis=-1)
    wq_p = wq_p.reshape(DEPTH, Q_LORA, B_HEADS * B_QK).astype(BF16)

    ukv = w_ukv.reshape(DEPTH, KV_LORA, B_HEADS, B_NOPE + B_V)
    wkv_p = jnp.concatenate([ukv[..., :B_NOPE].reshape(DEPTH, KV_LORA, B_HEADS * B_NOPE),
                             ukv[..., B_NOPE:].reshape(DEPTH, KV_LORA, B_WIDTH)],
                            axis=-1).astype(BF16)

    return w_in_p, wq_p, wkv_p


def _bias_table_kernel(w_ref, o_ref):
    v = pl.program_id(1)
    r = lax.broadcasted_iota(jnp.int32, (A_TQ, A_WIN), 0)
    t = lax.broadcasted_iota(jnp.int32, (A_TQ, A_WIN), 1)
    dchunk = t // CHUNK - r // CHUNK
    valid = (dchunk >= 0) & (dchunk <= LEFT_CHUNKS) & (v - (A_KBLKS - 1) + t // A_TQ >= 0)
    for hh in range(2):
        w = jnp.broadcast_to(w_ref[hh], (A_TQ, A_BIAS_VEC))
        tab = pltpu.roll(w, 0, axis=1, stride=1, stride_axis=0)[:, :A_WIN]
        o_ref[0, 0, :, hh * A_TQ:(hh + 1) * A_TQ] = jnp.where(valid, tab * LOG2E, -jnp.inf).T


def _relpos_bias_tables(rel_bias):
    u = jnp.arange(A_BIAS_VEC)
    delta = jnp.where(u <= A_WIN, u, u - A_BIAS_VEC)
    idx = jnp.clip(LEFT_CHUNKS * CHUNK - delta, -REL_CLIP, REL_CLIP) + REL_CLIP
    vec = rel_bias[:, :, idx].reshape(DEPTH * A_HEADS, 1, A_BIAS_VEC).astype(F32)
    return pl.pallas_call(
        _bias_table_kernel,
        out_shape=jax.ShapeDtypeStruct((DEPTH * A_HEADS // 2, A_KBLKS, A_WIN, 2 * A_TQ), F32),
        grid=(DEPTH * A_HEADS // 2, A_KBLKS),
        in_specs=[pl.BlockSpec((2, 1, A_BIAS_VEC), lambda i, v: (i, 0, 0))],
        out_specs=pl.BlockSpec((1, 1, A_WIN, 2 * A_TQ), lambda i, v: (i, v, 0, 0)),
        compiler_params=_cparams(("arbitrary", "arbitrary"), 32 << 20),
        name="bias_table",
    )(vec)


def kernel(x, mem, positions, norm_mix, w_in, rel_bias, q_norm, kv_norm, w_uq, w_ukv, w_out,
           norm_mem, mem_norm, w_xq, w_xkv, w_xo, norm_ffn, w_gate, w_up, w_down, norm_final):
    b, s, d = x.shape
    m = b * s
    assert (b, s, d) == (4, 4096, D_MODEL) and mem.shape == (b, N_MEM, d)

    w_in_p, wq_p, wkv_p = _prep_weights(w_in, w_uq, w_ukv)
    w_xq_b = w_xq.astype(BF16)
    w_xo_b = w_xo.astype(BF16)
    bias_tab = _relpos_bias_tables(rel_bias)

    half = B_ROPE // 2
    inv = ROPE_THETA ** (-jnp.arange(half, dtype=F32) / half)
    inv_row = jnp.tile(inv, 4)[None, :]
    sign_row = jnp.concatenate([jnp.ones((B_ROPE,), F32), -jnp.ones((half,), F32),
                                jnp.ones((half,), F32)])[None, :]
    cs = _rope_table(positions.reshape(m, 1), inv_row, sign_row, tm=2048)

    kvm = _mem_kv(mem.reshape(b * N_MEM, d), mem_norm[None, :], w_xkv)

    norm_mix, norm_mem, norm_ffn, q_norm, kv_norm = (
        g[:, None, :] for g in (norm_mix, norm_mem, norm_ffn, q_norm, kv_norm))
    xf = x.reshape(m, d)
    for l in range(DEPTH):
        proj = _norm_matmul(xf, norm_mix, w_in_p, l, tm=1024, tn=1536, name=f"in_proj_{l}")
        qt, kn, vt, kpe, w_out_l = _mla_prep(proj, cs, q_norm, kv_norm, wq_p, wkv_p, w_out, l,
                                             seq=s, tm=1024, name=f"mla_prep_{l}")
        ob, w_gate_l, w_up_l, oa = _mixers(proj, bias_tab, qt, kn, kpe, vt, w_gate, w_up, l,
                                           batch=b, seq=s, name=f"mixers_{l}")
        xf, h_ffn = _cross(xf, oa, ob, w_out_l, norm_mem, w_xq_b, kvm, w_xo_b, norm_ffn, l,
                           seq=s, tm=512, name=f"out_cross_{l}")
        act, w_down_l = _swiglu_gu(h_ffn, w_gate_l, w_up_l, w_down, l, tm=1024, tn=512,
                                   name=f"swiglu_gu_{l}")
        xf = _mm_res([act], w_down_l, xf, 0, tm=1024, tn=512, name=f"swiglu_down_{l}")
    out = _final_norm(xf, norm_final[None, :], tm=512)
    return out.reshape(b, s, d)
```

```python
import functools
import math

import jax
import jax.numpy as jnp
from jax import lax
from jax.experimental import pallas as pl
from jax.experimental.pallas import tpu as pltpu

D_MODEL = 2048
DEPTH = 4
CHUNK = 64
LEFT_CHUNKS = 8
A_HEAD_DIM = 128
A_HEADS = 8
A_WIDTH = A_HEADS * A_HEAD_DIM
REL_CLIP = 128
B_NOPE = 128
B_ROPE = 64
B_V = 128
B_HEADS = 8
B_WIDTH = B_HEADS * B_V
Q_LORA = 768
KV_LORA = 512
ROPE_THETA = 10000.0
N_MEM = 256
X_HEADS = 4
X_HEAD_DIM = 128
X_WIDTH = X_HEADS * X_HEAD_DIM
X_ROW_PARTS = 2
D_FF = 5632
EPS = 1e-6

V7X_LANES = 128
V7X_VMEM_BYTES = 64 * 1024 * 1024
V7X_VMEM_REQUEST_CAP = 56 * 1024 * 1024

BF16 = jnp.bfloat16
F32 = jnp.float32

IN_QLAT_OFF = 3 * A_WIDTH
IN_ROPE_OFF = IN_QLAT_OFF + Q_LORA
IN_KVLAT_OFF = IN_ROPE_OFF + 2 * V7X_LANES
IN_PAD_WIDTH = IN_KVLAT_OFF + KV_LORA

A_TQ = 4 * CHUNK
A_WIN = A_TQ + LEFT_CHUNKS * CHUNK
A_KBLKS = A_WIN // A_TQ
A_BIAS_VEC = 1024
A_LOOKAHEAD = 2

B_TQ = 512
B_TK = 512
B_HEADS_PER_STEP = 4
B_LOOKAHEAD = 2
B_RED_SLABS = 8
B_QK = B_NOPE + 2 * B_ROPE

LOG2E = math.log2(math.e)
A_SCORE_SCALE = A_HEAD_DIM ** -0.5 * LOG2E


def _cparams(sem, vmem_bytes):
    limit = int(min(max(vmem_bytes, 32 * 1024 * 1024), V7X_VMEM_REQUEST_CAP))
    return pltpu.CompilerParams(dimension_semantics=sem, vmem_limit_bytes=limit)


def _rms_rows(x, g):
    ms = jnp.mean(x * x, axis=-1, keepdims=True)
    return (x * lax.rsqrt(ms + EPS)) * g


def _norm_into(h_ref, x_ref, g_ref, rows=128):
    for r0 in range(0, x_ref.shape[0], rows):
        x = x_ref[r0:r0 + rows, :].astype(F32)
        h_ref[r0:r0 + rows, :] = _rms_rows(x, g_ref[...]).astype(h_ref.dtype)


def _norm_matmul_kernel(x_ref, g_ref, w_ref, o_ref, h_ref):
    @pl.when(pl.program_id(1) == 0)
    def _():
        _norm_into(h_ref, x_ref, g_ref)

    o_ref[...] = jnp.dot(h_ref[...], w_ref[...], preferred_element_type=F32).astype(o_ref.dtype)


def _norm_matmul(x, g, w, layer, *, tm, tn, name):
    m, k = x.shape
    n = w.shape[2]
    xb = x.dtype.itemsize
    est = 2 * tm * k * xb + tm * k * 2 + 2 * k * tn * 2 + 2 * tm * tn * 2 + tm * tn * 4 + (4 << 20)
    return pl.pallas_call(
        _norm_matmul_kernel,
        out_shape=jax.ShapeDtypeStruct((m, n), BF16),
        grid=(m // tm, n // tn),
        in_specs=[
            pl.BlockSpec((tm, k), lambda i, j: (i, 0)),
            pl.BlockSpec((None, 1, k), lambda i, j: (layer, 0, 0)),
            pl.BlockSpec((None, k, tn), lambda i, j: (layer, 0, j)),
        ],
        out_specs=pl.BlockSpec((tm, tn), lambda i, j: (i, j)),
        scratch_shapes=[pltpu.VMEM((tm, k), BF16)],
        compiler_params=_cparams(("arbitrary", "arbitrary"), est),
        name=name,
    )(x, g, w)


def _mem_kv_kernel(x_ref, g_ref, w_ref, o_ref, h_ref):
    @pl.when(pl.program_id(0) == 0)
    def _():
        _norm_into(h_ref, x_ref, g_ref)

    o_ref[...] = jnp.dot(h_ref[...], w_ref[...].astype(BF16),
                         preferred_element_type=F32).astype(o_ref.dtype)


def _mem_kv(mem2d, g, w_xkv):
    m, k = mem2d.shape
    layers, _, n = w_xkv.shape
    est = 2 * m * k * 4 + m * k * 2 + 2 * k * n * 4 + k * n * 2 + 2 * m * n * 2 + m * n * 4 + (4 << 20)
    return pl.pallas_call(
        _mem_kv_kernel,
        out_shape=jax.ShapeDtypeStruct((m, layers * n), BF16),
        grid=(layers,),
        in_specs=[
            pl.BlockSpec((m, k), lambda l: (0, 0)),
            pl.BlockSpec((1, k), lambda l: (0, 0)),
            pl.BlockSpec((None, k, n), lambda l: (l, 0, 0)),
        ],
        out_specs=pl.BlockSpec((m, n), lambda l: (0, l)),
        scratch_shapes=[pltpu.VMEM((m, k), BF16)],
        compiler_params=_cparams(("arbitrary",), est),
        name="mem_kv",
    )(mem2d, g, w_xkv)


def _mm_res_kernel(*refs, n_in):
    a_refs = refs[:n_in]
    w_refs = refs[n_in:2 * n_in]
    r_ref = refs[2 * n_in]
    o_ref = refs[2 * n_in + 1]
    acc = r_ref[...]
    for a_ref, w_ref in zip(a_refs, w_refs):
        acc = acc + jnp.dot(a_ref[...], w_ref[...], preferred_element_type=F32)
    o_ref[...] = acc


def _mm_res(a_list, w, res, layer, *, tm, tn, name):
    m, n = res.shape
    n_in = len(a_list)
    in_specs = []
    w_specs = []
    est = (4 + 2 * n_in) * tm * tn * 4 + (4 << 20)
    for idx, a in enumerate(a_list):
        k = a.shape[1]
        assert w.shape[1] == n_in * k
        in_specs.append(pl.BlockSpec((tm, k), lambda i, j: (i, 0)))
        w_specs.append(pl.BlockSpec((None, k, tn), lambda i, j, idx=idx: (layer, idx, j)))
        est += 2 * tm * k * 2 + 2 * k * tn * 2
    in_specs += w_specs
    in_specs.append(pl.BlockSpec((tm, tn), lambda i, j: (i, j)))
    return pl.pallas_call(
        functools.partial(_mm_res_kernel, n_in=n_in),
        out_shape=jax.ShapeDtypeStruct((m, n), F32),
        grid=(m // tm, n // tn),
        in_specs=in_specs,
        out_specs=pl.BlockSpec((tm, tn), lambda i, j: (i, j)),
        compiler_params=_cparams(("arbitrary", "arbitrary"), est),
        name=name,
    )(*a_list, *([w] * n_in), res)


def _swiglu_gu_kernel(h_ref, wg_ref, wu_ref, wd_ref, o_ref, wdb_ref):
    h = h_ref[...]
    gate = jnp.dot(h, wg_ref[...], preferred_element_type=F32)
    up = jnp.dot(h, wu_ref[...], preferred_element_type=F32)
    o_ref[...] = (gate * (1.0 / (1.0 + jnp.exp(-gate))) * up).astype(o_ref.dtype)
    wdb_ref[...] = wd_ref[...].astype(BF16)


def _swiglu_gu(h, wg, wu, wd, layer, *, tm, tn, name):
    m, k = h.shape
    n = wg.shape[2]
    steps = (m // tm) * (n // tn)
    nj = n // tn
    wd_rows, wd_cols = wd.shape[1], wd.shape[2]
    slab = wd_rows // steps
    assert slab * steps == wd_rows and slab % 16 == 0
    est = (2 * tm * k * 2 + 4 * k * tn * 2 + 2 * tm * tn * 2 + 3 * tm * tn * 4
           + 2 * slab * wd_cols * 6 + (4 << 20))
    return pl.pallas_call(
        _swiglu_gu_kernel,
        out_shape=(jax.ShapeDtypeStruct((m, n), BF16),
                   jax.ShapeDtypeStruct((1, wd_rows, wd_cols), BF16)),
        grid=(m // tm, nj),
        in_specs=[
            pl.BlockSpec((tm, k), lambda i, j: (i, 0)),
            pl.BlockSpec((None, k, tn), lambda i, j: (0, 0, j)),
            pl.BlockSpec((None, k, tn), lambda i, j: (0, 0, j)),
            pl.BlockSpec((None, slab, wd_cols), lambda i, j: (layer, i * nj + j, 0)),
        ],
        out_specs=(pl.BlockSpec((tm, tn), lambda i, j: (i, j)),
                   pl.BlockSpec((None, slab, wd_cols), lambda i, j: (0, i * nj + j, 0))),
        compiler_params=_cparams(("arbitrary", "arbitrary"), est),
        name=name,
    )(h, wg, wu, wd)


def _rope_table_kernel(pos_ref, inv_ref, sign_ref, cs_ref):
    ang = pos_ref[...].astype(F32) * inv_ref[...]
    lane = lax.broadcasted_iota(jnp.int32, ang.shape, 1)
    cs_ref[...] = jnp.where(lane < B_ROPE, jnp.cos(ang), jnp.sin(ang) * sign_ref[...])


def _rope_table(pos_col, inv_row, sign_row, *, tm):
    m = pos_col.shape[0]
    return pl.pallas_call(
        _rope_table_kernel,
        out_shape=jax.ShapeDtypeStruct((m, V7X_LANES), F32),
        grid=(m // tm,),
        in_specs=[
            pl.BlockSpec((tm, 1), lambda i: (i, 0)),
            pl.BlockSpec((1, V7X_LANES), lambda i: (0, 0)),
            pl.BlockSpec((1, V7X_LANES), lambda i: (0, 0)),
        ],
        out_specs=pl.BlockSpec((tm, V7X_LANES), lambda i: (i, 0)),
        compiler_params=_cparams(("arbitrary",), 32 << 20),
        name="rope_table",
    )(pos_col, inv_row, sign_row)


def _rope_pair_sum(slot, cs):
    t = slot * cs
    return t + pltpu.roll(t, B_ROPE, axis=1)


def _mla_prep_kernel(ql_ref, rope_ref, kvl_ref, cs_ref, gq_ref, gkv_ref, wq_ref, wkv_ref, wo_ref,
                     qt_ref, kn_ref, vt_ref, kpe_ref, wob_ref, *, q_scale):
    cs = cs_ref[...]
    hq = _rms_rows(ql_ref[...].astype(F32), gq_ref[...]).astype(BF16)
    qf = jnp.dot(hq, wq_ref[...], preferred_element_type=F32)
    for h in range(B_HEADS):
        base = h * B_QK
        nope = qf[:, base:base + B_NOPE]
        pe = _rope_pair_sum(qf[:, base + B_NOPE:base + B_QK], cs)
        qt_ref[base:base + B_NOPE, :] = (nope * q_scale).T.astype(BF16)
        qt_ref[base + B_NOPE:base + B_QK, :] = (pe * q_scale).T.astype(BF16)

    hkv = _rms_rows(kvl_ref[...].astype(F32), gkv_ref[...]).astype(BF16)
    kvf = jnp.dot(hkv, wkv_ref[...], preferred_element_type=F32)
    kn_ref[...] = kvf[:, :B_HEADS * B_NOPE].astype(BF16)
    vt_ref[...] = kvf[:, B_HEADS * B_NOPE:].T.astype(BF16)

    kpe = _rope_pair_sum(rope_ref[...].astype(F32), cs)
    lane = lax.broadcasted_iota(jnp.int32, kpe.shape, 1)
    kpe_ref[...] = jnp.where(lane < B_ROPE, kpe, 0.0).astype(BF16)
    wob_ref[...] = wo_ref[...].astype(BF16)


def _mla_prep(proj, cs, gq, gkv, wq, wkv, wo, layer, *, seq, tm, name):
    m = proj.shape[0]
    tiles_per_seq = seq // tm
    q_scale = (B_NOPE + B_ROPE) ** -0.5 * LOG2E
    wo_rows, wo_cols = wo.shape[1], wo.shape[2]
    slab = wo_rows // (m // tm)
    assert slab * (m // tm) == wo_rows and slab % 16 == 0
    est = (2 * tm * (Q_LORA + KV_LORA + 128) * 2 + 2 * tm * 128 * 4
           + 2 * (Q_LORA + KV_LORA) * 2048 * 2 + 2 * slab * wo_cols * 6
           + 2 * tm * (2048 + 1024 + 1024 + 128) * 2 + 3 * tm * 2048 * 4 + (4 << 20))
    return pl.pallas_call(
        functools.partial(_mla_prep_kernel, q_scale=q_scale),
        out_shape=(
            jax.ShapeDtypeStruct((m // seq * B_HEADS * B_QK, seq), BF16),
            jax.ShapeDtypeStruct((m, B_HEADS * B_NOPE), BF16),
            jax.ShapeDtypeStruct((m // seq * B_WIDTH, seq), BF16),
            jax.ShapeDtypeStruct((m, V7X_LANES), BF16),
            jax.ShapeDtypeStruct((1, wo_rows, wo_cols), BF16),
        ),
        grid=(m // tm,),
        in_specs=[
            pl.BlockSpec((tm, Q_LORA), lambda i: (i, IN_QLAT_OFF // Q_LORA)),
            pl.BlockSpec((tm, V7X_LANES), lambda i: (i, IN_ROPE_OFF // V7X_LANES)),
            pl.BlockSpec((tm, KV_LORA), lambda i: (i, IN_KVLAT_OFF // KV_LORA)),
            pl.BlockSpec((tm, V7X_LANES), lambda i: (i, 0)),
            pl.BlockSpec((None, 1, Q_LORA), lambda i: (layer, 0, 0)),
            pl.BlockSpec((None, 1, KV_LORA), lambda i: (layer, 0, 0)),
            pl.BlockSpec((None, Q_LORA, B_HEADS * B_QK), lambda i: (layer, 0, 0)),
            pl.BlockSpec((None, KV_LORA, 2 * B_WIDTH), lambda i: (layer, 0, 0)),
            pl.BlockSpec((None, slab, wo_cols), lambda i: (layer, i, 0)),
        ],
        out_specs=(
            pl.BlockSpec((B_HEADS * B_QK, tm), lambda i: (i // tiles_per_seq, i % tiles_per_seq)),
            pl.BlockSpec((tm, B_HEADS * B_NOPE), lambda i: (i, 0)),
            pl.BlockSpec((B_WIDTH, tm), lambda i: (i // tiles_per_seq, i % tiles_per_seq)),
            pl.BlockSpec((tm, V7X_LANES), lambda i: (i, 0)),
            pl.BlockSpec((None, slab, wo_cols), lambda i: (0, i, 0)),
        ),
        compiler_params=_cparams(("arbitrary",), est),
        name=name,
    )(proj, proj, proj, cs, gq, gkv, wq, wkv, wo)


class _MixerAHeads:
    def __init__(self, q_ref, k_refs, v_refs, bias_ref, o_ref):
        self.q_ref, self.k_refs, self.v_refs = q_ref, k_refs, v_refs
        self.bias_ref, self.o_ref = bias_ref, o_ref
        self.tiles = [None] * A_HEADS
        self.end = 0

    def _scores(self, h):
        hs = slice(h * A_HEAD_DIM, (h + 1) * A_HEAD_DIM)
        k = jnp.concatenate([r[:, hs] for r in self.k_refs], axis=0)
        return jnp.dot(k, self.q_ref[:, hs].T, preferred_element_type=F32)

    def begin(self, first, end):
        self.end = end
        for h in range(first, min(first + A_LOOKAHEAD, end)):
            self.tiles[h] = self._scores(h)

    def head(self, h):
        d, tq = A_HEAD_DIM, A_TQ
        hs = slice(h * d, (h + 1) * d)
        if h + A_LOOKAHEAD < self.end:
            self.tiles[h + A_LOOKAHEAD] = self._scores(h + A_LOOKAHEAD)
        v = jnp.concatenate([r[:, hs] for r in self.v_refs], axis=0)
        st = self.bias_ref[h // 2][:, (h % 2) * tq:(h % 2 + 1) * tq] + self.tiles[h]
        self.tiles[h] = None
        m = _col_reduce(st, jnp.max)
        p = jnp.exp2(st - m)
        l = _col_reduce(p, jnp.sum)
        ot = lax.dot_general(v, p.astype(BF16), (((0,), (0,)), ((), ())),
                             preferred_element_type=F32)
        self.o_ref[:, hs] = (ot / l).T.astype(self.o_ref.dtype)


def _col_reduce(x, op):
    rows, cols = x.shape
    slabs = op(x.reshape(B_RED_SLABS, rows // B_RED_SLABS, cols), axis=0)
    return op(slabs, axis=0, keepdims=True)


def _mixers_kernel(q_ref, kn_ref, kpe_ref, vt_ref, wg_ref, wu_ref,
                   qa_ref, ka0_ref, ka1_ref, ka2_ref, va0_ref, va1_ref, va2_ref, bias_ref,
                   o_ref, wgb_ref, wub_ref, oa_ref,
                   m_ref, l_ref, acc_ref, st_ref):
    i = pl.program_id(2)
    hp = B_HEADS_PER_STEP
    m_ref[...] = jnp.full(m_ref.shape, -jnp.inf, F32)
    l_ref[...] = jnp.zeros(l_ref.shape, F32)
    acc_ref[...] = jnp.zeros(acc_ref.shape, F32)

    def scores(j, h):
        r0 = pl.multiple_of(j * B_TK, B_TK)
        k = jnp.concatenate([kn_ref[pl.ds(r0, B_TK), h * B_NOPE:(h + 1) * B_NOPE],
                             kpe_ref[pl.ds(r0, B_TK), :]], axis=1)
        return jnp.dot(k, q_ref[h * B_QK:(h + 1) * B_QK, :], preferred_element_type=F32)

    def softmax_pv(j, h, st, allowed):
        r0 = pl.multiple_of(j * B_TK, B_TK)
        if allowed is not None:
            st = jnp.where(allowed, st, -jnp.inf)
        m_prev = m_ref[h]
        m_new = jnp.maximum(m_prev, _col_reduce(st, jnp.max))
        alpha = jnp.exp2(m_prev - m_new)
        p = jnp.exp2(st - m_new)
        l_ref[h] = alpha * l_ref[h] + _col_reduce(p, jnp.sum)
        pv = jnp.dot(vt_ref[h * B_V:(h + 1) * B_V, pl.ds(r0, B_TK)], p.astype(BF16),
                     preferred_element_type=F32)
        acc_ref[h] = alpha * acc_ref[h] + pv
        m_ref[h] = m_new

    def key_block(j, masked, after_head=None):
        allowed = None
        if masked:
            krow = lax.broadcasted_iota(jnp.int32, (B_TK, B_TQ), 0)
            qcol = lax.broadcasted_iota(jnp.int32, (B_TK, B_TQ), 1)
            allowed = (krow // CHUNK) <= (qcol // CHUNK)
        look = B_LOOKAHEAD
        tiles = [st_ref[h] for h in range(look)] + [None] * (hp - look)
        for h in range(hp):
            ahead = h + look
            if ahead < hp:
                tiles[ahead] = scores(j, ahead)
            elif not masked:
                st_ref[ahead - hp] = scores(j + 1, ahead - hp)
            softmax_pv(j, h, tiles[h], allowed)
            tiles[h] = None
            if after_head is not None:
                after_head(h)

    def body_quad(u, carry):
        for t in range(4):
            key_block(4 * u + t, False)
        return carry

    mixer_a = _MixerAHeads(qa_ref, (ka0_ref, ka1_ref, ka2_ref), (va0_ref, va1_ref, va2_ref),
                           bias_ref, oa_ref)

    for h in range(B_LOOKAHEAD):
        st_ref[h] = scores(0, h)
    n_quads = lax.shift_right_logical(i, 2)
    lax.fori_loop(0, n_quads, body_quad, 0)
    done = n_quads * 4

    @pl.when(lax.bitwise_and(i, 2) != 0)
    def _():
        key_block(done, False)
        key_block(done + 1, False)

    @pl.when(lax.bitwise_and(i, 1) != 0)
    def _():
        key_block(i - 1, False)

    mixer_a.begin(0, A_HEADS)

    def mixer_a_heads(h):
        for ha in range(h * A_HEADS // hp, (h + 1) * A_HEADS // hp):
            mixer_a.head(ha)

    key_block(i, True, after_head=mixer_a_heads)

    for h in range(hp):
        o_ref[:, h * B_V:(h + 1) * B_V] = (acc_ref[h] / l_ref[h]).T.astype(o_ref.dtype)

    wgb_ref[...] = wg_ref[...].astype(BF16)
    wub_ref[...] = wu_ref[...].astype(BF16)


def _mixers(proj, bias, qt, kn, kpe, vt, wg, wu, layer, *, batch, seq, name):
    m = kn.shape[0]
    nq = seq // B_TQ
    hp = B_HEADS_PER_STEP
    ng = B_HEADS // hp
    steps = batch * ng * nq
    nqa = seq // A_TQ
    assert steps == batch * nqa
    w_rows, w_cols = wg.shape[1], wg.shape[2]
    slab = w_rows // steps
    assert slab * steps == w_rows and slab % 16 == 0 and wu.shape == wg.shape
    est = (2 * B_TQ * hp * B_QK * 2 + 2 * seq * hp * (B_NOPE + B_V) * 2 + 2 * seq * 128 * 2
           + 2 * B_TQ * hp * B_V * 2 + hp * (B_V + 16) * B_TQ * 4 + 8 * B_TQ * B_TK * 4
           + 4 * slab * w_cols * 6
           + 2 * 8 * A_TQ * A_WIDTH * 2 + 2 * A_HEADS * A_TQ * A_WIN * 4 + 6 * A_WIN * A_TQ * 4
           + (4 << 20))

    def slab_in(b, g, i):
        return (layer, (b * ng + g) * nq + i, 0)

    def slab_out(b, g, i):
        return (0, (b * ng + g) * nq + i, 0)

    def a_blk(b, g, i):
        return g * nq + i

    def a_kv_spec(col_blk, c):
        back = A_KBLKS - 1 - c
        return pl.BlockSpec((A_TQ, A_WIDTH),
                            lambda b, g, i: (b * nqa + jnp.maximum(a_blk(b, g, i) - back, 0), col_blk))

    return pl.pallas_call(
        _mixers_kernel,
        out_shape=(jax.ShapeDtypeStruct((m, B_WIDTH), BF16),
                   jax.ShapeDtypeStruct((1, w_rows, w_cols), BF16),
                   jax.ShapeDtypeStruct((1, w_rows, w_cols), BF16),
                   jax.ShapeDtypeStruct((m, A_WIDTH), BF16)),
        grid=(batch, ng, nq),
        in_specs=[
            pl.BlockSpec((hp * B_QK, B_TQ), lambda b, g, i: (b * ng + g, i)),
            pl.BlockSpec((seq, hp * B_NOPE), lambda b, g, i: (b, g)),
            pl.BlockSpec((seq, V7X_LANES), lambda b, g, i: (b, 0)),
            pl.BlockSpec((hp * B_V, seq), lambda b, g, i: (b * ng + g, 0)),
            pl.BlockSpec((None, slab, w_cols), slab_in),
            pl.BlockSpec((None, slab, w_cols), slab_in),
            pl.BlockSpec((A_TQ, A_WIDTH), lambda b, g, i: (b * nqa + a_blk(b, g, i), 0)),
            a_kv_spec(1, 0), a_kv_spec(1, 1), a_kv_spec(1, 2),
            a_kv_spec(2, 0), a_kv_spec(2, 1), a_kv_spec(2, 2),
            pl.BlockSpec((A_HEADS // 2, None, A_WIN, 2 * A_TQ),
                         lambda b, g, i: (layer, jnp.minimum(a_blk(b, g, i), A_KBLKS - 1), 0, 0)),
        ],
        out_specs=(pl.BlockSpec((B_TQ, hp * B_V), lambda b, g, i: (b * nq + i, g)),
                   pl.BlockSpec((None, slab, w_cols), slab_out),
                   pl.BlockSpec((None, slab, w_cols), slab_out),
                   pl.BlockSpec((A_TQ, A_WIDTH), lambda b, g, i: (b * nqa + a_blk(b, g, i), 0))),
        scratch_shapes=[
            pltpu.VMEM((hp, 1, B_TQ), F32),
            pltpu.VMEM((hp, 1, B_TQ), F32),
            pltpu.VMEM((hp, B_V, B_TQ), F32),
            pltpu.VMEM((B_LOOKAHEAD, B_TK, B_TQ), F32),
        ],
        compiler_params=_cparams(("arbitrary", "arbitrary", "arbitrary"), est),
        name=name,
    )(qt, kn, kpe, vt, wg, wu, proj, proj, proj, proj, proj, proj, proj, bias)


def _cross_kernel(x_ref, oa_ref, ob_ref, woa_ref, wob_ref, g_ref, wq_ref, k_ref, v_ref, wo_ref,
                  gf_ref, o_ref, hf_ref):
    scale = X_HEAD_DIM ** -0.5
    rows = x_ref.shape[0] // X_ROW_PARTS
    for r in range(X_ROW_PARTS):
        rs = slice(r * rows, (r + 1) * rows)
        x = (x_ref[rs, :] + jnp.dot(oa_ref[rs, :], woa_ref[...], preferred_element_type=F32)
             + jnp.dot(ob_ref[rs, :], wob_ref[...], preferred_element_type=F32))
        hn = _rms_rows(x, g_ref[...]).astype(BF16)
        q = (jnp.dot(hn, wq_ref[...], preferred_element_type=F32) * scale).astype(BF16)
        outs = []
        for h in range(X_HEADS):
            hs = slice(h * X_HEAD_DIM, (h + 1) * X_HEAD_DIM)
            s = lax.dot_general(q[:, hs], k_ref[:, hs], (((1,), (1,)), ((), ())),
                                preferred_element_type=F32)
            m = jnp.max(s, axis=1, keepdims=True)
            p = jnp.exp(s - m)
            l = jnp.sum(p, axis=1, keepdims=True)
            o = jnp.dot(p.astype(BF16), v_ref[:, hs], preferred_element_type=F32)
            outs.append((o / l).astype(BF16))
        o_all = jnp.concatenate(outs, axis=1)
        x2 = x + jnp.dot(o_all, wo_ref[...], preferred_element_type=F32)
        o_ref[rs, :] = x2
        hf_ref[rs, :] = _rms_rows(x2, gf_ref[...]).astype(BF16)


def _cross(x, oa, ob, w_out_l, g, wq, kvm, wo, g_ffn, layer, *, seq, tm, name):
    m, d = x.shape
    blocks_per_batch = seq // tm
    est = (4 * tm * d * 4 + tm * d * 2 + 2 * d * X_WIDTH * 2 + 2 * X_WIDTH * d * 2
           + 4 * N_MEM * X_WIDTH * 2 + 4 * tm * N_MEM * 4 + 3 * tm * d * 4
           + 2 * tm * (A_WIDTH + B_WIDTH) * 2 + 2 * (A_WIDTH + B_WIDTH) * d * 2
           + 2 * tm * d * 2 + (4 << 20))
    return pl.pallas_call(
        _cross_kernel,
        out_shape=(jax.ShapeDtypeStruct((m, d), F32), jax.ShapeDtypeStruct((m, d), BF16)),
        grid=(m // tm,),
        in_specs=[
            pl.BlockSpec((tm, d), lambda i: (i, 0)),
            pl.BlockSpec((tm, A_WIDTH), lambda i: (i, 0)),
            pl.BlockSpec((tm, B_WIDTH), lambda i: (i, 0)),
            pl.BlockSpec((None, A_WIDTH, d), lambda i: (0, 0, 0)),
            pl.BlockSpec((None, B_WIDTH, d), lambda i: (0, A_WIDTH // B_WIDTH, 0)),
            pl.BlockSpec((None, 1, d), lambda i: (layer, 0, 0)),
            pl.BlockSpec((None, d, X_WIDTH), lambda i: (layer, 0, 0)),
            pl.BlockSpec((N_MEM, X_WIDTH), lambda i: (i // blocks_per_batch, 2 * layer)),
            pl.BlockSpec((N_MEM, X_WIDTH), lambda i: (i // blocks_per_batch, 2 * layer + 1)),
            pl.BlockSpec((None, X_WIDTH, d), lambda i: (layer, 0, 0)),
            pl.BlockSpec((None, 1, d), lambda i: (layer, 0, 0)),
        ],
        out_specs=(pl.BlockSpec((tm, d), lambda i: (i, 0)),
                   pl.BlockSpec((tm, d), lambda i: (i, 0))),
        compiler_params=_cparams(("arbitrary",), est),
        name=name,
    )(x, oa, ob, w_out_l, w_out_l, g, wq, kvm, kvm, wo, g_ffn)


def _final_norm_kernel(x_ref, g_ref, o_ref):
    o_ref[...] = _rms_rows(x_ref[...], g_ref[...])


def _final_norm(x, g, *, tm):
    m, d = x.shape
    return pl.pallas_call(
        _final_norm_kernel,
        out_shape=jax.ShapeDtypeStruct((m, d), F32),
        grid=(m // tm,),
        in_specs=[pl.BlockSpec((tm, d), lambda i: (i, 0)), pl.BlockSpec((1, d), lambda i: (0, 0))],
        out_specs=pl.BlockSpec((tm, d), lambda i: (i, 0)),
        compiler_params=_cparams(("arbitrary",), 6 * tm * d * 4 + (4 << 20)),
        name="final_norm",
    )(x, g)


RELAYOUT_COLS = 256
RELAYOUT_ROPE_BLK = IN_ROPE_OFF // RELAYOUT_COLS


def _w_in_relayout_kernel(wt_ref, o_ref):
    j = pl.program_id(1)

    @pl.when(j != RELAYOUT_ROPE_BLK)
    def _():
        factor = jnp.where(j < A_WIDTH // RELAYOUT_COLS, A_SCORE_SCALE, 1.0)
        o_ref[...] = (wt_ref[...] * factor).T.astype(BF16)

    @pl.when(j == RELAYOUT_ROPE_BLK)
    def _():
        half = B_ROPE // 2
        kr = wt_ref[:B_ROPE, :]
        slot = jnp.concatenate([kr, kr[half:], kr[:half],
                                jnp.zeros((RELAYOUT_COLS - 2 * B_ROPE, kr.shape[1]), F32)], axis=0)
        o_ref[...] = slot.T.astype(BF16)


def _w_in_relayout(w_in_t):
    layers, n, k = w_in_t.shape
    lat_blk = IN_ROPE_OFF // RELAYOUT_COLS
    rope_src = (IN_ROPE_OFF + KV_LORA) // RELAYOUT_COLS
    assert IN_KVLAT_OFF == (RELAYOUT_ROPE_BLK + 1) * RELAYOUT_COLS

    def src_block(l, j):
        return (l, jnp.where(j < RELAYOUT_ROPE_BLK, j,
                             jnp.where(j == RELAYOUT_ROPE_BLK, rope_src, j - 1 + lat_blk - RELAYOUT_ROPE_BLK)), 0)

    return pl.pallas_call(
        _w_in_relayout_kernel,
        out_shape=jax.ShapeDtypeStruct((layers, k, IN_PAD_WIDTH), BF16),
        grid=(layers, IN_PAD_WIDTH // RELAYOUT_COLS),
        in_specs=[pl.BlockSpec((None, RELAYOUT_COLS, k), src_block)],
        out_specs=pl.BlockSpec((None, k, RELAYOUT_COLS), lambda l, j: (l, 0, j)),
        compiler_params=_cparams(("arbitrary", "arbitrary"), 32 << 20),
        name="w_in_relayout",
    )(w_in_t)


def _prep_weights(w_in, w_uq, w_ukv):
    swap = (jnp.arange(B_ROPE) + B_ROPE // 2) % B_ROPE
    w_in_p = _w_in_relayout(jnp.swapaxes(w_in, 1, 2))

    uq = w_uq.reshape(DEPTH, Q_LORA, B_HEADS, B_NOPE + B_ROPE)
    uq_pe = uq[..., B_NOPE:]
    wq_p = jnp.concatenate([uq[..., :B_NOPE], uq_pe, uq_pe[..., swap]], axis=-1)
    wq_p = wq_p.reshape(DEPTH, Q_LORA, B_HEADS * B_QK).astype(BF16)

    ukv = w_ukv.reshape(DEPTH, KV_LORA, B_HEADS, B_NOPE + B_V)
    wkv_p = jnp.concatenate([ukv[..., :B_NOPE].reshape(DEPTH, KV_LORA, B_HEADS * B_NOPE),
                             ukv[..., B_NOPE:].reshape(DEPTH, KV_LORA, B_WIDTH)],
                            axis=-1).astype(BF16)

    return w_in_p, wq_p, wkv_p


def _bias_table_kernel(w_ref, o_ref):
    v = pl.program_id(1)
    r = lax.broadcasted_iota(jnp.int32, (A_TQ, A_WIN), 0)
    t = lax.broadcasted_iota(jnp.int32, (A_TQ, A_WIN), 1)
    dchunk = t // CHUNK - r // CHUNK
    valid = (dchunk >= 0) & (dchunk <= LEFT_CHUNKS) & (v - (A_KBLKS - 1) + t // A_TQ >= 0)
    for hh in range(2):
        w = jnp.broadcast_to(w_ref[hh], (A_TQ, A_BIAS_VEC))
        tab = pltpu.roll(w, 0, axis=1, stride=1, stride_axis=0)[:, :A_WIN]
        o_ref[0, 0, :, hh * A_TQ:(hh + 1) * A_TQ] = jnp.where(valid, tab * LOG2E, -jnp.inf).T


def _relpos_bias_tables(rel_bias):
    u = jnp.arange(A_BIAS_VEC)
    delta = jnp.where(u <= A_WIN, u, u - A_BIAS_VEC)
    idx = jnp.clip(LEFT_CHUNKS * CHUNK - delta, -REL_CLIP, REL_CLIP) + REL_CLIP
    vec = rel_bias[:, :, idx].reshape(DEPTH * A_HEADS, 1, A_BIAS_VEC).astype(F32)
    return pl.pallas_call(
        _bias_table_kernel,
        out_shape=jax.ShapeDtypeStruct((DEPTH * A_HEADS // 2, A_KBLKS, A_WIN, 2 * A_TQ), F32),
        grid=(DEPTH * A_HEADS // 2, A_KBLKS),
        in_specs=[pl.BlockSpec((2, 1, A_BIAS_VEC), lambda i, v: (i, 0, 0))],
        out_specs=pl.BlockSpec((1, 1, A_WIN, 2 * A_TQ), lambda i, v: (i, v, 0, 0)),
        compiler_params=_cparams(("arbitrary", "arbitrary"), 32 << 20),
        name="bias_table",
    )(vec)


def kernel(x, mem, positions, norm_mix, w_in, rel_bias, q_norm, kv_norm, w_uq, w_ukv, w_out,
           norm_mem, mem_norm, w_xq, w_xkv, w_xo, norm_ffn, w_gate, w_up, w_down, norm_final):
    b, s, d = x.shape
    m = b * s
    assert (b, s, d) == (4, 4096, D_MODEL) and mem.shape == (b, N_MEM, d)

    w_in_p, wq_p, wkv_p = _prep_weights(w_in, w_uq, w_ukv)
    w_xq_b = w_xq.astype(BF16)
    w_xo_b = w_xo.astype(BF16)
    bias_tab = _relpos_bias_tables(rel_bias)

    half = B_ROPE // 2
    inv = ROPE_THETA ** (-jnp.arange(half, dtype=F32) / half)
    inv_row = jnp.tile(inv, 4)[None, :]
    sign_row = jnp.concatenate([jnp.ones((B_ROPE,), F32), -jnp.ones((half,), F32),
                                jnp.ones((half,), F32)])[None, :]
    cs = _rope_table(positions.reshape(m, 1), inv_row, sign_row, tm=2048)

    kvm = _mem_kv(mem.reshape(b * N_MEM, d), mem_norm[None, :], w_xkv)

    norm_mix, norm_mem, norm_ffn, q_norm, kv_norm = (
        g[:, None, :] for g in (norm_mix, norm_mem, norm_ffn, q_norm, kv_norm))
    xf = x.reshape(m, d)
    for l in range(DEPTH):
        proj = _norm_matmul(xf, norm_mix, w_in_p, l, tm=1024, tn=1536, name=f"in_proj_{l}")
        qt, kn, vt, kpe, w_out_l = _mla_prep(proj, cs, q_norm, kv_norm, wq_p, wkv_p, w_out, l,
                                             seq=s, tm=1024, name=f"mla_prep_{l}")
        ob, w_gate_l, w_up_l, oa = _mixers(proj, bias_tab, qt, kn, kpe, vt, w_gate, w_up, l,
                                           batch=b, seq=s, name=f"mixers_{l}")
        xf, h_ffn = _cross(xf, oa, ob, w_out_l, norm_mem, w_xq_b, kvm, w_xo_b, norm_ffn, l,
                           seq=s, tm=512, name=f"out_cross_{l}")
        act, w_down_l = _swiglu_gu(h_ffn, w_gate_l, w_up_l, w_down, l, tm=1024, tn=512,
                                   name=f"swiglu_gu_{l}")
        xf = _mm_res([act], w_down_l, xf, 0, tm=1024, tn=512, name=f"swiglu_down_{l}")
    out = _final_norm(xf, norm_final[None, :], tm=512)
    return out.reshape(b, s, d)
```

```python
import functools
import math

import jax
import jax.numpy as jnp
from jax import lax
from jax.experimental import pallas as pl
from jax.experimental.pallas import tpu as pltpu

D_MODEL = 2048
DEPTH = 4
CHUNK = 64
LEFT_CHUNKS = 8
A_HEAD_DIM = 128
A_HEADS = 8
A_WIDTH = A_HEADS * A_HEAD_DIM
REL_CLIP = 128
B_NOPE = 128
B_ROPE = 64
B_V = 128
B_HEADS = 8
B_WIDTH = B_HEADS * B_V
Q_LORA = 768
KV_LORA = 512
ROPE_THETA = 10000.0
N_MEM = 256
X_HEADS = 4
X_HEAD_DIM = 128
X_WIDTH = X_HEADS * X_HEAD_DIM
X_ROW_PARTS = 2
D_FF = 5632
EPS = 1e-6

V7X_LANES = 128
V7X_VMEM_BYTES = 64 * 1024 * 1024
V7X_VMEM_REQUEST_CAP = 56 * 1024 * 1024

BF16 = jnp.bfloat16
F32 = jnp.float32

IN_QLAT_OFF = 3 * A_WIDTH
IN_ROPE_OFF = IN_QLAT_OFF + Q_LORA
IN_KVLAT_OFF = IN_ROPE_OFF + 2 * V7X_LANES
IN_PAD_WIDTH = IN_KVLAT_OFF + KV_LORA

A_TQ = 4 * CHUNK
A_WIN = A_TQ + LEFT_CHUNKS * CHUNK
A_KBLKS = A_WIN // A_TQ
A_BIAS_VEC = 1024
A_LOOKAHEAD = 2

B_TQ = 512
B_TK = 512
B_HEADS_PER_STEP = 4
B_LOOKAHEAD = 2
B_RED_SLABS = 8
B_QK = B_NOPE + 2 * B_ROPE

LOG2E = math.log2(math.e)
A_SCORE_SCALE = A_HEAD_DIM ** -0.5 * LOG2E


def _cparams(sem, vmem_bytes):
    limit = int(min(max(vmem_bytes, 32 * 1024 * 1024), V7X_VMEM_REQUEST_CAP))
    return pltpu.CompilerParams(dimension_semantics=sem, vmem_limit_bytes=limit)


def _rms_rows(x, g):
    ms = jnp.mean(x * x, axis=-1, keepdims=True)
    return (x * lax.rsqrt(ms + EPS)) * g


def _norm_into(h_ref, x_ref, g_ref, rows=128):
    for r0 in range(0, x_ref.shape[0], rows):
        x = x_ref[r0:r0 + rows, :].astype(F32)
        h_ref[r0:r0 + rows, :] = _rms_rows(x, g_ref[...]).astype(h_ref.dtype)


def _norm_matmul_kernel(x_ref, g_ref, w_ref, o_ref, h_ref):
    @pl.when(pl.program_id(1) == 0)
    def _():
        _norm_into(h_ref, x_ref, g_ref)

    o_ref[...] = jnp.dot(h_ref[...], w_ref[...], preferred_element_type=F32).astype(o_ref.dtype)


def _norm_matmul(x, g, w, layer, *, tm, tn, name):
    m, k = x.shape
    n = w.shape[2]
    xb = x.dtype.itemsize
    est = 2 * tm * k * xb + tm * k * 2 + 2 * k * tn * 2 + 2 * tm * tn * 2 + tm * tn * 4 + (4 << 20)
    return pl.pallas_call(
        _norm_matmul_kernel,
        out_shape=jax.ShapeDtypeStruct((m, n), BF16),
        grid=(m // tm, n // tn),
        in_specs=[
            pl.BlockSpec((tm, k), lambda i, j: (i, 0)),
            pl.BlockSpec((None, 1, k), lambda i, j: (layer, 0, 0)),
            pl.BlockSpec((None, k, tn), lambda i, j: (layer, 0, j)),
        ],
        out_specs=pl.BlockSpec((tm, tn), lambda i, j: (i, j)),
        scratch_shapes=[pltpu.VMEM((tm, k), BF16)],
        compiler_params=_cparams(("arbitrary", "arbitrary"), est),
        name=name,
    )(x, g, w)


def _mem_kv_kernel(x_ref, g_ref, w_ref, o_ref, h_ref):
    @pl.when(pl.program_id(0) == 0)
    def _():
        _norm_into(h_ref, x_ref, g_ref)

    o_ref[...] = jnp.dot(h_ref[...], w_ref[...].astype(BF16),
                         preferred_element_type=F32).astype(o_ref.dtype)


def _mem_kv(mem2d, g, w_xkv):
    m, k = mem2d.shape
    layers, _, n = w_xkv.shape
    est = 2 * m * k * 4 + m * k * 2 + 2 * k * n * 4 + k * n * 2 + 2 * m * n * 2 + m * n * 4 + (4 << 20)
    return pl.pallas_call(
        _mem_kv_kernel,
        out_shape=jax.ShapeDtypeStruct((m, layers * n), BF16),
        grid=(layers,),
        in_specs=[
            pl.BlockSpec((m, k), lambda l: (0, 0)),
            pl.BlockSpec((1, k), lambda l: (0, 0)),
            pl.BlockSpec((None, k, n), lambda l: (l, 0, 0)),
        ],
        out_specs=pl.BlockSpec((m, n), lambda l: (0, l)),
        scratch_shapes=[pltpu.VMEM((m, k), BF16)],
        compiler_params=_cparams(("arbitrary",), est),
        name="mem_kv",
    )(mem2d, g, w_xkv)


def _mm_res_kernel(*refs, n_in):
    a_refs = refs[:n_in]
    w_refs = refs[n_in:2 * n_in]
    r_ref = refs[2 * n_in]
    o_ref = refs[2 * n_in + 1]
    acc = r_ref[...]
    for a_ref, w_ref in zip(a_refs, w_refs):
        acc = acc + jnp.dot(a_ref[...], w_ref[...], preferred_element_type=F32)
    o_ref[...] = acc


def _mm_res(a_list, w, res, layer, *, tm, tn, name):
    m, n = res.shape
    n_in = len(a_list)
    in_specs = []
    w_specs = []
    est = (4 + 2 * n_in) * tm * tn * 4 + (4 << 20)
    for idx, a in enumerate(a_list):
        k = a.shape[1]
        assert w.shape[1] == n_in * k
        in_specs.append(pl.BlockSpec((tm, k), lambda i, j: (i, 0)))
        w_specs.append(pl.BlockSpec((None, k, tn), lambda i, j, idx=idx: (layer, idx, j)))
        est += 2 * tm * k * 2 + 2 * k * tn * 2
    in_specs += w_specs
    in_specs.append(pl.BlockSpec((tm, tn), lambda i, j: (i, j)))
    return pl.pallas_call(
        functools.partial(_mm_res_kernel, n_in=n_in),
        out_shape=jax.ShapeDtypeStruct((m, n), F32),
        grid=(m // tm, n // tn),
        in_specs=in_specs,
        out_specs=pl.BlockSpec((tm, tn), lambda i, j: (i, j)),
        compiler_params=_cparams(("arbitrary", "arbitrary"), est),
        name=name,
    )(*a_list, *([w] * n_in), res)


def _swiglu_gu_kernel(h_ref, wg_ref, wu_ref, wd_ref, o_ref, wdb_ref):
    h = h_ref[...]
    gate = jnp.dot(h, wg_ref[...], preferred_element_type=F32)
    up = jnp.dot(h, wu_ref[...], preferred_element_type=F32)
    o_ref[...] = (gate * (1.0 / (1.0 + jnp.exp(-gate))) * up).astype(o_ref.dtype)
    wdb_ref[...] = wd_ref[...].astype(BF16)


def _swiglu_gu(h, wg, wu, wd, layer, *, tm, tn, name):
    m, k = h.shape
    n = wg.shape[2]
    steps = (m // tm) * (n // tn)
    nj = n // tn
    wd_rows, wd_cols = wd.shape[1], wd.shape[2]
    slab = wd_rows // steps
    assert slab * steps == wd_rows and slab % 16 == 0
    est = (2 * tm * k * 2 + 4 * k * tn * 2 + 2 * tm * tn * 2 + 3 * tm * tn * 4
           + 2 * slab * wd_cols * 6 + (4 << 20))
    return pl.pallas_call(
        _swiglu_gu_kernel,
        out_shape=(jax.ShapeDtypeStruct((m, n), BF16),
                   jax.ShapeDtypeStruct((1, wd_rows, wd_cols), BF16)),
        grid=(m // tm, nj),
        in_specs=[
            pl.BlockSpec((tm, k), lambda i, j: (i, 0)),
            pl.BlockSpec((None, k, tn), lambda i, j: (0, 0, j)),
            pl.BlockSpec((None, k, tn), lambda i, j: (0, 0, j)),
            pl.BlockSpec((None, slab, wd_cols), lambda i, j: (layer, i * nj + j, 0)),
        ],
        out_specs=(pl.BlockSpec((tm, tn), lambda i, j: (i, j)),
                   pl.BlockSpec((None, slab, wd_cols), lambda i, j: (0, i * nj + j, 0))),
        compiler_params=_cparams(("arbitrary", "arbitrary"), est),
        name=name,
    )(h, wg, wu, wd)


def _rope_table_kernel(pos_ref, inv_ref, sign_ref, cs_ref):
    ang = pos_ref[...].astype(F32) * inv_ref[...]
    lane = lax.broadcasted_iota(jnp.int32, ang.shape, 1)
    cs_ref[...] = jnp.where(lane < B_ROPE, jnp.cos(ang), jnp.sin(ang) * sign_ref[...])


def _rope_table(pos_col, inv_row, sign_row, *, tm):
    m = pos_col.shape[0]
    return pl.pallas_call(
        _rope_table_kernel,
        out_shape=jax.ShapeDtypeStruct((m, V7X_LANES), F32),
        grid=(m // tm,),
        in_specs=[
            pl.BlockSpec((tm, 1), lambda i: (i, 0)),
            pl.BlockSpec((1, V7X_LANES), lambda i: (0, 0)),
            pl.BlockSpec((1, V7X_LANES), lambda i: (0, 0)),
        ],
        out_specs=pl.BlockSpec((tm, V7X_LANES), lambda i: (i, 0)),
        compiler_params=_cparams(("arbitrary",), 32 << 20),
        name="rope_table",
    )(pos_col, inv_row, sign_row)


def _rope_pair_sum(slot, cs):
    t = slot * cs
    return t + pltpu.roll(t, B_ROPE, axis=1)


def _mla_prep_kernel(ql_ref, rope_ref, kvl_ref, cs_ref, gq_ref, gkv_ref, wq_ref, wkv_ref, wo_ref,
                     qt_ref, kn_ref, vt_ref, kpe_ref, wob_ref, *, q_scale):
    cs = cs_ref[...]
    hq = _rms_rows(ql_ref[...].astype(F32), gq_ref[...]).astype(BF16)
    qf = jnp.dot(hq, wq_ref[...], preferred_element_type=F32)
    for h in range(B_HEADS):
        base = h * B_QK
        nope = qf[:, base:base + B_NOPE]
        pe = _rope_pair_sum(qf[:, base + B_NOPE:base + B_QK], cs)
        qt_ref[base:base + B_NOPE, :] = (nope * q_scale).T.astype(BF16)
        qt_ref[base + B_NOPE:base + B_QK, :] = (pe * q_scale).T.astype(BF16)

    hkv = _rms_rows(kvl_ref[...].astype(F32), gkv_ref[...]).astype(BF16)
    kvf = jnp.dot(hkv, wkv_ref[...], preferred_element_type=F32)
    kn_ref[...] = kvf[:, :B_HEADS * B_NOPE].astype(BF16)
    vt_ref[...] = kvf[:, B_HEADS * B_NOPE:].T.astype(BF16)

    kpe = _rope_pair_sum(rope_ref[...].astype(F32), cs)
    lane = lax.broadcasted_iota(jnp.int32, kpe.shape, 1)
    kpe_ref[...] = jnp.where(lane < B_ROPE, kpe, 0.0).astype(BF16)
    wob_ref[...] = wo_ref[...].astype(BF16)


def _mla_prep(proj, cs, gq, gkv, wq, wkv, wo, layer, *, seq, tm, name):
    m = proj.shape[0]
    tiles_per_seq = seq // tm
    q_scale = (B_NOPE + B_ROPE) ** -0.5 * LOG2E
    wo_rows, wo_cols = wo.shape[1], wo.shape[2]
    slab = wo_rows // (m // tm)
    assert slab * (m // tm) == wo_rows and slab % 16 == 0
    est = (2 * tm * (Q_LORA + KV_LORA + 128) * 2 + 2 * tm * 128 * 4
           + 2 * (Q_LORA + KV_LORA) * 2048 * 2 + 2 * slab * wo_cols * 6
           + 2 * tm * (2048 + 1024 + 1024 + 128) * 2 + 3 * tm * 2048 * 4 + (4 << 20))
    return pl.pallas_call(
        functools.partial(_mla_prep_kernel, q_scale=q_scale),
        out_shape=(
            jax.ShapeDtypeStruct((m // seq * B_HEADS * B_QK, seq), BF16),
            jax.ShapeDtypeStruct((m, B_HEADS * B_NOPE), BF16),
            jax.ShapeDtypeStruct((m // seq * B_WIDTH, seq), BF16),
            jax.ShapeDtypeStruct((m, V7X_LANES), BF16),
            jax.ShapeDtypeStruct((1, wo_rows, wo_cols), BF16),
        ),
        grid=(m // tm,),
        in_specs=[
            pl.BlockSpec((tm, Q_LORA), lambda i: (i, IN_QLAT_OFF // Q_LORA)),
            pl.BlockSpec((tm, V7X_LANES), lambda i: (i, IN_ROPE_OFF // V7X_LANES)),
            pl.BlockSpec((tm, KV_LORA), lambda i: (i, IN_KVLAT_OFF // KV_LORA)),
            pl.BlockSpec((tm, V7X_LANES), lambda i: (i, 0)),
            pl.BlockSpec((None, 1, Q_LORA), lambda i: (layer, 0, 0)),
            pl.BlockSpec((None, 1, KV_LORA), lambda i: (layer, 0, 0)),
            pl.BlockSpec((None, Q_LORA, B_HEADS * B_QK), lambda i: (layer, 0, 0)),
            pl.BlockSpec((None, KV_LORA, 2 * B_WIDTH), lambda i: (layer, 0, 0)),
            pl.BlockSpec((None, slab, wo_cols), lambda i: (layer, i, 0)),
        ],
        out_specs=(
            pl.BlockSpec((B_HEADS * B_QK, tm), lambda i: (i // tiles_per_seq, i % tiles_per_seq)),
            pl.BlockSpec((tm, B_HEADS * B_NOPE), lambda i: (i, 0)),
            pl.BlockSpec((B_WIDTH, tm), lambda i: (i // tiles_per_seq, i % tiles_per_seq)),
            pl.BlockSpec((tm, V7X_LANES), lambda i: (i, 0)),
            pl.BlockSpec((None, slab, wo_cols), lambda i: (0, i, 0)),
        ),
        compiler_params=_cparams(("arbitrary",), est),
        name=name,
    )(proj, proj, proj, cs, gq, gkv, wq, wkv, wo)


class _MixerAHeads:
    def __init__(self, q_ref, k_refs, v_refs, bias_ref, o_ref):
        self.q_ref, self.k_refs, self.v_refs = q_ref, k_refs, v_refs
        self.bias_ref, self.o_ref = bias_ref, o_ref
        self.tiles = [None] * A_HEADS
        self.end = 0

    def _scores(self, h):
        hs = slice(h * A_HEAD_DIM, (h + 1) * A_HEAD_DIM)
        k = jnp.concatenate([r[:, hs] for r in self.k_refs], axis=0)
        return jnp.dot(k, self.q_ref[:, hs].T, preferred_element_type=F32)

    def begin(self, first, end):
        self.end = end
        for h in range(first, min(first + A_LOOKAHEAD, end)):
            self.tiles[h] = self._scores(h)

    def head(self, h):
        d, tq = A_HEAD_DIM, A_TQ
        hs = slice(h * d, (h + 1) * d)
        if h + A_LOOKAHEAD < self.end:
            self.tiles[h + A_LOOKAHEAD] = self._scores(h + A_LOOKAHEAD)
        v = jnp.concatenate([r[:, hs] for r in self.v_refs], axis=0)
        st = self.bias_ref[h // 2][:, (h % 2) * tq:(h % 2 + 1) * tq] + self.tiles[h]
        self.tiles[h] = None
        m = _col_reduce(st, jnp.max)
        p = jnp.exp2(st - m)
        l = _col_reduce(p, jnp.sum)
        ot = lax.dot_general(v, p.astype(BF16), (((0,), (0,)), ((), ())),
                             preferred_element_type=F32)
        self.o_ref[:, hs] = (ot / l).T.astype(self.o_ref.dtype)


def _col_reduce(x, op):
    rows, cols = x.shape
    slabs = op(x.reshape(B_RED_SLABS, rows // B_RED_SLABS, cols), axis=0)
    return op(slabs, axis=0, keepdims=True)


def _mixers_kernel(q_ref, kn_ref, kpe_ref, vt_ref, wg_ref, wu_ref,
                   qa_ref, ka0_ref, ka1_ref, ka2_ref, va0_ref, va1_ref, va2_ref, bias_ref,
                   o_ref, wgb_ref, wub_ref, oa_ref,
                   m_ref, l_ref, acc_ref, st_ref):
    i = pl.program_id(2)
    hp = B_HEADS_PER_STEP
    m_ref[...] = jnp.full(m_ref.shape, -jnp.inf, F32)
    l_ref[...] = jnp.zeros(l_ref.shape, F32)
    acc_ref[...] = jnp.zeros(acc_ref.shape, F32)

    def scores(j, h):
        r0 = pl.multiple_of(j * B_TK, B_TK)
        k = jnp.concatenate([kn_ref[pl.ds(r0, B_TK), h * B_NOPE:(h + 1) * B_NOPE],
                             kpe_ref[pl.ds(r0, B_TK), :]], axis=1)
        return jnp.dot(k, q_ref[h * B_QK:(h + 1) * B_QK, :], preferred_element_type=F32)

    def softmax_pv(j, h, st, allowed):
        r0 = pl.multiple_of(j * B_TK, B_TK)
        if allowed is not None:
            st = jnp.where(allowed, st, -jnp.inf)
        m_prev = m_ref[h]
        m_new = jnp.maximum(m_prev, _col_reduce(st, jnp.max))
        alpha = jnp.exp2(m_prev - m_new)
        p = jnp.exp2(st - m_new)
        l_ref[h] = alpha * l_ref[h] + _col_reduce(p, jnp.sum)
        pv = jnp.dot(vt_ref[h * B_V:(h + 1) * B_V, pl.ds(r0, B_TK)], p.astype(BF16),
                     preferred_element_type=F32)
        acc_ref[h] = alpha * acc_ref[h] + pv
        m_ref[h] = m_new

    def key_block(j, masked, after_head=None):
        allowed = None
        if masked:
            krow = lax.broadcasted_iota(jnp.int32, (B_TK, B_TQ), 0)
            qcol = lax.broadcasted_iota(jnp.int32, (B_TK, B_TQ), 1)
            allowed = (krow // CHUNK) <= (qcol // CHUNK)
        look = B_LOOKAHEAD
        tiles = [st_ref[h] for h in range(look)] + [None] * (hp - look)
        for h in range(hp):
            ahead = h + look
            if ahead < hp:
                tiles[ahead] = scores(j, ahead)
            elif not masked:
                st_ref[ahead - hp] = scores(j + 1, ahead - hp)
            softmax_pv(j, h, tiles[h], allowed)
            tiles[h] = None
            if after_head is not None:
                after_head(h)

    def body_pair(u, carry):
        key_block(2 * u, False)
        key_block(2 * u + 1, False)
        return carry

    mixer_a = _MixerAHeads(qa_ref, (ka0_ref, ka1_ref, ka2_ref), (va0_ref, va1_ref, va2_ref),
                           bias_ref, oa_ref)

    for h in range(B_LOOKAHEAD):
        st_ref[h] = scores(0, h)
    lax.fori_loop(0, lax.shift_right_logical(i, 1), body_pair, 0)

    @pl.when(lax.bitwise_and(i, 1) == 1)
    def _():
        key_block(i - 1, False)

    mixer_a.begin(0, A_HEADS)

    def mixer_a_heads(h):
        for ha in range(h * A_HEADS // hp, (h + 1) * A_HEADS // hp):
            mixer_a.head(ha)

    key_block(i, True, after_head=mixer_a_heads)

    for h in range(hp):
        o_ref[:, h * B_V:(h + 1) * B_V] = (acc_ref[h] / l_ref[h]).T.astype(o_ref.dtype)

    wgb_ref[...] = wg_ref[...].astype(BF16)
    wub_ref[...] = wu_ref[...].astype(BF16)


def _mixers(proj, bias, qt, kn, kpe, vt, wg, wu, layer, *, batch, seq, name):
    m = kn.shape[0]
    nq = seq // B_TQ
    hp = B_HEADS_PER_STEP
    ng = B_HEADS // hp
    steps = batch * ng * nq
    nqa = seq // A_TQ
    assert steps == batch * nqa
    w_rows, w_cols = wg.shape[1], wg.shape[2]
    slab = w_rows // steps
    assert slab * steps == w_rows and slab % 16 == 0 and wu.shape == wg.shape
    est = (2 * B_TQ * hp * B_QK * 2 + 2 * seq * hp * (B_NOPE + B_V) * 2 + 2 * seq * 128 * 2
           + 2 * B_TQ * hp * B_V * 2 + hp * (B_V + 16) * B_TQ * 4 + 8 * B_TQ * B_TK * 4
           + 4 * slab * w_cols * 6
           + 2 * 8 * A_TQ * A_WIDTH * 2 + 2 * A_HEADS * A_TQ * A_WIN * 4 + 6 * A_WIN * A_TQ * 4
           + (4 << 20))

    def slab_in(b, g, i):
        return (layer, (b * ng + g) * nq + i, 0)

    def slab_out(b, g, i):
        return (0, (b * ng + g) * nq + i, 0)

    def a_blk(b, g, i):
        return g * nq + i

    def a_kv_spec(col_blk, c):
        back = A_KBLKS - 1 - c
        return pl.BlockSpec((A_TQ, A_WIDTH),
                            lambda b, g, i: (b * nqa + jnp.maximum(a_blk(b, g, i) - back, 0), col_blk))

    return pl.pallas_call(
        _mixers_kernel,
        out_shape=(jax.ShapeDtypeStruct((m, B_WIDTH), BF16),
                   jax.ShapeDtypeStruct((1, w_rows, w_cols), BF16),
                   jax.ShapeDtypeStruct((1, w_rows, w_cols), BF16),
                   jax.ShapeDtypeStruct((m, A_WIDTH), BF16)),
        grid=(batch, ng, nq),
        in_specs=[
            pl.BlockSpec((hp * B_QK, B_TQ), lambda b, g, i: (b * ng + g, i)),
            pl.BlockSpec((seq, hp * B_NOPE), lambda b, g, i: (b, g)),
            pl.BlockSpec((seq, V7X_LANES), lambda b, g, i: (b, 0)),
            pl.BlockSpec((hp * B_V, seq), lambda b, g, i: (b * ng + g, 0)),
            pl.BlockSpec((None, slab, w_cols), slab_in),
            pl.BlockSpec((None, slab, w_cols), slab_in),
            pl.BlockSpec((A_TQ, A_WIDTH), lambda b, g, i: (b * nqa + a_blk(b, g, i), 0)),
            a_kv_spec(1, 0), a_kv_spec(1, 1), a_kv_spec(1, 2),
            a_kv_spec(2, 0), a_kv_spec(2, 1), a_kv_spec(2, 2),
            pl.BlockSpec((A_HEADS // 2, None, A_WIN, 2 * A_TQ),
                         lambda b, g, i: (layer, jnp.minimum(a_blk(b, g, i), A_KBLKS - 1), 0, 0)),
        ],
        out_specs=(pl.BlockSpec((B_TQ, hp * B_V), lambda b, g, i: (b * nq + i, g)),
                   pl.BlockSpec((None, slab, w_cols), slab_out),
                   pl.BlockSpec((None, slab, w_cols), slab_out),
                   pl.BlockSpec((A_TQ, A_WIDTH), lambda b, g, i: (b * nqa + a_blk(b, g, i), 0))),
        scratch_shapes=[
            pltpu.VMEM((hp, 1, B_TQ), F32),
            pltpu.VMEM((hp, 1, B_TQ), F32),
            pltpu.VMEM((hp, B_V, B_TQ), F32),
            pltpu.VMEM((B_LOOKAHEAD, B_TK, B_TQ), F32),
        ],
        compiler_params=_cparams(("arbitrary", "arbitrary", "arbitrary"), est),
        name=name,
    )(qt, kn, kpe, vt, wg, wu, proj, proj, proj, proj, proj, proj, proj, bias)


def _cross_kernel(x_ref, oa_ref, ob_ref, woa_ref, wob_ref, g_ref, wq_ref, k_ref, v_ref, wo_ref,
                  gf_ref, o_ref, hf_ref):
    scale = X_HEAD_DIM ** -0.5
    rows = x_ref.shape[0] // X_ROW_PARTS
    for r in range(X_ROW_PARTS):
        rs = slice(r * rows, (r + 1) * rows)
        x = (x_ref[rs, :] + jnp.dot(oa_ref[rs, :], woa_ref[...], preferred_element_type=F32)
             + jnp.dot(ob_ref[rs, :], wob_ref[...], preferred_element_type=F32))
        hn = _rms_rows(x, g_ref[...]).astype(BF16)
        q = (jnp.dot(hn, wq_ref[...], preferred_element_type=F32) * scale).astype(BF16)
        outs = []
        for h in range(X_HEADS):
            hs = slice(h * X_HEAD_DIM, (h + 1) * X_HEAD_DIM)
            s = lax.dot_general(q[:, hs], k_ref[:, hs], (((1,), (1,)), ((), ())),
                                preferred_element_type=F32)
            m = jnp.max(s, axis=1, keepdims=True)
            p = jnp.exp(s - m)
            l = jnp.sum(p, axis=1, keepdims=True)
            o = jnp.dot(p.astype(BF16), v_ref[:, hs], preferred_element_type=F32)
            outs.append((o / l).astype(BF16))
        o_all = jnp.concatenate(outs, axis=1)
        x2 = x + jnp.dot(o_all, wo_ref[...], preferred_element_type=F32)
        o_ref[rs, :] = x2
        hf_ref[rs, :] = _rms_rows(x2, gf_ref[...]).astype(BF16)


def _cross(x, oa, ob, w_out_l, g, wq, kvm, wo, g_ffn, layer, *, seq, tm, name):
    m, d = x.shape
    blocks_per_batch = seq // tm
    est = (4 * tm * d * 4 + tm * d * 2 + 2 * d * X_WIDTH * 2 + 2 * X_WIDTH * d * 2
           + 4 * N_MEM * X_WIDTH * 2 + 4 * tm * N_MEM * 4 + 3 * tm * d * 4
           + 2 * tm * (A_WIDTH + B_WIDTH) * 2 + 2 * (A_WIDTH + B_WIDTH) * d * 2
           + 2 * tm * d * 2 + (4 << 20))
    return pl.pallas_call(
        _cross_kernel,
        out_shape=(jax.ShapeDtypeStruct((m, d), F32), jax.ShapeDtypeStruct((m, d), BF16)),
        grid=(m // tm,),
        in_specs=[
            pl.BlockSpec((tm, d), lambda i: (i, 0)),
            pl.BlockSpec((tm, A_WIDTH), lambda i: (i, 0)),
            pl.BlockSpec((tm, B_WIDTH), lambda i: (i, 0)),
            pl.BlockSpec((None, A_WIDTH, d), lambda i: (0, 0, 0)),
            pl.BlockSpec((None, B_WIDTH, d), lambda i: (0, A_WIDTH // B_WIDTH, 0)),
            pl.BlockSpec((None, 1, d), lambda i: (layer, 0, 0)),
            pl.BlockSpec((None, d, X_WIDTH), lambda i: (layer, 0, 0)),
            pl.BlockSpec((N_MEM, X_WIDTH), lambda i: (i // blocks_per_batch, 2 * layer)),
            pl.BlockSpec((N_MEM, X_WIDTH), lambda i: (i // blocks_per_batch, 2 * layer + 1)),
            pl.BlockSpec((None, X_WIDTH, d), lambda i: (layer, 0, 0)),
            pl.BlockSpec((None, 1, d), lambda i: (layer, 0, 0)),
        ],
        out_specs=(pl.BlockSpec((tm, d), lambda i: (i, 0)),
                   pl.BlockSpec((tm, d), lambda i: (i, 0))),
        compiler_params=_cparams(("arbitrary",), est),
        name=name,
    )(x, oa, ob, w_out_l, w_out_l, g, wq, kvm, kvm, wo, g_ffn)


def _down_final_kernel(a_ref, w_ref, r_ref, g_ref, o_ref):
    j = pl.program_id(1)
    tn = r_ref.shape[1]
    c0 = pl.multiple_of(j * tn, tn)
    o_ref[:, pl.ds(c0, tn)] = r_ref[...] + jnp.dot(a_ref[...], w_ref[...],
                                                  preferred_element_type=F32)

    @pl.when(j == pl.num_programs(1) - 1)
    def _():
        o_ref[...] = _rms_rows(o_ref[...], g_ref[...])


def _down_final(a, w, res, g, *, tm, tn):
    m, d = res.shape
    k = a.shape[1]
    est = 2 * tm * k * 2 + 2 * k * tn * 2 + 2 * tm * tn * 4 + 2 * tm * d * 4 + 3 * tm * tn * 4 + (4 << 20)
    return pl.pallas_call(
        _down_final_kernel,
        out_shape=jax.ShapeDtypeStruct((m, d), F32),
        grid=(m // tm, d // tn),
        in_specs=[
            pl.BlockSpec((tm, k), lambda i, j: (i, 0)),
            pl.BlockSpec((None, k, tn), lambda i, j: (0, 0, j)),
            pl.BlockSpec((tm, tn), lambda i, j: (i, j)),
            pl.BlockSpec((1, d), lambda i, j: (0, 0)),
        ],
        out_specs=pl.BlockSpec((tm, d), lambda i, j: (i, 0)),
        compiler_params=_cparams(("arbitrary", "arbitrary"), est),
        name="swiglu_down_final_norm",
    )(a, w, res, g)


RELAYOUT_COLS = 256
RELAYOUT_ROPE_BLK = IN_ROPE_OFF // RELAYOUT_COLS


def _w_in_relayout_kernel(wt_ref, o_ref):
    j = pl.program_id(1)

    @pl.when(j != RELAYOUT_ROPE_BLK)
    def _():
        factor = jnp.where(j < A_WIDTH // RELAYOUT_COLS, A_SCORE_SCALE, 1.0)
        o_ref[...] = (wt_ref[...] * factor).T.astype(BF16)

    @pl.when(j == RELAYOUT_ROPE_BLK)
    def _():
        half = B_ROPE // 2
        kr = wt_ref[:B_ROPE, :]
        slot = jnp.concatenate([kr, kr[half:], kr[:half],
                                jnp.zeros((RELAYOUT_COLS - 2 * B_ROPE, kr.shape[1]), F32)], axis=0)
        o_ref[...] = slot.T.astype(BF16)


def _w_in_relayout(w_in_t):
    layers, n, k = w_in_t.shape
    lat_blk = IN_ROPE_OFF // RELAYOUT_COLS
    rope_src = (IN_ROPE_OFF + KV_LORA) // RELAYOUT_COLS
    assert IN_KVLAT_OFF == (RELAYOUT_ROPE_BLK + 1) * RELAYOUT_COLS

    def src_block(l, j):
        return (l, jnp.where(j < RELAYOUT_ROPE_BLK, j,
                             jnp.where(j == RELAYOUT_ROPE_BLK, rope_src, j - 1 + lat_blk - RELAYOUT_ROPE_BLK)), 0)

    return pl.pallas_call(
        _w_in_relayout_kernel,
        out_shape=jax.ShapeDtypeStruct((layers, k, IN_PAD_WIDTH), BF16),
        grid=(layers, IN_PAD_WIDTH // RELAYOUT_COLS),
        in_specs=[pl.BlockSpec((None, RELAYOUT_COLS, k), src_block)],
        out_specs=pl.BlockSpec((None, k, RELAYOUT_COLS), lambda l, j: (l, 0, j)),
        compiler_params=_cparams(("arbitrary", "arbitrary"), 32 << 20),
        name="w_in_relayout",
    )(w_in_t)


def _prep_weights(w_in, w_uq, w_ukv):
    swap = (jnp.arange(B_ROPE) + B_ROPE // 2) % B_ROPE
    w_in_p = _w_in_relayout(jnp.swapaxes(w_in, 1, 2))

    uq = w_uq.reshape(DEPTH, Q_LORA, B_HEADS, B_NOPE + B_ROPE)
    uq_pe = uq[..., B_NOPE:]
    wq_p = jnp.concatenate([uq[..., :B_NOPE], uq_pe, uq_pe[..., swap]], axis=-1)
    wq_p = wq_p.reshape(DEPTH, Q_LORA, B_HEADS * B_QK).astype(BF16)

    ukv = w_ukv.reshape(DEPTH, KV_LORA, B_HEADS, B_NOPE + B_V)
    wkv_p = jnp.concatenate([ukv[..., :B_NOPE].reshape(DEPTH, KV_LORA, B_HEADS * B_NOPE),
                             ukv[..., B_NOPE:].reshape(DEPTH, KV_LORA, B_WIDTH)],
                            axis=-1).astype(BF16)

    return w_in_p, wq_p, wkv_p


def _bias_table_kernel(w_ref, o_ref):
    v = pl.program_id(1)
    r = lax.broadcasted_iota(jnp.int32, (A_TQ, A_WIN), 0)
    t = lax.broadcasted_iota(jnp.int32, (A_TQ, A_WIN), 1)
    dchunk = t // CHUNK - r // CHUNK
    valid = (dchunk >= 0) & (dchunk <= LEFT_CHUNKS) & (v - (A_KBLKS - 1) + t // A_TQ >= 0)
    for hh in range(2):
        w = jnp.broadcast_to(w_ref[hh], (A_TQ, A_BIAS_VEC))
        tab = pltpu.roll(w, 0, axis=1, stride=1, stride_axis=0)[:, :A_WIN]
        o_ref[0, 0, :, hh * A_TQ:(hh + 1) * A_TQ] = jnp.where(valid, tab * LOG2E, -jnp.inf).T


def _relpos_bias_tables(rel_bias):
    u = jnp.arange(A_BIAS_VEC)
    delta = jnp.where(u <= A_WIN, u, u - A_BIAS_VEC)
    idx = jnp.clip(LEFT_CHUNKS * CHUNK - delta, -REL_CLIP, REL_CLIP) + REL_CLIP
    vec = rel_bias[:, :, idx].reshape(DEPTH * A_HEADS, 1, A_BIAS_VEC).astype(F32)
    return pl.pallas_call(
        _bias_table_kernel,
        out_shape=jax.ShapeDtypeStruct((DEPTH * A_HEADS // 2, A_KBLKS, A_WIN, 2 * A_TQ), F32),
        grid=(DEPTH * A_HEADS // 2, A_KBLKS),
        in_specs=[pl.BlockSpec((2, 1, A_BIAS_VEC), lambda i, v: (i, 0, 0))],
        out_specs=pl.BlockSpec((1, 1, A_WIN, 2 * A_TQ), lambda i, v: (i, v, 0, 0)),
        compiler_params=_cparams(("arbitrary", "arbitrary"), 32 << 20),
        name="bias_table",
    )(vec)


def kernel(x, mem, positions, norm_mix, w_in, rel_bias, q_norm, kv_norm, w_uq, w_ukv, w_out,
           norm_mem, mem_norm, w_xq, w_xkv, w_xo, norm_ffn, w_gate, w_up, w_down, norm_final):
    b, s, d = x.shape
    m = b * s
    assert (b, s, d) == (4, 4096, D_MODEL) and mem.shape == (b, N_MEM, d)

    w_in_p, wq_p, wkv_p = _prep_weights(w_in, w_uq, w_ukv)
    w_xq_b = w_xq.astype(BF16)
    w_xo_b = w_xo.astype(BF16)
    bias_tab = _relpos_bias_tables(rel_bias)

    half = B_ROPE // 2
    inv = ROPE_THETA ** (-jnp.arange(half, dtype=F32) / half)
    inv_row = jnp.tile(inv, 4)[None, :]
    sign_row = jnp.concatenate([jnp.ones((B_ROPE,), F32), -jnp.ones((half,), F32),
                                jnp.ones((half,), F32)])[None, :]
    cs = _rope_table(positions.reshape(m, 1), inv_row, sign_row, tm=2048)

    kvm = _mem_kv(mem.reshape(b * N_MEM, d), mem_norm[None, :], w_xkv)

    norm_mix, norm_mem, norm_ffn, q_norm, kv_norm = (
        g[:, None, :] for g in (norm_mix, norm_mem, norm_ffn, q_norm, kv_norm))
    xf = x.reshape(m, d)
    for l in range(DEPTH):
        proj = _norm_matmul(xf, norm_mix, w_in_p, l, tm=1024, tn=1536, name=f"in_proj_{l}")
        qt, kn, vt, kpe, w_out_l = _mla_prep(proj, cs, q_norm, kv_norm, wq_p, wkv_p, w_out, l,
                                             seq=s, tm=1024, name=f"mla_prep_{l}")
        ob, w_gate_l, w_up_l, oa = _mixers(proj, bias_tab, qt, kn, kpe, vt, w_gate, w_up, l,
                                           batch=b, seq=s, name=f"mixers_{l}")
        xf, h_ffn = _cross(xf, oa, ob, w_out_l, norm_mem, w_xq_b, kvm, w_xo_b, norm_ffn, l,
                           seq=s, tm=512, name=f"out_cross_{l}")
        act, w_down_l = _swiglu_gu(h_ffn, w_gate_l, w_up_l, w_down, l, tm=1024, tn=512,
                                   name=f"swiglu_gu_{l}")
        if l + 1 < DEPTH:
            xf = _mm_res([act], w_down_l, xf, 0, tm=1024, tn=512, name=f"swiglu_down_{l}")
        else:
            out = _down_final(act, w_down_l, xf, norm_final[None, :], tm=512, tn=512)
    return out.reshape(b, s, d)
```

```python
import functools
import math

import jax
import jax.numpy as jnp
from jax import lax
from jax.experimental import pallas as pl
from jax.experimental.pallas import tpu as pltpu

D_MODEL = 2048
DEPTH = 4
CHUNK = 64
LEFT_CHUNKS = 8
A_HEAD_DIM = 128
A_HEADS = 8
A_WIDTH = A_HEADS * A_HEAD_DIM
REL_CLIP = 128
B_NOPE = 128
B_ROPE = 64
B_V = 128
B_HEADS = 8
B_WIDTH = B_HEADS * B_V
Q_LORA = 768
KV_LORA = 512
ROPE_THETA = 10000.0
N_MEM = 256
X_HEADS = 4
X_HEAD_DIM = 128
X_WIDTH = X_HEADS * X_HEAD_DIM
X_ROW_PARTS = 2
D_FF = 5632
EPS = 1e-6

V7X_LANES = 128
V7X_VMEM_BYTES = 64 * 1024 * 1024
V7X_VMEM_REQUEST_CAP = 56 * 1024 * 1024

BF16 = jnp.bfloat16
F32 = jnp.float32

IN_QLAT_OFF = 3 * A_WIDTH
IN_ROPE_OFF = IN_QLAT_OFF + Q_LORA
IN_KVLAT_OFF = IN_ROPE_OFF + 2 * V7X_LANES
IN_PAD_WIDTH = IN_KVLAT_OFF + KV_LORA

A_TQ = 4 * CHUNK
A_WIN = A_TQ + LEFT_CHUNKS * CHUNK
A_KBLKS = A_WIN // A_TQ
A_BIAS_VEC = 1024
A_LOOKAHEAD = 2

B_TQ = 512
B_TK = 512
B_HEADS_PER_STEP = 4
B_LOOKAHEAD = 2
B_RED_SLABS = 8
B_QK = B_NOPE + 2 * B_ROPE

LOG2E = math.log2(math.e)
A_SCORE_SCALE = A_HEAD_DIM ** -0.5 * LOG2E


def _cparams(sem, vmem_bytes):
    limit = int(min(max(vmem_bytes, 32 * 1024 * 1024), V7X_VMEM_REQUEST_CAP))
    return pltpu.CompilerParams(dimension_semantics=sem, vmem_limit_bytes=limit)


def _rms_rows(x, g):
    ms = jnp.mean(x * x, axis=-1, keepdims=True)
    return (x * lax.rsqrt(ms + EPS)) * g


def _norm_into(h_ref, x_ref, g_ref, rows=128):
    for r0 in range(0, x_ref.shape[0], rows):
        x = x_ref[r0:r0 + rows, :].astype(F32)
        h_ref[r0:r0 + rows, :] = _rms_rows(x, g_ref[...]).astype(h_ref.dtype)


def _norm_matmul_kernel(x_ref, g_ref, w_ref, o_ref, h_ref):
    @pl.when(pl.program_id(1) == 0)
    def _():
        _norm_into(h_ref, x_ref, g_ref)

    o_ref[...] = jnp.dot(h_ref[...], w_ref[...], preferred_element_type=F32).astype(o_ref.dtype)


def _norm_matmul(x, g, w, layer, *, tm, tn, name):
    m, k = x.shape
    n = w.shape[2]
    xb = x.dtype.itemsize
    est = 2 * tm * k * xb + tm * k * 2 + 2 * k * tn * 2 + 2 * tm * tn * 2 + tm * tn * 4 + (4 << 20)
    return pl.pallas_call(
        _norm_matmul_kernel,
        out_shape=jax.ShapeDtypeStruct((m, n), BF16),
        grid=(m // tm, n // tn),
        in_specs=[
            pl.BlockSpec((tm, k), lambda i, j: (i, 0)),
            pl.BlockSpec((None, 1, k), lambda i, j: (layer, 0, 0)),
            pl.BlockSpec((None, k, tn), lambda i, j: (layer, 0, j)),
        ],
        out_specs=pl.BlockSpec((tm, tn), lambda i, j: (i, j)),
        scratch_shapes=[pltpu.VMEM((tm, k), BF16)],
        compiler_params=_cparams(("arbitrary", "arbitrary"), est),
        name=name,
    )(x, g, w)


def _mem_kv_kernel(x_ref, g_ref, w_ref, o_ref, h_ref):
    @pl.when(pl.program_id(0) == 0)
    def _():
        _norm_into(h_ref, x_ref, g_ref)

    o_ref[...] = jnp.dot(h_ref[...], w_ref[...].astype(BF16),
                         preferred_element_type=F32).astype(o_ref.dtype)


def _mem_kv(mem2d, g, w_xkv):
    m, k = mem2d.shape
    layers, _, n = w_xkv.shape
    est = 2 * m * k * 4 + m * k * 2 + 2 * k * n * 4 + k * n * 2 + 2 * m * n * 2 + m * n * 4 + (4 << 20)
    return pl.pallas_call(
        _mem_kv_kernel,
        out_shape=jax.ShapeDtypeStruct((m, layers * n), BF16),
        grid=(layers,),
        in_specs=[
            pl.BlockSpec((m, k), lambda l: (0, 0)),
            pl.BlockSpec((1, k), lambda l: (0, 0)),
            pl.BlockSpec((None, k, n), lambda l: (l, 0, 0)),
        ],
        out_specs=pl.BlockSpec((m, n), lambda l: (0, l)),
        scratch_shapes=[pltpu.VMEM((m, k), BF16)],
        compiler_params=_cparams(("arbitrary",), est),
        name="mem_kv",
    )(mem2d, g, w_xkv)


def _mm_res_kernel(*refs, n_in):
    a_refs = refs[:n_in]
    w_refs = refs[n_in:2 * n_in]
    r_ref = refs[2 * n_in]
    o_ref = refs[2 * n_in + 1]
    acc = r_ref[...]
    for a_ref, w_ref in zip(a_refs, w_refs):
        acc = acc + jnp.dot(a_ref[...], w_ref[...], preferred_element_type=F32)
    o_ref[...] = acc


def _mm_res(a_list, w, res, layer, *, tm, tn, name):
    m, n = res.shape
    n_in = len(a_list)
    in_specs = []
    w_specs = []
    est = (4 + 2 * n_in) * tm * tn * 4 + (4 << 20)
    for idx, a in enumerate(a_list):
        k = a.shape[1]
        assert w.shape[1] == n_in * k
        in_specs.append(pl.BlockSpec((tm, k), lambda i, j: (i, 0)))
        w_specs.append(pl.BlockSpec((None, k, tn), lambda i, j, idx=idx: (layer, idx, j)))
        est += 2 * tm * k * 2 + 2 * k * tn * 2
    in_specs += w_specs
    in_specs.append(pl.BlockSpec((tm, tn), lambda i, j: (i, j)))
    return pl.pallas_call(
        functools.partial(_mm_res_kernel, n_in=n_in),
        out_shape=jax.ShapeDtypeStruct((m, n), F32),
        grid=(m // tm, n // tn),
        in_specs=in_specs,
        out_specs=pl.BlockSpec((tm, tn), lambda i, j: (i, j)),
        compiler_params=_cparams(("arbitrary", "arbitrary"), est),
        name=name,
    )(*a_list, *([w] * n_in), res)


def _swiglu_gu_kernel(h_ref, wg_ref, wu_ref, wd_ref, o_ref, wdb_ref):
    h = h_ref[...]
    gate = jnp.dot(h, wg_ref[...], preferred_element_type=F32)
    up = jnp.dot(h, wu_ref[...], preferred_element_type=F32)
    o_ref[...] = (gate * (1.0 / (1.0 + jnp.exp(-gate))) * up).astype(o_ref.dtype)
    wdb_ref[...] = wd_ref[...].astype(BF16)


def _swiglu_gu(h, wg, wu, wd, layer, *, tm, tn, name):
    m, k = h.shape
    n = wg.shape[2]
    steps = (m // tm) * (n // tn)
    nj = n // tn
    wd_rows, wd_cols = wd.shape[1], wd.shape[2]
    slab = wd_rows // steps
    assert slab * steps == wd_rows and slab % 16 == 0
    est = (2 * tm * k * 2 + 4 * k * tn * 2 + 2 * tm * tn * 2 + 3 * tm * tn * 4
           + 2 * slab * wd_cols * 6 + (4 << 20))
    return pl.pallas_call(
        _swiglu_gu_kernel,
        out_shape=(jax.ShapeDtypeStruct((m, n), BF16),
                   jax.ShapeDtypeStruct((1, wd_rows, wd_cols), BF16)),
        grid=(m // tm, nj),
        in_specs=[
            pl.BlockSpec((tm, k), lambda i, j: (i, 0)),
            pl.BlockSpec((None, k, tn), lambda i, j: (0, 0, j)),
            pl.BlockSpec((None, k, tn), lambda i, j: (0, 0, j)),
            pl.BlockSpec((None, slab, wd_cols), lambda i, j: (layer, i * nj + j, 0)),
        ],
        out_specs=(pl.BlockSpec((tm, tn), lambda i, j: (i, j)),
                   pl.BlockSpec((None, slab, wd_cols), lambda i, j: (0, i * nj + j, 0))),
        compiler_params=_cparams(("arbitrary", "arbitrary"), est),
        name=name,
    )(h, wg, wu, wd)


def _rope_table_kernel(pos_ref, inv_ref, sign_ref, cs_ref):
    ang = pos_ref[...].astype(F32) * inv_ref[...]
    lane = lax.broadcasted_iota(jnp.int32, ang.shape, 1)
    cs_ref[...] = jnp.where(lane < B_ROPE, jnp.cos(ang), jnp.sin(ang) * sign_ref[...])


def _rope_table(pos_col, inv_row, sign_row, *, tm):
    m = pos_col.shape[0]
    return pl.pallas_call(
        _rope_table_kernel,
        out_shape=jax.ShapeDtypeStruct((m, V7X_LANES), F32),
        grid=(m // tm,),
        in_specs=[
            pl.BlockSpec((tm, 1), lambda i: (i, 0)),
            pl.BlockSpec((1, V7X_LANES), lambda i: (0, 0)),
            pl.BlockSpec((1, V7X_LANES), lambda i: (0, 0)),
        ],
        out_specs=pl.BlockSpec((tm, V7X_LANES), lambda i: (i, 0)),
        compiler_params=_cparams(("arbitrary",), 32 << 20),
        name="rope_table",
    )(pos_col, inv_row, sign_row)


def _rope_pair_sum(slot, cs):
    t = slot * cs
    return t + pltpu.roll(t, B_ROPE, axis=1)


def _mla_prep_kernel(ql_ref, rope_ref, kvl_ref, cs_ref, gq_ref, gkv_ref, wq_ref, wkv_ref, wo_ref,
                     qt_ref, kn_ref, vt_ref, kpe_ref, wob_ref, *, q_scale):
    cs = cs_ref[...]
    hq = _rms_rows(ql_ref[...].astype(F32), gq_ref[...]).astype(BF16)
    qf = jnp.dot(hq, wq_ref[...], preferred_element_type=F32)
    for h in range(B_HEADS):
        base = h * B_QK
        nope = qf[:, base:base + B_NOPE]
        pe = _rope_pair_sum(qf[:, base + B_NOPE:base + B_QK], cs)
        qt_ref[base:base + B_NOPE, :] = (nope * q_scale).T.astype(BF16)
        qt_ref[base + B_NOPE:base + B_QK, :] = (pe * q_scale).T.astype(BF16)

    hkv = _rms_rows(kvl_ref[...].astype(F32), gkv_ref[...]).astype(BF16)
    kvf = jnp.dot(hkv, wkv_ref[...], preferred_element_type=F32)
    kn_ref[...] = kvf[:, :B_HEADS * B_NOPE].astype(BF16)
    vt_ref[...] = kvf[:, B_HEADS * B_NOPE:].T.astype(BF16)

    kpe = _rope_pair_sum(rope_ref[...].astype(F32), cs)
    lane = lax.broadcasted_iota(jnp.int32, kpe.shape, 1)
    kpe_ref[...] = jnp.where(lane < B_ROPE, kpe, 0.0).astype(BF16)
    wob_ref[...] = wo_ref[...].astype(BF16)


def _mla_prep(proj, cs, gq, gkv, wq, wkv, wo, layer, *, seq, tm, name):
    m = proj.shape[0]
    tiles_per_seq = seq // tm
    q_scale = (B_NOPE + B_ROPE) ** -0.5 * LOG2E
    wo_rows, wo_cols = wo.shape[1], wo.shape[2]
    slab = wo_rows // (m // tm)
    assert slab * (m // tm) == wo_rows and slab % 16 == 0
    est = (2 * tm * (Q_LORA + KV_LORA + 128) * 2 + 2 * tm * 128 * 4
           + 2 * (Q_LORA + KV_LORA) * 2048 * 2 + 2 * slab * wo_cols * 6
           + 2 * tm * (2048 + 1024 + 1024 + 128) * 2 + 3 * tm * 2048 * 4 + (4 << 20))
    return pl.pallas_call(
        functools.partial(_mla_prep_kernel, q_scale=q_scale),
        out_shape=(
            jax.ShapeDtypeStruct((m // seq * B_HEADS * B_QK, seq), BF16),
            jax.ShapeDtypeStruct((m, B_HEADS * B_NOPE), BF16),
            jax.ShapeDtypeStruct((m // seq * B_WIDTH, seq), BF16),
            jax.ShapeDtypeStruct((m, V7X_LANES), BF16),
            jax.ShapeDtypeStruct((1, wo_rows, wo_cols), BF16),
        ),
        grid=(m // tm,),
        in_specs=[
            pl.BlockSpec((tm, Q_LORA), lambda i: (i, IN_QLAT_OFF // Q_LORA)),
            pl.BlockSpec((tm, V7X_LANES), lambda i: (i, IN_ROPE_OFF // V7X_LANES)),
            pl.BlockSpec((tm, KV_LORA), lambda i: (i, IN_KVLAT_OFF // KV_LORA)),
            pl.BlockSpec((tm, V7X_LANES), lambda i: (i, 0)),
            pl.BlockSpec((None, 1, Q_LORA), lambda i: (layer, 0, 0)),
            pl.BlockSpec((None, 1, KV_LORA), lambda i: (layer, 0, 0)),
            pl.BlockSpec((None, Q_LORA, B_HEADS * B_QK), lambda i: (layer, 0, 0)),
            pl.BlockSpec((None, KV_LORA, 2 * B_WIDTH), lambda i: (layer, 0, 0)),
            pl.BlockSpec((None, slab, wo_cols), lambda i: (layer, i, 0)),
        ],
        out_specs=(
            pl.BlockSpec((B_HEADS * B_QK, tm), lambda i: (i // tiles_per_seq, i % tiles_per_seq)),
            pl.BlockSpec((tm, B_HEADS * B_NOPE), lambda i: (i, 0)),
            pl.BlockSpec((B_WIDTH, tm), lambda i: (i // tiles_per_seq, i % tiles_per_seq)),
            pl.BlockSpec((tm, V7X_LANES), lambda i: (i, 0)),
            pl.BlockSpec((None, slab, wo_cols), lambda i: (0, i, 0)),
        ),
        compiler_params=_cparams(("arbitrary",), est),
        name=name,
    )(proj, proj, proj, cs, gq, gkv, wq, wkv, wo)


class _MixerAHeads:
    def __init__(self, q_ref, k_refs, v_refs, bias_ref, o_ref):
        self.q_ref, self.k_refs, self.v_refs = q_ref, k_refs, v_refs
        self.bias_ref, self.o_ref = bias_ref, o_ref
        self.tiles = [None] * A_HEADS
        self.end = 0

    def _scores(self, h):
        hs = slice(h * A_HEAD_DIM, (h + 1) * A_HEAD_DIM)
        k = jnp.concatenate([r[:, hs] for r in self.k_refs], axis=0)
        return jnp.dot(k, self.q_ref[:, hs].T, preferred_element_type=F32)

    def begin(self, first, end):
        self.end = end
        for h in range(first, min(first + A_LOOKAHEAD, end)):
            self.tiles[h] = self._scores(h)

    def head(self, h):
        d, tq = A_HEAD_DIM, A_TQ
        hs = slice(h * d, (h + 1) * d)
        if h + A_LOOKAHEAD < self.end:
            self.tiles[h + A_LOOKAHEAD] = self._scores(h + A_LOOKAHEAD)
        v = jnp.concatenate([r[:, hs] for r in self.v_refs], axis=0)
        st = self.bias_ref[h // 2][:, (h % 2) * tq:(h % 2 + 1) * tq] + self.tiles[h]
        self.tiles[h] = None
        m = _col_reduce(st, jnp.max)
        p = jnp.exp2(st - m)
        l = _col_reduce(p, jnp.sum)
        ot = lax.dot_general(v, p.astype(BF16), (((0,), (0,)), ((), ())),
                             preferred_element_type=F32)
        self.o_ref[:, hs] = (ot / l).T.astype(self.o_ref.dtype)


def _col_reduce(x, op):
    rows, cols = x.shape
    slabs = op(x.reshape(B_RED_SLABS, rows // B_RED_SLABS, cols), axis=0)
    return op(slabs, axis=0, keepdims=True)


def _mixers_kernel(q_ref, kn_ref, kpe_ref, vt_ref, wg_ref, wu_ref,
                   qa_ref, ka0_ref, ka1_ref, ka2_ref, va0_ref, va1_ref, va2_ref, bias_ref,
                   o_ref, wgb_ref, wub_ref, oa_ref,
                   m_ref, l_ref, acc_ref, st_ref):
    i = pl.program_id(2)
    hp = B_HEADS_PER_STEP
    m_ref[...] = jnp.full(m_ref.shape, -jnp.inf, F32)
    l_ref[...] = jnp.zeros(l_ref.shape, F32)
    acc_ref[...] = jnp.zeros(acc_ref.shape, F32)

    def scores(j, h):
        r0 = pl.multiple_of(j * B_TK, B_TK)
        k = jnp.concatenate([kn_ref[pl.ds(r0, B_TK), h * B_NOPE:(h + 1) * B_NOPE],
                             kpe_ref[pl.ds(r0, B_TK), :]], axis=1)
        return jnp.dot(k, q_ref[h * B_QK:(h + 1) * B_QK, :], preferred_element_type=F32)

    def softmax_pv(j, h, st, allowed):
        r0 = pl.multiple_of(j * B_TK, B_TK)
        if allowed is not None:
            st = jnp.where(allowed, st, -jnp.inf)
        m_prev = m_ref[h]
        m_new = jnp.maximum(m_prev, _col_reduce(st, jnp.max))
        alpha = jnp.exp2(m_prev - m_new)
        p = jnp.exp2(st - m_new)
        l_ref[h] = alpha * l_ref[h] + _col_reduce(p, jnp.sum)
        pv = jnp.dot(vt_ref[h * B_V:(h + 1) * B_V, pl.ds(r0, B_TK)], p.astype(BF16),
                     preferred_element_type=F32)
        acc_ref[h] = alpha * acc_ref[h] + pv
        m_ref[h] = m_new

    def key_block(j, masked, after_head=None):
        allowed = None
        if masked:
            krow = lax.broadcasted_iota(jnp.int32, (B_TK, B_TQ), 0)
            qcol = lax.broadcasted_iota(jnp.int32, (B_TK, B_TQ), 1)
            allowed = (krow // CHUNK) <= (qcol // CHUNK)
        look = B_LOOKAHEAD
        tiles = [st_ref[h] for h in range(look)] + [None] * (hp - look)
        for h in range(hp):
            ahead = h + look
            if ahead < hp:
                tiles[ahead] = scores(j, ahead)
            elif not masked:
                st_ref[ahead - hp] = scores(j + 1, ahead - hp)
            softmax_pv(j, h, tiles[h], allowed)
            tiles[h] = None
            if after_head is not None:
                after_head(h)

    def body_pair(u, carry):
        key_block(2 * u, False)
        key_block(2 * u + 1, False)
        return carry

    mixer_a = _MixerAHeads(qa_ref, (ka0_ref, ka1_ref, ka2_ref), (va0_ref, va1_ref, va2_ref),
                           bias_ref, oa_ref)

    for h in range(B_LOOKAHEAD):
        st_ref[h] = scores(0, h)
    lax.fori_loop(0, lax.shift_right_logical(i, 1), body_pair, 0)

    @pl.when(lax.bitwise_and(i, 1) == 1)
    def _():
        key_block(i - 1, False)

    mixer_a.begin(0, A_HEADS)

    def mixer_a_heads(h):
        for ha in range(h * A_HEADS // hp, (h + 1) * A_HEADS // hp):
            mixer_a.head(ha)

    key_block(i, True, after_head=mixer_a_heads)

    for h in range(hp):
        o_ref[:, h * B_V:(h + 1) * B_V] = (acc_ref[h] / l_ref[h]).T.astype(o_ref.dtype)

    wgb_ref[...] = wg_ref[...].astype(BF16)
    wub_ref[...] = wu_ref[...].astype(BF16)


def _mixers(proj, bias, qt, kn, kpe, vt, wg, wu, layer, *, batch, seq, name):
    m = kn.shape[0]
    nq = seq // B_TQ
    hp = B_HEADS_PER_STEP
    ng = B_HEADS // hp
    steps = batch * ng * nq
    nqa = seq // A_TQ
    assert steps == batch * nqa
    w_rows, w_cols = wg.shape[1], wg.shape[2]
    slab = w_rows // steps
    assert slab * steps == w_rows and slab % 16 == 0 and wu.shape == wg.shape
    est = (2 * B_TQ * hp * B_QK * 2 + 2 * seq * hp * (B_NOPE + B_V) * 2 + 2 * seq * 128 * 2
           + 2 * B_TQ * hp * B_V * 2 + hp * (B_V + 16) * B_TQ * 4 + 8 * B_TQ * B_TK * 4
           + 4 * slab * w_cols * 6
           + 2 * 8 * A_TQ * A_WIDTH * 2 + 2 * A_HEADS * A_TQ * A_WIN * 4 + 6 * A_WIN * A_TQ * 4
           + (4 << 20))

    def slab_in(b, g, i):
        return (layer, (b * ng + g) * nq + i, 0)

    def slab_out(b, g, i):
        return (0, (b * ng + g) * nq + i, 0)

    def a_blk(b, g, i):
        return g * nq + i

    def a_kv_spec(col_blk, c):
        back = A_KBLKS - 1 - c
        return pl.BlockSpec((A_TQ, A_WIDTH),
                            lambda b, g, i: (b * nqa + jnp.maximum(a_blk(b, g, i) - back, 0), col_blk))

    return pl.pallas_call(
        _mixers_kernel,
        out_shape=(jax.ShapeDtypeStruct((m, B_WIDTH), BF16),
                   jax.ShapeDtypeStruct((1, w_rows, w_cols), BF16),
                   jax.ShapeDtypeStruct((1, w_rows, w_cols), BF16),
                   jax.ShapeDtypeStruct((m, A_WIDTH), BF16)),
        grid=(batch, ng, nq),
        in_specs=[
            pl.BlockSpec((hp * B_QK, B_TQ), lambda b, g, i: (b * ng + g, i)),
            pl.BlockSpec((seq, hp * B_NOPE), lambda b, g, i: (b, g)),
            pl.BlockSpec((seq, V7X_LANES), lambda b, g, i: (b, 0)),
            pl.BlockSpec((hp * B_V, seq), lambda b, g, i: (b * ng + g, 0)),
            pl.BlockSpec((None, slab, w_cols), slab_in),
            pl.BlockSpec((None, slab, w_cols), slab_in),
            pl.BlockSpec((A_TQ, A_WIDTH), lambda b, g, i: (b * nqa + a_blk(b, g, i), 0)),
            a_kv_spec(1, 0), a_kv_spec(1, 1), a_kv_spec(1, 2),
            a_kv_spec(2, 0), a_kv_spec(2, 1), a_kv_spec(2, 2),
            pl.BlockSpec((A_HEADS // 2, None, A_WIN, 2 * A_TQ),
                         lambda b, g, i: (layer, jnp.minimum(a_blk(b, g, i), A_KBLKS - 1), 0, 0)),
        ],
        out_specs=(pl.BlockSpec((B_TQ, hp * B_V), lambda b, g, i: (b * nq + i, g)),
                   pl.BlockSpec((None, slab, w_cols), slab_out),
                   pl.BlockSpec((None, slab, w_cols), slab_out),
                   pl.BlockSpec((A_TQ, A_WIDTH), lambda b, g, i: (b * nqa + a_blk(b, g, i), 0))),
        scratch_shapes=[
            pltpu.VMEM((hp, 1, B_TQ), F32),
            pltpu.VMEM((hp, 1, B_TQ), F32),
            pltpu.VMEM((hp, B_V, B_TQ), F32),
            pltpu.VMEM((B_LOOKAHEAD, B_TK, B_TQ), F32),
        ],
        compiler_params=_cparams(("arbitrary", "arbitrary", "arbitrary"), est),
        name=name,
    )(qt, kn, kpe, vt, wg, wu, proj, proj, proj, proj, proj, proj, proj, bias)


def _cross_kernel(x_ref, oa_ref, ob_ref, woa_ref, wob_ref, g_ref, wq_ref, k_ref, v_ref, wo_ref,
                  gf_ref, o_ref, hf_ref):
    scale = X_HEAD_DIM ** -0.5
    rows = x_ref.shape[0] // X_ROW_PARTS
    for r in range(X_ROW_PARTS):
        rs = slice(r * rows, (r + 1) * rows)
        x = (x_ref[rs, :] + jnp.dot(oa_ref[rs, :], woa_ref[...], preferred_element_type=F32)
             + jnp.dot(ob_ref[rs, :], wob_ref[...], preferred_element_type=F32))
        hn = _rms_rows(x, g_ref[...]).astype(BF16)
        q = (jnp.dot(hn, wq_ref[...], preferred_element_type=F32) * scale).astype(BF16)
        outs = []
        for h in range(X_HEADS):
            hs = slice(h * X_HEAD_DIM, (h + 1) * X_HEAD_DIM)
            s = lax.dot_general(q[:, hs], k_ref[:, hs], (((1,), (1,)), ((), ())),
                                preferred_element_type=F32)
            m = jnp.max(s, axis=1, keepdims=True)
            p = jnp.exp(s - m)
            l = jnp.sum(p, axis=1, keepdims=True)
            o = jnp.dot(p.astype(BF16), v_ref[:, hs], preferred_element_type=F32)
            outs.append((o / l).astype(BF16))
        o_all = jnp.concatenate(outs, axis=1)
        x2 = x + jnp.dot(o_all, wo_ref[...], preferred_element_type=F32)
        o_ref[rs, :] = x2
        hf_ref[rs, :] = _rms_rows(x2, gf_ref[...]).astype(BF16)


def _cross(x, oa, ob, w_out_l, g, wq, kvm, wo, g_ffn, layer, *, seq, tm, name):
    m, d = x.shape
    blocks_per_batch = seq // tm
    est = (4 * tm * d * 4 + tm * d * 2 + 2 * d * X_WIDTH * 2 + 2 * X_WIDTH * d * 2
           + 4 * N_MEM * X_WIDTH * 2 + 4 * tm * N_MEM * 4 + 3 * tm * d * 4
           + 2 * tm * (A_WIDTH + B_WIDTH) * 2 + 2 * (A_WIDTH + B_WIDTH) * d * 2
           + 2 * tm * d * 2 + (4 << 20))
    return pl.pallas_call(
        _cross_kernel,
        out_shape=(jax.ShapeDtypeStruct((m, d), F32), jax.ShapeDtypeStruct((m, d), BF16)),
        grid=(m // tm,),
        in_specs=[
            pl.BlockSpec((tm, d), lambda i: (i, 0)),
            pl.BlockSpec((tm, A_WIDTH), lambda i: (i, 0)),
            pl.BlockSpec((tm, B_WIDTH), lambda i: (i, 0)),
            pl.BlockSpec((None, A_WIDTH, d), lambda i: (0, 0, 0)),
            pl.BlockSpec((None, B_WIDTH, d), lambda i: (0, A_WIDTH // B_WIDTH, 0)),
            pl.BlockSpec((None, 1, d), lambda i: (layer, 0, 0)),
            pl.BlockSpec((None, d, X_WIDTH), lambda i: (layer, 0, 0)),
            pl.BlockSpec((N_MEM, X_WIDTH), lambda i: (i // blocks_per_batch, 2 * layer)),
            pl.BlockSpec((N_MEM, X_WIDTH), lambda i: (i // blocks_per_batch, 2 * layer + 1)),
            pl.BlockSpec((None, X_WIDTH, d), lambda i: (layer, 0, 0)),
            pl.BlockSpec((None, 1, d), lambda i: (layer, 0, 0)),
        ],
        out_specs=(pl.BlockSpec((tm, d), lambda i: (i, 0)),
                   pl.BlockSpec((tm, d), lambda i: (i, 0))),
        compiler_params=_cparams(("arbitrary",), est),
        name=name,
    )(x, oa, ob, w_out_l, w_out_l, g, wq, kvm, kvm, wo, g_ffn)


def _final_norm_kernel(x_ref, g_ref, o_ref):
    o_ref[...] = _rms_rows(x_ref[...], g_ref[...])


def _final_norm(x, g, *, tm):
    m, d = x.shape
    return pl.pallas_call(
        _final_norm_kernel,
        out_shape=jax.ShapeDtypeStruct((m, d), F32),
        grid=(m // tm,),
        in_specs=[pl.BlockSpec((tm, d), lambda i: (i, 0)), pl.BlockSpec((1, d), lambda i: (0, 0))],
        out_specs=pl.BlockSpec((tm, d), lambda i: (i, 0)),
        compiler_params=_cparams(("arbitrary",), 6 * tm * d * 4 + (4 << 20)),
        name="final_norm",
    )(x, g)


RELAYOUT_COLS = 256
RELAYOUT_ROPE_BLK = IN_ROPE_OFF // RELAYOUT_COLS


def _w_in_relayout_kernel(wt_ref, o_ref):
    j = pl.program_id(1)

    @pl.when(j != RELAYOUT_ROPE_BLK)
    def _():
        factor = jnp.where(j < A_WIDTH // RELAYOUT_COLS, A_SCORE_SCALE, 1.0)
        o_ref[...] = (wt_ref[...] * factor).T.astype(BF16)

    @pl.when(j == RELAYOUT_ROPE_BLK)
    def _():
        half = B_ROPE // 2
        kr = wt_ref[:B_ROPE, :]
        slot = jnp.concatenate([kr, kr[half:], kr[:half],
                                jnp.zeros((RELAYOUT_COLS - 2 * B_ROPE, kr.shape[1]), F32)], axis=0)
        o_ref[...] = slot.T.astype(BF16)


def _w_in_relayout(w_in_t):
    layers, n, k = w_in_t.shape
    lat_blk = IN_ROPE_OFF // RELAYOUT_COLS
    rope_src = (IN_ROPE_OFF + KV_LORA) // RELAYOUT_COLS
    assert IN_KVLAT_OFF == (RELAYOUT_ROPE_BLK + 1) * RELAYOUT_COLS

    def src_block(l, j):
        return (l, jnp.where(j < RELAYOUT_ROPE_BLK, j,
                             jnp.where(j == RELAYOUT_ROPE_BLK, rope_src, j - 1 + lat_blk - RELAYOUT_ROPE_BLK)), 0)

    return pl.pallas_call(
        _w_in_relayout_kernel,
        out_shape=jax.ShapeDtypeStruct((layers, k, IN_PAD_WIDTH), BF16),
        grid=(layers, IN_PAD_WIDTH // RELAYOUT_COLS),
        in_specs=[pl.BlockSpec((None, RELAYOUT_COLS, k), src_block)],
        out_specs=pl.BlockSpec((None, k, RELAYOUT_COLS), lambda l, j: (l, 0, j)),
        compiler_params=_cparams(("arbitrary", "arbitrary"), 32 << 20),
        name="w_in_relayout",
    )(w_in_t)


def _prep_weights(w_in, w_uq, w_ukv):
    swap = (jnp.arange(B_ROPE) + B_ROPE // 2) % B_ROPE
    w_in_p = _w_in_relayout(jnp.swapaxes(w_in, 1, 2))

    uq = w_uq.reshape(DEPTH, Q_LORA, B_HEADS, B_NOPE + B_ROPE)
    uq_pe = uq[..., B_NOPE:]
    wq_p = jnp.concatenate([uq[..., :B_NOPE], uq_pe, uq_pe[..., swap]], axis=-1)
    wq_p = wq_p.reshape(DEPTH, Q_LORA, B_HEADS * B_QK).astype(BF16)

    ukv = w_ukv.reshape(DEPTH, KV_LORA, B_HEADS, B_NOPE + B_V)
    wkv_p = jnp.concatenate([ukv[..., :B_NOPE].reshape(DEPTH, KV_LORA, B_HEADS * B_NOPE),
                             ukv[..., B_NOPE:].reshape(DEPTH, KV_LORA, B_WIDTH)],
                            axis=-1).astype(BF16)

    return w_in_p, wq_p, wkv_p


def _bias_table_kernel(w_ref, o_ref):
    v = pl.program_id(1)
    r = lax.broadcasted_iota(jnp.int32, (A_TQ, A_WIN), 0)
    t = lax.broadcasted_iota(jnp.int32, (A_TQ, A_WIN), 1)
    dchunk = t // CHUNK - r // CHUNK
    valid = (dchunk >= 0) & (dchunk <= LEFT_CHUNKS) & (v - (A_KBLKS - 1) + t // A_TQ >= 0)
    for hh in range(2):
        w = jnp.broadcast_to(w_ref[hh], (A_TQ, A_BIAS_VEC))
        tab = pltpu.roll(w, 0, axis=1, stride=1, stride_axis=0)[:, :A_WIN]
        o_ref[0, 0, :, hh * A_TQ:(hh + 1) * A_TQ] = jnp.where(valid, tab * LOG2E, -jnp.inf).T


def _relpos_bias_tables(rel_bias):
    u = jnp.arange(A_BIAS_VEC)
    delta = jnp.where(u <= A_WIN, u, u - A_BIAS_VEC)
    idx = jnp.clip(LEFT_CHUNKS * CHUNK - delta, -REL_CLIP, REL_CLIP) + REL_CLIP
    vec = rel_bias[:, :, idx].reshape(DEPTH * A_HEADS, 1, A_BIAS_VEC).astype(F32)
    return pl.pallas_call(
        _bias_table_kernel,
        out_shape=jax.ShapeDtypeStruct((DEPTH * A_HEADS // 2, A_KBLKS, A_WIN, 2 * A_TQ), F32),
        grid=(DEPTH * A_HEADS // 2, A_KBLKS),
        in_specs=[pl.BlockSpec((2, 1, A_BIAS_VEC), lambda i, v: (i, 0, 0))],
        out_specs=pl.BlockSpec((1, 1, A_WIN, 2 * A_TQ), lambda i, v: (i, v, 0, 0)),
        compiler_params=_cparams(("arbitrary", "arbitrary"), 32 << 20),
        name="bias_table",
    )(vec)


def kernel(x, mem, positions, norm_mix, w_in, rel_bias, q_norm, kv_norm, w_uq, w_ukv, w_out,
           norm_mem, mem_norm, w_xq, w_xkv, w_xo, norm_ffn, w_gate, w_up, w_down, norm_final):
    b, s, d = x.shape
    m = b * s
    assert (b, s, d) == (4, 4096, D_MODEL) and mem.shape == (b, N_MEM, d)

    w_in_p, wq_p, wkv_p = _prep_weights(w_in, w_uq, w_ukv)
    w_xq_b = w_xq.astype(BF16)
    w_xo_b = w_xo.astype(BF16)
    bias_tab = _relpos_bias_tables(rel_bias)

    half = B_ROPE // 2
    inv = ROPE_THETA ** (-jnp.arange(half, dtype=F32) / half)
    inv_row = jnp.tile(inv, 4)[None, :]
    sign_row = jnp.concatenate([jnp.ones((B_ROPE,), F32), -jnp.ones((half,), F32),
                                jnp.ones((half,), F32)])[None, :]
    cs = _rope_table(positions.reshape(m, 1), inv_row, sign_row, tm=2048)

    kvm = _mem_kv(mem.reshape(b * N_MEM, d), mem_norm[None, :], w_xkv)

    norm_mix, norm_mem, norm_ffn, q_norm, kv_norm = (
        g[:, None, :] for g in (norm_mix, norm_mem, norm_ffn, q_norm, kv_norm))
    xf = x.reshape(m, d)
    for l in range(DEPTH):
        proj = _norm_matmul(xf, norm_mix, w_in_p, l, tm=1024, tn=1536, name=f"in_proj_{l}")
        qt, kn, vt, kpe, w_out_l = _mla_prep(proj, cs, q_norm, kv_norm, wq_p, wkv_p, w_out, l,
                                             seq=s, tm=1024, name=f"mla_prep_{l}")
        ob, w_gate_l, w_up_l, oa = _mixers(proj, bias_tab, qt, kn, kpe, vt, w_gate, w_up, l,
                                           batch=b, seq=s, name=f"mixers_{l}")
        xf, h_ffn = _cross(xf, oa, ob, w_out_l, norm_mem, w_xq_b, kvm, w_xo_b, norm_ffn, l,
                           seq=s, tm=512, name=f"out_cross_{l}")
        act, w_down_l = _swiglu_gu(h_ffn, w_gate_l, w_up_l, w_down, l, tm=1024, tn=512,
                                   name=f"swiglu_gu_{l}")
        xf = _mm_res([act], w_down_l, xf, 0, tm=1024, tn=512, name=f"swiglu_down_{l}")
    out = _final_norm(xf, norm_final[None, :], tm=512)
    return out.reshape(b, s, d)
```

```python
import functools
import math

import jax
import jax.numpy as jnp
from jax import lax
from jax.experimental import pallas as pl
from jax.experimental.pallas import tpu as pltpu

D_MODEL = 2048
DEPTH = 4
CHUNK = 64
LEFT_CHUNKS = 8
A_HEAD_DIM = 128
A_HEADS = 8
A_WIDTH = A_HEADS * A_HEAD_DIM
REL_CLIP = 128
B_NOPE = 128
B_ROPE = 64
B_V = 128
B_HEADS = 8
B_WIDTH = B_HEADS * B_V
Q_LORA = 768
KV_LORA = 512
ROPE_THETA = 10000.0
N_MEM = 256
X_HEADS = 4
X_HEAD_DIM = 128
X_WIDTH = X_HEADS * X_HEAD_DIM
X_ROW_PARTS = 2
GU_ROW_PARTS = 2
D_FF = 5632
EPS = 1e-6

V7X_LANES = 128
V7X_VMEM_BYTES = 64 * 1024 * 1024
V7X_VMEM_REQUEST_CAP = 56 * 1024 * 1024

BF16 = jnp.bfloat16
F32 = jnp.float32

IN_QLAT_OFF = 3 * A_WIDTH
IN_ROPE_OFF = IN_QLAT_OFF + Q_LORA
IN_KVLAT_OFF = IN_ROPE_OFF + 2 * V7X_LANES
IN_PAD_WIDTH = IN_KVLAT_OFF + KV_LORA

A_TQ = 4 * CHUNK
A_WIN = A_TQ + LEFT_CHUNKS * CHUNK
A_KBLKS = A_WIN // A_TQ
A_BIAS_VEC = 1024
A_LOOKAHEAD = 2

B_TQ = 512
B_TK = 512
B_HEADS_PER_STEP = 4
B_LOOKAHEAD = 2
B_RED_SLABS = 8
B_QK = B_NOPE + 2 * B_ROPE

LOG2E = math.log2(math.e)
A_SCORE_SCALE = A_HEAD_DIM ** -0.5 * LOG2E


def _cparams(sem, vmem_bytes):
    limit = int(min(max(vmem_bytes, 32 * 1024 * 1024), V7X_VMEM_REQUEST_CAP))
    return pltpu.CompilerParams(dimension_semantics=sem, vmem_limit_bytes=limit)


def _rms_rows(x, g):
    ms = jnp.mean(x * x, axis=-1, keepdims=True)
    return (x * lax.rsqrt(ms + EPS)) * g


def _norm_into(h_ref, x_ref, g_ref, rows=128):
    for r0 in range(0, x_ref.shape[0], rows):
        x = x_ref[r0:r0 + rows, :].astype(F32)
        h_ref[r0:r0 + rows, :] = _rms_rows(x, g_ref[...]).astype(h_ref.dtype)


def _norm_matmul_kernel(x_ref, g_ref, w_ref, o_ref, h_ref):
    @pl.when(pl.program_id(1) == 0)
    def _():
        _norm_into(h_ref, x_ref, g_ref)

    o_ref[...] = jnp.dot(h_ref[...], w_ref[...], preferred_element_type=F32).astype(o_ref.dtype)


def _norm_matmul(x, g, w, layer, *, tm, tn, name):
    m, k = x.shape
    n = w.shape[2]
    xb = x.dtype.itemsize
    est = 2 * tm * k * xb + tm * k * 2 + 2 * k * tn * 2 + 2 * tm * tn * 2 + tm * tn * 4 + (4 << 20)
    return pl.pallas_call(
        _norm_matmul_kernel,
        out_shape=jax.ShapeDtypeStruct((m, n), BF16),
        grid=(m // tm, n // tn),
        in_specs=[
            pl.BlockSpec((tm, k), lambda i, j: (i, 0)),
            pl.BlockSpec((None, 1, k), lambda i, j: (layer, 0, 0)),
            pl.BlockSpec((None, k, tn), lambda i, j: (layer, 0, j)),
        ],
        out_specs=pl.BlockSpec((tm, tn), lambda i, j: (i, j)),
        scratch_shapes=[pltpu.VMEM((tm, k), BF16)],
        compiler_params=_cparams(("arbitrary", "arbitrary"), est),
        name=name,
    )(x, g, w)


def _mem_kv_kernel(x_ref, g_ref, w_ref, o_ref, h_ref):
    @pl.when(pl.program_id(0) == 0)
    def _():
        _norm_into(h_ref, x_ref, g_ref)

    o_ref[...] = jnp.dot(h_ref[...], w_ref[...].astype(BF16),
                         preferred_element_type=F32).astype(o_ref.dtype)


def _mem_kv(mem2d, g, w_xkv):
    m, k = mem2d.shape
    layers, _, n = w_xkv.shape
    est = 2 * m * k * 4 + m * k * 2 + 2 * k * n * 4 + k * n * 2 + 2 * m * n * 2 + m * n * 4 + (4 << 20)
    return pl.pallas_call(
        _mem_kv_kernel,
        out_shape=jax.ShapeDtypeStruct((m, layers * n), BF16),
        grid=(layers,),
        in_specs=[
            pl.BlockSpec((m, k), lambda l: (0, 0)),
            pl.BlockSpec((1, k), lambda l: (0, 0)),
            pl.BlockSpec((None, k, n), lambda l: (l, 0, 0)),
        ],
        out_specs=pl.BlockSpec((m, n), lambda l: (0, l)),
        scratch_shapes=[pltpu.VMEM((m, k), BF16)],
        compiler_params=_cparams(("arbitrary",), est),
        name="mem_kv",
    )(mem2d, g, w_xkv)


def _mm_res_kernel(*refs, n_in):
    a_refs = refs[:n_in]
    w_refs = refs[n_in:2 * n_in]
    r_ref = refs[2 * n_in]
    o_ref = refs[2 * n_in + 1]
    acc = r_ref[...]
    for a_ref, w_ref in zip(a_refs, w_refs):
        acc = acc + jnp.dot(a_ref[...], w_ref[...], preferred_element_type=F32)
    o_ref[...] = acc


def _mm_res(a_list, w, res, layer, *, tm, tn, name):
    m, n = res.shape
    n_in = len(a_list)
    in_specs = []
    w_specs = []
    est = (4 + 2 * n_in) * tm * tn * 4 + (4 << 20)
    for idx, a in enumerate(a_list):
        k = a.shape[1]
        assert w.shape[1] == n_in * k
        in_specs.append(pl.BlockSpec((tm, k), lambda i, j: (i, 0)))
        w_specs.append(pl.BlockSpec((None, k, tn), lambda i, j, idx=idx: (layer, idx, j)))
        est += 2 * tm * k * 2 + 2 * k * tn * 2
    in_specs += w_specs
    in_specs.append(pl.BlockSpec((tm, tn), lambda i, j: (i, j)))
    return pl.pallas_call(
        functools.partial(_mm_res_kernel, n_in=n_in),
        out_shape=jax.ShapeDtypeStruct((m, n), F32),
        grid=(m // tm, n // tn),
        in_specs=in_specs,
        out_specs=pl.BlockSpec((tm, tn), lambda i, j: (i, j)),
        compiler_params=_cparams(("arbitrary", "arbitrary"), est),
        name=name,
    )(*a_list, *([w] * n_in), res)


def _swiglu_gu_kernel(h_ref, wg_ref, wu_ref, wd_ref, o_ref, wdb_ref):
    rows = h_ref.shape[0] // GU_ROW_PARTS
    for r in range(GU_ROW_PARTS):
        rs = slice(r * rows, (r + 1) * rows)
        h = h_ref[rs, :]
        gate = jnp.dot(h, wg_ref[...], preferred_element_type=F32)
        up = jnp.dot(h, wu_ref[...], preferred_element_type=F32)
        o_ref[rs, :] = (gate * (1.0 / (1.0 + jnp.exp(-gate))) * up).astype(o_ref.dtype)
    wdb_ref[...] = wd_ref[...].astype(BF16)


def _swiglu_gu(h, wg, wu, wd, layer, *, tm, tn, name):
    m, k = h.shape
    n = wg.shape[2]
    steps = (m // tm) * (n // tn)
    nj = n // tn
    wd_rows, wd_cols = wd.shape[1], wd.shape[2]
    slab = wd_rows // steps
    assert slab * steps == wd_rows and slab % 16 == 0
    est = (2 * tm * k * 2 + 4 * k * tn * 2 + 2 * tm * tn * 2 + 3 * tm * tn * 4
           + 2 * slab * wd_cols * 6 + (4 << 20))
    return pl.pallas_call(
        _swiglu_gu_kernel,
        out_shape=(jax.ShapeDtypeStruct((m, n), BF16),
                   jax.ShapeDtypeStruct((1, wd_rows, wd_cols), BF16)),
        grid=(m // tm, nj),
        in_specs=[
            pl.BlockSpec((tm, k), lambda i, j: (i, 0)),
            pl.BlockSpec((None, k, tn), lambda i, j: (0, 0, j)),
            pl.BlockSpec((None, k, tn), lambda i, j: (0, 0, j)),
            pl.BlockSpec((None, slab, wd_cols), lambda i, j: (layer, i * nj + j, 0)),
        ],
        out_specs=(pl.BlockSpec((tm, tn), lambda i, j: (i, j)),
                   pl.BlockSpec((None, slab, wd_cols), lambda i, j: (0, i * nj + j, 0))),
        compiler_params=_cparams(("arbitrary", "arbitrary"), est),
        name=name,
    )(h, wg, wu, wd)


def _rope_table_kernel(pos_ref, inv_ref, sign_ref, cs_ref):
    ang = pos_ref[...].astype(F32) * inv_ref[...]
    lane = lax.broadcasted_iota(jnp.int32, ang.shape, 1)
    cs_ref[...] = jnp.where(lane < B_ROPE, jnp.cos(ang), jnp.sin(ang) * sign_ref[...])


def _rope_table(pos_col, inv_row, sign_row, *, tm):
    m = pos_col.shape[0]
    return pl.pallas_call(
        _rope_table_kernel,
        out_shape=jax.ShapeDtypeStruct((m, V7X_LANES), F32),
        grid=(m // tm,),
        in_specs=[
            pl.BlockSpec((tm, 1), lambda i: (i, 0)),
            pl.BlockSpec((1, V7X_LANES), lambda i: (0, 0)),
            pl.BlockSpec((1, V7X_LANES), lambda i: (0, 0)),
        ],
        out_specs=pl.BlockSpec((tm, V7X_LANES), lambda i: (i, 0)),
        compiler_params=_cparams(("arbitrary",), 32 << 20),
        name="rope_table",
    )(pos_col, inv_row, sign_row)


def _rope_pair_sum(slot, cs):
    t = slot * cs
    return t + pltpu.roll(t, B_ROPE, axis=1)


def _mla_prep_kernel(ql_ref, rope_ref, kvl_ref, cs_ref, gq_ref, gkv_ref, wq_ref, wkv_ref, wo_ref,
                     qt_ref, kn_ref, vt_ref, kpe_ref, wob_ref, *, q_scale):
    cs = cs_ref[...]
    hq = _rms_rows(ql_ref[...].astype(F32), gq_ref[...]).astype(BF16)
    qf = jnp.dot(hq, wq_ref[...], preferred_element_type=F32)
    for h in range(B_HEADS):
        base = h * B_QK
        nope = qf[:, base:base + B_NOPE]
        pe = _rope_pair_sum(qf[:, base + B_NOPE:base + B_QK], cs)
        qt_ref[base:base + B_NOPE, :] = (nope * q_scale).T.astype(BF16)
        qt_ref[base + B_NOPE:base + B_QK, :] = (pe * q_scale).T.astype(BF16)

    hkv = _rms_rows(kvl_ref[...].astype(F32), gkv_ref[...]).astype(BF16)
    kvf = jnp.dot(hkv, wkv_ref[...], preferred_element_type=F32)
    kn_ref[...] = kvf[:, :B_HEADS * B_NOPE].astype(BF16)
    vt_ref[...] = kvf[:, B_HEADS * B_NOPE:].T.astype(BF16)

    kpe = _rope_pair_sum(rope_ref[...].astype(F32), cs)
    lane = lax.broadcasted_iota(jnp.int32, kpe.shape, 1)
    kpe_ref[...] = jnp.where(lane < B_ROPE, kpe, 0.0).astype(BF16)
    wob_ref[...] = wo_ref[...].astype(BF16)


def _mla_prep(proj, cs, gq, gkv, wq, wkv, wo, layer, *, seq, tm, name):
    m = proj.shape[0]
    tiles_per_seq = seq // tm
    q_scale = (B_NOPE + B_ROPE) ** -0.5 * LOG2E
    wo_rows, wo_cols = wo.shape[1], wo.shape[2]
    slab = wo_rows // (m // tm)
    assert slab * (m // tm) == wo_rows and slab % 16 == 0
    est = (2 * tm * (Q_LORA + KV_LORA + 128) * 2 + 2 * tm * 128 * 4
           + 2 * (Q_LORA + KV_LORA) * 2048 * 2 + 2 * slab * wo_cols * 6
           + 2 * tm * (2048 + 1024 + 1024 + 128) * 2 + 3 * tm * 2048 * 4 + (4 << 20))
    return pl.pallas_call(
        functools.partial(_mla_prep_kernel, q_scale=q_scale),
        out_shape=(
            jax.ShapeDtypeStruct((m // seq * B_HEADS * B_QK, seq), BF16),
            jax.ShapeDtypeStruct((m, B_HEADS * B_NOPE), BF16),
            jax.ShapeDtypeStruct((m // seq * B_WIDTH, seq), BF16),
            jax.ShapeDtypeStruct((m, V7X_LANES), BF16),
            jax.ShapeDtypeStruct((1, wo_rows, wo_cols), BF16),
        ),
        grid=(m // tm,),
        in_specs=[
            pl.BlockSpec((tm, Q_LORA), lambda i: (i, IN_QLAT_OFF // Q_LORA)),
            pl.BlockSpec((tm, V7X_LANES), lambda i: (i, IN_ROPE_OFF // V7X_LANES)),
            pl.BlockSpec((tm, KV_LORA), lambda i: (i, IN_KVLAT_OFF // KV_LORA)),
            pl.BlockSpec((tm, V7X_LANES), lambda i: (i, 0)),
            pl.BlockSpec((None, 1, Q_LORA), lambda i: (layer, 0, 0)),
            pl.BlockSpec((None, 1, KV_LORA), lambda i: (layer, 0, 0)),
            pl.BlockSpec((None, Q_LORA, B_HEADS * B_QK), lambda i: (layer, 0, 0)),
            pl.BlockSpec((None, KV_LORA, 2 * B_WIDTH), lambda i: (layer, 0, 0)),
            pl.BlockSpec((None, slab, wo_cols), lambda i: (layer, i, 0)),
        ],
        out_specs=(
            pl.BlockSpec((B_HEADS * B_QK, tm), lambda i: (i // tiles_per_seq, i % tiles_per_seq)),
            pl.BlockSpec((tm, B_HEADS * B_NOPE), lambda i: (i, 0)),
            pl.BlockSpec((B_WIDTH, tm), lambda i: (i // tiles_per_seq, i % tiles_per_seq)),
            pl.BlockSpec((tm, V7X_LANES), lambda i: (i, 0)),
            pl.BlockSpec((None, slab, wo_cols), lambda i: (0, i, 0)),
        ),
        compiler_params=_cparams(("arbitrary",), est),
        name=name,
    )(proj, proj, proj, cs, gq, gkv, wq, wkv, wo)


class _MixerAHeads:
    def __init__(self, q_ref, k_refs, v_refs, bias_ref, o_ref):
        self.q_ref, self.k_refs, self.v_refs = q_ref, k_refs, v_refs
        self.bias_ref, self.o_ref = bias_ref, o_ref
        self.tiles = [None] * A_HEADS
        self.end = 0

    def _scores(self, h):
        hs = slice(h * A_HEAD_DIM, (h + 1) * A_HEAD_DIM)
        k = jnp.concatenate([r[:, hs] for r in self.k_refs], axis=0)
        return jnp.dot(k, self.q_ref[:, hs].T, preferred_element_type=F32)

    def begin(self, first, end):
        self.end = end
        for h in range(first, min(first + A_LOOKAHEAD, end)):
            self.tiles[h] = self._scores(h)

    def head(self, h):
        d, tq = A_HEAD_DIM, A_TQ
        hs = slice(h * d, (h + 1) * d)
        if h + A_LOOKAHEAD < self.end:
            self.tiles[h + A_LOOKAHEAD] = self._scores(h + A_LOOKAHEAD)
        v = jnp.concatenate([r[:, hs] for r in self.v_refs], axis=0)
        st = self.bias_ref[h // 2][:, (h % 2) * tq:(h % 2 + 1) * tq] + self.tiles[h]
        self.tiles[h] = None
        m = _col_reduce(st, jnp.max)
        p = jnp.exp2(st - m)
        l = _col_reduce(p, jnp.sum)
        ot = lax.dot_general(v, p.astype(BF16), (((0,), (0,)), ((), ())),
                             preferred_element_type=F32)
        self.o_ref[:, hs] = (ot / l).T.astype(self.o_ref.dtype)


def _col_reduce(x, op):
    rows, cols = x.shape
    slabs = op(x.reshape(B_RED_SLABS, rows // B_RED_SLABS, cols), axis=0)
    return op(slabs, axis=0, keepdims=True)


def _mixers_kernel(q_ref, kn_ref, kpe_ref, vt_ref, wg_ref, wu_ref,
                   qa_ref, ka0_ref, ka1_ref, ka2_ref, va0_ref, va1_ref, va2_ref, bias_ref,
                   o_ref, wgb_ref, wub_ref, oa_ref,
                   m_ref, l_ref, acc_ref, st_ref):
    i = pl.program_id(2)
    hp = B_HEADS_PER_STEP
    m_ref[...] = jnp.full(m_ref.shape, -jnp.inf, F32)
    l_ref[...] = jnp.zeros(l_ref.shape, F32)
    acc_ref[...] = jnp.zeros(acc_ref.shape, F32)

    def scores(j, h):
        r0 = pl.multiple_of(j * B_TK, B_TK)
        k = jnp.concatenate([kn_ref[pl.ds(r0, B_TK), h * B_NOPE:(h + 1) * B_NOPE],
                             kpe_ref[pl.ds(r0, B_TK), :]], axis=1)
        return jnp.dot(k, q_ref[h * B_QK:(h + 1) * B_QK, :], preferred_element_type=F32)

    def softmax_pv(j, h, st, allowed):
        r0 = pl.multiple_of(j * B_TK, B_TK)
        if allowed is not None:
            st = jnp.where(allowed, st, -jnp.inf)
        m_prev = m_ref[h]
        m_new = jnp.maximum(m_prev, _col_reduce(st, jnp.max))
        alpha = jnp.exp2(m_prev - m_new)
        p = jnp.exp2(st - m_new)
        l_ref[h] = alpha * l_ref[h] + _col_reduce(p, jnp.sum)
        pv = jnp.dot(vt_ref[h * B_V:(h + 1) * B_V, pl.ds(r0, B_TK)], p.astype(BF16),
                     preferred_element_type=F32)
        acc_ref[h] = alpha * acc_ref[h] + pv
        m_ref[h] = m_new

    def key_block(j, masked, after_head=None):
        allowed = None
        if masked:
            krow = lax.broadcasted_iota(jnp.int32, (B_TK, B_TQ), 0)
            qcol = lax.broadcasted_iota(jnp.int32, (B_TK, B_TQ), 1)
            allowed = (krow // CHUNK) <= (qcol // CHUNK)
        look = B_LOOKAHEAD
        tiles = [st_ref[h] for h in range(look)] + [None] * (hp - look)
        for h in range(hp):
            ahead = h + look
            if ahead < hp:
                tiles[ahead] = scores(j, ahead)
            elif not masked:
                st_ref[ahead - hp] = scores(j + 1, ahead - hp)
            softmax_pv(j, h, tiles[h], allowed)
            tiles[h] = None
            if after_head is not None:
                after_head(h)

    def body_pair(u, carry):
        key_block(2 * u, False)
        key_block(2 * u + 1, False)
        return carry

    mixer_a = _MixerAHeads(qa_ref, (ka0_ref, ka1_ref, ka2_ref), (va0_ref, va1_ref, va2_ref),
                           bias_ref, oa_ref)

    for h in range(B_LOOKAHEAD):
        st_ref[h] = scores(0, h)
    lax.fori_loop(0, lax.shift_right_logical(i, 1), body_pair, 0)

    @pl.when(lax.bitwise_and(i, 1) == 1)
    def _():
        key_block(i - 1, False)

    mixer_a.begin(0, A_HEADS)

    def mixer_a_heads(h):
        for ha in range(h * A_HEADS // hp, (h + 1) * A_HEADS // hp):
            mixer_a.head(ha)

    key_block(i, True, after_head=mixer_a_heads)

    for h in range(hp):
        o_ref[:, h * B_V:(h + 1) * B_V] = (acc_ref[h] / l_ref[h]).T.astype(o_ref.dtype)

    wgb_ref[...] = wg_ref[...].astype(BF16)
    wub_ref[...] = wu_ref[...].astype(BF16)


def _mixers(proj, bias, qt, kn, kpe, vt, wg, wu, layer, *, batch, seq, name):
    m = kn.shape[0]
    nq = seq // B_TQ
    hp = B_HEADS_PER_STEP
    ng = B_HEADS // hp
    steps = batch * ng * nq
    nqa = seq // A_TQ
    assert steps == batch * nqa
    w_rows, w_cols = wg.shape[1], wg.shape[2]
    slab = w_rows // steps
    assert slab * steps == w_rows and slab % 16 == 0 and wu.shape == wg.shape
    est = (2 * B_TQ * hp * B_QK * 2 + 2 * seq * hp * (B_NOPE + B_V) * 2 + 2 * seq * 128 * 2
           + 2 * B_TQ * hp * B_V * 2 + hp * (B_V + 16) * B_TQ * 4 + 8 * B_TQ * B_TK * 4
           + 4 * slab * w_cols * 6
           + 2 * 8 * A_TQ * A_WIDTH * 2 + 2 * A_HEADS * A_TQ * A_WIN * 4 + 6 * A_WIN * A_TQ * 4
           + (4 << 20))

    def slab_in(b, g, i):
        return (layer, (b * ng + g) * nq + i, 0)

    def slab_out(b, g, i):
        return (0, (b * ng + g) * nq + i, 0)

    def a_blk(b, g, i):
        return g * nq + i

    def a_kv_spec(col_blk, c):
        back = A_KBLKS - 1 - c
        return pl.BlockSpec((A_TQ, A_WIDTH),
                            lambda b, g, i: (b * nqa + jnp.maximum(a_blk(b, g, i) - back, 0), col_blk))

    return pl.pallas_call(
        _mixers_kernel,
        out_shape=(jax.ShapeDtypeStruct((m, B_WIDTH), BF16),
                   jax.ShapeDtypeStruct((1, w_rows, w_cols), BF16),
                   jax.ShapeDtypeStruct((1, w_rows, w_cols), BF16),
                   jax.ShapeDtypeStruct((m, A_WIDTH), BF16)),
        grid=(batch, ng, nq),
        in_specs=[
            pl.BlockSpec((hp * B_QK, B_TQ), lambda b, g, i: (b * ng + g, i)),
            pl.BlockSpec((seq, hp * B_NOPE), lambda b, g, i: (b, g)),
            pl.BlockSpec((seq, V7X_LANES), lambda b, g, i: (b, 0)),
            pl.BlockSpec((hp * B_V, seq), lambda b, g, i: (b * ng + g, 0)),
            pl.BlockSpec((None, slab, w_cols), slab_in),
            pl.BlockSpec((None, slab, w_cols), slab_in),
            pl.BlockSpec((A_TQ, A_WIDTH), lambda b, g, i: (b * nqa + a_blk(b, g, i), 0)),
            a_kv_spec(1, 0), a_kv_spec(1, 1), a_kv_spec(1, 2),
            a_kv_spec(2, 0), a_kv_spec(2, 1), a_kv_spec(2, 2),
            pl.BlockSpec((A_HEADS // 2, None, A_WIN, 2 * A_TQ),
                         lambda b, g, i: (layer, jnp.minimum(a_blk(b, g, i), A_KBLKS - 1), 0, 0)),
        ],
        out_specs=(pl.BlockSpec((B_TQ, hp * B_V), lambda b, g, i: (b * nq + i, g)),
                   pl.BlockSpec((None, slab, w_cols), slab_out),
                   pl.BlockSpec((None, slab, w_cols), slab_out),
                   pl.BlockSpec((A_TQ, A_WIDTH), lambda b, g, i: (b * nqa + a_blk(b, g, i), 0))),
        scratch_shapes=[
            pltpu.VMEM((hp, 1, B_TQ), F32),
            pltpu.VMEM((hp, 1, B_TQ), F32),
            pltpu.VMEM((hp, B_V, B_TQ), F32),
            pltpu.VMEM((B_LOOKAHEAD, B_TK, B_TQ), F32),
        ],
        compiler_params=_cparams(("arbitrary", "arbitrary", "arbitrary"), est),
        name=name,
    )(qt, kn, kpe, vt, wg, wu, proj, proj, proj, proj, proj, proj, proj, bias)


def _cross_kernel(x_ref, oa_ref, ob_ref, woa_ref, wob_ref, g_ref, wq_ref, k_ref, v_ref, wo_ref,
                  gf_ref, o_ref, hf_ref):
    scale = X_HEAD_DIM ** -0.5
    rows = x_ref.shape[0] // X_ROW_PARTS
    for r in range(X_ROW_PARTS):
        rs = slice(r * rows, (r + 1) * rows)
        x = (x_ref[rs, :] + jnp.dot(oa_ref[rs, :], woa_ref[...], preferred_element_type=F32)
             + jnp.dot(ob_ref[rs, :], wob_ref[...], preferred_element_type=F32))
        hn = _rms_rows(x, g_ref[...]).astype(BF16)
        q = (jnp.dot(hn, wq_ref[...], preferred_element_type=F32) * scale).astype(BF16)
        outs = []
        for h in range(X_HEADS):
            hs = slice(h * X_HEAD_DIM, (h + 1) * X_HEAD_DIM)
            s = lax.dot_general(q[:, hs], k_ref[:, hs], (((1,), (1,)), ((), ())),
                                preferred_element_type=F32)
            m = jnp.max(s, axis=1, keepdims=True)
            p = jnp.exp(s - m)
            l = jnp.sum(p, axis=1, keepdims=True)
            o = jnp.dot(p.astype(BF16), v_ref[:, hs], preferred_element_type=F32)
            outs.append((o / l).astype(BF16))
        o_all = jnp.concatenate(outs, axis=1)
        x2 = x + jnp.dot(o_all, wo_ref[...], preferred_element_type=F32)
        o_ref[rs, :] = x2
        hf_ref[rs, :] = _rms_rows(x2, gf_ref[...]).astype(BF16)


def _cross(x, oa, ob, w_out_l, g, wq, kvm, wo, g_ffn, layer, *, seq, tm, name):
    m, d = x.shape
    blocks_per_batch = seq // tm
    est = (4 * tm * d * 4 + tm * d * 2 + 2 * d * X_WIDTH * 2 + 2 * X_WIDTH * d * 2
           + 4 * N_MEM * X_WIDTH * 2 + 4 * tm * N_MEM * 4 + 3 * tm * d * 4
           + 2 * tm * (A_WIDTH + B_WIDTH) * 2 + 2 * (A_WIDTH + B_WIDTH) * d * 2
           + 2 * tm * d * 2 + (4 << 20))
    return pl.pallas_call(
        _cross_kernel,
        out_shape=(jax.ShapeDtypeStruct((m, d), F32), jax.ShapeDtypeStruct((m, d), BF16)),
        grid=(m // tm,),
        in_specs=[
            pl.BlockSpec((tm, d), lambda i: (i, 0)),
            pl.BlockSpec((tm, A_WIDTH), lambda i: (i, 0)),
            pl.BlockSpec((tm, B_WIDTH), lambda i: (i, 0)),
            pl.BlockSpec((None, A_WIDTH, d), lambda i: (0, 0, 0)),
            pl.BlockSpec((None, B_WIDTH, d), lambda i: (0, A_WIDTH // B_WIDTH, 0)),
            pl.BlockSpec((None, 1, d), lambda i: (layer, 0, 0)),
            pl.BlockSpec((None, d, X_WIDTH), lambda i: (layer, 0, 0)),
            pl.BlockSpec((N_MEM, X_WIDTH), lambda i: (i // blocks_per_batch, 2 * layer)),
            pl.BlockSpec((N_MEM, X_WIDTH), lambda i: (i // blocks_per_batch, 2 * layer + 1)),
            pl.BlockSpec((None, X_WIDTH, d), lambda i: (layer, 0, 0)),
            pl.BlockSpec((None, 1, d), lambda i: (layer, 0, 0)),
        ],
        out_specs=(pl.BlockSpec((tm, d), lambda i: (i, 0)),
                   pl.BlockSpec((tm, d), lambda i: (i, 0))),
        compiler_params=_cparams(("arbitrary",), est),
        name=name,
    )(x, oa, ob, w_out_l, w_out_l, g, wq, kvm, kvm, wo, g_ffn)


def _final_norm_kernel(x_ref, g_ref, o_ref):
    o_ref[...] = _rms_rows(x_ref[...], g_ref[...])


def _final_norm(x, g, *, tm):
    m, d = x.shape
    return pl.pallas_call(
        _final_norm_kernel,
        out_shape=jax.ShapeDtypeStruct((m, d), F32),
        grid=(m // tm,),
        in_specs=[pl.BlockSpec((tm, d), lambda i: (i, 0)), pl.BlockSpec((1, d), lambda i: (0, 0))],
        out_specs=pl.BlockSpec((tm, d), lambda i: (i, 0)),
        compiler_params=_cparams(("arbitrary",), 6 * tm * d * 4 + (4 << 20)),
        name="final_norm",
    )(x, g)


RELAYOUT_COLS = 256
RELAYOUT_ROPE_BLK = IN_ROPE_OFF // RELAYOUT_COLS


def _w_in_relayout_kernel(wt_ref, o_ref):
    j = pl.program_id(1)

    @pl.when(j != RELAYOUT_ROPE_BLK)
    def _():
        factor = jnp.where(j < A_WIDTH // RELAYOUT_COLS, A_SCORE_SCALE, 1.0)
        o_ref[...] = (wt_ref[...] * factor).T.astype(BF16)

    @pl.when(j == RELAYOUT_ROPE_BLK)
    def _():
        half = B_ROPE // 2
        kr = wt_ref[:B_ROPE, :]
        slot = jnp.concatenate([kr, kr[half:], kr[:half],
                                jnp.zeros((RELAYOUT_COLS - 2 * B_ROPE, kr.shape[1]), F32)], axis=0)
        o_ref[...] = slot.T.astype(BF16)


def _w_in_relayout(w_in_t):
    layers, n, k = w_in_t.shape
    lat_blk = IN_ROPE_OFF // RELAYOUT_COLS
    rope_src = (IN_ROPE_OFF + KV_LORA) // RELAYOUT_COLS
    assert IN_KVLAT_OFF == (RELAYOUT_ROPE_BLK + 1) * RELAYOUT_COLS

    def src_block(l, j):
        return (l, jnp.where(j < RELAYOUT_ROPE_BLK, j,
                             jnp.where(j == RELAYOUT_ROPE_BLK, rope_src, j - 1 + lat_blk - RELAYOUT_ROPE_BLK)), 0)

    return pl.pallas_call(
        _w_in_relayout_kernel,
        out_shape=jax.ShapeDtypeStruct((layers, k, IN_PAD_WIDTH), BF16),
        grid=(layers, IN_PAD_WIDTH // RELAYOUT_COLS),
        in_specs=[pl.BlockSpec((None, RELAYOUT_COLS, k), src_block)],
        out_specs=pl.BlockSpec((None, k, RELAYOUT_COLS), lambda l, j: (l, 0, j)),
        compiler_params=_cparams(("arbitrary", "arbitrary"), 32 << 20),
        name="w_in_relayout",
    )(w_in_t)


def _prep_weights(w_in, w_uq, w_ukv):
    swap = (jnp.arange(B_ROPE) + B_ROPE // 2) % B_ROPE
    w_in_p = _w_in_relayout(jnp.swapaxes(w_in, 1, 2))

    uq = w_uq.reshape(DEPTH, Q_LORA, B_HEADS, B_NOPE + B_ROPE)
    uq_pe = uq[..., B_NOPE:]
    wq_p = jnp.concatenate([uq[..., :B_NOPE], uq_pe, uq_pe[..., swap]], axis=-1)
    wq_p = wq_p.reshape(DEPTH, Q_LORA, B_HEADS * B_QK).astype(BF16)

    ukv = w_ukv.reshape(DEPTH, KV_LORA, B_HEADS, B_NOPE + B_V)
    wkv_p = jnp.concatenate([ukv[..., :B_NOPE].reshape(DEPTH, KV_LORA, B_HEADS * B_NOPE),
                             ukv[..., B_NOPE:].reshape(DEPTH, KV_LORA, B_WIDTH)],
                            axis=-1).astype(BF16)

    return w_in_p, wq_p, wkv_p


def _bias_table_kernel(w_ref, o_ref):
    v = pl.program_id(1)
    r = lax.broadcasted_iota(jnp.int32, (A_TQ, A_WIN), 0)
    t = lax.broadcasted_iota(jnp.int32, (A_TQ, A_WIN), 1)
    dchunk = t // CHUNK - r // CHUNK
    valid = (dchunk >= 0) & (dchunk <= LEFT_CHUNKS) & (v - (A_KBLKS - 1) + t // A_TQ >= 0)
    for hh in range(2):
        w = jnp.broadcast_to(w_ref[hh], (A_TQ, A_BIAS_VEC))
        tab = pltpu.roll(w, 0, axis=1, stride=1, stride_axis=0)[:, :A_WIN]
        o_ref[0, 0, :, hh * A_TQ:(hh + 1) * A_TQ] = jnp.where(valid, tab * LOG2E, -jnp.inf).T


def _relpos_bias_tables(rel_bias):
    u = jnp.arange(A_BIAS_VEC)
    delta = jnp.where(u <= A_WIN, u, u - A_BIAS_VEC)
    idx = jnp.clip(LEFT_CHUNKS * CHUNK - delta, -REL_CLIP, REL_CLIP) + REL_CLIP
    vec = rel_bias[:, :, idx].reshape(DEPTH * A_HEADS, 1, A_BIAS_VEC).astype(F32)
    return pl.pallas_call(
        _bias_table_kernel,
        out_shape=jax.ShapeDtypeStruct((DEPTH * A_HEADS // 2, A_KBLKS, A_WIN, 2 * A_TQ), F32),
        grid=(DEPTH * A_HEADS // 2, A_KBLKS),
        in_specs=[pl.BlockSpec((2, 1, A_BIAS_VEC), lambda i, v: (i, 0, 0))],
        out_specs=pl.BlockSpec((1, 1, A_WIN, 2 * A_TQ), lambda i, v: (i, v, 0, 0)),
        compiler_params=_cparams(("arbitrary", "arbitrary"), 32 << 20),
        name="bias_table",
    )(vec)


def kernel(x, mem, positions, norm_mix, w_in, rel_bias, q_norm, kv_norm, w_uq, w_ukv, w_out,
           norm_mem, mem_norm, w_xq, w_xkv, w_xo, norm_ffn, w_gate, w_up, w_down, norm_final):
    b, s, d = x.shape
    m = b * s
    assert (b, s, d) == (4, 4096, D_MODEL) and mem.shape == (b, N_MEM, d)

    w_in_p, wq_p, wkv_p = _prep_weights(w_in, w_uq, w_ukv)
    w_xq_b = w_xq.astype(BF16)
    w_xo_b = w_xo.astype(BF16)
    bias_tab = _relpos_bias_tables(rel_bias)

    half = B_ROPE // 2
    inv = ROPE_THETA ** (-jnp.arange(half, dtype=F32) / half)
    inv_row = jnp.tile(inv, 4)[None, :]
    sign_row = jnp.concatenate([jnp.ones((B_ROPE,), F32), -jnp.ones((half,), F32),
                                jnp.ones((half,), F32)])[None, :]
    cs = _rope_table(positions.reshape(m, 1), inv_row, sign_row, tm=2048)

    kvm = _mem_kv(mem.reshape(b * N_MEM, d), mem_norm[None, :], w_xkv)

    norm_mix, norm_mem, norm_ffn, q_norm, kv_norm = (
        g[:, None, :] for g in (norm_mix, norm_mem, norm_ffn, q_norm, kv_norm))
    xf = x.reshape(m, d)
    for l in range(DEPTH):
        proj = _norm_matmul(xf, norm_mix, w_in_p, l, tm=1024, tn=1536, name=f"in_proj_{l}")
        qt, kn, vt, kpe, w_out_l = _mla_prep(proj, cs, q_norm, kv_norm, wq_p, wkv_p, w_out, l,
                                             seq=s, tm=1024, name=f"mla_prep_{l}")
        ob, w_gate_l, w_up_l, oa = _mixers(proj, bias_tab, qt, kn, kpe, vt, w_gate, w_up, l,
                                           batch=b, seq=s, name=f"mixers_{l}")
        xf, h_ffn = _cross(xf, oa, ob, w_out_l, norm_mem, w_xq_b, kvm, w_xo_b, norm_ffn, l,
                           seq=s, tm=512, name=f"out_cross_{l}")
        act, w_down_l = _swiglu_gu(h_ffn, w_gate_l, w_up_l, w_down, l, tm=1024, tn=512,
                                   name=f"swiglu_gu_{l}")
        xf = _mm_res([act], w_down_l, xf, 0, tm=1024, tn=512, name=f"swiglu_down_{l}")
    out = _final_norm(xf, norm_final[None, :], tm=512)
    return out.reshape(b, s, d)
```

```python
import functools
import math

import jax
import jax.numpy as jnp
from jax import lax
from jax.experimental import pallas as pl
from jax.experimental.pallas import tpu as pltpu

D_MODEL = 2048
DEPTH = 4
CHUNK = 64
LEFT_CHUNKS = 8
A_HEAD_DIM = 128
A_HEADS = 8
A_WIDTH = A_HEADS * A_HEAD_DIM
REL_CLIP = 128
B_NOPE = 128
B_ROPE = 64
B_V = 128
B_HEADS = 8
B_WIDTH = B_HEADS * B_V
Q_LORA = 768
KV_LORA = 512
ROPE_THETA = 10000.0
N_MEM = 256
X_HEADS = 4
X_HEAD_DIM = 128
X_WIDTH = X_HEADS * X_HEAD_DIM
X_ROW_PARTS = 2
D_FF = 5632
EPS = 1e-6

V7X_LANES = 128
V7X_VMEM_BYTES = 64 * 1024 * 1024
V7X_VMEM_REQUEST_CAP = 56 * 1024 * 1024

BF16 = jnp.bfloat16
F32 = jnp.float32

IN_QLAT_OFF = 3 * A_WIDTH
IN_ROPE_OFF = IN_QLAT_OFF + Q_LORA
IN_KVLAT_OFF = IN_ROPE_OFF + 2 * V7X_LANES
IN_PAD_WIDTH = IN_KVLAT_OFF + KV_LORA

A_TQ = 4 * CHUNK
A_WIN = A_TQ + LEFT_CHUNKS * CHUNK
A_KBLKS = A_WIN // A_TQ
A_BIAS_VEC = 1024
A_LOOKAHEAD = 2

B_TQ = 512
B_TK = 512
B_HEADS_PER_STEP = 4
B_LOOKAHEAD = 2
B_RED_SLABS = 8
B_QK = B_NOPE + 2 * B_ROPE

LOG2E = math.log2(math.e)
A_SCORE_SCALE = A_HEAD_DIM ** -0.5 * LOG2E


def _cparams(sem, vmem_bytes):
    limit = int(min(max(vmem_bytes, 32 * 1024 * 1024), V7X_VMEM_REQUEST_CAP))
    return pltpu.CompilerParams(dimension_semantics=sem, vmem_limit_bytes=limit)


def _rms_rows(x, g):
    ms = jnp.mean(x * x, axis=-1, keepdims=True)
    return (x * lax.rsqrt(ms + EPS)) * g


def _norm_into(h_ref, x_ref, g_ref, rows=128):
    for r0 in range(0, x_ref.shape[0], rows):
        x = x_ref[r0:r0 + rows, :].astype(F32)
        h_ref[r0:r0 + rows, :] = _rms_rows(x, g_ref[...]).astype(h_ref.dtype)


def _norm_matmul_kernel(x_ref, g_ref, w_ref, o_ref, h_ref):
    @pl.when(pl.program_id(1) == 0)
    def _():
        _norm_into(h_ref, x_ref, g_ref)

    o_ref[...] = jnp.dot(h_ref[...], w_ref[...], preferred_element_type=F32).astype(o_ref.dtype)


def _norm_matmul(x, g, w, layer, *, tm, tn, name):
    m, k = x.shape
    n = w.shape[2]
    xb = x.dtype.itemsize
    est = 2 * tm * k * xb + tm * k * 2 + 2 * k * tn * 2 + 2 * tm * tn * 2 + tm * tn * 4 + (4 << 20)
    return pl.pallas_call(
        _norm_matmul_kernel,
        out_shape=jax.ShapeDtypeStruct((m, n), BF16),
        grid=(m // tm, n // tn),
        in_specs=[
            pl.BlockSpec((tm, k), lambda i, j: (i, 0)),
            pl.BlockSpec((None, 1, k), lambda i, j: (layer, 0, 0)),
            pl.BlockSpec((None, k, tn), lambda i, j: (layer, 0, j)),
        ],
        out_specs=pl.BlockSpec((tm, tn), lambda i, j: (i, j)),
        scratch_shapes=[pltpu.VMEM((tm, k), BF16)],
        compiler_params=_cparams(("arbitrary", "arbitrary"), est),
        name=name,
    )(x, g, w)


def _mem_kv_kernel(x_ref, g_ref, w_ref, o_ref, h_ref):
    @pl.when(pl.program_id(0) == 0)
    def _():
        _norm_into(h_ref, x_ref, g_ref)

    o_ref[...] = jnp.dot(h_ref[...], w_ref[...].astype(BF16),
                         preferred_element_type=F32).astype(o_ref.dtype)


def _mem_kv(mem2d, g, w_xkv):
    m, k = mem2d.shape
    layers, _, n = w_xkv.shape
    est = 2 * m * k * 4 + m * k * 2 + 2 * k * n * 4 + k * n * 2 + 2 * m * n * 2 + m * n * 4 + (4 << 20)
    return pl.pallas_call(
        _mem_kv_kernel,
        out_shape=jax.ShapeDtypeStruct((m, layers * n), BF16),
        grid=(layers,),
        in_specs=[
            pl.BlockSpec((m, k), lambda l: (0, 0)),
            pl.BlockSpec((1, k), lambda l: (0, 0)),
            pl.BlockSpec((None, k, n), lambda l: (l, 0, 0)),
        ],
        out_specs=pl.BlockSpec((m, n), lambda l: (0, l)),
        scratch_shapes=[pltpu.VMEM((m, k), BF16)],
        compiler_params=_cparams(("arbitrary",), est),
        name="mem_kv",
    )(mem2d, g, w_xkv)


def _mm_res_kernel(*refs, n_in):
    a_refs = refs[:n_in]
    w_refs = refs[n_in:2 * n_in]
    r_ref = refs[2 * n_in]
    o_ref = refs[2 * n_in + 1]
    acc = r_ref[...]
    for a_ref, w_ref in zip(a_refs, w_refs):
        acc = acc + jnp.dot(a_ref[...], w_ref[...], preferred_element_type=F32)
    o_ref[...] = acc


def _mm_res(a_list, w, res, layer, *, tm, tn, name):
    m, n = res.shape
    n_in = len(a_list)
    in_specs = []
    w_specs = []
    est = (4 + 2 * n_in) * tm * tn * 4 + (4 << 20)
    for idx, a in enumerate(a_list):
        k = a.shape[1]
        assert w.shape[1] == n_in * k
        in_specs.append(pl.BlockSpec((tm, k), lambda i, j: (i, 0)))
        w_specs.append(pl.BlockSpec((None, k, tn), lambda i, j, idx=idx: (layer, idx, j)))
        est += 2 * tm * k * 2 + 2 * k * tn * 2
    in_specs += w_specs
    in_specs.append(pl.BlockSpec((tm, tn), lambda i, j: (i, j)))
    return pl.pallas_call(
        functools.partial(_mm_res_kernel, n_in=n_in),
        out_shape=jax.ShapeDtypeStruct((m, n), F32),
        grid=(m // tm, n // tn),
        in_specs=in_specs,
        out_specs=pl.BlockSpec((tm, tn), lambda i, j: (i, j)),
        compiler_params=_cparams(("arbitrary", "arbitrary"), est),
        name=name,
    )(*a_list, *([w] * n_in), res)


def _swiglu_gu_kernel(h_ref, wg_ref, wu_ref, wd_ref, o_ref, wdb_ref):
    h = h_ref[...]
    gate = jnp.dot(h, wg_ref[...], preferred_element_type=F32)
    up = jnp.dot(h, wu_ref[...], preferred_element_type=F32)
    o_ref[...] = (gate * (1.0 / (1.0 + jnp.exp(-gate))) * up).astype(o_ref.dtype)
    wdb_ref[...] = wd_ref[...].astype(BF16)


def _swiglu_gu(h, wg, wu, wd, layer, *, tm, tn, name):
    m, k = h.shape
    n = wg.shape[2]
    steps = (m // tm) * (n // tn)
    nj = n // tn
    wd_rows, wd_cols = wd.shape[1], wd.shape[2]
    slab = wd_rows // steps
    assert slab * steps == wd_rows and slab % 16 == 0
    est = (2 * tm * k * 2 + 4 * k * tn * 2 + 2 * tm * tn * 2 + 3 * tm * tn * 4
           + 2 * slab * wd_cols * 6 + (4 << 20))
    return pl.pallas_call(
        _swiglu_gu_kernel,
        out_shape=(jax.ShapeDtypeStruct((m, n), BF16),
                   jax.ShapeDtypeStruct((1, wd_rows, wd_cols), BF16)),
        grid=(m // tm, nj),
        in_specs=[
            pl.BlockSpec((tm, k), lambda i, j: (i, 0)),
            pl.BlockSpec((None, k, tn), lambda i, j: (0, 0, j)),
            pl.BlockSpec((None, k, tn), lambda i, j: (0, 0, j)),
            pl.BlockSpec((None, slab, wd_cols), lambda i, j: (layer, i * nj + j, 0)),
        ],
        out_specs=(pl.BlockSpec((tm, tn), lambda i, j: (i, j)),
                   pl.BlockSpec((None, slab, wd_cols), lambda i, j: (0, i * nj + j, 0))),
        compiler_params=_cparams(("arbitrary", "arbitrary"), est),
        name=name,
    )(h, wg, wu, wd)


def _rope_table_kernel(pos_ref, inv_ref, sign_ref, cs_ref):
    ang = pos_ref[...].astype(F32) * inv_ref[...]
    lane = lax.broadcasted_iota(jnp.int32, ang.shape, 1)
    cs_ref[...] = jnp.where(lane < B_ROPE, jnp.cos(ang), jnp.sin(ang) * sign_ref[...])


def _rope_table(pos_col, inv_row, sign_row, *, tm):
    m = pos_col.shape[0]
    return pl.pallas_call(
        _rope_table_kernel,
        out_shape=jax.ShapeDtypeStruct((m, V7X_LANES), F32),
        grid=(m // tm,),
        in_specs=[
            pl.BlockSpec((tm, 1), lambda i: (i, 0)),
            pl.BlockSpec((1, V7X_LANES), lambda i: (0, 0)),
            pl.BlockSpec((1, V7X_LANES), lambda i: (0, 0)),
        ],
        out_specs=pl.BlockSpec((tm, V7X_LANES), lambda i: (i, 0)),
        compiler_params=_cparams(("arbitrary",), 32 << 20),
        name="rope_table",
    )(pos_col, inv_row, sign_row)


def _rope_pair_sum(slot, cs):
    t = slot * cs
    return t + pltpu.roll(t, B_ROPE, axis=1)


def _mla_prep_kernel(ql_ref, rope_ref, kvl_ref, cs_ref, gq_ref, gkv_ref, wq_ref, wkv_ref, wo_ref,
                     qt_ref, kn_ref, vt_ref, kpe_ref, wob_ref, *, q_scale):
    cs = cs_ref[...]
    hq = _rms_rows(ql_ref[...].astype(F32), gq_ref[...]).astype(BF16)
    qf = jnp.dot(hq, wq_ref[...], preferred_element_type=F32)
    for h in range(B_HEADS):
        base = h * B_QK
        nope = qf[:, base:base + B_NOPE]
        pe = _rope_pair_sum(qf[:, base + B_NOPE:base + B_QK], cs)
        qt_ref[base:base + B_NOPE, :] = (nope * q_scale).T.astype(BF16)
        qt_ref[base + B_NOPE:base + B_QK, :] = (pe * q_scale).T.astype(BF16)

    hkv = _rms_rows(kvl_ref[...].astype(F32), gkv_ref[...]).astype(BF16)
    kvf = jnp.dot(hkv, wkv_ref[...], preferred_element_type=F32)
    kn_ref[...] = kvf[:, :B_HEADS * B_NOPE].astype(BF16)
    vt_ref[...] = kvf[:, B_HEADS * B_NOPE:].T.astype(BF16)

    kpe = _rope_pair_sum(rope_ref[...].astype(F32), cs)
    lane = lax.broadcasted_iota(jnp.int32, kpe.shape, 1)
    kpe_ref[...] = jnp.where(lane < B_ROPE, kpe, 0.0).astype(BF16)
    wob_ref[...] = wo_ref[...].astype(BF16)


def _mla_prep(proj, cs, gq, gkv, wq, wkv, wo, layer, *, seq, tm, name):
    m = proj.shape[0]
    tiles_per_seq = seq // tm
    q_scale = (B_NOPE + B_ROPE) ** -0.5 * LOG2E
    wo_rows, wo_cols = wo.shape[1], wo.shape[2]
    slab = wo_rows // (m // tm)
    assert slab * (m // tm) == wo_rows and slab % 16 == 0
    est = (2 * tm * (Q_LORA + KV_LORA + 128) * 2 + 2 * tm * 128 * 4
           + 2 * (Q_LORA + KV_LORA) * 2048 * 2 + 2 * slab * wo_cols * 6
           + 2 * tm * (2048 + 1024 + 1024 + 128) * 2 + 3 * tm * 2048 * 4 + (4 << 20))
    return pl.pallas_call(
        functools.partial(_mla_prep_kernel, q_scale=q_scale),
        out_shape=(
            jax.ShapeDtypeStruct((m // seq * B_HEADS * B_QK, seq), BF16),
            jax.ShapeDtypeStruct((m, B_HEADS * B_NOPE), BF16),
            jax.ShapeDtypeStruct((m // seq * B_WIDTH, seq), BF16),
            jax.ShapeDtypeStruct((m, V7X_LANES), BF16),
            jax.ShapeDtypeStruct((1, wo_rows, wo_cols), BF16),
        ),
        grid=(m // tm,),
        in_specs=[
            pl.BlockSpec((tm, Q_LORA), lambda i: (i, IN_QLAT_OFF // Q_LORA)),
            pl.BlockSpec((tm, V7X_LANES), lambda i: (i, IN_ROPE_OFF // V7X_LANES)),
            pl.BlockSpec((tm, KV_LORA), lambda i: (i, IN_KVLAT_OFF // KV_LORA)),
            pl.BlockSpec((tm, V7X_LANES), lambda i: (i, 0)),
            pl.BlockSpec((None, 1, Q_LORA), lambda i: (layer, 0, 0)),
            pl.BlockSpec((None, 1, KV_LORA), lambda i: (layer, 0, 0)),
            pl.BlockSpec((None, Q_LORA, B_HEADS * B_QK), lambda i: (layer, 0, 0)),
            pl.BlockSpec((None, KV_LORA, 2 * B_WIDTH), lambda i: (layer, 0, 0)),
            pl.BlockSpec((None, slab, wo_cols), lambda i: (layer, i, 0)),
        ],
        out_specs=(
            pl.BlockSpec((B_HEADS * B_QK, tm), lambda i: (i // tiles_per_seq, i % tiles_per_seq)),
            pl.BlockSpec((tm, B_HEADS * B_NOPE), lambda i: (i, 0)),
            pl.BlockSpec((B_WIDTH, tm), lambda i: (i // tiles_per_seq, i % tiles_per_seq)),
            pl.BlockSpec((tm, V7X_LANES), lambda i: (i, 0)),
            pl.BlockSpec((None, slab, wo_cols), lambda i: (0, i, 0)),
        ),
        compiler_params=_cparams(("arbitrary",), est),
        name=name,
    )(proj, proj, proj, cs, gq, gkv, wq, wkv, wo)


class _MixerAHeads:
    def __init__(self, q_ref, k_refs, v_refs, bias_ref, o_ref):
        self.q_ref, self.k_refs, self.v_refs = q_ref, k_refs, v_refs
        self.bias_ref, self.o_ref = bias_ref, o_ref
        self.tiles = [None] * A_HEADS
        self.end = 0

    def _scores(self, h):
        hs = slice(h * A_HEAD_DIM, (h + 1) * A_HEAD_DIM)
        k = jnp.concatenate([r[:, hs] for r in self.k_refs], axis=0)
        return jnp.dot(k, self.q_ref[:, hs].T, preferred_element_type=F32)

    def begin(self, first, end):
        self.end = end
        for h in range(first, min(first + A_LOOKAHEAD, end)):
            self.tiles[h] = self._scores(h)

    def head(self, h):
        d, tq = A_HEAD_DIM, A_TQ
        hs = slice(h * d, (h + 1) * d)
        if h + A_LOOKAHEAD < self.end:
            self.tiles[h + A_LOOKAHEAD] = self._scores(h + A_LOOKAHEAD)
        v = jnp.concatenate([r[:, hs] for r in self.v_refs], axis=0)
        st = self.bias_ref[h // 2][:, (h % 2) * tq:(h % 2 + 1) * tq] + self.tiles[h]
        self.tiles[h] = None
        m = _col_reduce(st, jnp.max)
        p = jnp.exp2(st - m)
        l = _col_reduce(p, jnp.sum)
        ot = lax.dot_general(v, p.astype(BF16), (((0,), (0,)), ((), ())),
                             preferred_element_type=F32)
        self.o_ref[hs, :] = (ot / l).astype(self.o_ref.dtype)


def _col_reduce(x, op):
    rows, cols = x.shape
    slabs = op(x.reshape(B_RED_SLABS, rows // B_RED_SLABS, cols), axis=0)
    return op(slabs, axis=0, keepdims=True)


def _mixers_kernel(q_ref, kn_ref, kpe_ref, vt_ref, wg_ref, wu_ref,
                   qa_ref, ka0_ref, ka1_ref, ka2_ref, va0_ref, va1_ref, va2_ref, bias_ref,
                   o_ref, wgb_ref, wub_ref, oa_ref,
                   m_ref, l_ref, acc_ref, st_ref):
    i = pl.program_id(2)
    hp = B_HEADS_PER_STEP
    m_ref[...] = jnp.full(m_ref.shape, -jnp.inf, F32)
    l_ref[...] = jnp.zeros(l_ref.shape, F32)
    acc_ref[...] = jnp.zeros(acc_ref.shape, F32)

    def scores(j, h):
        r0 = pl.multiple_of(j * B_TK, B_TK)
        k = jnp.concatenate([kn_ref[pl.ds(r0, B_TK), h * B_NOPE:(h + 1) * B_NOPE],
                             kpe_ref[pl.ds(r0, B_TK), :]], axis=1)
        return jnp.dot(k, q_ref[h * B_QK:(h + 1) * B_QK, :], preferred_element_type=F32)

    def softmax_pv(j, h, st, allowed):
        r0 = pl.multiple_of(j * B_TK, B_TK)
        if allowed is not None:
            st = jnp.where(allowed, st, -jnp.inf)
        m_prev = m_ref[h]
        m_new = jnp.maximum(m_prev, _col_reduce(st, jnp.max))
        alpha = jnp.exp2(m_prev - m_new)
        p = jnp.exp2(st - m_new)
        l_ref[h] = alpha * l_ref[h] + _col_reduce(p, jnp.sum)
        pv = jnp.dot(vt_ref[h * B_V:(h + 1) * B_V, pl.ds(r0, B_TK)], p.astype(BF16),
                     preferred_element_type=F32)
        acc_ref[h] = alpha * acc_ref[h] + pv
        m_ref[h] = m_new

    def key_block(j, masked, after_head=None):
        allowed = None
        if masked:
            krow = lax.broadcasted_iota(jnp.int32, (B_TK, B_TQ), 0)
            qcol = lax.broadcasted_iota(jnp.int32, (B_TK, B_TQ), 1)
            allowed = (krow // CHUNK) <= (qcol // CHUNK)
        look = B_LOOKAHEAD
        tiles = [st_ref[h] for h in range(look)] + [None] * (hp - look)
        for h in range(hp):
            ahead = h + look
            if ahead < hp:
                tiles[ahead] = scores(j, ahead)
            elif not masked:
                st_ref[ahead - hp] = scores(j + 1, ahead - hp)
            softmax_pv(j, h, tiles[h], allowed)
            tiles[h] = None
            if after_head is not None:
                after_head(h)

    def body_pair(u, carry):
        key_block(2 * u, False)
        key_block(2 * u + 1, False)
        return carry

    mixer_a = _MixerAHeads(qa_ref, (ka0_ref, ka1_ref, ka2_ref), (va0_ref, va1_ref, va2_ref),
                           bias_ref, oa_ref)

    for h in range(B_LOOKAHEAD):
        st_ref[h] = scores(0, h)
    lax.fori_loop(0, lax.shift_right_logical(i, 1), body_pair, 0)

    @pl.when(lax.bitwise_and(i, 1) == 1)
    def _():
        key_block(i - 1, False)

    mixer_a.begin(0, A_HEADS)

    def mixer_a_heads(h):
        for ha in range(h * A_HEADS // hp, (h + 1) * A_HEADS // hp):
            mixer_a.head(ha)

    key_block(i, True, after_head=mixer_a_heads)

    for h in range(hp):
        o_ref[h * B_V:(h + 1) * B_V, :] = (acc_ref[h] / l_ref[h]).astype(o_ref.dtype)

    wgb_ref[...] = wg_ref[...].astype(BF16)
    wub_ref[...] = wu_ref[...].astype(BF16)


def _mixers(proj, bias, qt, kn, kpe, vt, wg, wu, layer, *, batch, seq, name):
    m = kn.shape[0]
    nq = seq // B_TQ
    hp = B_HEADS_PER_STEP
    ng = B_HEADS // hp
    steps = batch * ng * nq
    nqa = seq // A_TQ
    assert steps == batch * nqa
    w_rows, w_cols = wg.shape[1], wg.shape[2]
    slab = w_rows // steps
    assert slab * steps == w_rows and slab % 16 == 0 and wu.shape == wg.shape
    est = (2 * B_TQ * hp * B_QK * 2 + 2 * seq * hp * (B_NOPE + B_V) * 2 + 2 * seq * 128 * 2
           + 2 * B_TQ * hp * B_V * 2 + hp * (B_V + 16) * B_TQ * 4 + 8 * B_TQ * B_TK * 4
           + 4 * slab * w_cols * 6
           + 2 * 8 * A_TQ * A_WIDTH * 2 + 2 * A_HEADS * A_TQ * A_WIN * 4 + 6 * A_WIN * A_TQ * 4
           + (4 << 20))

    def slab_in(b, g, i):
        return (layer, (b * ng + g) * nq + i, 0)

    def slab_out(b, g, i):
        return (0, (b * ng + g) * nq + i, 0)

    def a_blk(b, g, i):
        return g * nq + i

    def a_kv_spec(col_blk, c):
        back = A_KBLKS - 1 - c
        return pl.BlockSpec((A_TQ, A_WIDTH),
                            lambda b, g, i: (b * nqa + jnp.maximum(a_blk(b, g, i) - back, 0), col_blk))

    return pl.pallas_call(
        _mixers_kernel,
        out_shape=(jax.ShapeDtypeStruct((m // seq * B_WIDTH, seq), BF16),
                   jax.ShapeDtypeStruct((1, w_rows, w_cols), BF16),
                   jax.ShapeDtypeStruct((1, w_rows, w_cols), BF16),
                   jax.ShapeDtypeStruct((m // seq * A_WIDTH, seq), BF16)),
        grid=(batch, ng, nq),
        in_specs=[
            pl.BlockSpec((hp * B_QK, B_TQ), lambda b, g, i: (b * ng + g, i)),
            pl.BlockSpec((seq, hp * B_NOPE), lambda b, g, i: (b, g)),
            pl.BlockSpec((seq, V7X_LANES), lambda b, g, i: (b, 0)),
            pl.BlockSpec((hp * B_V, seq), lambda b, g, i: (b * ng + g, 0)),
            pl.BlockSpec((None, slab, w_cols), slab_in),
            pl.BlockSpec((None, slab, w_cols), slab_in),
            pl.BlockSpec((A_TQ, A_WIDTH), lambda b, g, i: (b * nqa + a_blk(b, g, i), 0)),
            a_kv_spec(1, 0), a_kv_spec(1, 1), a_kv_spec(1, 2),
            a_kv_spec(2, 0), a_kv_spec(2, 1), a_kv_spec(2, 2),
            pl.BlockSpec((A_HEADS // 2, None, A_WIN, 2 * A_TQ),
                         lambda b, g, i: (layer, jnp.minimum(a_blk(b, g, i), A_KBLKS - 1), 0, 0)),
        ],
        out_specs=(pl.BlockSpec((hp * B_V, B_TQ), lambda b, g, i: (b * ng + g, i)),
                   pl.BlockSpec((None, slab, w_cols), slab_out),
                   pl.BlockSpec((None, slab, w_cols), slab_out),
                   pl.BlockSpec((A_WIDTH, A_TQ), lambda b, g, i: (b, a_blk(b, g, i)))),
        scratch_shapes=[
            pltpu.VMEM((hp, 1, B_TQ), F32),
            pltpu.VMEM((hp, 1, B_TQ), F32),
            pltpu.VMEM((hp, B_V, B_TQ), F32),
            pltpu.VMEM((B_LOOKAHEAD, B_TK, B_TQ), F32),
        ],
        compiler_params=_cparams(("arbitrary", "arbitrary", "arbitrary"), est),
        name=name,
    )(qt, kn, kpe, vt, wg, wu, proj, proj, proj, proj, proj, proj, proj, bias)


def _cross_kernel(x_ref, oa_ref, ob_ref, woa_ref, wob_ref, g_ref, wq_ref, k_ref, v_ref, wo_ref,
                  gf_ref, o_ref, hf_ref):
    scale = X_HEAD_DIM ** -0.5
    rows = x_ref.shape[0] // X_ROW_PARTS
    for r in range(X_ROW_PARTS):
        rs = slice(r * rows, (r + 1) * rows)
        tdims = (((0,), (0,)), ((), ()))
        x = (x_ref[rs, :]
             + lax.dot_general(oa_ref[:, rs], woa_ref[...], tdims, preferred_element_type=F32)
             + lax.dot_general(ob_ref[:, rs], wob_ref[...], tdims, preferred_element_type=F32))
        hn = _rms_rows(x, g_ref[...]).astype(BF16)
        q = (jnp.dot(hn, wq_ref[...], preferred_element_type=F32) * scale).astype(BF16)
        outs = []
        for h in range(X_HEADS):
            hs = slice(h * X_HEAD_DIM, (h + 1) * X_HEAD_DIM)
            s = lax.dot_general(q[:, hs], k_ref[:, hs], (((1,), (1,)), ((), ())),
                                preferred_element_type=F32)
            m = jnp.max(s, axis=1, keepdims=True)
            p = jnp.exp(s - m)
            l = jnp.sum(p, axis=1, keepdims=True)
            o = jnp.dot(p.astype(BF16), v_ref[:, hs], preferred_element_type=F32)
            outs.append((o / l).astype(BF16))
        o_all = jnp.concatenate(outs, axis=1)
        x2 = x + jnp.dot(o_all, wo_ref[...], preferred_element_type=F32)
        o_ref[rs, :] = x2
        hf_ref[rs, :] = _rms_rows(x2, gf_ref[...]).astype(BF16)


def _cross(x, oa, ob, w_out_l, g, wq, kvm, wo, g_ffn, layer, *, seq, tm, name):
    m, d = x.shape
    blocks_per_batch = seq // tm
    est = (4 * tm * d * 4 + tm * d * 2 + 2 * d * X_WIDTH * 2 + 2 * X_WIDTH * d * 2
           + 4 * N_MEM * X_WIDTH * 2 + 4 * tm * N_MEM * 4 + 3 * tm * d * 4
           + 2 * tm * (A_WIDTH + B_WIDTH) * 2 + 2 * (A_WIDTH + B_WIDTH) * d * 2
           + 2 * tm * d * 2 + (4 << 20))
    return pl.pallas_call(
        _cross_kernel,
        out_shape=(jax.ShapeDtypeStruct((m, d), F32), jax.ShapeDtypeStruct((m, d), BF16)),
        grid=(m // tm,),
        in_specs=[
            pl.BlockSpec((tm, d), lambda i: (i, 0)),
            pl.BlockSpec((A_WIDTH, tm), lambda i: (i // blocks_per_batch, i % blocks_per_batch)),
            pl.BlockSpec((B_WIDTH, tm), lambda i: (i // blocks_per_batch, i % blocks_per_batch)),
            pl.BlockSpec((None, A_WIDTH, d), lambda i: (0, 0, 0)),
            pl.BlockSpec((None, B_WIDTH, d), lambda i: (0, A_WIDTH // B_WIDTH, 0)),
            pl.BlockSpec((None, 1, d), lambda i: (layer, 0, 0)),
            pl.BlockSpec((None, d, X_WIDTH), lambda i: (layer, 0, 0)),
            pl.BlockSpec((N_MEM, X_WIDTH), lambda i: (i // blocks_per_batch, 2 * layer)),
            pl.BlockSpec((N_MEM, X_WIDTH), lambda i: (i // blocks_per_batch, 2 * layer + 1)),
            pl.BlockSpec((None, X_WIDTH, d), lambda i: (layer, 0, 0)),
            pl.BlockSpec((None, 1, d), lambda i: (layer, 0, 0)),
        ],
        out_specs=(pl.BlockSpec((tm, d), lambda i: (i, 0)),
                   pl.BlockSpec((tm, d), lambda i: (i, 0))),
        compiler_params=_cparams(("arbitrary",), est),
        name=name,
    )(x, oa, ob, w_out_l, w_out_l, g, wq, kvm, kvm, wo, g_ffn)


def _final_norm_kernel(x_ref, g_ref, o_ref):
    o_ref[...] = _rms_rows(x_ref[...], g_ref[...])


def _final_norm(x, g, *, tm):
    m, d = x.shape
    return pl.pallas_call(
        _final_norm_kernel,
        out_shape=jax.ShapeDtypeStruct((m, d), F32),
        grid=(m // tm,),
        in_specs=[pl.BlockSpec((tm, d), lambda i: (i, 0)), pl.BlockSpec((1, d), lambda i: (0, 0))],
        out_specs=pl.BlockSpec((tm, d), lambda i: (i, 0)),
        compiler_params=_cparams(("arbitrary",), 6 * tm * d * 4 + (4 << 20)),
        name="final_norm",
    )(x, g)


RELAYOUT_COLS = 256
RELAYOUT_ROPE_BLK = IN_ROPE_OFF // RELAYOUT_COLS


def _w_in_relayout_kernel(wt_ref, o_ref):
    j = pl.program_id(1)

    @pl.when(j != RELAYOUT_ROPE_BLK)
    def _():
        factor = jnp.where(j < A_WIDTH // RELAYOUT_COLS, A_SCORE_SCALE, 1.0)
        o_ref[...] = (wt_ref[...] * factor).T.astype(BF16)

    @pl.when(j == RELAYOUT_ROPE_BLK)
    def _():
        half = B_ROPE // 2
        kr = wt_ref[:B_ROPE, :]
        slot = jnp.concatenate([kr, kr[half:], kr[:half],
                                jnp.zeros((RELAYOUT_COLS - 2 * B_ROPE, kr.shape[1]), F32)], axis=0)
        o_ref[...] = slot.T.astype(BF16)


def _w_in_relayout(w_in_t):
    layers, n, k = w_in_t.shape
    lat_blk = IN_ROPE_OFF // RELAYOUT_COLS
    rope_src = (IN_ROPE_OFF + KV_LORA) // RELAYOUT_COLS
    assert IN_KVLAT_OFF == (RELAYOUT_ROPE_BLK + 1) * RELAYOUT_COLS

    def src_block(l, j):
        return (l, jnp.where(j < RELAYOUT_ROPE_BLK, j,
                             jnp.where(j == RELAYOUT_ROPE_BLK, rope_src, j - 1 + lat_blk - RELAYOUT_ROPE_BLK)), 0)

    return pl.pallas_call(
        _w_in_relayout_kernel,
        out_shape=jax.ShapeDtypeStruct((layers, k, IN_PAD_WIDTH), BF16),
        grid=(layers, IN_PAD_WIDTH // RELAYOUT_COLS),
        in_specs=[pl.BlockSpec((None, RELAYOUT_COLS, k), src_block)],
        out_specs=pl.BlockSpec((None, k, RELAYOUT_COLS), lambda l, j: (l, 0, j)),
        compiler_params=_cparams(("arbitrary", "arbitrary"), 32 << 20),
        name="w_in_relayout",
    )(w_in_t)


def _prep_weights(w_in, w_uq, w_ukv):
    swap = (jnp.arange(B_ROPE) + B_ROPE // 2) % B_ROPE
    w_in_p = _w_in_relayout(jnp.swapaxes(w_in, 1, 2))

    uq = w_uq.reshape(DEPTH, Q_LORA, B_HEADS, B_NOPE + B_ROPE)
    uq_pe = uq[..., B_NOPE:]
    wq_p = jnp.concatenate([uq[..., :B_NOPE], uq_pe, uq_pe[..., swap]], axis=-1)
    wq_p = wq_p.reshape(DEPTH, Q_LORA, B_HEADS * B_QK).astype(BF16)

    ukv = w_ukv.reshape(DEPTH, KV_LORA, B_HEADS, B_NOPE + B_V)
    wkv_p = jnp.concatenate([ukv[..., :B_NOPE].reshape(DEPTH, KV_LORA, B_HEADS * B_NOPE),
                             ukv[..., B_NOPE:].reshape(DEPTH, KV_LORA, B_WIDTH)],
                            axis=-1).astype(BF16)

    return w_in_p, wq_p, wkv_p


def _bias_table_kernel(w_ref, o_ref):
    v = pl.program_id(1)
    r = lax.broadcasted_iota(jnp.int32, (A_TQ, A_WIN), 0)
    t = lax.broadcasted_iota(jnp.int32, (A_TQ, A_WIN), 1)
    dchunk = t // CHUNK - r // CHUNK
    valid = (dchunk >= 0) & (dchunk <= LEFT_CHUNKS) & (v - (A_KBLKS - 1) + t // A_TQ >= 0)
    for hh in range(2):
        w = jnp.broadcast_to(w_ref[hh], (A_TQ, A_BIAS_VEC))
        tab = pltpu.roll(w, 0, axis=1, stride=1, stride_axis=0)[:, :A_WIN]
        o_ref[0, 0, :, hh * A_TQ:(hh + 1) * A_TQ] = jnp.where(valid, tab * LOG2E, -jnp.inf).T


def _relpos_bias_tables(rel_bias):
    u = jnp.arange(A_BIAS_VEC)
    delta = jnp.where(u <= A_WIN, u, u - A_BIAS_VEC)
    idx = jnp.clip(LEFT_CHUNKS * CHUNK - delta, -REL_CLIP, REL_CLIP) + REL_CLIP
    vec = rel_bias[:, :, idx].reshape(DEPTH * A_HEADS, 1, A_BIAS_VEC).astype(F32)
    return pl.pallas_call(
        _bias_table_kernel,
        out_shape=jax.ShapeDtypeStruct((DEPTH * A_HEADS // 2, A_KBLKS, A_WIN, 2 * A_TQ), F32),
        grid=(DEPTH * A_HEADS // 2, A_KBLKS),
        in_specs=[pl.BlockSpec((2, 1, A_BIAS_VEC), lambda i, v: (i, 0, 0))],
        out_specs=pl.BlockSpec((1, 1, A_WIN, 2 * A_TQ), lambda i, v: (i, v, 0, 0)),
        compiler_params=_cparams(("arbitrary", "arbitrary"), 32 << 20),
        name="bias_table",
    )(vec)


def kernel(x, mem, positions, norm_mix, w_in, rel_bias, q_norm, kv_norm, w_uq, w_ukv, w_out,
           norm_mem, mem_norm, w_xq, w_xkv, w_xo, norm_ffn, w_gate, w_up, w_down, norm_final):
    b, s, d = x.shape
    m = b * s
    assert (b, s, d) == (4, 4096, D_MODEL) and mem.shape == (b, N_MEM, d)

    w_in_p, wq_p, wkv_p = _prep_weights(w_in, w_uq, w_ukv)
    w_xq_b = w_xq.astype(BF16)
    w_xo_b = w_xo.astype(BF16)
    bias_tab = _relpos_bias_tables(rel_bias)

    half = B_ROPE // 2
    inv = ROPE_THETA ** (-jnp.arange(half, dtype=F32) / half)
    inv_row = jnp.tile(inv, 4)[None, :]
    sign_row = jnp.concatenate([jnp.ones((B_ROPE,), F32), -jnp.ones((half,), F32),
                                jnp.ones((half,), F32)])[None, :]
    cs = _rope_table(positions.reshape(m, 1), inv_row, sign_row, tm=2048)

    kvm = _mem_kv(mem.reshape(b * N_MEM, d), mem_norm[None, :], w_xkv)

    norm_mix, norm_mem, norm_ffn, q_norm, kv_norm = (
        g[:, None, :] for g in (norm_mix, norm_mem, norm_ffn, q_norm, kv_norm))
    xf = x.reshape(m, d)
    for l in range(DEPTH):
        proj = _norm_matmul(xf, norm_mix, w_in_p, l, tm=1024, tn=1536, name=f"in_proj_{l}")
        qt, kn, vt, kpe, w_out_l = _mla_prep(proj, cs, q_norm, kv_norm, wq_p, wkv_p, w_out, l,
                                             seq=s, tm=1024, name=f"mla_prep_{l}")
        ob, w_gate_l, w_up_l, oa = _mixers(proj, bias_tab, qt, kn, kpe, vt, w_gate, w_up, l,
                                           batch=b, seq=s, name=f"mixers_{l}")
        xf, h_ffn = _cross(xf, oa, ob, w_out_l, norm_mem, w_xq_b, kvm, w_xo_b, norm_ffn, l,
                           seq=s, tm=512, name=f"out_cross_{l}")
        act, w_down_l = _swiglu_gu(h_ffn, w_gate_l, w_up_l, w_down, l, tm=1024, tn=512,
                                   name=f"swiglu_gu_{l}")
        xf = _mm_res([act], w_down_l, xf, 0, tm=1024, tn=512, name=f"swiglu_down_{l}")
    out = _final_norm(xf, norm_final[None, :], tm=512)
    return out.reshape(b, s, d)
```
